```python
import jax, jax.numpy as jnp
from jax import lax
import numpy as np

D_MODEL = 1024
BATCH = 32
SEQ = 2048
DEPTH = 1
DEC_BATCH = 1
DEC_SEQ = 16384
PAST_LEN = 128

MLA_HEADS = 16
QK_NOPE = 64
QK_ROPE = 32
V_HEAD = 64
Q_LORA = 384
KV_LORA = 256
MLA_WIDTH = MLA_HEADS * V_HEAD
ROPE_THETA = 10000.0
Q_BLOCK = 128
SSM_HEADS = 16
SSM_HEAD_DIM = 64
D_SSM = SSM_HEADS * SSM_HEAD_DIM
SSM_GROUPS = 2
D_STATE = 64
SSM_CONV = 3
CHUNK = 128
D_XBC = D_SSM + 2 * SSM_GROUPS * D_STATE
MIX_WIDTH = MLA_WIDTH + D_SSM
D_IN = Q_LORA + (KV_LORA + QK_ROPE) + D_SSM + D_XBC + 2 * SSM_HEADS
SPLITS = (Q_LORA,
          Q_LORA + KV_LORA + QK_ROPE,
          Q_LORA + KV_LORA + QK_ROPE + D_SSM,
          Q_LORA + KV_LORA + QK_ROPE + D_SSM + D_XBC)
D_FF = 2816
FFN_CONV = 3
EPS = 1e-6

kernel_name = "hymba_mla_ssd_convffn_encoder"


def rmsnorm(x, w):
    xf = x.astype(jnp.float32)
    y = xf * lax.rsqrt(jnp.mean(xf * xf, axis=-1, keepdims=True) + EPS)
    return (y * w.astype(jnp.float32)).astype(x.dtype)


def dwconv_centred(x, w, b):
    K = w.shape[0]
    p = K // 2
    L = x.shape[1]
    xp = jnp.pad(x, ((0, 0), (p, p), (0, 0)))
    y = xp[:, 0:L] * w[0]
    for k in range(1, K):
        y = y + xp[:, k:k + L] * w[k]
    return y + b


def rope_tables(L):
    inv = ROPE_THETA ** (-jnp.arange(0, QK_ROPE, 2, dtype=jnp.float32) / QK_ROPE)
    ang = jnp.arange(L, dtype=jnp.float32)[:, None] * inv[None, :]
    return jnp.cos(ang), jnp.sin(ang)


def apply_rope(x, cos, sin):
    x1, x2 = jnp.split(x, 2, axis=-1)
    return jnp.concatenate([x1 * cos - x2 * sin, x2 * cos + x1 * sin], axis=-1).astype(x.dtype)


def mla(q_lat, kv_lat, q_a_norm, kv_a_norm, w_q_b, w_kv_b):
    b, L, _ = q_lat.shape
    q = (rmsnorm(q_lat, q_a_norm) @ w_q_b).reshape(b, L, MLA_HEADS, QK_NOPE + QK_ROPE)
    c_kv, k_rope = kv_lat[..., :KV_LORA], kv_lat[..., KV_LORA:]
    kv = (rmsnorm(c_kv, kv_a_norm) @ w_kv_b).reshape(b, L, MLA_HEADS, QK_NOPE + V_HEAD)
    k_nope, v = kv[..., :QK_NOPE], kv[..., QK_NOPE:]
    cos, sin = rope_tables(L)
    q_nope = q[..., :QK_NOPE]
    q_rope = apply_rope(q[..., QK_NOPE:], cos[:, None, :], sin[:, None, :])
    k_rope = apply_rope(k_rope, cos, sin)
    scale = (QK_NOPE + QK_ROPE) ** -0.5
    nb = L // Q_BLOCK

    def blocks(t):
        return jnp.moveaxis(t.reshape(b, nb, Q_BLOCK, *t.shape[2:]), 1, 0)

    def attend(qs):
        qn, qr = qs
        s = (jnp.einsum('bqhd,bkhd->bhqk', qn, k_nope, preferred_element_type=jnp.float32)
             + jnp.einsum('bqhr,bkr->bhqk', qr, k_rope, preferred_element_type=jnp.float32))
        p = jax.nn.softmax(s * scale, axis=-1)
        return jnp.einsum('bhqk,bkhv->bqhv', p.astype(v.dtype), v)

    o = lax.map(attend, (blocks(q_nope), blocks(q_rope)))
    return jnp.moveaxis(o, 0, 1).reshape(b, L, MLA_WIDTH)


def ssd_chunked(x, dt, A, Bm, Cm):
    b, L, H, P = x.shape
    G, N = Bm.shape[2], Bm.shape[3]
    hg = H // G
    c = L // CHUNK
    f32 = jnp.float32
    xf = x.astype(f32).reshape(b, c, CHUNK, G, hg, P)
    dtc = dt.reshape(b, c, CHUNK, G, hg)
    Bc = Bm.astype(f32).reshape(b, c, CHUNK, G, N)
    Cc = Cm.astype(f32).reshape(b, c, CHUNK, G, N)
    xdt = xf * dtc[..., None]
    acs = jnp.cumsum(dtc * A.reshape(G, hg), axis=2)
    acs_t = jnp.moveaxis(acs, 2, -1)
    seg = acs_t[..., :, None] - acs_t[..., None, :]
    lower = jnp.tril(jnp.ones((CHUNK, CHUNK), dtype=bool))
    Lm = jnp.exp(jnp.where(lower, seg, -jnp.inf))
    CB = jnp.einsum('bctgn,bcsgn->bcgts', Cc, Bc)
    y_diag = jnp.einsum('bcgts,bcghts,bcsghp->bctghp', CB, Lm, xdt)
    decay_to_end = jnp.exp(acs[:, :, -1:] - acs)
    states = jnp.einsum('bcsgn,bcsgh,bcsghp->bcghpn', Bc, decay_to_end, xdt)
    chunk_decay = jnp.exp(acs[:, :, -1])

    def step(h, inp):
        s_k, d_k = inp
        return h * d_k[..., None, None] + s_k, h

    h0 = jnp.zeros((b, G, hg, P, N), f32)
    _, h_prev = lax.scan(step, h0, (jnp.moveaxis(states, 1, 0), jnp.moveaxis(chunk_decay, 1, 0)))
    h_prev = jnp.moveaxis(h_prev, 0, 1)
    y_off = jnp.einsum('bctgn,bcghpn,bctgh->bctghp', Cc, h_prev, jnp.exp(acs))
    return (y_diag + y_off).reshape(b, L, H, P)


def ssd_mixer(z, xbc, dt_raw, conv_w, conv_b, dt_bias_f, dt_bias_b, a_log_f, a_log_b, d_skip, ssm_norm):
    b, L, _ = z.shape
    f32 = jnp.float32
    GN = SSM_GROUPS * D_STATE
    xbc = jax.nn.silu(dwconv_centred(xbc, conv_w, conv_b))
    xs = xbc[..., :D_SSM].reshape(b, L, SSM_HEADS, SSM_HEAD_DIM)
    Bm = xbc[..., D_SSM:D_SSM + GN].reshape(b, L, SSM_GROUPS, D_STATE)
    Cm = xbc[..., D_SSM + GN:].reshape(b, L, SSM_GROUPS, D_STATE)
    dt_f = jax.nn.softplus(dt_raw[..., :SSM_HEADS].astype(f32) + dt_bias_f.astype(f32))
    dt_b = jax.nn.softplus(dt_raw[..., SSM_HEADS:].astype(f32) + dt_bias_b.astype(f32))
    A_f = -jnp.exp(a_log_f.astype(f32))
    A_b = -jnp.exp(a_log_b.astype(f32))
    flip = lambda t: jnp.flip(t, axis=1)
    y_f = ssd_chunked(xs, dt_f, A_f, Bm, Cm)
    y_b = flip(ssd_chunked(flip(xs), flip(dt_b), A_b, flip(Bm), flip(Cm)))
    y = y_f + y_b + xs.astype(f32) * d_skip.astype(f32)[:, None]
    y = y.reshape(b, L, D_SSM).astype(z.dtype) * jax.nn.silu(z)
    gs = D_SSM // SSM_GROUPS
    y = rmsnorm(y.reshape(b, L, SSM_GROUPS, gs), ssm_norm.reshape(SSM_GROUPS, gs))
    return y.reshape(b, L, D_SSM)


def conv_ffn(h, w_gate, w_up, conv_w, conv_b, w_down):
    g = dwconv_centred(h @ w_gate, conv_w, conv_b)
    return (jax.nn.silu(g) * (h @ w_up)) @ w_down


def encoder(x, norm1, w_in, q_a_norm, kv_a_norm, w_q_b, w_kv_b, conv_w, conv_b,
            dt_bias_f, dt_bias_b, a_log_f, a_log_b, d_skip, ssm_norm, w_out,
            norm2, w_gate, w_up, ffn_conv_w, ffn_conv_b, w_down, final_norm):
    for l in range(DEPTH):
        h = rmsnorm(x, norm1[l])
        proj = h @ w_in[l]
        q_lat, kv_lat, z, xbc, dt_raw = jnp.split(proj, SPLITS, axis=-1)
        attn = mla(q_lat, kv_lat, q_a_norm[l], kv_a_norm[l], w_q_b[l], w_kv_b[l])
        ssm = ssd_mixer(z, xbc, dt_raw, conv_w[l], conv_b[l], dt_bias_f[l], dt_bias_b[l],
                        a_log_f[l], a_log_b[l], d_skip[l], ssm_norm[l])
        x = x + jnp.concatenate([attn, ssm], axis=-1) @ w_out[l]
        x = x + conv_ffn(rmsnorm(x, norm2[l]), w_gate[l], w_up[l], ffn_conv_w[l], ffn_conv_b[l], w_down[l])
    return rmsnorm(x, final_norm)


def setup_inputs(seed: int = 0) -> dict:
    key = jax.random.key(seed)
    ks = jax.random.split(key, 24)
    f32 = jnp.float32
    nrm = lambda k, shape, fan_in: jax.random.normal(k, shape, f32) * (fan_in ** -0.5)
    gain = lambda k, shape: 1.0 + 0.01 * jax.random.normal(k, shape, f32)
    dt0 = jnp.exp(jax.random.uniform(ks[10], (DEPTH, SSM_HEADS), f32, np.log(1e-3), np.log(1e-1)))
    dt1 = jnp.exp(jax.random.uniform(ks[11], (DEPTH, SSM_HEADS), f32, np.log(1e-3), np.log(1e-1)))
    inv_softplus = lambda d: d + jnp.log(-jnp.expm1(-d))
    return {
        "x_prompt": jax.random.normal(ks[0], (BATCH, SEQ, D_MODEL), f32),
        "x_sample": jax.random.normal(ks[1], (DEC_BATCH, DEC_SEQ, D_MODEL), f32),
        "norm1": gain(ks[2], (DEPTH, D_MODEL)),
        "w_in": nrm(ks[3], (DEPTH, D_MODEL, D_IN), D_MODEL),
        "q_a_norm": gain(ks[4], (DEPTH, Q_LORA)),
        "kv_a_norm": gain(ks[5], (DEPTH, KV_LORA)),
        "w_q_b": nrm(ks[6], (DEPTH, Q_LORA, MLA_HEADS * (QK_NOPE + QK_ROPE)), Q_LORA),
        "w_kv_b": nrm(ks[7], (DEPTH, KV_LORA, MLA_HEADS * (QK_NOPE + V_HEAD)), KV_LORA),
        "conv_w": nrm(ks[8], (DEPTH, SSM_CONV, D_XBC), SSM_CONV),
        "conv_b": 0.01 * jax.random.normal(ks[9], (DEPTH, D_XBC), f32),
        "dt_bias_f": inv_softplus(dt0),
        "dt_bias_b": inv_softplus(dt1),
        "a_log_f": jnp.log(jax.random.uniform(ks[12], (DEPTH, SSM_HEADS), f32, 1.0, 16.0)),
        "a_log_b": jnp.log(jax.random.uniform(ks[13], (DEPTH, SSM_HEADS), f32, 1.0, 16.0)),
        "d_skip": gain(ks[14], (DEPTH, SSM_HEADS)),
        "ssm_norm": gain(ks[15], (DEPTH, D_SSM)),
        "w_out": nrm(ks[16], (DEPTH, MIX_WIDTH, D_MODEL), MIX_WIDTH),
        "norm2": gain(ks[17], (DEPTH, D_MODEL)),
        "w_gate": nrm(ks[18], (DEPTH, D_MODEL, D_FF), D_MODEL),
        "w_up": nrm(ks[19], (DEPTH, D_MODEL, D_FF), D_MODEL),
        "ffn_conv_w": nrm(ks[20], (DEPTH, FFN_CONV, D_FF), FFN_CONV),
        "ffn_conv_b": 0.01 * jax.random.normal(ks[21], (DEPTH, D_FF), f32),
        "w_down": nrm(ks[22], (DEPTH, D_FF, D_MODEL), D_FF),
        "final_norm": gain(ks[23], (D_MODEL,)),
    }


def reference(x_prompt, x_sample, norm1, w_in, q_a_norm, kv_a_norm, w_q_b, w_kv_b, conv_w, conv_b,
              dt_bias_f, dt_bias_b, a_log_f, a_log_b, d_skip, ssm_norm, w_out,
              norm2, w_gate, w_up, ffn_conv_w, ffn_conv_b, w_down, final_norm):
    y_prompt = encoder(x_prompt, norm1, w_in, q_a_norm, kv_a_norm, w_q_b, w_kv_b, conv_w, conv_b,
                       dt_bias_f, dt_bias_b, a_log_f, a_log_b, d_skip, ssm_norm, w_out,
                       norm2, w_gate, w_up, ffn_conv_w, ffn_conv_b, w_down, final_norm)
    y_sample = encoder(x_sample, norm1, w_in, q_a_norm, kv_a_norm, w_q_b, w_kv_b, conv_w, conv_b,
                       dt_bias_f, dt_bias_b, a_log_f, a_log_b, d_skip, ssm_norm, w_out,
                       norm2, w_gate, w_up, ffn_conv_w, ffn_conv_b, w_down, final_norm)
    return (y_prompt, y_sample)
```

```python
import functools

import numpy as np
import jax
import jax.numpy as jnp
from jax import lax
from jax.experimental import pallas as pl
from jax.experimental.pallas import tpu as pltpu

D_MODEL = 1024
N_HEADS = 16
QK_NOPE = 64
QK_ROPE = 32
HALF_ROPE = QK_ROPE // 2
QK_DIM = QK_NOPE + QK_ROPE
V_HEAD = 64
Q_LORA = 384
KV_LORA = 256
ROPE_THETA = 10000.0
SSM_HEADS = 16
SSM_HEAD_DIM = 64
D_SSM = SSM_HEADS * SSM_HEAD_DIM
SSM_GROUPS = 2
HEADS_PER_GROUP = SSM_HEADS // SSM_GROUPS
GROUP_WIDTH = D_SSM // SSM_GROUPS
D_STATE = 64
D_XBC = D_SSM + 2 * SSM_GROUPS * D_STATE
D_FF = 2816
EPS = 1e-6

LANES = 128
HEAD_PAD = LANES
BF16_ROWS = 16
F32_ROWS = 8
DT_PAD = LANES
VMEM_LIMIT = 56 * 1024 * 1024

OFF_Q = 0
OFF_CKV = OFF_Q + Q_LORA
OFF_KR = OFF_CKV + KV_LORA
OFF_Z = OFF_KR + HEAD_PAD
OFF_XBC = OFF_Z + D_SSM
OFF_DT = OFF_XBC + D_XBC
D_IN_PAD = OFF_DT + DT_PAD

PROJ_TM = 256
ATTN_TQ = 512
ATTN_TK = 512
SSD_Q = 256
OPROJ_TM = 512
FFN_TM = 512
FFN_FC = 256

NT_DIMS = (((1,), (1,)), ((), ()))
TN_DIMS = (((0,), (0,)), ((), ()))

f32 = jnp.float32
bf16 = jnp.bfloat16


def _rms(x, w):
    return x * lax.rsqrt(jnp.mean(x * x, axis=-1, keepdims=True) + EPS) * w


def _dot(a, b):
    return jnp.dot(a, b, preferred_element_type=f32)


def _silu(x):
    return x / (1.0 + jnp.exp(-x))


def _proj_kernel(x_ref, n1_ref, win_ref, qan_ref, kvan_ref, wqT_ref, wk_ref, wvT_ref, wdtT_ref,
                 c_ref, s_ref, cT_ref, sT_ref,
                 qT_out, k_out, vT_out, z_out, xbc_out, dt_out, dtT_out):
    h = _rms(x_ref[...], n1_ref[...]).astype(bf16)
    proj = _dot(h, win_ref[...])
    z_out[...] = proj[:, OFF_Z:OFF_Z + D_SSM].astype(bf16)
    xbc_out[...] = proj[:, OFF_XBC:OFF_XBC + D_XBC].astype(bf16)
    dt_out[...] = proj[:, OFF_DT:OFF_DT + DT_PAD]
    dtT_out[...] = lax.dot_general(wdtT_ref[...], h, NT_DIMS, preferred_element_type=f32)

    hq = _rms(proj[:, OFF_Q:OFF_Q + Q_LORA], qan_ref[...]).astype(bf16)
    qT = lax.dot_general(wqT_ref[...], hq, NT_DIMS, preferred_element_type=f32)
    cT = cT_ref[...]
    sT = sT_ref[...]
    half = HEAD_PAD // 2
    for hh in range(N_HEADS):
        blk = qT[hh * HEAD_PAD:(hh + 1) * HEAD_PAD, :]
        rot = jnp.concatenate([blk[half:], blk[:half]], axis=0)
        qT_out[hh * HEAD_PAD:(hh + 1) * HEAD_PAD, :] = (blk * cT + rot * sT).astype(bf16)

    hc = _rms(proj[:, OFF_CKV:OFF_CKV + KV_LORA], kvan_ref[...]).astype(bf16)
    kn = _dot(hc, wk_ref[...])
    kr = proj[:, OFF_KR:OFF_KR + HEAD_PAD]
    krf = kr * c_ref[...] + pltpu.roll(kr, half, axis=1) * s_ref[...]
    for hh in range(N_HEADS):
        k_out[:, hh * HEAD_PAD:(hh + 1) * HEAD_PAD] = (
            kn[:, hh * HEAD_PAD:(hh + 1) * HEAD_PAD] + krf).astype(bf16)
    vT_out[...] = lax.dot_general(wvT_ref[...], hc, NT_DIMS, preferred_element_type=f32).astype(bf16)


def _const_spec(shape):
    nd = len(shape)
    return pl.BlockSpec(shape, lambda *_: (0,) * nd)


def _proj_call(x, n1, win, qan, kvan, wqT, wk, wvT, wdtT, c_tab, s_tab, cT_tab, sT_tab):
    B, L, _ = x.shape
    tm = min(PROJ_TM, L)
    grid = (B, L // tm)
    tok = lambda w: pl.BlockSpec((None, tm, w), lambda b, i: (b, i, 0))
    tokT = lambda w: pl.BlockSpec((None, w, tm), lambda b, i: (b, 0, i))
    in_specs = [
        tok(D_MODEL), _const_spec(n1.shape), _const_spec(win.shape), _const_spec(qan.shape),
        _const_spec(kvan.shape), _const_spec(wqT.shape), _const_spec(wk.shape), _const_spec(wvT.shape),
        _const_spec(wdtT.shape),
        pl.BlockSpec((tm, HEAD_PAD), lambda b, i: (i, 0)),
        pl.BlockSpec((tm, HEAD_PAD), lambda b, i: (i, 0)),
        pl.BlockSpec((HEAD_PAD, tm), lambda b, i: (0, i)),
        pl.BlockSpec((HEAD_PAD, tm), lambda b, i: (0, i)),
    ]
    out_shape = [
        jax.ShapeDtypeStruct((B, N_HEADS * HEAD_PAD, L), bf16),
        jax.ShapeDtypeStruct((B, L, N_HEADS * HEAD_PAD), bf16),
        jax.ShapeDtypeStruct((B, N_HEADS * V_HEAD, L), bf16),
        jax.ShapeDtypeStruct((B, L, D_SSM), bf16),
        jax.ShapeDtypeStruct((B, L, D_XBC), bf16),
        jax.ShapeDtypeStruct((B, L, DT_PAD), f32),
        jax.ShapeDtypeStruct((B, 2 * SSM_HEADS, L), f32),
    ]
    out_specs = [tokT(N_HEADS * HEAD_PAD), tok(N_HEADS * HEAD_PAD), tokT(N_HEADS * V_HEAD),
                 tok(D_SSM), tok(D_XBC), tok(DT_PAD), tokT(2 * SSM_HEADS)]
    return pl.pallas_call(
        _proj_kernel, grid=grid, in_specs=in_specs, out_specs=out_specs, out_shape=out_shape,
        name="proj",
        compiler_params=pltpu.CompilerParams(
            dimension_semantics=("parallel", "parallel"), vmem_limit_bytes=VMEM_LIMIT),
    )(x, n1, win, qan, kvan, wqT, wk, wvT, wdtT, c_tab, s_tab, cT_tab, sT_tab)


def _attn_kernel(qT_ref, k_ref, vT_ref, o_ref, *, tk, nk):
    q = qT_ref[...]
    tq = q.shape[1]

    def body(j, carry):
        m, l, acc = carry
        start = pl.multiple_of(j * tk, tk)
        kt = k_ref[pl.ds(start, tk), :]
        sT = _dot(kt, q)
        m_new = jnp.maximum(m, jnp.max(sT, axis=0, keepdims=True))
        alpha = jnp.exp(m - m_new)
        p = jnp.exp(sT - m_new)
        l = alpha * l + jnp.sum(p, axis=0, keepdims=True)
        v = vT_ref[:, pl.ds(start, tk)]
        acc = alpha * acc + _dot(v, p.astype(bf16))
        return m_new, l, acc

    m0 = jnp.full((1, tq), -jnp.inf, f32)
    l0 = jnp.zeros((1, tq), f32)
    a0 = jnp.zeros((V_HEAD, tq), f32)
    _, l, acc = lax.fori_loop(0, nk, body, (m0, l0, a0))
    o_ref[...] = (acc / l).astype(o_ref.dtype)


def _attn_call(qT, k, vT):
    B, _, L = qT.shape
    tq = min(ATTN_TQ, L)
    tk = min(ATTN_TK, L)
    grid = (B, N_HEADS, L // tq)
    return pl.pallas_call(
        functools.partial(_attn_kernel, tk=tk, nk=L // tk),
        grid=grid,
        in_specs=[
            pl.BlockSpec((None, HEAD_PAD, tq), lambda b, h, i: (b, h, i)),
            pl.BlockSpec((None, L, HEAD_PAD), lambda b, h, i: (b, 0, h)),
            pl.BlockSpec((None, V_HEAD, L), lambda b, h, i: (b, h, 0)),
        ],
        out_specs=pl.BlockSpec((None, V_HEAD, tq), lambda b, h, i: (b, h, i)),
        out_shape=jax.ShapeDtypeStruct((B, N_HEADS * V_HEAD, L), bf16),
        name="attn",
        compiler_params=pltpu.CompilerParams(
            dimension_semantics=("parallel", "parallel", "arbitrary"), vmem_limit_bytes=VMEM_LIMIT),
    )(qT, k, vT)


def _split_hi_lo(v):
    hi = v.astype(bf16)
    lo = (v - hi.astype(f32)).astype(bf16)
    return hi, lo


def _expand_heads(v, e_ref):
    hi, lo = _split_hi_lo(v)
    return _dot(hi, e_ref[...]) + _dot(lo, e_ref[...])


def _ssd_kernel(xbc_ref, xp_ref, xn_ref, dt_ref, dtT_ref, z_ref,
                cw_ref, cb_ref, bias_ref, biasT_ref, a_ref, aT_ref, dskip_ref, norm_ref,
                tri_ref, e_ref, o_ref, hf_ref, hb_ref, hbs_ref, *, nc):
    ph = pl.program_id(1)
    c = pl.program_id(2)
    cidx = jnp.where(ph == 0, nc - 1 - c, c)
    Q = xbc_ref.shape[0]
    GN = SSM_GROUPS * D_STATE

    xm = xbc_ref[...].astype(f32)
    prev_row = jnp.where(cidx > 0, xp_ref[BF16_ROWS - 1:BF16_ROWS, :].astype(f32), 0.0)
    next_row = jnp.where(cidx < nc - 1, xn_ref[0:1, :].astype(f32), 0.0)
    rows = lax.broadcasted_iota(jnp.int32, (Q, 1), 0)
    x_prev = jnp.where(rows == 0, prev_row, pltpu.roll(xm, 1, axis=0))
    x_next = jnp.where(rows == Q - 1, next_row, pltpu.roll(xm, Q - 1, axis=0))
    cw = cw_ref[...]
    act = _silu(x_prev * cw[0:1, :] + xm * cw[1:2, :] + x_next * cw[2:3, :] + cb_ref[...])
    xs = act[:, :D_SSM]
    xs_b = xs.astype(bf16)
    Bm = act[:, D_SSM:D_SSM + GN].astype(bf16)
    Cm = act[:, D_SSM + GN:].astype(bf16)

    nh2 = 2 * SSM_HEADS
    dt = jax.nn.softplus(dt_ref[:, :nh2] + bias_ref[...])
    dtT = jax.nn.softplus(dtT_ref[...] + biasT_ref[...])
    a = dt * a_ref[...]
    aT = dtT * aT_ref[...]
    tri = tri_ref[...]
    cs = jnp.dot(tri, a, preferred_element_type=f32, precision=lax.Precision.HIGHEST)
    csT = lax.dot_general(aT, tri, NT_DIMS, preferred_element_type=f32,
                          precision=lax.Precision.HIGHEST)
    tot = cs[Q - 1:Q, :]
    cs_f = cs[:, :SSM_HEADS]
    cb_b = cs[:, SSM_HEADS:] - a[:, SSM_HEADS:]
    csT_f = csT[:SSM_HEADS, :]
    cbT_b = csT[SSM_HEADS:, :] - aT[SSM_HEADS:, :]
    dt_f, dt_b = dt[:, :SSM_HEADS], dt[:, SSM_HEADS:]
    dtT_f, dtT_b = dtT[:SSM_HEADS, :], dtT[SSM_HEADS:, :]
    eb = _expand_heads(jnp.exp(tot[:, SSM_HEADS:] - cb_b), e_ref)

    @pl.when(ph == 0)
    def _backward_states():
        @pl.when(c == 0)
        def _():
            hb_ref[...] = jnp.zeros_like(hb_ref)

        hbs_ref[cidx] = hb_ref[...].astype(bf16)
        xw = (xs * _expand_heads(jnp.exp(cb_b) * dt_b, e_ref)).astype(bf16)
        for g in range(SSM_GROUPS):
            sl = slice(g * GROUP_WIDTH, (g + 1) * GROUP_WIDTH)
            upd = lax.dot_general(Bm[:, g * D_STATE:(g + 1) * D_STATE], xw[:, sl], TN_DIMS,
                                  preferred_element_type=f32)
            hb_ref[g] = eb[0:1, sl] * hb_ref[g] + upd

    @pl.when(ph == 1)
    def _outputs():
        @pl.when(c == 0)
        def _():
            hf_ref[...] = jnp.zeros_like(hf_ref)

        ef = _expand_heads(jnp.exp(cs_f), e_ref)
        ti = lax.broadcasted_iota(jnp.int32, (Q, Q), 0)
        si = lax.broadcasted_iota(jnp.int32, (Q, Q), 1)
        lower = ti >= si
        upper = si >= ti
        lane = lax.broadcasted_iota(jnp.int32, (1, LANES), 1)
        first_half = lane < SSM_HEAD_DIM

        y_groups = []
        for g in range(SSM_GROUPS):
            sl = slice(g * GROUP_WIDTH, (g + 1) * GROUP_WIDTH)
            Cg = Cm[:, g * D_STATE:(g + 1) * D_STATE]
            Bg = Bm[:, g * D_STATE:(g + 1) * D_STATE]
            cbm = lax.dot_general(Cg, Bg, NT_DIMS, preferred_element_type=f32)
            y_off = (ef[:, sl] * _dot(Cg, hf_ref[g].astype(bf16))
                     + eb[:, sl] * _dot(Cg, hbs_ref[c, g]))
            pairs = []
            for hp in range(HEADS_PER_GROUP // 2):
                x_pair = xs_b[:, g * GROUP_WIDTH + hp * LANES: g * GROUP_WIDTH + (hp + 1) * LANES]
                y_pair = None
                for k in range(2):
                    hh = g * HEADS_PER_GROUP + 2 * hp + k
                    seg_f = cs_f[:, hh:hh + 1] - csT_f[hh:hh + 1, :]
                    seg_b = cbT_b[hh:hh + 1, :] - cb_b[:, hh:hh + 1]
                    dec = (jnp.exp(jnp.where(lower, seg_f, -jnp.inf)) * dtT_f[hh:hh + 1, :]
                           + jnp.exp(jnp.where(upper, seg_b, -jnp.inf)) * dtT_b[hh:hh + 1, :])
                    mat = (cbm * dec).astype(bf16)
                    keep = first_half if k == 0 else jnp.logical_not(first_half)
                    xk = jnp.where(keep, x_pair, jnp.zeros_like(x_pair))
                    contrib = _dot(mat, xk)
                    y_pair = contrib if y_pair is None else y_pair + contrib
                pairs.append(y_pair)
            y_groups.append(jnp.concatenate(pairs, axis=1) + y_off)
            xw = (xs[:, sl] * _expand_heads(jnp.exp(tot[:, :SSM_HEADS] - cs_f) * dt_f, e_ref)[:, sl]).astype(bf16)
            upd = lax.dot_general(Bg, xw, TN_DIMS, preferred_element_type=f32)
            hf_ref[g] = ef[Q - 1:Q, sl] * hf_ref[g] + upd

        y = jnp.concatenate(y_groups, axis=1) + xs * dskip_ref[...]
        y = y * _silu(z_ref[...].astype(f32))
        nw = norm_ref[...]
        for g in range(SSM_GROUPS):
            sl = slice(g * GROUP_WIDTH, (g + 1) * GROUP_WIDTH)
            o_ref[:, sl] = _rms(y[:, sl], nw[:, sl]).astype(o_ref.dtype)


def _ssd_call(xbc, dt, dtT, z, cw, cb, bias, biasT, a, aT, dskip, norm, tri, expand):
    B, L, _ = xbc.shape
    Q = min(SSD_Q, L)
    nc = L // Q
    hb = Q // BF16_ROWS
    nhalo = L // BF16_ROWS

    def cidx(p, c):
        return jnp.where(p == 0, nc - 1 - c, c)

    in_specs = [
        pl.BlockSpec((None, Q, D_XBC), lambda b, p, c: (b, cidx(p, c), 0)),
        pl.BlockSpec((None, BF16_ROWS, D_XBC), lambda b, p, c: (b, jnp.maximum(cidx(p, c) * hb - 1, 0), 0)),
        pl.BlockSpec((None, BF16_ROWS, D_XBC),
                     lambda b, p, c: (b, jnp.minimum((cidx(p, c) + 1) * hb, nhalo - 1), 0)),
        pl.BlockSpec((None, Q, DT_PAD), lambda b, p, c: (b, cidx(p, c), 0)),
        pl.BlockSpec((None, 2 * SSM_HEADS, Q), lambda b, p, c: (b, 0, cidx(p, c))),
        pl.BlockSpec((None, Q, D_SSM), lambda b, p, c: (b, p * c, 0)),
        _const_spec(cw.shape), _const_spec(cb.shape), _const_spec(bias.shape), _const_spec(biasT.shape),
        _const_spec(a.shape), _const_spec(aT.shape), _const_spec(dskip.shape), _const_spec(norm.shape),
        _const_spec(tri.shape), _const_spec(expand.shape),
    ]
    state = (SSM_GROUPS, D_STATE, GROUP_WIDTH)
    return pl.pallas_call(
        functools.partial(_ssd_kernel, nc=nc),
        grid=(B, 2, nc),
        in_specs=in_specs,
        out_specs=pl.BlockSpec((None, Q, D_SSM), lambda b, p, c: (b, p * c, 0)),
        out_shape=jax.ShapeDtypeStruct((B, L, D_SSM), bf16),
        scratch_shapes=[pltpu.VMEM(state, f32), pltpu.VMEM(state, f32),
                        pltpu.VMEM((nc,) + state, bf16)],
        name="ssd",
        compiler_params=pltpu.CompilerParams(
            dimension_semantics=("parallel", "arbitrary", "arbitrary"), vmem_limit_bytes=VMEM_LIMIT),
    )(xbc, xbc, xbc, dt, dtT, z, cw, cb, bias, biasT, a, aT, dskip, norm, tri, expand)


def _oproj_kernel(x_ref, aT_ref, s_ref, wa_ref, ws_ref, o_ref):
    o_ref[...] = (x_ref[...]
                  + lax.dot_general(aT_ref[...], wa_ref[...], TN_DIMS, preferred_element_type=f32)
                  + _dot(s_ref[...], ws_ref[...]))


def _oproj_call(x, attnT, ssm, wa, ws):
    B, L, _ = x.shape
    tm = min(OPROJ_TM, L)
    tok = lambda w: pl.BlockSpec((None, tm, w), lambda b, i: (b, i, 0))
    return pl.pallas_call(
        _oproj_kernel, grid=(B, L // tm),
        in_specs=[tok(D_MODEL), pl.BlockSpec((None, N_HEADS * V_HEAD, tm), lambda b, i: (b, 0, i)),
                  tok(D_SSM), _const_spec(wa.shape), _const_spec(ws.shape)],
        out_specs=tok(D_MODEL),
        out_shape=jax.ShapeDtypeStruct((B, L, D_MODEL), f32),
        name="oproj",
        compiler_params=pltpu.CompilerParams(
            dimension_semantics=("parallel", "parallel"), vmem_limit_bytes=VMEM_LIMIT),
    )(x, attnT, ssm, wa, ws)


def _ffn_kernel(x_ref, xp_ref, xn_ref, n2_ref, wg_ref, wu_ref, cw_ref, cb_ref, wd_ref, fn_ref,
                o_ref, h_scr, acc_scr, *, nfc):
    i = pl.program_id(1)
    nt = pl.num_programs(1)
    tm = x_ref.shape[0]
    n2 = n2_ref[...]
    x = x_ref[...]
    h_scr[0:tm, :] = _rms(x, n2).astype(bf16)
    halo = jnp.concatenate([xp_ref[...], xn_ref[...]], axis=0)
    h_scr[tm:tm + 2 * F32_ROWS, :] = _rms(halo, n2).astype(bf16)
    acc_scr[...] = x
    rows = lax.broadcasted_iota(jnp.int32, (tm, 1), 0)
    has_prev = i > 0
    has_next = i < nt - 1

    def chunk(cf, carry):
        col = pl.multiple_of(cf * FFN_FC, FFN_FC)
        wg = wg_ref[:, pl.ds(col, FFN_FC)]
        g_ext = _dot(h_scr[...], wg)
        g = g_ext[:tm]
        g_before = jnp.where(has_prev, g_ext[tm + F32_ROWS - 1:tm + F32_ROWS], 0.0)
        g_after = jnp.where(has_next, g_ext[tm + F32_ROWS:tm + F32_ROWS + 1], 0.0)
        g_prev = jnp.where(rows == 0, g_before, pltpu.roll(g, 1, axis=0))
        g_next = jnp.where(rows == tm - 1, g_after, pltpu.roll(g, tm - 1, axis=0))
        cw = cw_ref[:, pl.ds(col, FFN_FC)]
        gc = g_prev * cw[0:1] + g * cw[1:2] + g_next * cw[2:3] + cb_ref[:, pl.ds(col, FFN_FC)]
        u = _dot(h_scr[0:tm, :], wu_ref[:, pl.ds(col, FFN_FC)])
        act = (_silu(gc) * u).astype(bf16)
        acc_scr[...] += _dot(act, wd_ref[pl.ds(col, FFN_FC), :])
        return carry

    lax.fori_loop(0, nfc, chunk, 0)
    o_ref[...] = _rms(acc_scr[...], fn_ref[...])


def _ffn_call(x, n2, wg, wu, cw, cb, wd, fn):
    B, L, _ = x.shape
    tm = min(FFN_TM, L)
    hb = tm // F32_ROWS
    nhalo = L // F32_ROWS
    tok = pl.BlockSpec((None, tm, D_MODEL), lambda b, i: (b, i, 0))
    return pl.pallas_call(
        functools.partial(_ffn_kernel, nfc=D_FF // FFN_FC),
        grid=(B, L // tm),
        in_specs=[
            tok,
            pl.BlockSpec((None, F32_ROWS, D_MODEL), lambda b, i: (b, jnp.maximum(i * hb - 1, 0), 0)),
            pl.BlockSpec((None, F32_ROWS, D_MODEL), lambda b, i: (b, jnp.minimum((i + 1) * hb, nhalo - 1), 0)),
            _const_spec(n2.shape), _const_spec(wg.shape), _const_spec(wu.shape), _const_spec(cw.shape),
            _const_spec(cb.shape), _const_spec(wd.shape), _const_spec(fn.shape),
        ],
        out_specs=tok,
        out_shape=jax.ShapeDtypeStruct((B, L, D_MODEL), f32),
        scratch_shapes=[pltpu.VMEM((tm + 2 * F32_ROWS, D_MODEL), bf16), pltpu.VMEM((tm, D_MODEL), f32)],
        name="ffn",
        compiler_params=pltpu.CompilerParams(
            dimension_semantics=("parallel", "parallel"), vmem_limit_bytes=VMEM_LIMIT),
    )(x, x, x, n2, wg, wu, cw, cb, wd, fn)


def _head_lane_sources():
    src = np.full((HEAD_PAD,), QK_DIM, np.int32)
    half = HEAD_PAD // 2
    src[0:HALF_ROPE] = QK_NOPE + np.arange(HALF_ROPE)
    src[HALF_ROPE:half] = np.arange(half - HALF_ROPE)
    src[half:half + HALF_ROPE] = QK_NOPE + HALF_ROPE + np.arange(HALF_ROPE)
    n_rest = QK_NOPE - (half - HALF_ROPE)
    src[half + HALF_ROPE:half + HALF_ROPE + n_rest] = (half - HALF_ROPE) + np.arange(n_rest)
    return src


def _rope_tables(L):
    inv = ROPE_THETA ** (-jnp.arange(0, QK_ROPE, 2, dtype=f32) / QK_ROPE)
    ang = jnp.arange(L, dtype=f32)[:, None] * inv[None, :]
    cos, sin = jnp.cos(ang), jnp.sin(ang)
    half = HEAD_PAD // 2
    c_tab = jnp.ones((L, HEAD_PAD), f32)
    c_tab = c_tab.at[:, 0:HALF_ROPE].set(cos).at[:, half:half + HALF_ROPE].set(cos)
    s_tab = jnp.zeros((L, HEAD_PAD), f32)
    s_tab = s_tab.at[:, 0:HALF_ROPE].set(-sin).at[:, half:half + HALF_ROPE].set(sin)
    return c_tab, s_tab


def _prepare_weights(norm1, w_in, q_a_norm, kv_a_norm, w_q_b, w_kv_b, conv_w, conv_b,
                     dt_bias_f, dt_bias_b, a_log_f, a_log_b, d_skip, ssm_norm, w_out,
                     norm2, w_gate, w_up, ffn_conv_w, ffn_conv_b, w_down, final_norm):
    half = HEAD_PAD // 2
    o_kr = Q_LORA + KV_LORA
    o_z = o_kr + QK_ROPE
    o_xbc = o_z + D_SSM
    o_dt = o_xbc + D_XBC
    kr_blk = jnp.zeros((D_MODEL, HEAD_PAD), f32)
    kr_blk = kr_blk.at[:, 0:HALF_ROPE].set(w_in[:, o_kr:o_kr + HALF_ROPE])
    kr_blk = kr_blk.at[:, half:half + HALF_ROPE].set(w_in[:, o_kr + HALF_ROPE:o_kr + QK_ROPE])
    w_dt = w_in[:, o_dt:o_dt + 2 * SSM_HEADS]
    dt_blk = jnp.zeros((D_MODEL, DT_PAD), f32).at[:, :2 * SSM_HEADS].set(w_dt)
    win = jnp.concatenate([w_in[:, :o_kr], kr_blk, w_in[:, o_z:o_dt], dt_blk], axis=1).astype(bf16)

    src = _head_lane_sources()
    wq = w_q_b.reshape(Q_LORA, N_HEADS, QK_DIM)
    wq = jnp.concatenate([wq, jnp.zeros((Q_LORA, N_HEADS, 1), f32)], axis=-1)[:, :, src]
    wqT = wq.reshape(Q_LORA, N_HEADS * HEAD_PAD).T.astype(bf16)
    wkv = w_kv_b.reshape(KV_LORA, N_HEADS, QK_NOPE + V_HEAD)
    src_k = np.where(src < QK_NOPE, src, QK_NOPE)
    wk = jnp.concatenate([wkv[:, :, :QK_NOPE], jnp.zeros((KV_LORA, N_HEADS, 1), f32)], axis=-1)[:, :, src_k]
    wk = wk.reshape(KV_LORA, N_HEADS * HEAD_PAD).astype(bf16)
    wvT = wkv[:, :, QK_NOPE:].reshape(KV_LORA, N_HEADS * V_HEAD).T.astype(bf16)

    row = lambda v: v.reshape(1, -1).astype(f32)
    bias = jnp.concatenate([dt_bias_f, dt_bias_b]).astype(f32)
    a_neg = -jnp.exp(jnp.concatenate([a_log_f, a_log_b]).astype(f32))
    expand = jnp.repeat(jnp.eye(SSM_HEADS, dtype=f32), SSM_HEAD_DIM, axis=1).astype(bf16)
    return dict(
        n1=row(norm1), win=win, qan=row(q_a_norm), kvan=row(kv_a_norm), wqT=wqT, wk=wk, wvT=wvT,
        wdtT=w_dt.T.astype(bf16),
        cw=conv_w.astype(f32), cb=row(conv_b), bias=row(bias), biasT=bias.reshape(-1, 1),
        a=row(a_neg), aT=a_neg.reshape(-1, 1),
        dskip=row(jnp.repeat(d_skip.astype(f32), SSM_HEAD_DIM)), ssm_norm=row(ssm_norm), expand=expand,
        wa=w_out[:N_HEADS * V_HEAD].astype(bf16), ws=w_out[N_HEADS * V_HEAD:].astype(bf16),
        n2=row(norm2), wg=w_gate.astype(bf16), wu=w_up.astype(bf16), fcw=ffn_conv_w.astype(f32),
        fcb=row(ffn_conv_b), wd=w_down.astype(bf16), fn=row(final_norm),
    )


def _encoder(x, w):
    B, L, _ = x.shape
    c_tab, s_tab = _rope_tables(L)
    scale = QK_DIM ** -0.5
    qT, k, vT, z, xbc, dt, dtT = _proj_call(
        x, w["n1"], w["win"], w["qan"], w["kvan"], w["wqT"], w["wk"], w["wvT"], w["wdtT"],
        c_tab, s_tab, (c_tab * scale).T, (s_tab * scale).T)
    attnT = _attn_call(qT, k, vT)
    Q = min(SSD_Q, L)
    tri = jnp.tril(jnp.ones((Q, Q), f32))
    ssm = _ssd_call(xbc, dt, dtT, z, w["cw"], w["cb"], w["bias"], w["biasT"], w["a"], w["aT"],
                    w["dskip"], w["ssm_norm"], tri, w["expand"])
    x1 = _oproj_call(x, attnT, ssm, w["wa"], w["ws"])
    return _ffn_call(x1, w["n2"], w["wg"], w["wu"], w["fcw"], w["fcb"], w["wd"], w["fn"])


def kernel(x_prompt, x_sample, norm1, w_in, q_a_norm, kv_a_norm, w_q_b, w_kv_b, conv_w, conv_b,
           dt_bias_f, dt_bias_b, a_log_f, a_log_b, d_skip, ssm_norm, w_out, norm2, w_gate, w_up,
           ffn_conv_w, ffn_conv_b, w_down, final_norm):
    w = _prepare_weights(norm1[0], w_in[0], q_a_norm[0], kv_a_norm[0], w_q_b[0], w_kv_b[0], conv_w[0],
                         conv_b[0], dt_bias_f[0], dt_bias_b[0], a_log_f[0], a_log_b[0], d_skip[0],
                         ssm_norm[0], w_out[0], norm2[0], w_gate[0], w_up[0], ffn_conv_w[0],
                         ffn_conv_b[0], w_down[0], final_norm)
    return (_encoder(x_prompt, w), _encoder(x_sample, w))
```

```python
import functools

import numpy as np
import jax
import jax.numpy as jnp
from jax import lax
from jax.experimental import pallas as pl
from jax.experimental.pallas import tpu as pltpu

D_MODEL = 1024
N_HEADS = 16
QK_NOPE = 64
QK_ROPE = 32
HALF_ROPE = QK_ROPE // 2
QK_DIM = QK_NOPE + QK_ROPE
V_HEAD = 64
Q_LORA = 384
KV_LORA = 256
ROPE_THETA = 10000.0
SSM_HEADS = 16
SSM_HEAD_DIM = 64
D_SSM = SSM_HEADS * SSM_HEAD_DIM
SSM_GROUPS = 2
HEADS_PER_GROUP = SSM_HEADS // SSM_GROUPS
GROUP_WIDTH = D_SSM // SSM_GROUPS
D_STATE = 64
D_XBC = D_SSM + 2 * SSM_GROUPS * D_STATE
D_FF = 2816
EPS = 1e-6

LANES = 128
HEAD_PAD = LANES
BF16_ROWS = 16
F32_ROWS = 8
DT_PAD = LANES
VMEM_LIMIT = 56 * 1024 * 1024

OFF_Q = 0
OFF_CKV = OFF_Q + Q_LORA
OFF_KR = OFF_CKV + KV_LORA
OFF_Z = OFF_KR + HEAD_PAD
OFF_XBC = OFF_Z + D_SSM
OFF_DT = OFF_XBC + D_XBC
D_IN_PAD = OFF_DT + DT_PAD

PROJ_TM = 256
ATTN_TQ = 512
ATTN_TK = 512
SSD_Q = 256
OPROJ_TM = 512
FFN_TM = 512
FFN_FC = 256

NT_DIMS = (((1,), (1,)), ((), ()))
TN_DIMS = (((0,), (0,)), ((), ()))

f32 = jnp.float32
bf16 = jnp.bfloat16


def _rms(x, w):
    return x * lax.rsqrt(jnp.mean(x * x, axis=-1, keepdims=True) + EPS) * w


def _dot(a, b):
    return jnp.dot(a, b, preferred_element_type=f32)


def _silu(x):
    return x / (1.0 + jnp.exp(-x))


def _proj_kernel(x_ref, n1_ref, win_ref, qan_ref, kvan_ref, wqT_ref, wk_ref, wvT_ref, wdtT_ref,
                 c_ref, s_ref, cT_ref, sT_ref,
                 qT_out, k_out, vT_out, z_out, xbc_out, dt_out, dtT_out):
    h = _rms(x_ref[...], n1_ref[...]).astype(bf16)
    proj = _dot(h, win_ref[...])
    z_out[...] = proj[:, OFF_Z:OFF_Z + D_SSM].astype(bf16)
    xbc_out[...] = proj[:, OFF_XBC:OFF_XBC + D_XBC].astype(bf16)
    dt_out[...] = proj[:, OFF_DT:OFF_DT + DT_PAD]
    dtT_out[...] = lax.dot_general(wdtT_ref[...], h, NT_DIMS, preferred_element_type=f32)

    hq = _rms(proj[:, OFF_Q:OFF_Q + Q_LORA], qan_ref[...]).astype(bf16)
    qT = lax.dot_general(wqT_ref[...], hq, NT_DIMS, preferred_element_type=f32)
    cT = cT_ref[...]
    sT = sT_ref[...]
    half = HEAD_PAD // 2
    for hh in range(N_HEADS):
        blk = qT[hh * HEAD_PAD:(hh + 1) * HEAD_PAD, :]
        rot = jnp.concatenate([blk[half:], blk[:half]], axis=0)
        qT_out[hh * HEAD_PAD:(hh + 1) * HEAD_PAD, :] = (blk * cT + rot * sT).astype(bf16)

    hc = _rms(proj[:, OFF_CKV:OFF_CKV + KV_LORA], kvan_ref[...]).astype(bf16)
    kn = _dot(hc, wk_ref[...])
    kr = proj[:, OFF_KR:OFF_KR + HEAD_PAD]
    krf = kr * c_ref[...] + pltpu.roll(kr, half, axis=1) * s_ref[...]
    for hh in range(N_HEADS):
        k_out[:, hh * HEAD_PAD:(hh + 1) * HEAD_PAD] = (
            kn[:, hh * HEAD_PAD:(hh + 1) * HEAD_PAD] + krf).astype(bf16)
    vT_out[...] = lax.dot_general(wvT_ref[...], hc, NT_DIMS, preferred_element_type=f32).astype(bf16)


def _const_spec(shape):
    nd = len(shape)
    return pl.BlockSpec(shape, lambda *_: (0,) * nd)


def _proj_call(x, n1, win, qan, kvan, wqT, wk, wvT, wdtT, c_tab, s_tab, cT_tab, sT_tab):
    B, L, _ = x.shape
    tm = min(PROJ_TM, L)
    grid = (B, L // tm)
    tok = lambda w: pl.BlockSpec((None, tm, w), lambda b, i: (b, i, 0))
    tokT = lambda w: pl.BlockSpec((None, w, tm), lambda b, i: (b, 0, i))
    in_specs = [
        tok(D_MODEL), _const_spec(n1.shape), _const_spec(win.shape), _const_spec(qan.shape),
        _const_spec(kvan.shape), _const_spec(wqT.shape), _const_spec(wk.shape), _const_spec(wvT.shape),
        _const_spec(wdtT.shape),
        pl.BlockSpec((tm, HEAD_PAD), lambda b, i: (i, 0)),
        pl.BlockSpec((tm, HEAD_PAD), lambda b, i: (i, 0)),
        pl.BlockSpec((HEAD_PAD, tm), lambda b, i: (0, i)),
        pl.BlockSpec((HEAD_PAD, tm), lambda b, i: (0, i)),
    ]
    out_shape = [
        jax.ShapeDtypeStruct((B, N_HEADS * HEAD_PAD, L), bf16),
        jax.ShapeDtypeStruct((B, L, N_HEADS * HEAD_PAD), bf16),
        jax.ShapeDtypeStruct((B, N_HEADS * V_HEAD, L), bf16),
        jax.ShapeDtypeStruct((B, L, D_SSM), bf16),
        jax.ShapeDtypeStruct((B, L, D_XBC), bf16),
        jax.ShapeDtypeStruct((B, L, DT_PAD), f32),
        jax.ShapeDtypeStruct((B, 2 * SSM_HEADS, L), f32),
    ]
    out_specs = [tokT(N_HEADS * HEAD_PAD), tok(N_HEADS * HEAD_PAD), tokT(N_HEADS * V_HEAD),
                 tok(D_SSM), tok(D_XBC), tok(DT_PAD), tokT(2 * SSM_HEADS)]
    return pl.pallas_call(
        _proj_kernel, grid=grid, in_specs=in_specs, out_specs=out_specs, out_shape=out_shape,
        name="proj",
        compiler_params=pltpu.CompilerParams(
            dimension_semantics=("parallel", "parallel"), vmem_limit_bytes=VMEM_LIMIT),
    )(x, n1, win, qan, kvan, wqT, wk, wvT, wdtT, c_tab, s_tab, cT_tab, sT_tab)


def _attn_kernel(qT_ref, k_ref, vT_ref, o_ref, sa_ref, sb_ref, m_ref, acc_ref, *, tk, nk):
    q = qT_ref[...]
    ones = jnp.ones((BF16_ROWS, tk), bf16)

    def scores(j, dst):
        start = pl.multiple_of(j * tk, tk)
        dst[...] = _dot(k_ref[pl.ds(start, tk), :], q)

    def update(j, src):
        s = src[...]
        m = m_ref[...]
        m_new = jnp.maximum(m, jnp.max(s, axis=0, keepdims=True))
        alpha = jnp.exp2(m - m_new)
        p = jnp.exp2(s - m_new).astype(bf16)
        start = pl.multiple_of(j * tk, tk)
        v = jnp.concatenate([vT_ref[:, pl.ds(start, tk)], ones], axis=0)
        acc_ref[...] = alpha * acc_ref[...] + _dot(v, p)
        m_ref[...] = m_new

    m_ref[...] = jnp.full(m_ref.shape, -jnp.inf, f32)
    acc_ref[...] = jnp.zeros(acc_ref.shape, f32)
    scores(0, sa_ref)

    def body(i, carry):
        j = 2 * i
        scores(j + 1, sb_ref)
        update(j, sa_ref)
        scores(j + 2, sa_ref)
        update(j + 1, sb_ref)
        return carry

    lax.fori_loop(0, nk // 2 - 1, body, 0)
    scores(nk - 1, sb_ref)
    update(nk - 2, sa_ref)
    update(nk - 1, sb_ref)
    acc = acc_ref[...]
    o_ref[...] = (acc[:V_HEAD] / acc[V_HEAD:V_HEAD + 1]).astype(o_ref.dtype)


def _attn_call(qT, k, vT):
    B, _, L = qT.shape
    tq = min(ATTN_TQ, L)
    tk = min(ATTN_TK, L)
    grid = (B, N_HEADS, L // tq)
    nk = L // tk
    assert nk >= 2 and nk % 2 == 0, "key tiles are processed in pairs"
    return pl.pallas_call(
        functools.partial(_attn_kernel, tk=tk, nk=nk),
        grid=grid,
        scratch_shapes=[pltpu.VMEM((tk, tq), f32), pltpu.VMEM((tk, tq), f32),
                        pltpu.VMEM((1, tq), f32), pltpu.VMEM((V_HEAD + BF16_ROWS, tq), f32)],
        in_specs=[
            pl.BlockSpec((None, HEAD_PAD, tq), lambda b, h, i: (b, h, i)),
            pl.BlockSpec((None, L, HEAD_PAD), lambda b, h, i: (b, 0, h)),
            pl.BlockSpec((None, V_HEAD, L), lambda b, h, i: (b, h, 0)),
        ],
        out_specs=pl.BlockSpec((None, V_HEAD, tq), lambda b, h, i: (b, h, i)),
        out_shape=jax.ShapeDtypeStruct((B, N_HEADS * V_HEAD, L), bf16),
        name="attn",
        compiler_params=pltpu.CompilerParams(
            dimension_semantics=("parallel", "parallel", "arbitrary"), vmem_limit_bytes=VMEM_LIMIT),
    )(qT, k, vT)


def _split_hi_lo(v):
    hi = v.astype(bf16)
    lo = (v - hi.astype(f32)).astype(bf16)
    return hi, lo


def _expand_heads(v, e_ref):
    hi, lo = _split_hi_lo(v)
    return _dot(hi, e_ref[...]) + _dot(lo, e_ref[...])


def _ssd_kernel(xbc_ref, xp_ref, xn_ref, dt_ref, dtT_ref, z_ref,
                cw_ref, cb_ref, bias_ref, biasT_ref, a_ref, aT_ref, dskip_ref, norm_ref,
                tri_ref, e_ref, o_ref, hf_ref, hb_ref, hbs_ref, *, nc):
    ph = pl.program_id(1)
    c = pl.program_id(2)
    cidx = jnp.where(ph == 0, nc - 1 - c, c)
    Q = xbc_ref.shape[0]
    GN = SSM_GROUPS * D_STATE

    xm = xbc_ref[...].astype(f32)
    prev_row = jnp.where(cidx > 0, xp_ref[BF16_ROWS - 1:BF16_ROWS, :].astype(f32), 0.0)
    next_row = jnp.where(cidx < nc - 1, xn_ref[0:1, :].astype(f32), 0.0)
    rows = lax.broadcasted_iota(jnp.int32, (Q, 1), 0)
    x_prev = jnp.where(rows == 0, prev_row, pltpu.roll(xm, 1, axis=0))
    x_next = jnp.where(rows == Q - 1, next_row, pltpu.roll(xm, Q - 1, axis=0))
    cw = cw_ref[...]
    act = _silu(x_prev * cw[0:1, :] + xm * cw[1:2, :] + x_next * cw[2:3, :] + cb_ref[...])
    xs = act[:, :D_SSM]
    xs_b = xs.astype(bf16)
    Bm = act[:, D_SSM:D_SSM + GN].astype(bf16)
    Cm = act[:, D_SSM + GN:].astype(bf16)

    nh2 = 2 * SSM_HEADS
    dt = jax.nn.softplus(dt_ref[:, :nh2] + bias_ref[...])
    dtT = jax.nn.softplus(dtT_ref[...] + biasT_ref[...])
    a = dt * a_ref[...]
    aT = dtT * aT_ref[...]
    tri = tri_ref[...]
    cs = jnp.dot(tri, a, preferred_element_type=f32, precision=lax.Precision.HIGHEST)
    csT = lax.dot_general(aT, tri, NT_DIMS, preferred_element_type=f32,
                          precision=lax.Precision.HIGHEST)
    tot = cs[Q - 1:Q, :]
    cs_f = cs[:, :SSM_HEADS]
    cb_b = cs[:, SSM_HEADS:] - a[:, SSM_HEADS:]
    csT_f = csT[:SSM_HEADS, :]
    cbT_b = csT[SSM_HEADS:, :] - aT[SSM_HEADS:, :]
    dt_f, dt_b = dt[:, :SSM_HEADS], dt[:, SSM_HEADS:]
    dtT_f, dtT_b = dtT[:SSM_HEADS, :], dtT[SSM_HEADS:, :]
    eb = _expand_heads(jnp.exp(tot[:, SSM_HEADS:] - cb_b), e_ref)

    @pl.when(ph == 0)
    def _backward_states():
        @pl.when(c == 0)
        def _():
            hb_ref[...] = jnp.zeros_like(hb_ref)

        hbs_ref[cidx] = hb_ref[...].astype(bf16)
        xw = (xs * _expand_heads(jnp.exp(cb_b) * dt_b, e_ref)).astype(bf16)
        for g in range(SSM_GROUPS):
            sl = slice(g * GROUP_WIDTH, (g + 1) * GROUP_WIDTH)
            upd = lax.dot_general(Bm[:, g * D_STATE:(g + 1) * D_STATE], xw[:, sl], TN_DIMS,
                                  preferred_element_type=f32)
            hb_ref[g] = eb[0:1, sl] * hb_ref[g] + upd

    @pl.when(ph == 1)
    def _outputs():
        @pl.when(c == 0)
        def _():
            hf_ref[...] = jnp.zeros_like(hf_ref)

        ef = _expand_heads(jnp.exp(cs_f), e_ref)
        ti = lax.broadcasted_iota(jnp.int32, (Q, Q), 0)
        si = lax.broadcasted_iota(jnp.int32, (Q, Q), 1)
        lower = ti >= si
        upper = si >= ti
        lane = lax.broadcasted_iota(jnp.int32, (1, LANES), 1)
        first_half = lane < SSM_HEAD_DIM

        y_groups = []
        for g in range(SSM_GROUPS):
            sl = slice(g * GROUP_WIDTH, (g + 1) * GROUP_WIDTH)
            Cg = Cm[:, g * D_STATE:(g + 1) * D_STATE]
            Bg = Bm[:, g * D_STATE:(g + 1) * D_STATE]
            cbm = lax.dot_general(Cg, Bg, NT_DIMS, preferred_element_type=f32)
            y_off = (ef[:, sl] * _dot(Cg, hf_ref[g].astype(bf16))
                     + eb[:, sl] * _dot(Cg, hbs_ref[c, g]))
            pairs = []
            for hp in range(HEADS_PER_GROUP // 2):
                x_pair = xs_b[:, g * GROUP_WIDTH + hp * LANES: g * GROUP_WIDTH + (hp + 1) * LANES]
                y_pair = None
                for k in range(2):
                    hh = g * HEADS_PER_GROUP + 2 * hp + k
                    seg_f = cs_f[:, hh:hh + 1] - csT_f[hh:hh + 1, :]
                    seg_b = cbT_b[hh:hh + 1, :] - cb_b[:, hh:hh + 1]
                    dec = (jnp.exp(jnp.where(lower, seg_f, -jnp.inf)) * dtT_f[hh:hh + 1, :]
                           + jnp.exp(jnp.where(upper, seg_b, -jnp.inf)) * dtT_b[hh:hh + 1, :])
                    mat = (cbm * dec).astype(bf16)
                    keep = first_half if k == 0 else jnp.logical_not(first_half)
                    xk = jnp.where(keep, x_pair, jnp.zeros_like(x_pair))
                    contrib = _dot(mat, xk)
                    y_pair = contrib if y_pair is None else y_pair + contrib
                pairs.append(y_pair)
            y_groups.append(jnp.concatenate(pairs, axis=1) + y_off)
            xw = (xs[:, sl] * _expand_heads(jnp.exp(tot[:, :SSM_HEADS] - cs_f) * dt_f, e_ref)[:, sl]).astype(bf16)
            upd = lax.dot_general(Bg, xw, TN_DIMS, preferred_element_type=f32)
            hf_ref[g] = ef[Q - 1:Q, sl] * hf_ref[g] + upd

        y = jnp.concatenate(y_groups, axis=1) + xs * dskip_ref[...]
        y = y * _silu(z_ref[...].astype(f32))
        nw = norm_ref[...]
        for g in range(SSM_GROUPS):
            sl = slice(g * GROUP_WIDTH, (g + 1) * GROUP_WIDTH)
            o_ref[:, sl] = _rms(y[:, sl], nw[:, sl]).astype(o_ref.dtype)


def _ssd_call(xbc, dt, dtT, z, cw, cb, bias, biasT, a, aT, dskip, norm, tri, expand):
    B, L, _ = xbc.shape
    Q = min(SSD_Q, L)
    nc = L // Q
    hb = Q // BF16_ROWS
    nhalo = L // BF16_ROWS

    def cidx(p, c):
        return jnp.where(p == 0, nc - 1 - c, c)

    in_specs = [
        pl.BlockSpec((None, Q, D_XBC), lambda b, p, c: (b, cidx(p, c), 0)),
        pl.BlockSpec((None, BF16_ROWS, D_XBC), lambda b, p, c: (b, jnp.maximum(cidx(p, c) * hb - 1, 0), 0)),
        pl.BlockSpec((None, BF16_ROWS, D_XBC),
                     lambda b, p, c: (b, jnp.minimum((cidx(p, c) + 1) * hb, nhalo - 1), 0)),
        pl.BlockSpec((None, Q, DT_PAD), lambda b, p, c: (b, cidx(p, c), 0)),
        pl.BlockSpec((None, 2 * SSM_HEADS, Q), lambda b, p, c: (b, 0, cidx(p, c))),
        pl.BlockSpec((None, Q, D_SSM), lambda b, p, c: (b, p * c, 0)),
        _const_spec(cw.shape), _const_spec(cb.shape), _const_spec(bias.shape), _const_spec(biasT.shape),
        _const_spec(a.shape), _const_spec(aT.shape), _const_spec(dskip.shape), _const_spec(norm.shape),
        _const_spec(tri.shape), _const_spec(expand.shape),
    ]
    state = (SSM_GROUPS, D_STATE, GROUP_WIDTH)
    return pl.pallas_call(
        functools.partial(_ssd_kernel, nc=nc),
        grid=(B, 2, nc),
        in_specs=in_specs,
        out_specs=pl.BlockSpec((None, Q, D_SSM), lambda b, p, c: (b, p * c, 0)),
        out_shape=jax.ShapeDtypeStruct((B, L, D_SSM), bf16),
        scratch_shapes=[pltpu.VMEM(state, f32), pltpu.VMEM(state, f32),
                        pltpu.VMEM((nc,) + state, bf16)],
        name="ssd",
        compiler_params=pltpu.CompilerParams(
            dimension_semantics=("parallel", "arbitrary", "arbitrary"), vmem_limit_bytes=VMEM_LIMIT),
    )(xbc, xbc, xbc, dt, dtT, z, cw, cb, bias, biasT, a, aT, dskip, norm, tri, expand)


def _oproj_kernel(x_ref, aT_ref, s_ref, wa_ref, ws_ref, o_ref):
    o_ref[...] = (x_ref[...]
                  + lax.dot_general(aT_ref[...], wa_ref[...], TN_DIMS, preferred_element_type=f32)
                  + _dot(s_ref[...], ws_ref[...]))


def _oproj_call(x, attnT, ssm, wa, ws):
    B, L, _ = x.shape
    tm = min(OPROJ_TM, L)
    tok = lambda w: pl.BlockSpec((None, tm, w), lambda b, i: (b, i, 0))
    return pl.pallas_call(
        _oproj_kernel, grid=(B, L // tm),
        in_specs=[tok(D_MODEL), pl.BlockSpec((None, N_HEADS * V_HEAD, tm), lambda b, i: (b, 0, i)),
                  tok(D_SSM), _const_spec(wa.shape), _const_spec(ws.shape)],
        out_specs=tok(D_MODEL),
        out_shape=jax.ShapeDtypeStruct((B, L, D_MODEL), f32),
        name="oproj",
        compiler_params=pltpu.CompilerParams(
            dimension_semantics=("parallel", "parallel"), vmem_limit_bytes=VMEM_LIMIT),
    )(x, attnT, ssm, wa, ws)


def _ffn_kernel(x_ref, xp_ref, xn_ref, n2_ref, wg_ref, wu_ref, cw_ref, cb_ref, wd_ref, fn_ref,
                o_ref, h_scr, acc_scr, *, nfc):
    i = pl.program_id(1)
    nt = pl.num_programs(1)
    tm = x_ref.shape[0]
    n2 = n2_ref[...]
    x = x_ref[...]
    h_scr[0:tm, :] = _rms(x, n2).astype(bf16)
    halo = jnp.concatenate([xp_ref[...], xn_ref[...]], axis=0)
    h_scr[tm:tm + 2 * F32_ROWS, :] = _rms(halo, n2).astype(bf16)
    acc_scr[...] = x
    rows = lax.broadcasted_iota(jnp.int32, (tm, 1), 0)
    has_prev = i > 0
    has_next = i < nt - 1

    def chunk(cf, carry):
        col = pl.multiple_of(cf * FFN_FC, FFN_FC)
        wg = wg_ref[:, pl.ds(col, FFN_FC)]
        g_ext = _dot(h_scr[...], wg)
        g = g_ext[:tm]
        g_before = jnp.where(has_prev, g_ext[tm + F32_ROWS - 1:tm + F32_ROWS], 0.0)
        g_after = jnp.where(has_next, g_ext[tm + F32_ROWS:tm + F32_ROWS + 1], 0.0)
        g_prev = jnp.where(rows == 0, g_before, pltpu.roll(g, 1, axis=0))
        g_next = jnp.where(rows == tm - 1, g_after, pltpu.roll(g, tm - 1, axis=0))
        cw = cw_ref[:, pl.ds(col, FFN_FC)]
        gc = g_prev * cw[0:1] + g * cw[1:2] + g_next * cw[2:3] + cb_ref[:, pl.ds(col, FFN_FC)]
        u = _dot(h_scr[0:tm, :], wu_ref[:, pl.ds(col, FFN_FC)])
        act = (_silu(gc) * u).astype(bf16)
        acc_scr[...] += _dot(act, wd_ref[pl.ds(col, FFN_FC), :])
        return carry

    lax.fori_loop(0, nfc, chunk, 0)
    o_ref[...] = _rms(acc_scr[...], fn_ref[...])


def _ffn_call(x, n2, wg, wu, cw, cb, wd, fn):
    B, L, _ = x.shape
    tm = min(FFN_TM, L)
    hb = tm // F32_ROWS
    nhalo = L // F32_ROWS
    tok = pl.BlockSpec((None, tm, D_MODEL), lambda b, i: (b, i, 0))
    return pl.pallas_call(
        functools.partial(_ffn_kernel, nfc=D_FF // FFN_FC),
        grid=(B, L // tm),
        in_specs=[
            tok,
            pl.BlockSpec((None, F32_ROWS, D_MODEL), lambda b, i: (b, jnp.maximum(i * hb - 1, 0), 0)),
            pl.BlockSpec((None, F32_ROWS, D_MODEL), lambda b, i: (b, jnp.minimum((i + 1) * hb, nhalo - 1), 0)),
            _const_spec(n2.shape), _const_spec(wg.shape), _const_spec(wu.shape), _const_spec(cw.shape),
            _const_spec(cb.shape), _const_spec(wd.shape), _const_spec(fn.shape),
        ],
        out_specs=tok,
        out_shape=jax.ShapeDtypeStruct((B, L, D_MODEL), f32),
        scratch_shapes=[pltpu.VMEM((tm + 2 * F32_ROWS, D_MODEL), bf16), pltpu.VMEM((tm, D_MODEL), f32)],
        name="ffn",
        compiler_params=pltpu.CompilerParams(
            dimension_semantics=("parallel", "parallel"), vmem_limit_bytes=VMEM_LIMIT),
    )(x, x, x, n2, wg, wu, cw, cb, wd, fn)


def _head_lane_sources():
    src = np.full((HEAD_PAD,), QK_DIM, np.int32)
    half = HEAD_PAD // 2
    src[0:HALF_ROPE] = QK_NOPE + np.arange(HALF_ROPE)
    src[HALF_ROPE:half] = np.arange(half - HALF_ROPE)
    src[half:half + HALF_ROPE] = QK_NOPE + HALF_ROPE + np.arange(HALF_ROPE)
    n_rest = QK_NOPE - (half - HALF_ROPE)
    src[half + HALF_ROPE:half + HALF_ROPE + n_rest] = (half - HALF_ROPE) + np.arange(n_rest)
    return src


def _rope_tables(L):
    inv = ROPE_THETA ** (-jnp.arange(0, QK_ROPE, 2, dtype=f32) / QK_ROPE)
    ang = jnp.arange(L, dtype=f32)[:, None] * inv[None, :]
    cos, sin = jnp.cos(ang), jnp.sin(ang)
    half = HEAD_PAD // 2
    c_tab = jnp.ones((L, HEAD_PAD), f32)
    c_tab = c_tab.at[:, 0:HALF_ROPE].set(cos).at[:, half:half + HALF_ROPE].set(cos)
    s_tab = jnp.zeros((L, HEAD_PAD), f32)
    s_tab = s_tab.at[:, 0:HALF_ROPE].set(-sin).at[:, half:half + HALF_ROPE].set(sin)
    return c_tab, s_tab


def _prepare_weights(norm1, w_in, q_a_norm, kv_a_norm, w_q_b, w_kv_b, conv_w, conv_b,
                     dt_bias_f, dt_bias_b, a_log_f, a_log_b, d_skip, ssm_norm, w_out,
                     norm2, w_gate, w_up, ffn_conv_w, ffn_conv_b, w_down, final_norm):
    half = HEAD_PAD // 2
    o_kr = Q_LORA + KV_LORA
    o_z = o_kr + QK_ROPE
    o_xbc = o_z + D_SSM
    o_dt = o_xbc + D_XBC
    kr_blk = jnp.zeros((D_MODEL, HEAD_PAD), f32)
    kr_blk = kr_blk.at[:, 0:HALF_ROPE].set(w_in[:, o_kr:o_kr + HALF_ROPE])
    kr_blk = kr_blk.at[:, half:half + HALF_ROPE].set(w_in[:, o_kr + HALF_ROPE:o_kr + QK_ROPE])
    w_dt = w_in[:, o_dt:o_dt + 2 * SSM_HEADS]
    dt_blk = jnp.zeros((D_MODEL, DT_PAD), f32).at[:, :2 * SSM_HEADS].set(w_dt)
    win = jnp.concatenate([w_in[:, :o_kr], kr_blk, w_in[:, o_z:o_dt], dt_blk], axis=1).astype(bf16)

    src = _head_lane_sources()
    wq = w_q_b.reshape(Q_LORA, N_HEADS, QK_DIM)
    wq = jnp.concatenate([wq, jnp.zeros((Q_LORA, N_HEADS, 1), f32)], axis=-1)[:, :, src]
    wqT = wq.reshape(Q_LORA, N_HEADS * HEAD_PAD).T.astype(bf16)
    wkv = w_kv_b.reshape(KV_LORA, N_HEADS, QK_NOPE + V_HEAD)
    src_k = np.where(src < QK_NOPE, src, QK_NOPE)
    wk = jnp.concatenate([wkv[:, :, :QK_NOPE], jnp.zeros((KV_LORA, N_HEADS, 1), f32)], axis=-1)[:, :, src_k]
    wk = wk.reshape(KV_LORA, N_HEADS * HEAD_PAD).astype(bf16)
    wvT = wkv[:, :, QK_NOPE:].reshape(KV_LORA, N_HEADS * V_HEAD).T.astype(bf16)

    row = lambda v: v.reshape(1, -1).astype(f32)
    bias = jnp.concatenate([dt_bias_f, dt_bias_b]).astype(f32)
    a_neg = -jnp.exp(jnp.concatenate([a_log_f, a_log_b]).astype(f32))
    expand = jnp.repeat(jnp.eye(SSM_HEADS, dtype=f32), SSM_HEAD_DIM, axis=1).astype(bf16)
    return dict(
        n1=row(norm1), win=win, qan=row(q_a_norm), kvan=row(kv_a_norm), wqT=wqT, wk=wk, wvT=wvT,
        wdtT=w_dt.T.astype(bf16),
        cw=conv_w.astype(f32), cb=row(conv_b), bias=row(bias), biasT=bias.reshape(-1, 1),
        a=row(a_neg), aT=a_neg.reshape(-1, 1),
        dskip=row(jnp.repeat(d_skip.astype(f32), SSM_HEAD_DIM)), ssm_norm=row(ssm_norm), expand=expand,
        wa=w_out[:N_HEADS * V_HEAD].astype(bf16), ws=w_out[N_HEADS * V_HEAD:].astype(bf16),
        n2=row(norm2), wg=w_gate.astype(bf16), wu=w_up.astype(bf16), fcw=ffn_conv_w.astype(f32),
        fcb=row(ffn_conv_b), wd=w_down.astype(bf16), fn=row(final_norm),
    )


def _encoder(x, w):
    B, L, _ = x.shape
    c_tab, s_tab = _rope_tables(L)
    scale = QK_DIM ** -0.5 * float(np.log2(np.e))
    qT, k, vT, z, xbc, dt, dtT = _proj_call(
        x, w["n1"], w["win"], w["qan"], w["kvan"], w["wqT"], w["wk"], w["wvT"], w["wdtT"],
        c_tab, s_tab, (c_tab * scale).T, (s_tab * scale).T)
    attnT = _attn_call(qT, k, vT)
    Q = min(SSD_Q, L)
    tri = jnp.tril(jnp.ones((Q, Q), f32))
    ssm = _ssd_call(xbc, dt, dtT, z, w["cw"], w["cb"], w["bias"], w["biasT"], w["a"], w["aT"],
                    w["dskip"], w["ssm_norm"], tri, w["expand"])
    x1 = _oproj_call(x, attnT, ssm, w["wa"], w["ws"])
    return _ffn_call(x1, w["n2"], w["wg"], w["wu"], w["fcw"], w["fcb"], w["wd"], w["fn"])


def kernel(x_prompt, x_sample, norm1, w_in, q_a_norm, kv_a_norm, w_q_b, w_kv_b, conv_w, conv_b,
           dt_bias_f, dt_bias_b, a_log_f, a_log_b, d_skip, ssm_norm, w_out, norm2, w_gate, w_up,
           ffn_conv_w, ffn_conv_b, w_down, final_norm):
    w = _prepare_weights(norm1[0], w_in[0], q_a_norm[0], kv_a_norm[0], w_q_b[0], w_kv_b[0], conv_w[0],
                         conv_b[0], dt_bias_f[0], dt_bias_b[0], a_log_f[0], a_log_b[0], d_skip[0],
                         ssm_norm[0], w_out[0], norm2[0], w_gate[0], w_up[0], ffn_conv_w[0],
                         ffn_conv_b[0], w_down[0], final_norm)
    return (_encoder(x_prompt, w), _encoder(x_sample, w))
```

```python
import functools

import numpy as np
import jax
import jax.numpy as jnp
from jax import lax
from jax.experimental import pallas as pl
from jax.experimental.pallas import tpu as pltpu

D_MODEL = 1024
N_HEADS = 16
QK_NOPE = 64
QK_ROPE = 32
HALF_ROPE = QK_ROPE // 2
QK_DIM = QK_NOPE + QK_ROPE
V_HEAD = 64
Q_LORA = 384
KV_LORA = 256
ROPE_THETA = 10000.0
SSM_HEADS = 16
SSM_HEAD_DIM = 64
D_SSM = SSM_HEADS * SSM_HEAD_DIM
SSM_GROUPS = 2
HEADS_PER_GROUP = SSM_HEADS // SSM_GROUPS
GROUP_WIDTH = D_SSM // SSM_GROUPS
D_STATE = 64
D_XBC = D_SSM + 2 * SSM_GROUPS * D_STATE
D_FF = 2816
EPS = 1e-6

LANES = 128
HEAD_PAD = LANES
BF16_ROWS = 16
F32_ROWS = 8
DT_PAD = LANES
VMEM_LIMIT = 56 * 1024 * 1024

OFF_Q = 0
OFF_CKV = OFF_Q + Q_LORA
OFF_KR = OFF_CKV + KV_LORA
OFF_Z = OFF_KR + HEAD_PAD
OFF_XBC = OFF_Z + D_SSM
OFF_DT = OFF_XBC + D_XBC
D_IN_PAD = OFF_DT + DT_PAD

PROJ_TM = 256
ATTN_TQ = 512
ATTN_TK = 512
ATTN_UNROLL = 4
SSD_Q = 256
OPROJ_TM = 512
FFN_TM = 512
FFN_FC = 256

NT_DIMS = (((1,), (1,)), ((), ()))
TN_DIMS = (((0,), (0,)), ((), ()))

f32 = jnp.float32
bf16 = jnp.bfloat16


def _rms(x, w):
    return x * lax.rsqrt(jnp.mean(x * x, axis=-1, keepdims=True) + EPS) * w


def _dot(a, b):
    return jnp.dot(a, b, preferred_element_type=f32)


def _silu(x):
    return x / (1.0 + jnp.exp(-x))


def _proj_kernel(x_ref, n1_ref, win_ref, qan_ref, kvan_ref, wqT_ref, wk_ref, wvT_ref, wdtT_ref,
                 c_ref, s_ref, cT_ref, sT_ref,
                 qT_out, k_out, vT_out, z_out, xbc_out, dt_out, dtT_out):
    h = _rms(x_ref[...], n1_ref[...]).astype(bf16)
    proj = _dot(h, win_ref[...])
    z_out[...] = proj[:, OFF_Z:OFF_Z + D_SSM].astype(bf16)
    xbc_out[...] = proj[:, OFF_XBC:OFF_XBC + D_XBC].astype(bf16)
    dt_out[...] = proj[:, OFF_DT:OFF_DT + DT_PAD]
    dtT_out[...] = lax.dot_general(wdtT_ref[...], h, NT_DIMS, preferred_element_type=f32)

    hq = _rms(proj[:, OFF_Q:OFF_Q + Q_LORA], qan_ref[...]).astype(bf16)
    qT = lax.dot_general(wqT_ref[...], hq, NT_DIMS, preferred_element_type=f32)
    cT = cT_ref[...]
    sT = sT_ref[...]
    half = HEAD_PAD // 2
    for hh in range(N_HEADS):
        blk = qT[hh * HEAD_PAD:(hh + 1) * HEAD_PAD, :]
        rot = jnp.concatenate([blk[half:], blk[:half]], axis=0)
        qT_out[hh * HEAD_PAD:(hh + 1) * HEAD_PAD, :] = (blk * cT + rot * sT).astype(bf16)

    hc = _rms(proj[:, OFF_CKV:OFF_CKV + KV_LORA], kvan_ref[...]).astype(bf16)
    kn = _dot(hc, wk_ref[...])
    kr = proj[:, OFF_KR:OFF_KR + HEAD_PAD]
    krf = kr * c_ref[...] + pltpu.roll(kr, half, axis=1) * s_ref[...]
    for hh in range(N_HEADS):
        k_out[:, hh * HEAD_PAD:(hh + 1) * HEAD_PAD] = (
            kn[:, hh * HEAD_PAD:(hh + 1) * HEAD_PAD] + krf).astype(bf16)
    vT_out[...] = lax.dot_general(wvT_ref[...], hc, NT_DIMS, preferred_element_type=f32).astype(bf16)


def _const_spec(shape):
    nd = len(shape)
    return pl.BlockSpec(shape, lambda *_: (0,) * nd)


def _proj_call(x, n1, win, qan, kvan, wqT, wk, wvT, wdtT, c_tab, s_tab, cT_tab, sT_tab):
    B, L, _ = x.shape
    tm = min(PROJ_TM, L)
    grid = (B, L // tm)
    tok = lambda w: pl.BlockSpec((None, tm, w), lambda b, i: (b, i, 0))
    tokT = lambda w: pl.BlockSpec((None, w, tm), lambda b, i: (b, 0, i))
    in_specs = [
        tok(D_MODEL), _const_spec(n1.shape), _const_spec(win.shape), _const_spec(qan.shape),
        _const_spec(kvan.shape), _const_spec(wqT.shape), _const_spec(wk.shape), _const_spec(wvT.shape),
        _const_spec(wdtT.shape),
        pl.BlockSpec((tm, HEAD_PAD), lambda b, i: (i, 0)),
        pl.BlockSpec((tm, HEAD_PAD), lambda b, i: (i, 0)),
        pl.BlockSpec((HEAD_PAD, tm), lambda b, i: (0, i)),
        pl.BlockSpec((HEAD_PAD, tm), lambda b, i: (0, i)),
    ]
    out_shape = [
        jax.ShapeDtypeStruct((B, N_HEADS * HEAD_PAD, L), bf16),
        jax.ShapeDtypeStruct((B, L, N_HEADS * HEAD_PAD), bf16),
        jax.ShapeDtypeStruct((B, N_HEADS * V_HEAD, L), bf16),
        jax.ShapeDtypeStruct((B, L, D_SSM), bf16),
        jax.ShapeDtypeStruct((B, L, D_XBC), bf16),
        jax.ShapeDtypeStruct((B, L, DT_PAD), f32),
        jax.ShapeDtypeStruct((B, 2 * SSM_HEADS, L), f32),
    ]
    out_specs = [tokT(N_HEADS * HEAD_PAD), tok(N_HEADS * HEAD_PAD), tokT(N_HEADS * V_HEAD),
                 tok(D_SSM), tok(D_XBC), tok(DT_PAD), tokT(2 * SSM_HEADS)]
    return pl.pallas_call(
        _proj_kernel, grid=grid, in_specs=in_specs, out_specs=out_specs, out_shape=out_shape,
        name="proj",
        compiler_params=pltpu.CompilerParams(
            dimension_semantics=("parallel", "parallel"), vmem_limit_bytes=VMEM_LIMIT),
    )(x, n1, win, qan, kvan, wqT, wk, wvT, wdtT, c_tab, s_tab, cT_tab, sT_tab)


def _attn_kernel(qT_ref, k_ref, vT_ref, o_ref, sa_ref, sb_ref, m_ref, acc_ref, *, tq, tk, nq, nk, unroll):
    bufs = (sa_ref, sb_ref)
    ones = jnp.ones((BF16_ROWS, tk), bf16)
    ng = nk // unroll

    def scores(qi, j, dst):
        start = pl.multiple_of(j * tk, tk)
        dst[...] = _dot(k_ref[pl.ds(start, tk), :], qT_ref[:, qi * tq:(qi + 1) * tq])

    def update(j, src):
        s = src[...]
        m = m_ref[...]
        m_new = jnp.maximum(m, jnp.max(s, axis=0, keepdims=True))
        alpha = jnp.exp2(m - m_new)
        p = jnp.exp2(s - m_new).astype(bf16)
        start = pl.multiple_of(j * tk, tk)
        v = jnp.concatenate([vT_ref[:, pl.ds(start, tk)], ones], axis=0)
        acc_ref[...] = alpha * acc_ref[...] + _dot(v, p)
        m_ref[...] = m_new

    def group(qi, g, last):
        for u in range(unroll):
            j = g * unroll + u
            nxt = bufs[(u + 1) % 2]
            if not (last and u == unroll - 1):
                scores(qi, j + 1, nxt)
            elif qi + 1 < nq:
                scores(qi + 1, 0, nxt)
            update(j, bufs[u % 2])

    scores(0, 0, bufs[0])
    for qi in range(nq):
        m_ref[...] = jnp.full(m_ref.shape, -jnp.inf, f32)
        acc_ref[...] = jnp.zeros(acc_ref.shape, f32)
        if ng > 1:
            def body(g, carry, qi=qi):
                group(qi, g, False)
                return carry
            lax.fori_loop(0, ng - 1, body, 0)
        group(qi, ng - 1, True)
        acc = acc_ref[...]
        o_ref[:, qi * tq:(qi + 1) * tq] = (acc[:V_HEAD] / acc[V_HEAD:V_HEAD + 1]).astype(o_ref.dtype)


def _attn_call(qT, k, vT):
    B, _, L = qT.shape
    tq = min(ATTN_TQ, L)
    tk = min(ATTN_TK, L)
    nk = L // tk
    unroll = min(ATTN_UNROLL, nk)
    assert unroll % 2 == 0 and nk % unroll == 0, "key tiles alternate between two score buffers"
    nq = L // tq if nk == unroll else 1
    tqb = nq * tq
    return pl.pallas_call(
        functools.partial(_attn_kernel, tq=tq, tk=tk, nq=nq, nk=nk, unroll=unroll),
        grid=(B, N_HEADS, L // tqb),
        scratch_shapes=[pltpu.VMEM((tk, tq), f32), pltpu.VMEM((tk, tq), f32),
                        pltpu.VMEM((1, tq), f32), pltpu.VMEM((V_HEAD + BF16_ROWS, tq), f32)],
        in_specs=[
            pl.BlockSpec((None, HEAD_PAD, tqb), lambda b, h, i: (b, h, i)),
            pl.BlockSpec((None, L, HEAD_PAD), lambda b, h, i: (b, 0, h)),
            pl.BlockSpec((None, V_HEAD, L), lambda b, h, i: (b, h, 0)),
        ],
        out_specs=pl.BlockSpec((None, V_HEAD, tqb), lambda b, h, i: (b, h, i)),
        out_shape=jax.ShapeDtypeStruct((B, N_HEADS * V_HEAD, L), bf16),
        name="attn",
        compiler_params=pltpu.CompilerParams(
            dimension_semantics=("parallel", "parallel", "arbitrary"), vmem_limit_bytes=VMEM_LIMIT),
    )(qT, k, vT)


def _split_hi_lo(v):
    hi = v.astype(bf16)
    lo = (v - hi.astype(f32)).astype(bf16)
    return hi, lo


def _expand_heads(v, e_ref):
    hi, lo = _split_hi_lo(v)
    return _dot(hi, e_ref[...]) + _dot(lo, e_ref[...])


def _ssd_kernel(xbc_ref, xp_ref, xn_ref, dt_ref, dtT_ref, z_ref,
                cw_ref, cb_ref, bias_ref, biasT_ref, a_ref, aT_ref, dskip_ref, norm_ref,
                tri_ref, e_ref, o_ref, hf_ref, hb_ref, hbs_ref, *, nc):
    ph = pl.program_id(1)
    c = pl.program_id(2)
    cidx = jnp.where(ph == 0, nc - 1 - c, c)
    Q = xbc_ref.shape[0]
    GN = SSM_GROUPS * D_STATE

    xm = xbc_ref[...].astype(f32)
    prev_row = jnp.where(cidx > 0, xp_ref[BF16_ROWS - 1:BF16_ROWS, :].astype(f32), 0.0)
    next_row = jnp.where(cidx < nc - 1, xn_ref[0:1, :].astype(f32), 0.0)
    rows = lax.broadcasted_iota(jnp.int32, (Q, 1), 0)
    x_prev = jnp.where(rows == 0, prev_row, pltpu.roll(xm, 1, axis=0))
    x_next = jnp.where(rows == Q - 1, next_row, pltpu.roll(xm, Q - 1, axis=0))
    cw = cw_ref[...]
    act = _silu(x_prev * cw[0:1, :] + xm * cw[1:2, :] + x_next * cw[2:3, :] + cb_ref[...])
    xs = act[:, :D_SSM]
    xs_b = xs.astype(bf16)
    Bm = act[:, D_SSM:D_SSM + GN].astype(bf16)
    Cm = act[:, D_SSM + GN:].astype(bf16)

    nh2 = 2 * SSM_HEADS
    dt = jax.nn.softplus(dt_ref[:, :nh2] + bias_ref[...])
    dtT = jax.nn.softplus(dtT_ref[...] + biasT_ref[...])
    a = dt * a_ref[...]
    aT = dtT * aT_ref[...]
    tri = tri_ref[...]
    cs = jnp.dot(tri, a, preferred_element_type=f32, precision=lax.Precision.HIGHEST)
    csT = lax.dot_general(aT, tri, NT_DIMS, preferred_element_type=f32,
                          precision=lax.Precision.HIGHEST)
    tot = cs[Q - 1:Q, :]
    cs_f = cs[:, :SSM_HEADS]
    cb_b = cs[:, SSM_HEADS:] - a[:, SSM_HEADS:]
    csT_f = csT[:SSM_HEADS, :]
    cbT_b = csT[SSM_HEADS:, :] - aT[SSM_HEADS:, :]
    dt_f, dt_b = dt[:, :SSM_HEADS], dt[:, SSM_HEADS:]
    dtT_f, dtT_b = dtT[:SSM_HEADS, :], dtT[SSM_HEADS:, :]
    eb = _expand_heads(jnp.exp(tot[:, SSM_HEADS:] - cb_b), e_ref)

    @pl.when(ph == 0)
    def _backward_states():
        @pl.when(c == 0)
        def _():
            hb_ref[...] = jnp.zeros_like(hb_ref)

        hbs_ref[cidx] = hb_ref[...].astype(bf16)
        xw = (xs * _expand_heads(jnp.exp(cb_b) * dt_b, e_ref)).astype(bf16)
        for g in range(SSM_GROUPS):
            sl = slice(g * GROUP_WIDTH, (g + 1) * GROUP_WIDTH)
            upd = lax.dot_general(Bm[:, g * D_STATE:(g + 1) * D_STATE], xw[:, sl], TN_DIMS,
                                  preferred_element_type=f32)
            hb_ref[g] = eb[0:1, sl] * hb_ref[g] + upd

    @pl.when(ph == 1)
    def _outputs():
        @pl.when(c == 0)
        def _():
            hf_ref[...] = jnp.zeros_like(hf_ref)

        ef = _expand_heads(jnp.exp(cs_f), e_ref)
        ti = lax.broadcasted_iota(jnp.int32, (Q, Q), 0)
        si = lax.broadcasted_iota(jnp.int32, (Q, Q), 1)
        lower = ti >= si
        upper = si >= ti
        lane = lax.broadcasted_iota(jnp.int32, (1, LANES), 1)
        first_half = lane < SSM_HEAD_DIM

        y_groups = []
        for g in range(SSM_GROUPS):
            sl = slice(g * GROUP_WIDTH, (g + 1) * GROUP_WIDTH)
            Cg = Cm[:, g * D_STATE:(g + 1) * D_STATE]
            Bg = Bm[:, g * D_STATE:(g + 1) * D_STATE]
            cbm = lax.dot_general(Cg, Bg, NT_DIMS, preferred_element_type=f32)
            y_off = (ef[:, sl] * _dot(Cg, hf_ref[g].astype(bf16))
                     + eb[:, sl] * _dot(Cg, hbs_ref[c, g]))
            pairs = []
            for hp in range(HEADS_PER_GROUP // 2):
                x_pair = xs_b[:, g * GROUP_WIDTH + hp * LANES: g * GROUP_WIDTH + (hp + 1) * LANES]
                y_pair = None
                for k in range(2):
                    hh = g * HEADS_PER_GROUP + 2 * hp + k
                    seg_f = cs_f[:, hh:hh + 1] - csT_f[hh:hh + 1, :]
                    seg_b = cbT_b[hh:hh + 1, :] - cb_b[:, hh:hh + 1]
                    dec = (jnp.exp(jnp.where(lower, seg_f, -jnp.inf)) * dtT_f[hh:hh + 1, :]
                           + jnp.exp(jnp.where(upper, seg_b, -jnp.inf)) * dtT_b[hh:hh + 1, :])
                    mat = (cbm * dec).astype(bf16)
                    keep = first_half if k == 0 else jnp.logical_not(first_half)
                    xk = jnp.where(keep, x_pair, jnp.zeros_like(x_pair))
                    contrib = _dot(mat, xk)
                    y_pair = contrib if y_pair is None else y_pair + contrib
                pairs.append(y_pair)
            y_groups.append(jnp.concatenate(pairs, axis=1) + y_off)
            xw = (xs[:, sl] * _expand_heads(jnp.exp(tot[:, :SSM_HEADS] - cs_f) * dt_f, e_ref)[:, sl]).astype(bf16)
            upd = lax.dot_general(Bg, xw, TN_DIMS, preferred_element_type=f32)
            hf_ref[g] = ef[Q - 1:Q, sl] * hf_ref[g] + upd

        y = jnp.concatenate(y_groups, axis=1) + xs * dskip_ref[...]
        y = y * _silu(z_ref[...].astype(f32))
        nw = norm_ref[...]
        for g in range(SSM_GROUPS):
            sl = slice(g * GROUP_WIDTH, (g + 1) * GROUP_WIDTH)
            o_ref[:, sl] = _rms(y[:, sl], nw[:, sl]).astype(o_ref.dtype)


def _ssd_call(xbc, dt, dtT, z, cw, cb, bias, biasT, a, aT, dskip, norm, tri, expand):
    B, L, _ = xbc.shape
    Q = min(SSD_Q, L)
    nc = L // Q
    hb = Q // BF16_ROWS
    nhalo = L // BF16_ROWS

    def cidx(p, c):
        return jnp.where(p == 0, nc - 1 - c, c)

    in_specs = [
        pl.BlockSpec((None, Q, D_XBC), lambda b, p, c: (b, cidx(p, c), 0)),
        pl.BlockSpec((None, BF16_ROWS, D_XBC), lambda b, p, c: (b, jnp.maximum(cidx(p, c) * hb - 1, 0), 0)),
        pl.BlockSpec((None, BF16_ROWS, D_XBC),
                     lambda b, p, c: (b, jnp.minimum((cidx(p, c) + 1) * hb, nhalo - 1), 0)),
        pl.BlockSpec((None, Q, DT_PAD), lambda b, p, c: (b, cidx(p, c), 0)),
        pl.BlockSpec((None, 2 * SSM_HEADS, Q), lambda b, p, c: (b, 0, cidx(p, c))),
        pl.BlockSpec((None, Q, D_SSM), lambda b, p, c: (b, p * c, 0)),
        _const_spec(cw.shape), _const_spec(cb.shape), _const_spec(bias.shape), _const_spec(biasT.shape),
        _const_spec(a.shape), _const_spec(aT.shape), _const_spec(dskip.shape), _const_spec(norm.shape),
        _const_spec(tri.shape), _const_spec(expand.shape),
    ]
    state = (SSM_GROUPS, D_STATE, GROUP_WIDTH)
    return pl.pallas_call(
        functools.partial(_ssd_kernel, nc=nc),
        grid=(B, 2, nc),
        in_specs=in_specs,
        out_specs=pl.BlockSpec((None, Q, D_SSM), lambda b, p, c: (b, p * c, 0)),
        out_shape=jax.ShapeDtypeStruct((B, L, D_SSM), bf16),
        scratch_shapes=[pltpu.VMEM(state, f32), pltpu.VMEM(state, f32),
                        pltpu.VMEM((nc,) + state, bf16)],
        name="ssd",
        compiler_params=pltpu.CompilerParams(
            dimension_semantics=("parallel", "arbitrary", "arbitrary"), vmem_limit_bytes=VMEM_LIMIT),
    )(xbc, xbc, xbc, dt, dtT, z, cw, cb, bias, biasT, a, aT, dskip, norm, tri, expand)


def _oproj_kernel(x_ref, aT_ref, s_ref, wa_ref, ws_ref, o_ref):
    o_ref[...] = (x_ref[...]
                  + lax.dot_general(aT_ref[...], wa_ref[...], TN_DIMS, preferred_element_type=f32)
                  + _dot(s_ref[...], ws_ref[...]))


def _oproj_call(x, attnT, ssm, wa, ws):
    B, L, _ = x.shape
    tm = min(OPROJ_TM, L)
    tok = lambda w: pl.BlockSpec((None, tm, w), lambda b, i: (b, i, 0))
    return pl.pallas_call(
        _oproj_kernel, grid=(B, L // tm),
        in_specs=[tok(D_MODEL), pl.BlockSpec((None, N_HEADS * V_HEAD, tm), lambda b, i: (b, 0, i)),
                  tok(D_SSM), _const_spec(wa.shape), _const_spec(ws.shape)],
        out_specs=tok(D_MODEL),
        out_shape=jax.ShapeDtypeStruct((B, L, D_MODEL), f32),
        name="oproj",
        compiler_params=pltpu.CompilerParams(
            dimension_semantics=("parallel", "parallel"), vmem_limit_bytes=VMEM_LIMIT),
    )(x, attnT, ssm, wa, ws)


def _ffn_kernel(x_ref, xp_ref, xn_ref, n2_ref, wg_ref, wu_ref, cw_ref, cb_ref, wd_ref, fn_ref,
                o_ref, h_scr, act_scr, *, nfc):
    i = pl.program_id(1)
    nt = pl.num_programs(1)
    tm = x_ref.shape[0]
    n2 = n2_ref[...]
    h_scr[0:tm, :] = _rms(x_ref[...], n2).astype(bf16)
    halo = jnp.concatenate([xp_ref[...], xn_ref[...]], axis=0)
    h_scr[tm:tm + 2 * F32_ROWS, :] = _rms(halo, n2).astype(bf16)
    rows = lax.broadcasted_iota(jnp.int32, (tm, 1), 0)
    has_prev = i > 0
    has_next = i < nt - 1

    def chunk(cf):
        col = cf * FFN_FC
        wg = wg_ref[:, pl.ds(col, FFN_FC)]
        g_ext = _dot(h_scr[...], wg)
        g = g_ext[:tm]
        g_before = jnp.where(has_prev, g_ext[tm + F32_ROWS - 1:tm + F32_ROWS], 0.0)
        g_after = jnp.where(has_next, g_ext[tm + F32_ROWS:tm + F32_ROWS + 1], 0.0)
        g_prev = jnp.where(rows == 0, g_before, pltpu.roll(g, 1, axis=0))
        g_next = jnp.where(rows == tm - 1, g_after, pltpu.roll(g, tm - 1, axis=0))
        cw = cw_ref[:, pl.ds(col, FFN_FC)]
        gc = g_prev * cw[0:1] + g * cw[1:2] + g_next * cw[2:3] + cb_ref[:, pl.ds(col, FFN_FC)]
        u = _dot(h_scr[0:tm, :], wu_ref[:, pl.ds(col, FFN_FC)])
        act_scr[:, pl.ds(col, FFN_FC)] = (_silu(gc) * u).astype(bf16)

    for cf in range(nfc):
        chunk(cf)
    y = x_ref[...] + _dot(act_scr[...], wd_ref[...])
    o_ref[...] = _rms(y, fn_ref[...])


def _ffn_call(x, n2, wg, wu, cw, cb, wd, fn):
    B, L, _ = x.shape
    tm = min(FFN_TM, L)
    hb = tm // F32_ROWS
    nhalo = L // F32_ROWS
    tok = pl.BlockSpec((None, tm, D_MODEL), lambda b, i: (b, i, 0))
    return pl.pallas_call(
        functools.partial(_ffn_kernel, nfc=D_FF // FFN_FC),
        grid=(B, L // tm),
        in_specs=[
            tok,
            pl.BlockSpec((None, F32_ROWS, D_MODEL), lambda b, i: (b, jnp.maximum(i * hb - 1, 0), 0)),
            pl.BlockSpec((None, F32_ROWS, D_MODEL), lambda b, i: (b, jnp.minimum((i + 1) * hb, nhalo - 1), 0)),
            _const_spec(n2.shape), _const_spec(wg.shape), _const_spec(wu.shape), _const_spec(cw.shape),
            _const_spec(cb.shape), _const_spec(wd.shape), _const_spec(fn.shape),
        ],
        out_specs=tok,
        out_shape=jax.ShapeDtypeStruct((B, L, D_MODEL), f32),
        scratch_shapes=[pltpu.VMEM((tm + 2 * F32_ROWS, D_MODEL), bf16), pltpu.VMEM((tm, D_FF), bf16)],
        name="ffn",
        compiler_params=pltpu.CompilerParams(
            dimension_semantics=("parallel", "parallel"), vmem_limit_bytes=VMEM_LIMIT),
    )(x, x, x, n2, wg, wu, cw, cb, wd, fn)


def _head_lane_sources():
    src = np.full((HEAD_PAD,), QK_DIM, np.int32)
    half = HEAD_PAD // 2
    src[0:HALF_ROPE] = QK_NOPE + np.arange(HALF_ROPE)
    src[HALF_ROPE:half] = np.arange(half - HALF_ROPE)
    src[half:half + HALF_ROPE] = QK_NOPE + HALF_ROPE + np.arange(HALF_ROPE)
    n_rest = QK_NOPE - (half - HALF_ROPE)
    src[half + HALF_ROPE:half + HALF_ROPE + n_rest] = (half - HALF_ROPE) + np.arange(n_rest)
    return src


def _rope_tables(L):
    inv = ROPE_THETA ** (-jnp.arange(0, QK_ROPE, 2, dtype=f32) / QK_ROPE)
    ang = jnp.arange(L, dtype=f32)[:, None] * inv[None, :]
    cos, sin = jnp.cos(ang), jnp.sin(ang)
    half = HEAD_PAD // 2
    c_tab = jnp.ones((L, HEAD_PAD), f32)
    c_tab = c_tab.at[:, 0:HALF_ROPE].set(cos).at[:, half:half + HALF_ROPE].set(cos)
    s_tab = jnp.zeros((L, HEAD_PAD), f32)
    s_tab = s_tab.at[:, 0:HALF_ROPE].set(-sin).at[:, half:half + HALF_ROPE].set(sin)
    return c_tab, s_tab


def _prepare_weights(norm1, w_in, q_a_norm, kv_a_norm, w_q_b, w_kv_b, conv_w, conv_b,
                     dt_bias_f, dt_bias_b, a_log_f, a_log_b, d_skip, ssm_norm, w_out,
                     norm2, w_gate, w_up, ffn_conv_w, ffn_conv_b, w_down, final_norm):
    half = HEAD_PAD // 2
    o_kr = Q_LORA + KV_LORA
    o_z = o_kr + QK_ROPE
    o_xbc = o_z + D_SSM
    o_dt = o_xbc + D_XBC
    kr_blk = jnp.zeros((D_MODEL, HEAD_PAD), f32)
    kr_blk = kr_blk.at[:, 0:HALF_ROPE].set(w_in[:, o_kr:o_kr + HALF_ROPE])
    kr_blk = kr_blk.at[:, half:half + HALF_ROPE].set(w_in[:, o_kr + HALF_ROPE:o_kr + QK_ROPE])
    w_dt = w_in[:, o_dt:o_dt + 2 * SSM_HEADS]
    dt_blk = jnp.zeros((D_MODEL, DT_PAD), f32).at[:, :2 * SSM_HEADS].set(w_dt)
    win = jnp.concatenate([w_in[:, :o_kr], kr_blk, w_in[:, o_z:o_dt], dt_blk], axis=1).astype(bf16)

    src = _head_lane_sources()
    wq = w_q_b.reshape(Q_LORA, N_HEADS, QK_DIM)
    wq = jnp.concatenate([wq, jnp.zeros((Q_LORA, N_HEADS, 1), f32)], axis=-1)[:, :, src]
    wqT = wq.reshape(Q_LORA, N_HEADS * HEAD_PAD).T.astype(bf16)
    wkv = w_kv_b.reshape(KV_LORA, N_HEADS, QK_NOPE + V_HEAD)
    src_k = np.where(src < QK_NOPE, src, QK_NOPE)
    wk = jnp.concatenate([wkv[:, :, :QK_NOPE], jnp.zeros((KV_LORA, N_HEADS, 1), f32)], axis=-1)[:, :, src_k]
    wk = wk.reshape(KV_LORA, N_HEADS * HEAD_PAD).astype(bf16)
    wvT = wkv[:, :, QK_NOPE:].reshape(KV_LORA, N_HEADS * V_HEAD).T.astype(bf16)

    row = lambda v: v.reshape(1, -1).astype(f32)
    bias = jnp.concatenate([dt_bias_f, dt_bias_b]).astype(f32)
    a_neg = -jnp.exp(jnp.concatenate([a_log_f, a_log_b]).astype(f32))
    expand = jnp.repeat(jnp.eye(SSM_HEADS, dtype=f32), SSM_HEAD_DIM, axis=1).astype(bf16)
    return dict(
        n1=row(norm1), win=win, qan=row(q_a_norm), kvan=row(kv_a_norm), wqT=wqT, wk=wk, wvT=wvT,
        wdtT=w_dt.T.astype(bf16),
        cw=conv_w.astype(f32), cb=row(conv_b), bias=row(bias), biasT=bias.reshape(-1, 1),
        a=row(a_neg), aT=a_neg.reshape(-1, 1),
        dskip=row(jnp.repeat(d_skip.astype(f32), SSM_HEAD_DIM)), ssm_norm=row(ssm_norm), expand=expand,
        wa=w_out[:N_HEADS * V_HEAD].astype(bf16), ws=w_out[N_HEADS * V_HEAD:].astype(bf16),
        n2=row(norm2), wg=w_gate.astype(bf16), wu=w_up.astype(bf16), fcw=ffn_conv_w.astype(f32),
        fcb=row(ffn_conv_b), wd=w_down.astype(bf16), fn=row(final_norm),
    )


def _encoder(x, w):
    B, L, _ = x.shape
    c_tab, s_tab = _rope_tables(L)
    scale = QK_DIM ** -0.5 * float(np.log2(np.e))
    qT, k, vT, z, xbc, dt, dtT = _proj_call(
        x, w["n1"], w["win"], w["qan"], w["kvan"], w["wqT"], w["wk"], w["wvT"], w["wdtT"],
        c_tab, s_tab, (c_tab * scale).T, (s_tab * scale).T)
    attnT = _attn_call(qT, k, vT)
    Q = min(SSD_Q, L)
    tri = jnp.tril(jnp.ones((Q, Q), f32))
    ssm = _ssd_call(xbc, dt, dtT, z, w["cw"], w["cb"], w["bias"], w["biasT"], w["a"], w["aT"],
                    w["dskip"], w["ssm_norm"], tri, w["expand"])
    x1 = _oproj_call(x, attnT, ssm, w["wa"], w["ws"])
    return _ffn_call(x1, w["n2"], w["wg"], w["wu"], w["fcw"], w["fcb"], w["wd"], w["fn"])


def kernel(x_prompt, x_sample, norm1, w_in, q_a_norm, kv_a_norm, w_q_b, w_kv_b, conv_w, conv_b,
           dt_bias_f, dt_bias_b, a_log_f, a_log_b, d_skip, ssm_norm, w_out, norm2, w_gate, w_up,
           ffn_conv_w, ffn_conv_b, w_down, final_norm):
    w = _prepare_weights(norm1[0], w_in[0], q_a_norm[0], kv_a_norm[0], w_q_b[0], w_kv_b[0], conv_w[0],
                         conv_b[0], dt_bias_f[0], dt_bias_b[0], a_log_f[0], a_log_b[0], d_skip[0],
                         ssm_norm[0], w_out[0], norm2[0], w_gate[0], w_up[0], ffn_conv_w[0],
                         ffn_conv_b[0], w_down[0], final_norm)
    return (_encoder(x_prompt, w), _encoder(x_sample, w))
```

```python
import functools

import numpy as np
import jax
import jax.numpy as jnp
from jax import lax
from jax.experimental import pallas as pl
from jax.experimental.pallas import tpu as pltpu

D_MODEL = 1024
N_HEADS = 16
QK_NOPE = 64
QK_ROPE = 32
HALF_ROPE = QK_ROPE // 2
QK_DIM = QK_NOPE + QK_ROPE
V_HEAD = 64
Q_LORA = 384
KV_LORA = 256
ROPE_THETA = 10000.0
SSM_HEADS = 16
SSM_HEAD_DIM = 64
D_SSM = SSM_HEADS * SSM_HEAD_DIM
SSM_GROUPS = 2
HEADS_PER_GROUP = SSM_HEADS // SSM_GROUPS
GROUP_WIDTH = D_SSM // SSM_GROUPS
D_STATE = 64
GN = SSM_GROUPS * D_STATE
D_XBC = D_SSM + 2 * GN
D_FF = 2816
EPS = 1e-6
LOG2E = float(np.log2(np.e))

LANES = 128
HEAD_PAD = LANES
BF16_ROWS = 16
F32_ROWS = 8
VMEM_LIMIT = 56 * 1024 * 1024

OFF_Q = 0
OFF_CKV = OFF_Q + Q_LORA
OFF_KR = OFF_CKV + KV_LORA
OFF_Z = OFF_KR + HEAD_PAD
OFF_XBC = OFF_Z + D_SSM
D_IN_PAD = OFF_XBC + D_XBC

PROJ_TM = 256
ATTN_TQ = 512
ATTN_TK = 512
ATTN_UNROLL = 4
SSD_Q = 128
SSD_BLOCK = 512
OPROJ_TM = 512
FFN_TM = 512
FFN_FC = 256

NT_DIMS = (((1,), (1,)), ((), ()))
TN_DIMS = (((0,), (0,)), ((), ()))

f32 = jnp.float32
bf16 = jnp.bfloat16


def _rms(x, w):
    return x * lax.rsqrt(jnp.mean(x * x, axis=-1, keepdims=True) + EPS) * w


def _dot(a, b):
    return jnp.dot(a, b, preferred_element_type=f32)


def _silu(x):
    h = 0.5 * x
    return h * jnp.tanh(h) + h


def _conv3_rows(x, before, after, cw, cb):
    n = x.shape[0]
    sub = lax.broadcasted_iota(jnp.int32, (F32_ROWS, 1), 0)
    down = pltpu.roll(x, 1, axis=0)
    up = pltpu.roll(x, n - 1, axis=0)
    x_prev = jnp.concatenate([jnp.where(sub == 0, before, down[:F32_ROWS]), down[F32_ROWS:]], axis=0)
    x_next = jnp.concatenate([up[:n - F32_ROWS],
                              jnp.where(sub == F32_ROWS - 1, after, up[n - F32_ROWS:])], axis=0)
    return x_prev * cw[0:1] + x * cw[1:2] + x_next * cw[2:3] + cb


def _halo_specs(tm, L, width):
    hb = tm // F32_ROWS
    last = L // F32_ROWS - 1
    return [pl.BlockSpec((None, F32_ROWS, width), lambda b, i: (b, jnp.maximum(i * hb - 1, 0), 0)),
            pl.BlockSpec((None, F32_ROWS, width), lambda b, i: (b, jnp.minimum((i + 1) * hb, last), 0))]


def _const_spec(shape):
    nd = len(shape)
    return pl.BlockSpec(shape, lambda *_: (0,) * nd)


def _proj_kernel(x_ref, xp_ref, xn_ref, n1_ref, win_ref, qan_ref, kvan_ref, wqT_ref, wk_ref, wvT_ref, wdtT_ref,
                 cw_ref, cb_ref, c_ref, s_ref, cT_ref, sT_ref,
                 qT_out, k_out, vT_out, z_out, act_out, dtT_out, h_scr):
    i = pl.program_id(1)
    tm = x_ref.shape[0]
    n1 = n1_ref[...]
    h_scr[0:tm, :] = _rms(x_ref[...], n1).astype(bf16)
    halo = jnp.concatenate([xp_ref[...], xn_ref[...]], axis=0)
    h_scr[tm:tm + 2 * F32_ROWS, :] = _rms(halo, n1).astype(bf16)
    h = h_scr[0:tm, :]
    proj = _dot(h_scr[...], win_ref[...])
    z_out[...] = proj[:tm, OFF_Z:OFF_Z + D_SSM].astype(bf16)
    dtT_out[...] = lax.dot_general(wdtT_ref[...], h, NT_DIMS, preferred_element_type=f32)

    xbc = proj[:, OFF_XBC:OFF_XBC + D_XBC]
    before = jnp.where(i > 0, xbc[tm + F32_ROWS - 1:tm + F32_ROWS], 0.0)
    after = jnp.where(i < pl.num_programs(1) - 1, xbc[tm + F32_ROWS:tm + F32_ROWS + 1], 0.0)
    act_out[...] = _silu(_conv3_rows(xbc[:tm], before, after, cw_ref[...], cb_ref[...])).astype(bf16)

    hq = _rms(proj[:tm, OFF_Q:OFF_Q + Q_LORA], qan_ref[...]).astype(bf16)
    qT = lax.dot_general(wqT_ref[...], hq, NT_DIMS, preferred_element_type=f32)
    cT = cT_ref[...]
    sT = sT_ref[...]
    half = HEAD_PAD // 2
    for hh in range(N_HEADS):
        blk = qT[hh * HEAD_PAD:(hh + 1) * HEAD_PAD, :]
        rot = jnp.concatenate([blk[half:], blk[:half]], axis=0)
        qT_out[hh * HEAD_PAD:(hh + 1) * HEAD_PAD, :] = (blk * cT + rot * sT).astype(bf16)

    hc = _rms(proj[:tm, OFF_CKV:OFF_CKV + KV_LORA], kvan_ref[...]).astype(bf16)
    kn = _dot(hc, wk_ref[...])
    kr = proj[:tm, OFF_KR:OFF_KR + HEAD_PAD]
    krf = kr * c_ref[...] + pltpu.roll(kr, half, axis=1) * s_ref[...]
    for hh in range(N_HEADS):
        k_out[:, hh * HEAD_PAD:(hh + 1) * HEAD_PAD] = (
            kn[:, hh * HEAD_PAD:(hh + 1) * HEAD_PAD] + krf).astype(bf16)
    vT_out[...] = lax.dot_general(wvT_ref[...], hc, NT_DIMS, preferred_element_type=f32).astype(bf16)


def _proj_call(x, n1, win, qan, kvan, wqT, wk, wvT, wdtT, cw, cb, c_tab, s_tab, cT_tab, sT_tab):
    B, L, _ = x.shape
    tm = min(PROJ_TM, L)
    grid = (B, L // tm)
    tok = lambda w: pl.BlockSpec((None, tm, w), lambda b, i: (b, i, 0))
    tokT = lambda w: pl.BlockSpec((None, w, tm), lambda b, i: (b, 0, i))
    consts = (n1, win, qan, kvan, wqT, wk, wvT, wdtT, cw, cb)
    in_specs = [tok(D_MODEL)] + _halo_specs(tm, L, D_MODEL) + [_const_spec(a.shape) for a in consts] + [
        pl.BlockSpec((tm, HEAD_PAD), lambda b, i: (i, 0)),
        pl.BlockSpec((tm, HEAD_PAD), lambda b, i: (i, 0)),
        pl.BlockSpec((HEAD_PAD, tm), lambda b, i: (0, i)),
        pl.BlockSpec((HEAD_PAD, tm), lambda b, i: (0, i)),
    ]
    out_shape = [
        jax.ShapeDtypeStruct((B, N_HEADS * HEAD_PAD, L), bf16),
        jax.ShapeDtypeStruct((B, L, N_HEADS * HEAD_PAD), bf16),
        jax.ShapeDtypeStruct((B, N_HEADS * V_HEAD, L), bf16),
        jax.ShapeDtypeStruct((B, L, D_SSM), bf16),
        jax.ShapeDtypeStruct((B, L, D_XBC), bf16),
        jax.ShapeDtypeStruct((B, 2 * SSM_HEADS, L), f32),
    ]
    out_specs = [tokT(N_HEADS * HEAD_PAD), tok(N_HEADS * HEAD_PAD), tokT(N_HEADS * V_HEAD),
                 tok(D_SSM), tok(D_XBC), tokT(2 * SSM_HEADS)]
    return pl.pallas_call(
        _proj_kernel, grid=grid, in_specs=in_specs, out_specs=out_specs, out_shape=out_shape,
        scratch_shapes=[pltpu.VMEM((tm + 2 * F32_ROWS, D_MODEL), bf16)],
        name="proj",
        compiler_params=pltpu.CompilerParams(
            dimension_semantics=("parallel", "parallel"), vmem_limit_bytes=VMEM_LIMIT),
    )(x, x, x, *consts, c_tab, s_tab, cT_tab, sT_tab)


def _attn_kernel(qT_ref, k_ref, vT_ref, o_ref, sa_ref, sb_ref, m_ref, acc_ref, *, tq, tk, nq, nk, unroll):
    bufs = (sa_ref, sb_ref)
    ones = jnp.ones((BF16_ROWS, tk), bf16)
    ng = nk // unroll

    def scores(qi, j, dst):
        start = pl.multiple_of(j * tk, tk)
        dst[...] = _dot(k_ref[pl.ds(start, tk), :], qT_ref[:, qi * tq:(qi + 1) * tq])

    def update(j, src):
        s = src[...]
        m = m_ref[...]
        m_new = jnp.maximum(m, jnp.max(s, axis=0, keepdims=True))
        alpha = jnp.exp2(m - m_new)
        p = jnp.exp2(s - m_new).astype(bf16)
        start = pl.multiple_of(j * tk, tk)
        v = jnp.concatenate([vT_ref[:, pl.ds(start, tk)], ones], axis=0)
        acc_ref[...] = alpha * acc_ref[...] + _dot(v, p)
        m_ref[...] = m_new

    def group(qi, g, last):
        for u in range(unroll):
            j = g * unroll + u
            nxt = bufs[(u + 1) % 2]
            if not (last and u == unroll - 1):
                scores(qi, j + 1, nxt)
            elif qi + 1 < nq:
                scores(qi + 1, 0, nxt)
            update(j, bufs[u % 2])

    scores(0, 0, bufs[0])
    for qi in range(nq):
        m_ref[...] = jnp.full(m_ref.shape, -jnp.inf, f32)
        acc_ref[...] = jnp.zeros(acc_ref.shape, f32)
        if ng > 1:
            def body(g, carry, qi=qi):
                group(qi, g, False)
                return carry
            lax.fori_loop(0, ng - 1, body, 0)
        group(qi, ng - 1, True)
        acc = acc_ref[...]
        o_ref[:, qi * tq:(qi + 1) * tq] = (acc[:V_HEAD] / acc[V_HEAD:V_HEAD + 1]).astype(o_ref.dtype)


def _attn_call(qT, k, vT):
    B, _, L = qT.shape
    tq = min(ATTN_TQ, L)
    tk = min(ATTN_TK, L)
    nk = L // tk
    unroll = min(ATTN_UNROLL, nk)
    assert unroll % 2 == 0 and nk % unroll == 0, "key tiles alternate between two score buffers"
    nq = L // tq if nk == unroll else 1
    tqb = nq * tq
    return pl.pallas_call(
        functools.partial(_attn_kernel, tq=tq, tk=tk, nq=nq, nk=nk, unroll=unroll),
        grid=(B, N_HEADS, L // tqb),
        scratch_shapes=[pltpu.VMEM((tk, tq), f32), pltpu.VMEM((tk, tq), f32),
                        pltpu.VMEM((1, tq), f32), pltpu.VMEM((V_HEAD + BF16_ROWS, tq), f32)],
        in_specs=[
            pl.BlockSpec((None, HEAD_PAD, tqb), lambda b, h, i: (b, h, i)),
            pl.BlockSpec((None, L, HEAD_PAD), lambda b, h, i: (b, 0, h)),
            pl.BlockSpec((None, V_HEAD, L), lambda b, h, i: (b, h, 0)),
        ],
        out_specs=pl.BlockSpec((None, V_HEAD, tqb), lambda b, h, i: (b, h, i)),
        out_shape=jax.ShapeDtypeStruct((B, N_HEADS * V_HEAD, L), bf16),
        name="attn",
        compiler_params=pltpu.CompilerParams(
            dimension_semantics=("parallel", "parallel", "arbitrary"), vmem_limit_bytes=VMEM_LIMIT),
    )(qT, k, vT)


def _split3(v):
    a1 = v.astype(bf16)
    r1 = v - a1.astype(f32)
    a2 = r1.astype(bf16)
    a3 = (r1 - a2.astype(f32)).astype(bf16)
    return a1, a2, a3


def _chunk_scalars(dtT_raw, biasT, aT, triu):
    nh2 = 2 * SSM_HEADS
    dtT = jax.nn.softplus(dtT_raw + biasT)
    stepT = dtT * aT
    cs3 = _dot(jnp.concatenate(_split3(stepT), axis=0), triu)
    csT = cs3[0:nh2] + cs3[nh2:2 * nh2] + cs3[2 * nh2:3 * nh2]
    return dtT, stepT, csT


def _expand_heads(partsT, e_ref):
    n = len(partsT)
    stk = jnp.concatenate(partsT, axis=0)
    hi = stk.astype(bf16).astype(f32)
    pieces = [hi, stk - hi]
    pad = LANES - 2 * stk.shape[0]
    if pad:
        pieces.append(jnp.zeros((pad, stk.shape[1]), f32))
    nat = jnp.transpose(jnp.concatenate(pieces, axis=0)).astype(bf16)
    full = _dot(nat, e_ref[...])
    return [full[:, j * D_SSM:(j + 1) * D_SSM] for j in range(n)]


def _ssd_kernel(act_ref, dtT_ref, z_ref, biasT_ref, aT_ref, dskipT_ref, norm_ref, triu_ref, e4_ref, e2_ref,
                gT_ref, o_ref, hf_ref, hb_ref, hbs_ref, *, nblk, nsub, Q):
    ph = pl.program_id(1)
    c = pl.program_id(2)
    H = SSM_HEADS

    def group_slices(g):
        return slice(g * GROUP_WIDTH, (g + 1) * GROUP_WIDTH), slice(g * D_STATE, (g + 1) * D_STATE)

    dtT, stepT, csT = _chunk_scalars(dtT_ref[...], biasT_ref[...], aT_ref[...], triu_ref[...])
    totT = jnp.concatenate(
        [jnp.broadcast_to(csT[:, (j + 1) * Q - 1:(j + 1) * Q], (2 * H, Q)) for j in range(nsub)], axis=1)
    cbT_b = csT[H:] - stepT[H:]
    Bm = act_ref[:, D_SSM:D_SSM + GN]

    @pl.when(ph == 0)
    def _backward_states():
        @pl.when(c == 0)
        def _():
            hb_ref[...] = jnp.zeros_like(hb_ref)

        blk = nblk - 1 - c
        wb, eb = _expand_heads([jnp.exp2(cbT_b) * dtT[H:], jnp.exp2(totT[H:] - cbT_b)], e2_ref)
        xw = (act_ref[:, :D_SSM].astype(f32) * wb).astype(bf16)
        for j in reversed(range(nsub)):
            rows = slice(j * Q, (j + 1) * Q)
            hbs_ref[blk * nsub + j] = hb_ref[...].astype(bf16)
            for g in range(SSM_GROUPS):
                sl, sn = group_slices(g)
                upd = lax.dot_general(Bm[rows, sn], xw[rows, sl], TN_DIMS, preferred_element_type=f32)
                hb_ref[g] = eb[j * Q:j * Q + 1, sl] * hb_ref[g] + upd

    @pl.when(ph == 1)
    def _outputs():
        @pl.when(c == 0)
        def _():
            hf_ref[...] = jnp.zeros_like(hf_ref)

        ti = lax.broadcasted_iota(jnp.int32, (Q, Q), 0)
        si = lax.broadcasted_iota(jnp.int32, (Q, Q), 1)
        lower = ti >= si
        lane = lax.broadcasted_iota(jnp.int32, (1, LANES), 1)
        first_half = lane < SSM_HEAD_DIM
        gT = gT_ref[...]
        nw = norm_ref[...]

        xs_b = act_ref[:, :D_SSM]
        Cm = act_ref[:, D_SSM + GN:]
        xs = xs_b.astype(f32)
        csT_f = csT[:H]
        l2dt = jnp.log2(dtT)
        rowf = csT_f - l2dt[:H]
        rowb = cbT_b + l2dt[H:]
        cols = jnp.transpose(jnp.concatenate([csT_f, cbT_b], axis=0))
        cb = Cm.astype(f32) * Bm.astype(f32)
        cb_h = cb.astype(bf16)
        cb_l = (cb - cb_h.astype(f32)).astype(bf16)
        diagT = (lax.dot_general(gT, cb_h, NT_DIMS, preferred_element_type=f32)
                 + lax.dot_general(gT, cb_l, NT_DIMS, preferred_element_type=f32))
        ef, eb, wf, coef = _expand_heads(
            [jnp.exp2(csT_f), jnp.exp2(totT[H:] - cbT_b), jnp.exp2(totT[:H] - csT_f) * dtT[:H],
             dskipT_ref[...] + dtT[H:] * diagT], e4_ref)
        xw = (xs * wf).astype(bf16)
        skip = xs * coef

        for j in range(nsub):
            rows = slice(j * Q, (j + 1) * Q)
            last = (j + 1) * Q - 1
            chunk_id = c * nsub + j
            y_groups = []
            for g in range(SSM_GROUPS):
                sl, sn = group_slices(g)
                Cg = Cm[rows, sn]
                Bg = Bm[rows, sn]
                cbm = lax.dot_general(Cg, Bg, NT_DIMS, preferred_element_type=f32)
                y_off = (ef[rows, sl] * _dot(Cg, hf_ref[g].astype(bf16))
                         + eb[rows, sl] * _dot(Cg, hbs_ref[chunk_id, g]))
                pairs = []
                for hp in range(HEADS_PER_GROUP // 2):
                    lo = g * GROUP_WIDTH + hp * LANES
                    x_pair = xs_b[rows, lo:lo + LANES]
                    y_pair = None
                    for k in range(2):
                        hh = g * HEADS_PER_GROUP + 2 * hp + k
                        arg = jnp.where(lower, cols[rows, hh:hh + 1] - rowf[hh:hh + 1, rows],
                                        rowb[hh:hh + 1, rows] - cols[rows, H + hh:H + hh + 1])
                        mat = (cbm * jnp.exp2(arg)).astype(bf16)
                        keep = first_half if k == 0 else jnp.logical_not(first_half)
                        contrib = _dot(mat, jnp.where(keep, x_pair, jnp.zeros_like(x_pair)))
                        y_pair = contrib if y_pair is None else y_pair + contrib
                    pairs.append(y_pair)
                y_groups.append(jnp.concatenate(pairs, axis=1) + y_off)
                upd = lax.dot_general(Bg, xw[rows, sl], TN_DIMS, preferred_element_type=f32)
                hf_ref[g] = ef[last:last + 1, sl] * hf_ref[g] + upd

            y = jnp.concatenate(y_groups, axis=1) + skip[rows]
            y = y * _silu(z_ref[rows, :].astype(f32))
            for g in range(SSM_GROUPS):
                sl, _ = group_slices(g)
                o_ref[rows, sl] = _rms(y[:, sl], nw[:, sl]).astype(o_ref.dtype)


def _ssd_call(act, dtT, z, biasT, aT, dskipT, norm, triu, e4, e2, gT):
    B, L, _ = act.shape
    blk = min(SSD_BLOCK, L)
    Q = min(SSD_Q, blk)
    nsub = blk // Q
    nblk = L // blk

    def bidx(p, c):
        return jnp.where(p == 0, nblk - 1 - c, c)

    consts = (biasT, aT, dskipT, norm, triu, e4, e2, gT)
    in_specs = [
        pl.BlockSpec((None, blk, D_XBC), lambda b, p, c: (b, bidx(p, c), 0)),
        pl.BlockSpec((None, 2 * SSM_HEADS, blk), lambda b, p, c: (b, 0, bidx(p, c))),
        pl.BlockSpec((None, blk, D_SSM), lambda b, p, c: (b, p * c, 0)),
    ] + [_const_spec(a.shape) for a in consts]
    state = (SSM_GROUPS, D_STATE, GROUP_WIDTH)
    return pl.pallas_call(
        functools.partial(_ssd_kernel, nblk=nblk, nsub=nsub, Q=Q),
        grid=(B, 2, nblk),
        in_specs=in_specs,
        out_specs=pl.BlockSpec((None, blk, D_SSM), lambda b, p, c: (b, p * c, 0)),
        out_shape=jax.ShapeDtypeStruct((B, L, D_SSM), bf16),
        scratch_shapes=[pltpu.VMEM(state, f32), pltpu.VMEM(state, f32),
                        pltpu.VMEM((nblk * nsub,) + state, bf16)],
        name="ssd",
        compiler_params=pltpu.CompilerParams(
            dimension_semantics=("parallel", "arbitrary", "arbitrary"), vmem_limit_bytes=VMEM_LIMIT),
    )(act, dtT, z, *consts)


def _oproj_kernel(x_ref, aT_ref, s_ref, wa_ref, ws_ref, o_ref):
    o_ref[...] = (x_ref[...]
                  + lax.dot_general(aT_ref[...], wa_ref[...], TN_DIMS, preferred_element_type=f32)
                  + _dot(s_ref[...], ws_ref[...]))


def _oproj_call(x, attnT, ssm, wa, ws):
    B, L, _ = x.shape
    tm = min(OPROJ_TM, L)
    tok = lambda w: pl.BlockSpec((None, tm, w), lambda b, i: (b, i, 0))
    return pl.pallas_call(
        _oproj_kernel, grid=(B, L // tm),
        in_specs=[tok(D_MODEL), pl.BlockSpec((None, N_HEADS * V_HEAD, tm), lambda b, i: (b, 0, i)),
                  tok(D_SSM), _const_spec(wa.shape), _const_spec(ws.shape)],
        out_specs=tok(D_MODEL),
        out_shape=jax.ShapeDtypeStruct((B, L, D_MODEL), f32),
        name="oproj",
        compiler_params=pltpu.CompilerParams(
            dimension_semantics=("parallel", "parallel"), vmem_limit_bytes=VMEM_LIMIT),
    )(x, attnT, ssm, wa, ws)


def _ffn_kernel(x_ref, xp_ref, xn_ref, n2_ref, wg_ref, wu_ref, cw_ref, cb_ref, wd_ref, fn_ref,
                o_ref, h_scr, act_scr, *, nfc):
    i = pl.program_id(1)
    nt = pl.num_programs(1)
    tm = x_ref.shape[0]
    n2 = n2_ref[...]
    h_scr[0:tm, :] = _rms(x_ref[...], n2).astype(bf16)
    halo = jnp.concatenate([xp_ref[...], xn_ref[...]], axis=0)
    h_scr[tm:tm + 2 * F32_ROWS, :] = _rms(halo, n2).astype(bf16)
    has_prev = i > 0
    has_next = i < nt - 1

    def chunk(cf):
        col = cf * FFN_FC
        g_ext = _dot(h_scr[...], wg_ref[:, pl.ds(col, FFN_FC)])
        before = jnp.where(has_prev, g_ext[tm + F32_ROWS - 1:tm + F32_ROWS], 0.0)
        after = jnp.where(has_next, g_ext[tm + F32_ROWS:tm + F32_ROWS + 1], 0.0)
        gc = _conv3_rows(g_ext[:tm], before, after, cw_ref[:, pl.ds(col, FFN_FC)], cb_ref[:, pl.ds(col, FFN_FC)])
        u = _dot(h_scr[0:tm, :], wu_ref[:, pl.ds(col, FFN_FC)])
        act_scr[:, pl.ds(col, FFN_FC)] = (_silu(gc) * u).astype(bf16)

    for cf in range(nfc):
        chunk(cf)
    y = x_ref[...] + _dot(act_scr[...], wd_ref[...])
    o_ref[...] = _rms(y, fn_ref[...])


def _ffn_call(x, n2, wg, wu, cw, cb, wd, fn):
    B, L, _ = x.shape
    tm = min(FFN_TM, L)
    tok = pl.BlockSpec((None, tm, D_MODEL), lambda b, i: (b, i, 0))
    consts = (n2, wg, wu, cw, cb, wd, fn)
    return pl.pallas_call(
        functools.partial(_ffn_kernel, nfc=D_FF // FFN_FC),
        grid=(B, L // tm),
        in_specs=[tok] + _halo_specs(tm, L, D_MODEL) + [_const_spec(a.shape) for a in consts],
        out_specs=tok,
        out_shape=jax.ShapeDtypeStruct((B, L, D_MODEL), f32),
        scratch_shapes=[pltpu.VMEM((tm + 2 * F32_ROWS, D_MODEL), bf16), pltpu.VMEM((tm, D_FF), bf16)],
        name="ffn",
        compiler_params=pltpu.CompilerParams(
            dimension_semantics=("parallel", "parallel"), vmem_limit_bytes=VMEM_LIMIT),
    )(x, x, x, *consts)


def _head_lane_sources():
    src = np.full((HEAD_PAD,), QK_DIM, np.int32)
    half = HEAD_PAD // 2
    src[0:HALF_ROPE] = QK_NOPE + np.arange(HALF_ROPE)
    src[HALF_ROPE:half] = np.arange(half - HALF_ROPE)
    src[half:half + HALF_ROPE] = QK_NOPE + HALF_ROPE + np.arange(HALF_ROPE)
    n_rest = QK_NOPE - (half - HALF_ROPE)
    src[half + HALF_ROPE:half + HALF_ROPE + n_rest] = (half - HALF_ROPE) + np.arange(n_rest)
    return src


def _rope_tables(L):
    inv = ROPE_THETA ** (-jnp.arange(0, QK_ROPE, 2, dtype=f32) / QK_ROPE)
    ang = jnp.arange(L, dtype=f32)[:, None] * inv[None, :]
    cos, sin = jnp.cos(ang), jnp.sin(ang)
    half = HEAD_PAD // 2
    c_tab = jnp.ones((L, HEAD_PAD), f32)
    c_tab = c_tab.at[:, 0:HALF_ROPE].set(cos).at[:, half:half + HALF_ROPE].set(cos)
    s_tab = jnp.zeros((L, HEAD_PAD), f32)
    s_tab = s_tab.at[:, 0:HALF_ROPE].set(-sin).at[:, half:half + HALF_ROPE].set(sin)
    return c_tab, s_tab


def _expand_matrix(n):
    m = np.zeros((LANES, n * D_SSM), np.float32)
    for part in range(2):
        for j in range(n):
            for h in range(SSM_HEADS):
                r = part * n * SSM_HEADS + j * SSM_HEADS + h
                m[r, j * D_SSM + h * SSM_HEAD_DIM:j * D_SSM + (h + 1) * SSM_HEAD_DIM] = 1.0
    return jnp.asarray(m, bf16)


def _prepare_weights(norm1, w_in, q_a_norm, kv_a_norm, w_q_b, w_kv_b, conv_w, conv_b,
                     dt_bias_f, dt_bias_b, a_log_f, a_log_b, d_skip, ssm_norm, w_out,
                     norm2, w_gate, w_up, ffn_conv_w, ffn_conv_b, w_down, final_norm):
    half = HEAD_PAD // 2
    o_kr = Q_LORA + KV_LORA
    o_z = o_kr + QK_ROPE
    o_dt = o_z + D_SSM + D_XBC
    kr_blk = jnp.zeros((D_MODEL, HEAD_PAD), f32)
    kr_blk = kr_blk.at[:, 0:HALF_ROPE].set(w_in[:, o_kr:o_kr + HALF_ROPE])
    kr_blk = kr_blk.at[:, half:half + HALF_ROPE].set(w_in[:, o_kr + HALF_ROPE:o_kr + QK_ROPE])
    win = jnp.concatenate([w_in[:, :o_kr], kr_blk, w_in[:, o_z:o_dt]], axis=1).astype(bf16)
    w_dt = w_in[:, o_dt:o_dt + 2 * SSM_HEADS]

    src = _head_lane_sources()
    wq = w_q_b.reshape(Q_LORA, N_HEADS, QK_DIM)
    wq = jnp.concatenate([wq, jnp.zeros((Q_LORA, N_HEADS, 1), f32)], axis=-1)[:, :, src]
    wqT = wq.reshape(Q_LORA, N_HEADS * HEAD_PAD).T.astype(bf16)
    wkv = w_kv_b.reshape(KV_LORA, N_HEADS, QK_NOPE + V_HEAD)
    src_k = np.where(src < QK_NOPE, src, QK_NOPE)
    wk = jnp.concatenate([wkv[:, :, :QK_NOPE], jnp.zeros((KV_LORA, N_HEADS, 1), f32)], axis=-1)[:, :, src_k]
    wk = wk.reshape(KV_LORA, N_HEADS * HEAD_PAD).astype(bf16)
    wvT = wkv[:, :, QK_NOPE:].reshape(KV_LORA, N_HEADS * V_HEAD).T.astype(bf16)

    row = lambda v: v.reshape(1, -1).astype(f32)
    col = lambda v: v.reshape(-1, 1).astype(f32)
    a_neg = -jnp.exp(jnp.concatenate([a_log_f, a_log_b]).astype(f32)) * LOG2E
    group_of_lane = np.arange(GN) // D_STATE
    group_of_head = np.arange(SSM_HEADS) // HEADS_PER_GROUP
    gT = jnp.asarray(group_of_head[:, None] == group_of_lane[None, :], bf16)
    return dict(
        n1=row(norm1), win=win, qan=row(q_a_norm), kvan=row(kv_a_norm), wqT=wqT, wk=wk, wvT=wvT,
        wdtT=w_dt.T.astype(bf16), cw=conv_w.astype(f32), cb=row(conv_b),
        biasT=col(jnp.concatenate([dt_bias_f, dt_bias_b])), aT=col(a_neg), dskipT=col(d_skip),
        ssm_norm=row(ssm_norm), e4=_expand_matrix(4), e2=_expand_matrix(2), gT=gT,
        wa=w_out[:N_HEADS * V_HEAD].astype(bf16), ws=w_out[N_HEADS * V_HEAD:].astype(bf16),
        n2=row(norm2), wg=w_gate.astype(bf16), wu=w_up.astype(bf16), fcw=ffn_conv_w.astype(f32),
        fcb=row(ffn_conv_b), wd=w_down.astype(bf16), fn=row(final_norm),
    )


def _encoder(x, w):
    B, L, _ = x.shape
    c_tab, s_tab = _rope_tables(L)
    scale = QK_DIM ** -0.5 * LOG2E
    qT, k, vT, z, act, dtT = _proj_call(
        x, w["n1"], w["win"], w["qan"], w["kvan"], w["wqT"], w["wk"], w["wvT"], w["wdtT"], w["cw"], w["cb"],
        c_tab, s_tab, (c_tab * scale).T, (s_tab * scale).T)
    attnT = _attn_call(qT, k, vT)
    blk = min(SSD_BLOCK, L)
    Q = min(SSD_Q, blk)
    triu = jnp.kron(jnp.eye(blk // Q, dtype=f32), jnp.triu(jnp.ones((Q, Q), f32))).astype(bf16)
    ssm = _ssd_call(act, dtT, z, w["biasT"], w["aT"], w["dskipT"], w["ssm_norm"], triu, w["e4"], w["e2"], w["gT"])
    x1 = _oproj_call(x, attnT, ssm, w["wa"], w["ws"])
    return _ffn_call(x1, w["n2"], w["wg"], w["wu"], w["fcw"], w["fcb"], w["wd"], w["fn"])


def kernel(x_prompt, x_sample, norm1, w_in, q_a_norm, kv_a_norm, w_q_b, w_kv_b, conv_w, conv_b,
           dt_bias_f, dt_bias_b, a_log_f, a_log_b, d_skip, ssm_norm, w_out, norm2, w_gate, w_up,
           ffn_conv_w, ffn_conv_b, w_down, final_norm):
    w = _prepare_weights(norm1[0], w_in[0], q_a_norm[0], kv_a_norm[0], w_q_b[0], w_kv_b[0], conv_w[0],
                         conv_b[0], dt_bias_f[0], dt_bias_b[0], a_log_f[0], a_log_b[0], d_skip[0],
                         ssm_norm[0], w_out[0], norm2[0], w_gate[0], w_up[0], ffn_conv_w[0],
                         ffn_conv_b[0], w_down[0], final_norm)
    return (_encoder(x_prompt, w), _encoder(x_sample, w))
```

```python
import functools

import numpy as np
import jax
import jax.numpy as jnp
from jax import lax
from jax.experimental import pallas as pl
from jax.experimental.pallas import tpu as pltpu

D_MODEL = 1024
N_HEADS = 16
QK_NOPE = 64
QK_ROPE = 32
HALF_ROPE = QK_ROPE // 2
QK_DIM = QK_NOPE + QK_ROPE
V_HEAD = 64
Q_LORA = 384
KV_LORA = 256
ROPE_THETA = 10000.0
SSM_HEADS = 16
SSM_HEAD_DIM = 64
D_SSM = SSM_HEADS * SSM_HEAD_DIM
SSM_GROUPS = 2
HEADS_PER_GROUP = SSM_HEADS // SSM_GROUPS
GROUP_WIDTH = D_SSM // SSM_GROUPS
D_STATE = 64
GN = SSM_GROUPS * D_STATE
D_XBC = D_SSM + 2 * GN
D_FF = 2816
EPS = 1e-6
LOG2E = float(np.log2(np.e))

LANES = 128
HEAD_PAD = LANES
BF16_ROWS = 16
F32_ROWS = 8
VMEM_LIMIT = 56 * 1024 * 1024

OFF_Q = 0
OFF_CKV = OFF_Q + Q_LORA
OFF_KR = OFF_CKV + KV_LORA
OFF_Z = OFF_KR + HEAD_PAD
OFF_XBC = OFF_Z + D_SSM
D_IN_PAD = OFF_XBC + D_XBC

PROJ_TM = 512
ATTN_TQ = 512
ATTN_TK = 512
ATTN_UNROLL = 4
ATTN_HEADS_PER_STEP = 2
SSD_Q = 128
SSD_BLOCK = 512
OPROJ_TM = 512
FFN_TM = 512
FFN_FC = 256

NT_DIMS = (((1,), (1,)), ((), ()))
TN_DIMS = (((0,), (0,)), ((), ()))

f32 = jnp.float32
bf16 = jnp.bfloat16


def _rms(x, w):
    return x * lax.rsqrt(jnp.mean(x * x, axis=-1, keepdims=True) + EPS) * w


def _dot(a, b):
    return jnp.dot(a, b, preferred_element_type=f32)


def _silu(x):
    h = 0.5 * x
    return h * jnp.tanh(h) + h


def _conv3_rows(x, before, after, cw, cb):
    n = x.shape[0]
    sub = lax.broadcasted_iota(jnp.int32, (F32_ROWS, 1), 0)
    down = pltpu.roll(x, 1, axis=0)
    up = pltpu.roll(x, n - 1, axis=0)
    x_prev = jnp.concatenate([jnp.where(sub == 0, before, down[:F32_ROWS]), down[F32_ROWS:]], axis=0)
    x_next = jnp.concatenate([up[:n - F32_ROWS],
                              jnp.where(sub == F32_ROWS - 1, after, up[n - F32_ROWS:])], axis=0)
    return x_prev * cw[0:1] + x * cw[1:2] + x_next * cw[2:3] + cb


def _halo_specs(tm, L, width):
    hb = tm // F32_ROWS
    last = L // F32_ROWS - 1
    return [pl.BlockSpec((None, F32_ROWS, width), lambda b, i: (b, jnp.maximum(i * hb - 1, 0), 0)),
            pl.BlockSpec((None, F32_ROWS, width), lambda b, i: (b, jnp.minimum((i + 1) * hb, last), 0))]


def _const_spec(shape):
    nd = len(shape)
    return pl.BlockSpec(shape, lambda *_: (0,) * nd, pipeline_mode=pl.Buffered(1))


def _proj_kernel(x_ref, xp_ref, xn_ref, n1_ref, win_ref, qan_ref, kvan_ref, wqT_ref, wk_ref, wvT_ref, wdtT_ref,
                 cw_ref, cb_ref, c_ref, s_ref, cT_ref, sT_ref,
                 qT_out, k_out, vT_out, z_out, act_out, dtT_out, h_scr):
    i = pl.program_id(1)
    tm = x_ref.shape[0]
    n1 = n1_ref[...]
    h_scr[0:tm, :] = _rms(x_ref[...], n1).astype(bf16)
    halo = jnp.concatenate([xp_ref[...], xn_ref[...]], axis=0)
    h_scr[tm:tm + 2 * F32_ROWS, :] = _rms(halo, n1).astype(bf16)
    h = h_scr[0:tm, :]
    proj = _dot(h_scr[...], win_ref[...])
    z_out[...] = proj[:tm, OFF_Z:OFF_Z + D_SSM].astype(bf16)
    dtT_out[...] = lax.dot_general(wdtT_ref[...], h, NT_DIMS, preferred_element_type=f32)

    xbc = proj[:, OFF_XBC:OFF_XBC + D_XBC]
    before = jnp.where(i > 0, xbc[tm + F32_ROWS - 1:tm + F32_ROWS], 0.0)
    after = jnp.where(i < pl.num_programs(1) - 1, xbc[tm + F32_ROWS:tm + F32_ROWS + 1], 0.0)
    act_out[...] = _silu(_conv3_rows(xbc[:tm], before, after, cw_ref[...], cb_ref[...])).astype(bf16)

    hq = _rms(proj[:tm, OFF_Q:OFF_Q + Q_LORA], qan_ref[...]).astype(bf16)
    qT = lax.dot_general(wqT_ref[...], hq, NT_DIMS, preferred_element_type=f32)
    cT = cT_ref[...]
    sT = sT_ref[...]
    half = HEAD_PAD // 2
    for hh in range(N_HEADS):
        blk = qT[hh * HEAD_PAD:(hh + 1) * HEAD_PAD, :]
        rot = jnp.concatenate([blk[half:], blk[:half]], axis=0)
        qT_out[hh * HEAD_PAD:(hh + 1) * HEAD_PAD, :] = (blk * cT + rot * sT).astype(bf16)

    hc = _rms(proj[:tm, OFF_CKV:OFF_CKV + KV_LORA], kvan_ref[...]).astype(bf16)
    kn = _dot(hc, wk_ref[...])
    kr = proj[:tm, OFF_KR:OFF_KR + HEAD_PAD]
    krf = kr * c_ref[...] + pltpu.roll(kr, half, axis=1) * s_ref[...]
    for hh in range(N_HEADS):
        k_out[hh] = (kn[:, hh * HEAD_PAD:(hh + 1) * HEAD_PAD] + krf).astype(bf16)
    vT_out[...] = lax.dot_general(wvT_ref[...], hc, NT_DIMS, preferred_element_type=f32).astype(bf16)


def _proj_call(x, n1, win, qan, kvan, wqT, wk, wvT, wdtT, cw, cb, c_tab, s_tab, cT_tab, sT_tab):
    B, L, _ = x.shape
    tm = min(PROJ_TM, L)
    grid = (B, L // tm)
    tok = lambda w: pl.BlockSpec((None, tm, w), lambda b, i: (b, i, 0))
    tokT = lambda w: pl.BlockSpec((None, w, tm), lambda b, i: (b, 0, i))
    consts = (n1, win, qan, kvan, wqT, wk, wvT, wdtT, cw, cb)
    in_specs = [tok(D_MODEL)] + _halo_specs(tm, L, D_MODEL) + [_const_spec(a.shape) for a in consts] + [
        pl.BlockSpec((tm, HEAD_PAD), lambda b, i: (i, 0)),
        pl.BlockSpec((tm, HEAD_PAD), lambda b, i: (i, 0)),
        pl.BlockSpec((HEAD_PAD, tm), lambda b, i: (0, i)),
        pl.BlockSpec((HEAD_PAD, tm), lambda b, i: (0, i)),
    ]
    out_shape = [
        jax.ShapeDtypeStruct((B, N_HEADS * HEAD_PAD, L), bf16),
        jax.ShapeDtypeStruct((B, N_HEADS, L, HEAD_PAD), bf16),
        jax.ShapeDtypeStruct((B, N_HEADS * V_HEAD, L), bf16),
        jax.ShapeDtypeStruct((B, L, D_SSM), bf16),
        jax.ShapeDtypeStruct((B, L, D_XBC), bf16),
        jax.ShapeDtypeStruct((B, 2 * SSM_HEADS, L), f32),
    ]
    k_spec = pl.BlockSpec((None, N_HEADS, tm, HEAD_PAD), lambda b, i: (b, 0, i, 0))
    out_specs = [tokT(N_HEADS * HEAD_PAD), k_spec, tokT(N_HEADS * V_HEAD),
                 tok(D_SSM), tok(D_XBC), tokT(2 * SSM_HEADS)]
    return pl.pallas_call(
        _proj_kernel, grid=grid, in_specs=in_specs, out_specs=out_specs, out_shape=out_shape,
        scratch_shapes=[pltpu.VMEM((tm + 2 * F32_ROWS, D_MODEL), bf16)],
        name="proj",
        compiler_params=pltpu.CompilerParams(
            dimension_semantics=("parallel", "parallel"), vmem_limit_bytes=VMEM_LIMIT),
    )(x, x, x, *consts, c_tab, s_tab, cT_tab, sT_tab)


def _attn_kernel(qT_ref, k_ref, vT_ref, o_ref, s_ref, tmax_ref, m_ref, acc_ref, *, tq, tk, nq, nk, unroll, nh):
    ones = jnp.ones((BF16_ROWS, tk), bf16)
    ng = nk // unroll
    heads = range(nh)

    def q_cols(qi):
        return pl.ds(pl.multiple_of(qi * tq, tq), tq)

    def scores(hd, qi, j, slot):
        start = pl.multiple_of(j * tk, tk)
        s = _dot(k_ref[hd, pl.ds(start, tk), :],
                 qT_ref[hd * HEAD_PAD:(hd + 1) * HEAD_PAD, q_cols(qi)])
        s_ref[hd, slot] = s
        tmax_ref[hd, slot] = jnp.max(s, axis=0, keepdims=True)

    def update(hd, j, slot):
        m = m_ref[hd]
        m_new = jnp.maximum(m, tmax_ref[hd, slot])
        alpha = jnp.exp2(m - m_new)
        p = jnp.exp2(s_ref[hd, slot] - m_new).astype(bf16)
        start = pl.multiple_of(j * tk, tk)
        v = jnp.concatenate([vT_ref[hd * V_HEAD:(hd + 1) * V_HEAD, pl.ds(start, tk)], ones], axis=0)
        acc_ref[hd] = alpha * acc_ref[hd] + _dot(v, p)
        m_ref[hd] = m_new

    def group(qi, g, next_q):
        for u in range(unroll):
            j = g * unroll + u
            nxt = (u + 1) % 2
            for hd in heads:
                if next_q is False or u < unroll - 1:
                    scores(hd, qi, j + 1, nxt)
                elif next_q is not None:
                    scores(hd, next_q, 0, nxt)
            for hd in heads:
                update(hd, j, u % 2)

    def query_tile(qi, next_q):
        m_ref[...] = jnp.full(m_ref.shape, -jnp.inf, f32)
        acc_ref[...] = jnp.zeros(acc_ref.shape, f32)
        if ng > 1:
            def body(g, carry):
                group(qi, g, False)
                return carry
            lax.fori_loop(0, ng - 1, body, 0)
        group(qi, ng - 1, next_q)
        for hd in heads:
            acc = acc_ref[hd]
            o_ref[hd * V_HEAD:(hd + 1) * V_HEAD, q_cols(qi)] = (
                acc[:V_HEAD] / acc[V_HEAD:V_HEAD + 1]).astype(o_ref.dtype)

    for hd in heads:
        scores(hd, 0, 0, 0)
    if nq > 1:
        def q_body(qi, carry):
            query_tile(qi, qi + 1)
            return carry
        lax.fori_loop(0, nq - 1, q_body, 0)
    query_tile(nq - 1, None)


def _attn_call(qT, k, vT):
    B, _, L = qT.shape
    tq = min(ATTN_TQ, L)
    tk = min(ATTN_TK, L)
    nk = L // tk
    unroll = min(ATTN_UNROLL, nk)
    assert unroll % 2 == 0 and nk % unroll == 0, "key tiles alternate between two score buffers"
    nq = L // tq if nk == unroll else 1
    tqb = nq * tq
    nh = ATTN_HEADS_PER_STEP
    return pl.pallas_call(
        functools.partial(_attn_kernel, tq=tq, tk=tk, nq=nq, nk=nk, unroll=unroll, nh=nh),
        grid=(B, N_HEADS // nh, L // tqb),
        scratch_shapes=[pltpu.VMEM((nh, 2, tk, tq), f32), pltpu.VMEM((nh, 2, 1, tq), f32),
                        pltpu.VMEM((nh, 1, tq), f32), pltpu.VMEM((nh, V_HEAD + BF16_ROWS, tq), f32)],
        in_specs=[
            pl.BlockSpec((None, nh * HEAD_PAD, tqb), lambda b, h, i: (b, h, i)),
            pl.BlockSpec((None, nh, L, HEAD_PAD), lambda b, h, i: (b, h, 0, 0)),
            pl.BlockSpec((None, nh * V_HEAD, L), lambda b, h, i: (b, h, 0)),
        ],
        out_specs=pl.BlockSpec((None, nh * V_HEAD, tqb), lambda b, h, i: (b, h, i)),
        out_shape=jax.ShapeDtypeStruct((B, N_HEADS * V_HEAD, L), bf16),
        name="attn",
        compiler_params=pltpu.CompilerParams(
            dimension_semantics=("parallel", "parallel", "arbitrary"), vmem_limit_bytes=VMEM_LIMIT),
    )(qT, k, vT)


def _split3(v):
    a1 = v.astype(bf16)
    r1 = v - a1.astype(f32)
    a2 = r1.astype(bf16)
    a3 = (r1 - a2.astype(f32)).astype(bf16)
    return a1, a2, a3


def _chunk_scalars(dtT_raw, biasT, aT, triu):
    nh2 = 2 * SSM_HEADS
    dtT = jax.nn.softplus(dtT_raw + biasT)
    stepT = dtT * aT
    cs3 = _dot(jnp.concatenate(_split3(stepT), axis=0), triu)
    csT = cs3[0:nh2] + cs3[nh2:2 * nh2] + cs3[2 * nh2:3 * nh2]
    return dtT, stepT, csT


def _expand_heads(partsT, e_ref):
    n = len(partsT)
    stk = jnp.concatenate(partsT, axis=0)
    hi = stk.astype(bf16).astype(f32)
    pieces = [hi, stk - hi]
    pad = LANES - 2 * stk.shape[0]
    if pad:
        pieces.append(jnp.zeros((pad, stk.shape[1]), f32))
    nat = jnp.transpose(jnp.concatenate(pieces, axis=0)).astype(bf16)
    full = _dot(nat, e_ref[...])
    return [full[:, j * D_SSM:(j + 1) * D_SSM] for j in range(n)]


def _ssd_kernel(act_ref, dtT_ref, z_ref, biasT_ref, aT_ref, dskipT_ref, norm_ref, triu_ref, e4_ref, e2_ref,
                gT_ref, o_ref, hf_ref, hb_ref, hbs_ref, *, nblk, nsub, Q):
    ph = pl.program_id(1)
    c = pl.program_id(2)
    H = SSM_HEADS

    def group_slices(g):
        return slice(g * GROUP_WIDTH, (g + 1) * GROUP_WIDTH), slice(g * D_STATE, (g + 1) * D_STATE)

    dtT, stepT, csT = _chunk_scalars(dtT_ref[...], biasT_ref[...], aT_ref[...], triu_ref[...])
    totT = jnp.concatenate(
        [jnp.broadcast_to(csT[:, (j + 1) * Q - 1:(j + 1) * Q], (2 * H, Q)) for j in range(nsub)], axis=1)
    cbT_b = csT[H:] - stepT[H:]
    Bm = act_ref[:, D_SSM:D_SSM + GN]

    @pl.when(ph == 0)
    def _backward_states():
        @pl.when(c == 0)
        def _():
            hb_ref[...] = jnp.zeros_like(hb_ref)

        blk = nblk - 1 - c
        wb, eb = _expand_heads([jnp.exp2(cbT_b) * dtT[H:], jnp.exp2(totT[H:] - cbT_b)], e2_ref)
        xw = (act_ref[:, :D_SSM].astype(f32) * wb).astype(bf16)
        for j in reversed(range(nsub)):
            rows = slice(j * Q, (j + 1) * Q)
            hbs_ref[blk * nsub + j] = hb_ref[...].astype(bf16)
            for g in range(SSM_GROUPS):
                sl, sn = group_slices(g)
                upd = lax.dot_general(Bm[rows, sn], xw[rows, sl], TN_DIMS, preferred_element_type=f32)
                hb_ref[g] = eb[j * Q:j * Q + 1, sl] * hb_ref[g] + upd

    @pl.when(ph == 1)
    def _outputs():
        @pl.when(c == 0)
        def _():
            hf_ref[...] = jnp.zeros_like(hf_ref)

        ti = lax.broadcasted_iota(jnp.int32, (Q, Q), 0)
        si = lax.broadcasted_iota(jnp.int32, (Q, Q), 1)
        lower = ti >= si
        lane = lax.broadcasted_iota(jnp.int32, (1, LANES), 1)
        first_half = lane < SSM_HEAD_DIM
        gT = gT_ref[...]
        nw = norm_ref[...]

        xs_b = act_ref[:, :D_SSM]
        Cm = act_ref[:, D_SSM + GN:]
        xs = xs_b.astype(f32)
        csT_f = csT[:H]
        l2dt = jnp.log2(dtT)
        rowf = csT_f - l2dt[:H]
        rowb = cbT_b + l2dt[H:]
        cols = jnp.transpose(jnp.concatenate([csT_f, cbT_b], axis=0))
        cb = Cm.astype(f32) * Bm.astype(f32)
        cb_h = cb.astype(bf16)
        cb_l = (cb - cb_h.astype(f32)).astype(bf16)
        diagT = (lax.dot_general(gT, cb_h, NT_DIMS, preferred_element_type=f32)
                 + lax.dot_general(gT, cb_l, NT_DIMS, preferred_element_type=f32))
        ef, eb, wf, coef = _expand_heads(
            [jnp.exp2(csT_f), jnp.exp2(totT[H:] - cbT_b), jnp.exp2(totT[:H] - csT_f) * dtT[:H],
             dskipT_ref[...] + dtT[H:] * diagT], e4_ref)
        xw = (xs * wf).astype(bf16)
        skip = xs * coef

        for j in range(nsub):
            rows = slice(j * Q, (j + 1) * Q)
            last = (j + 1) * Q - 1
            chunk_id = c * nsub + j
            y_groups = []
            for g in range(SSM_GROUPS):
                sl, sn = group_slices(g)
                Cg = Cm[rows, sn]
                Bg = Bm[rows, sn]
                cbm = lax.dot_general(Cg, Bg, NT_DIMS, preferred_element_type=f32)
                y_off = (ef[rows, sl] * _dot(Cg, hf_ref[g].astype(bf16))
                         + eb[rows, sl] * _dot(Cg, hbs_ref[chunk_id, g]))
                pairs = []
                for hp in range(HEADS_PER_GROUP // 2):
                    lo = g * GROUP_WIDTH + hp * LANES
                    x_pair = xs_b[rows, lo:lo + LANES]
                    y_pair = None
                    for k in range(2):
                        hh = g * HEADS_PER_GROUP + 2 * hp + k
                        arg = jnp.where(lower, cols[rows, hh:hh + 1] - rowf[hh:hh + 1, rows],
                                        rowb[hh:hh + 1, rows] - cols[rows, H + hh:H + hh + 1])
                        mat = (cbm * jnp.exp2(arg)).astype(bf16)
                        keep = first_half if k == 0 else jnp.logical_not(first_half)
                        contrib = _dot(mat, jnp.where(keep, x_pair, jnp.zeros_like(x_pair)))
                        y_pair = contrib if y_pair is None else y_pair + contrib
                    pairs.append(y_pair)
                y_groups.append(jnp.concatenate(pairs, axis=1) + y_off)
                upd = lax.dot_general(Bg, xw[rows, sl], TN_DIMS, preferred_element_type=f32)
                hf_ref[g] = ef[last:last + 1, sl] * hf_ref[g] + upd

            y = jnp.concatenate(y_groups, axis=1) + skip[rows]
            y = y * _silu(z_ref[rows, :].astype(f32))
            for g in range(SSM_GROUPS):
                sl, _ = group_slices(g)
                o_ref[rows, sl] = _rms(y[:, sl], nw[:, sl]).astype(o_ref.dtype)


def _ssd_call(act, dtT, z, biasT, aT, dskipT, norm, triu, e4, e2, gT):
    B, L, _ = act.shape
    blk = min(SSD_BLOCK, L)
    Q = min(SSD_Q, blk)
    nsub = blk // Q
    nblk = L // blk

    def bidx(p, c):
        return jnp.where(p == 0, nblk - 1 - c, c)

    consts = (biasT, aT, dskipT, norm, triu, e4, e2, gT)
    in_specs = [
        pl.BlockSpec((None, blk, D_XBC), lambda b, p, c: (b, bidx(p, c), 0)),
        pl.BlockSpec((None, 2 * SSM_HEADS, blk), lambda b, p, c: (b, 0, bidx(p, c))),
        pl.BlockSpec((None, blk, D_SSM), lambda b, p, c: (b, p * c, 0)),
    ] + [_const_spec(a.shape) for a in consts]
    state = (SSM_GROUPS, D_STATE, GROUP_WIDTH)
    return pl.pallas_call(
        functools.partial(_ssd_kernel, nblk=nblk, nsub=nsub, Q=Q),
        grid=(B, 2, nblk),
        in_specs=in_specs,
        out_specs=pl.BlockSpec((None, blk, D_SSM), lambda b, p, c: (b, p * c, 0)),
        out_shape=jax.ShapeDtypeStruct((B, L, D_SSM), bf16),
        scratch_shapes=[pltpu.VMEM(state, f32), pltpu.VMEM(state, f32),
                        pltpu.VMEM((nblk * nsub,) + state, bf16)],
        name="ssd",
        compiler_params=pltpu.CompilerParams(
            dimension_semantics=("parallel", "arbitrary", "arbitrary"), vmem_limit_bytes=VMEM_LIMIT),
    )(act, dtT, z, *consts)


def _oproj_kernel(x_ref, aT_ref, s_ref, wa_ref, ws_ref, o_ref):
    o_ref[...] = (x_ref[...]
                  + lax.dot_general(aT_ref[...], wa_ref[...], TN_DIMS, preferred_element_type=f32)
                  + _dot(s_ref[...], ws_ref[...]))


def _oproj_call(x, attnT, ssm, wa, ws):
    B, L, _ = x.shape
    tm = min(OPROJ_TM, L)
    tok = lambda w: pl.BlockSpec((None, tm, w), lambda b, i: (b, i, 0))
    return pl.pallas_call(
        _oproj_kernel, grid=(B, L // tm),
        in_specs=[tok(D_MODEL), pl.BlockSpec((None, N_HEADS * V_HEAD, tm), lambda b, i: (b, 0, i)),
                  tok(D_SSM), _const_spec(wa.shape), _const_spec(ws.shape)],
        out_specs=tok(D_MODEL),
        out_shape=jax.ShapeDtypeStruct((B, L, D_MODEL), f32),
        name="oproj",
        compiler_params=pltpu.CompilerParams(
            dimension_semantics=("parallel", "parallel"), vmem_limit_bytes=VMEM_LIMIT),
    )(x, attnT, ssm, wa, ws)


def _ffn_kernel(x_ref, xp_ref, xn_ref, n2_ref, wg_ref, wu_ref, cw_ref, cb_ref, wd_ref, fn_ref,
                o_ref, h_scr, act_scr, *, nfc):
    i = pl.program_id(1)
    nt = pl.num_programs(1)
    tm = x_ref.shape[0]
    n2 = n2_ref[...]
    h_scr[0:tm, :] = _rms(x_ref[...], n2).astype(bf16)
    halo = jnp.concatenate([xp_ref[...], xn_ref[...]], axis=0)
    h_scr[tm:tm + 2 * F32_ROWS, :] = _rms(halo, n2).astype(bf16)
    has_prev = i > 0
    has_next = i < nt - 1

    def chunk(cf):
        col = cf * FFN_FC
        g_ext = _dot(h_scr[...], wg_ref[:, pl.ds(col, FFN_FC)])
        before = jnp.where(has_prev, g_ext[tm + F32_ROWS - 1:tm + F32_ROWS], 0.0)
        after = jnp.where(has_next, g_ext[tm + F32_ROWS:tm + F32_ROWS + 1], 0.0)
        gc = _conv3_rows(g_ext[:tm], before, after, cw_ref[:, pl.ds(col, FFN_FC)], cb_ref[:, pl.ds(col, FFN_FC)])
        u = _dot(h_scr[0:tm, :], wu_ref[:, pl.ds(col, FFN_FC)])
        act_scr[:, pl.ds(col, FFN_FC)] = (_silu(gc) * u).astype(bf16)

    for cf in range(nfc):
        chunk(cf)
    y = x_ref[...] + _dot(act_scr[...], wd_ref[...])
    o_ref[...] = _rms(y, fn_ref[...])


def _ffn_call(x, n2, wg, wu, cw, cb, wd, fn):
    B, L, _ = x.shape
    tm = min(FFN_TM, L)
    tok = pl.BlockSpec((None, tm, D_MODEL), lambda b, i: (b, i, 0))
    consts = (n2, wg, wu, cw, cb, wd, fn)
    return pl.pallas_call(
        functools.partial(_ffn_kernel, nfc=D_FF // FFN_FC),
        grid=(B, L // tm),
        in_specs=[tok] + _halo_specs(tm, L, D_MODEL) + [_const_spec(a.shape) for a in consts],
        out_specs=tok,
        out_shape=jax.ShapeDtypeStruct((B, L, D_MODEL), f32),
        scratch_shapes=[pltpu.VMEM((tm + 2 * F32_ROWS, D_MODEL), bf16), pltpu.VMEM((tm, D_FF), bf16)],
        name="ffn",
        compiler_params=pltpu.CompilerParams(
            dimension_semantics=("parallel", "parallel"), vmem_limit_bytes=VMEM_LIMIT),
    )(x, x, x, *consts)


def _head_lane_sources():
    src = np.full((HEAD_PAD,), QK_DIM, np.int32)
    half = HEAD_PAD // 2
    src[0:HALF_ROPE] = QK_NOPE + np.arange(HALF_ROPE)
    src[HALF_ROPE:half] = np.arange(half - HALF_ROPE)
    src[half:half + HALF_ROPE] = QK_NOPE + HALF_ROPE + np.arange(HALF_ROPE)
    n_rest = QK_NOPE - (half - HALF_ROPE)
    src[half + HALF_ROPE:half + HALF_ROPE + n_rest] = (half - HALF_ROPE) + np.arange(n_rest)
    return src


def _rope_tables(L):
    inv = ROPE_THETA ** (-jnp.arange(0, QK_ROPE, 2, dtype=f32) / QK_ROPE)
    ang = jnp.arange(L, dtype=f32)[:, None] * inv[None, :]
    cos, sin = jnp.cos(ang), jnp.sin(ang)
    half = HEAD_PAD // 2
    c_tab = jnp.ones((L, HEAD_PAD), f32)
    c_tab = c_tab.at[:, 0:HALF_ROPE].set(cos).at[:, half:half + HALF_ROPE].set(cos)
    s_tab = jnp.zeros((L, HEAD_PAD), f32)
    s_tab = s_tab.at[:, 0:HALF_ROPE].set(-sin).at[:, half:half + HALF_ROPE].set(sin)
    return c_tab, s_tab


def _expand_matrix(n):
    m = np.zeros((LANES, n * D_SSM), np.float32)
    for part in range(2):
        for j in range(n):
            for h in range(SSM_HEADS):
                r = part * n * SSM_HEADS + j * SSM_HEADS + h
                m[r, j * D_SSM + h * SSM_HEAD_DIM:j * D_SSM + (h + 1) * SSM_HEAD_DIM] = 1.0
    return jnp.asarray(m, bf16)


def _prepare_weights(norm1, w_in, q_a_norm, kv_a_norm, w_q_b, w_kv_b, conv_w, conv_b,
                     dt_bias_f, dt_bias_b, a_log_f, a_log_b, d_skip, ssm_norm, w_out,
                     norm2, w_gate, w_up, ffn_conv_w, ffn_conv_b, w_down, final_norm):
    half = HEAD_PAD // 2
    o_kr = Q_LORA + KV_LORA
    o_z = o_kr + QK_ROPE
    o_dt = o_z + D_SSM + D_XBC
    kr_blk = jnp.zeros((D_MODEL, HEAD_PAD), f32)
    kr_blk = kr_blk.at[:, 0:HALF_ROPE].set(w_in[:, o_kr:o_kr + HALF_ROPE])
    kr_blk = kr_blk.at[:, half:half + HALF_ROPE].set(w_in[:, o_kr + HALF_ROPE:o_kr + QK_ROPE])
    win = jnp.concatenate([w_in[:, :o_kr], kr_blk, w_in[:, o_z:o_dt]], axis=1).astype(bf16)
    w_dt = w_in[:, o_dt:o_dt + 2 * SSM_HEADS]

    src = _head_lane_sources()
    wq = w_q_b.reshape(Q_LORA, N_HEADS, QK_DIM)
    wq = jnp.concatenate([wq, jnp.zeros((Q_LORA, N_HEADS, 1), f32)], axis=-1)[:, :, src]
    wqT = wq.reshape(Q_LORA, N_HEADS * HEAD_PAD).T.astype(bf16)
    wkv = w_kv_b.reshape(KV_LORA, N_HEADS, QK_NOPE + V_HEAD)
    src_k = np.where(src < QK_NOPE, src, QK_NOPE)
    wk = jnp.concatenate([wkv[:, :, :QK_NOPE], jnp.zeros((KV_LORA, N_HEADS, 1), f32)], axis=-1)[:, :, src_k]
    wk = wk.reshape(KV_LORA, N_HEADS * HEAD_PAD).astype(bf16)
    wvT = wkv[:, :, QK_NOPE:].reshape(KV_LORA, N_HEADS * V_HEAD).T.astype(bf16)

    row = lambda v: v.reshape(1, -1).astype(f32)
    col = lambda v: v.reshape(-1, 1).astype(f32)
    a_neg = -jnp.exp(jnp.concatenate([a_log_f, a_log_b]).astype(f32)) * LOG2E
    group_of_lane = np.arange(GN) // D_STATE
    group_of_head = np.arange(SSM_HEADS) // HEADS_PER_GROUP
    gT = jnp.asarray(group_of_head[:, None] == group_of_lane[None, :], bf16)
    return dict(
        n1=row(norm1), win=win, qan=row(q_a_norm), kvan=row(kv_a_norm), wqT=wqT, wk=wk, wvT=wvT,
        wdtT=w_dt.T.astype(bf16), cw=conv_w.astype(f32), cb=row(conv_b),
        biasT=col(jnp.concatenate([dt_bias_f, dt_bias_b])), aT=col(a_neg), dskipT=col(d_skip),
        ssm_norm=row(ssm_norm), e4=_expand_matrix(4), e2=_expand_matrix(2), gT=gT,
        wa=w_out[:N_HEADS * V_HEAD].astype(bf16), ws=w_out[N_HEADS * V_HEAD:].astype(bf16),
        n2=row(norm2), wg=w_gate.astype(bf16), wu=w_up.astype(bf16), fcw=ffn_conv_w.astype(f32),
        fcb=row(ffn_conv_b), wd=w_down.astype(bf16), fn=row(final_norm),
    )


def _encoder(x, w):
    B, L, _ = x.shape
    c_tab, s_tab = _rope_tables(L)
    scale = QK_DIM ** -0.5 * LOG2E
    qT, k, vT, z, act, dtT = _proj_call(
        x, w["n1"], w["win"], w["qan"], w["kvan"], w["wqT"], w["wk"], w["wvT"], w["wdtT"], w["cw"], w["cb"],
        c_tab, s_tab, (c_tab * scale).T, (s_tab * scale).T)
    attnT = _attn_call(qT, k, vT)
    blk = min(SSD_BLOCK, L)
    Q = min(SSD_Q, blk)
    triu = jnp.kron(jnp.eye(blk // Q, dtype=f32), jnp.triu(jnp.ones((Q, Q), f32))).astype(bf16)
    ssm = _ssd_call(act, dtT, z, w["biasT"], w["aT"], w["dskipT"], w["ssm_norm"], triu, w["e4"], w["e2"], w["gT"])
    x1 = _oproj_call(x, attnT, ssm, w["wa"], w["ws"])
    return _ffn_call(x1, w["n2"], w["wg"], w["wu"], w["fcw"], w["fcb"], w["wd"], w["fn"])


def kernel(x_prompt, x_sample, norm1, w_in, q_a_norm, kv_a_norm, w_q_b, w_kv_b, conv_w, conv_b,
           dt_bias_f, dt_bias_b, a_log_f, a_log_b, d_skip, ssm_norm, w_out, norm2, w_gate, w_up,
           ffn_conv_w, ffn_conv_b, w_down, final_norm):
    w = _prepare_weights(norm1[0], w_in[0], q_a_norm[0], kv_a_norm[0], w_q_b[0], w_kv_b[0], conv_w[0],
                         conv_b[0], dt_bias_f[0], dt_bias_b[0], a_log_f[0], a_log_b[0], d_skip[0],
                         ssm_norm[0], w_out[0], norm2[0], w_gate[0], w_up[0], ffn_conv_w[0],
                         ffn_conv_b[0], w_down[0], final_norm)
    return (_encoder(x_prompt, w), _encoder(x_sample, w))
```

```python
import functools

import numpy as np
import jax
import jax.numpy as jnp
from jax import lax
from jax.experimental import pallas as pl
from jax.experimental.pallas import tpu as pltpu

D_MODEL = 1024
N_HEADS = 16
QK_NOPE = 64
QK_ROPE = 32
HALF_ROPE = QK_ROPE // 2
QK_DIM = QK_NOPE + QK_ROPE
V_HEAD = 64
Q_LORA = 384
KV_LORA = 256
ROPE_THETA = 10000.0
SSM_HEADS = 16
SSM_HEAD_DIM = 64
D_SSM = SSM_HEADS * SSM_HEAD_DIM
SSM_GROUPS = 2
HEADS_PER_GROUP = SSM_HEADS // SSM_GROUPS
GROUP_WIDTH = D_SSM // SSM_GROUPS
D_STATE = 64
GN = SSM_GROUPS * D_STATE
D_XBC = D_SSM + 2 * GN
D_FF = 2816
EPS = 1e-6
LOG2E = float(np.log2(np.e))

LANES = 128
HEAD_PAD = LANES
BF16_ROWS = 16
F32_ROWS = 8
VMEM_LIMIT = 56 * 1024 * 1024

OFF_Q = 0
OFF_CKV = OFF_Q + Q_LORA
OFF_KR = OFF_CKV + KV_LORA
OFF_Z = OFF_KR + HEAD_PAD
OFF_XBC = OFF_Z + D_SSM
D_IN_PAD = OFF_XBC + D_XBC

PROJ_TM = 512
ATTN_TQ = 512
ATTN_TK = 512
ATTN_UNROLL = 4
ATTN_HEADS_PER_STEP = 2
SSD_Q = 128
SSD_BLOCK = 512
OPROJ_TM = 512
FFN_TM = 512
FFN_FC = 256

NT_DIMS = (((1,), (1,)), ((), ()))
TN_DIMS = (((0,), (0,)), ((), ()))

f32 = jnp.float32
bf16 = jnp.bfloat16


def _rms(x, w):
    return x * lax.rsqrt(jnp.mean(x * x, axis=-1, keepdims=True) + EPS) * w


def _dot(a, b):
    return jnp.dot(a, b, preferred_element_type=f32)


def _silu(x):
    h = 0.5 * x
    return h * jnp.tanh(h) + h


def _conv3_rows(x, before, after, cw, cb):
    n = x.shape[0]
    sub = lax.broadcasted_iota(jnp.int32, (F32_ROWS, 1), 0)
    down = pltpu.roll(x, 1, axis=0)
    up = pltpu.roll(x, n - 1, axis=0)
    x_prev = jnp.concatenate([jnp.where(sub == 0, before, down[:F32_ROWS]), down[F32_ROWS:]], axis=0)
    x_next = jnp.concatenate([up[:n - F32_ROWS],
                              jnp.where(sub == F32_ROWS - 1, after, up[n - F32_ROWS:])], axis=0)
    return x_prev * cw[0:1] + x * cw[1:2] + x_next * cw[2:3] + cb


def _halo_specs(tm, L, width):
    hb = tm // F32_ROWS
    last = L // F32_ROWS - 1
    return [pl.BlockSpec((None, F32_ROWS, width), lambda b, i: (b, jnp.maximum(i * hb - 1, 0), 0)),
            pl.BlockSpec((None, F32_ROWS, width), lambda b, i: (b, jnp.minimum((i + 1) * hb, last), 0))]


def _const_spec(shape):
    nd = len(shape)
    return pl.BlockSpec(shape, lambda *_: (0,) * nd, pipeline_mode=pl.Buffered(1))


def _proj_kernel(x_ref, xp_ref, xn_ref, n1_ref, win_ref, qan_ref, kvan_ref, wqT_ref, wk_ref, wvT_ref, wdtT_ref,
                 cw_ref, cb_ref, c_ref, s_ref, cT_ref, sT_ref,
                 qT_out, k_out, vT_out, z_out, act_out, dtT_out, h_scr):
    i = pl.program_id(1)
    tm = x_ref.shape[0]
    n1 = n1_ref[...]
    h_scr[0:tm, :] = _rms(x_ref[...], n1).astype(bf16)
    halo = jnp.concatenate([xp_ref[...], xn_ref[...]], axis=0)
    h_scr[tm:tm + 2 * F32_ROWS, :] = _rms(halo, n1).astype(bf16)
    h = h_scr[0:tm, :]
    half = HEAD_PAD // 2
    lat = _dot(h, win_ref[:, OFF_Q:OFF_Z])
    xbc = _dot(h_scr[...], win_ref[:, OFF_XBC:OFF_XBC + D_XBC])
    hq = _rms(lat[:, OFF_Q:OFF_Q + Q_LORA], qan_ref[...]).astype(bf16)
    hc = _rms(lat[:, OFF_CKV:OFF_CKV + KV_LORA], kvan_ref[...]).astype(bf16)
    qT = lax.dot_general(wqT_ref[...], hq, NT_DIMS, preferred_element_type=f32)

    before = jnp.where(i > 0, xbc[tm + F32_ROWS - 1:tm + F32_ROWS], 0.0)
    after = jnp.where(i < pl.num_programs(1) - 1, xbc[tm + F32_ROWS:tm + F32_ROWS + 1], 0.0)
    act_out[...] = _silu(_conv3_rows(xbc[:tm], before, after, cw_ref[...], cb_ref[...])).astype(bf16)

    kn = _dot(hc, wk_ref[...])
    cT = cT_ref[...]
    sT = sT_ref[...]
    for hh in range(N_HEADS):
        blk = qT[hh * HEAD_PAD:(hh + 1) * HEAD_PAD, :]
        rot = jnp.concatenate([blk[half:], blk[:half]], axis=0)
        qT_out[hh * HEAD_PAD:(hh + 1) * HEAD_PAD, :] = (blk * cT + rot * sT).astype(bf16)

    z_out[...] = _dot(h, win_ref[:, OFF_Z:OFF_Z + D_SSM]).astype(bf16)
    kr = lat[:, OFF_KR:OFF_KR + HEAD_PAD]
    krf = kr * c_ref[...] + pltpu.roll(kr, half, axis=1) * s_ref[...]
    for hh in range(N_HEADS):
        k_out[hh] = (kn[:, hh * HEAD_PAD:(hh + 1) * HEAD_PAD] + krf).astype(bf16)
    vT_out[...] = lax.dot_general(wvT_ref[...], hc, NT_DIMS, preferred_element_type=f32).astype(bf16)
    dtT_out[...] = lax.dot_general(wdtT_ref[...], h, NT_DIMS, preferred_element_type=f32)


def _proj_call(x, n1, win, qan, kvan, wqT, wk, wvT, wdtT, cw, cb, c_tab, s_tab, cT_tab, sT_tab):
    B, L, _ = x.shape
    tm = min(PROJ_TM, L)
    grid = (B, L // tm)
    tok = lambda w: pl.BlockSpec((None, tm, w), lambda b, i: (b, i, 0))
    tokT = lambda w: pl.BlockSpec((None, w, tm), lambda b, i: (b, 0, i))
    consts = (n1, win, qan, kvan, wqT, wk, wvT, wdtT, cw, cb)
    in_specs = [tok(D_MODEL)] + _halo_specs(tm, L, D_MODEL) + [_const_spec(a.shape) for a in consts] + [
        pl.BlockSpec((tm, HEAD_PAD), lambda b, i: (i, 0)),
        pl.BlockSpec((tm, HEAD_PAD), lambda b, i: (i, 0)),
        pl.BlockSpec((HEAD_PAD, tm), lambda b, i: (0, i)),
        pl.BlockSpec((HEAD_PAD, tm), lambda b, i: (0, i)),
    ]
    out_shape = [
        jax.ShapeDtypeStruct((B, N_HEADS * HEAD_PAD, L), bf16),
        jax.ShapeDtypeStruct((B, N_HEADS, L, HEAD_PAD), bf16),
        jax.ShapeDtypeStruct((B, N_HEADS * V_HEAD, L), bf16),
        jax.ShapeDtypeStruct((B, L, D_SSM), bf16),
        jax.ShapeDtypeStruct((B, L, D_XBC), bf16),
        jax.ShapeDtypeStruct((B, 2 * SSM_HEADS, L), f32),
    ]
    k_spec = pl.BlockSpec((None, N_HEADS, tm, HEAD_PAD), lambda b, i: (b, 0, i, 0))
    out_specs = [tokT(N_HEADS * HEAD_PAD), k_spec, tokT(N_HEADS * V_HEAD),
                 tok(D_SSM), tok(D_XBC), tokT(2 * SSM_HEADS)]
    return pl.pallas_call(
        _proj_kernel, grid=grid, in_specs=in_specs, out_specs=out_specs, out_shape=out_shape,
        scratch_shapes=[pltpu.VMEM((tm + 2 * F32_ROWS, D_MODEL), bf16)],
        name="proj",
        compiler_params=pltpu.CompilerParams(
            dimension_semantics=("parallel", "parallel"), vmem_limit_bytes=VMEM_LIMIT),
    )(x, x, x, *consts, c_tab, s_tab, cT_tab, sT_tab)


def _attn_kernel(qT_ref, k_ref, vT_ref, o_ref, s_ref, tmax_ref, m_ref, acc_ref, *, tq, tk, nq, nk, unroll, nh):
    ones = jnp.ones((BF16_ROWS, tk), bf16)
    ng = nk // unroll
    heads = range(nh)

    def q_cols(qi):
        return pl.ds(pl.multiple_of(qi * tq, tq), tq)

    def scores(hd, qi, j, slot):
        start = pl.multiple_of(j * tk, tk)
        s = _dot(k_ref[hd, pl.ds(start, tk), :],
                 qT_ref[hd * HEAD_PAD:(hd + 1) * HEAD_PAD, q_cols(qi)])
        s_ref[hd, slot] = s
        tmax_ref[hd, slot] = jnp.max(s, axis=0, keepdims=True)

    def update(hd, j, slot):
        m = m_ref[hd]
        m_new = jnp.maximum(m, tmax_ref[hd, slot])
        alpha = jnp.exp2(m - m_new)
        nb = tk // 2
        p = jnp.concatenate([jnp.exp2((s_ref[hd, slot, :nb, :] - m_new).astype(bf16)),
                             jnp.exp2(s_ref[hd, slot, nb:, :] - m_new).astype(bf16)], axis=0)
        start = pl.multiple_of(j * tk, tk)
        v = jnp.concatenate([vT_ref[hd * V_HEAD:(hd + 1) * V_HEAD, pl.ds(start, tk)], ones], axis=0)
        acc_ref[hd] = alpha * acc_ref[hd] + _dot(v, p)
        m_ref[hd] = m_new

    def group(qi, g, next_q):
        for u in range(unroll):
            j = g * unroll + u
            nxt = (u + 1) % 2
            for hd in heads:
                if next_q is False or u < unroll - 1:
                    scores(hd, qi, j + 1, nxt)
                elif next_q is not None:
                    scores(hd, next_q, 0, nxt)
            for hd in heads:
                update(hd, j, u % 2)

    def query_tile(qi, next_q):
        m_ref[...] = jnp.full(m_ref.shape, -jnp.inf, f32)
        acc_ref[...] = jnp.zeros(acc_ref.shape, f32)
        if ng > 1:
            def body(g, carry):
                group(qi, g, False)
                return carry
            lax.fori_loop(0, ng - 1, body, 0)
        group(qi, ng - 1, next_q)
        for hd in heads:
            acc = acc_ref[hd]
            o_ref[hd * V_HEAD:(hd + 1) * V_HEAD, q_cols(qi)] = (
                acc[:V_HEAD] / acc[V_HEAD:V_HEAD + 1]).astype(o_ref.dtype)

    for hd in heads:
        scores(hd, 0, 0, 0)
    if nq > 1:
        def q_body(qi, carry):
            query_tile(qi, qi + 1)
            return carry
        lax.fori_loop(0, nq - 1, q_body, 0)
    query_tile(nq - 1, None)


def _attn_call(qT, k, vT):
    B, _, L = qT.shape
    tq = min(ATTN_TQ, L)
    tk = min(ATTN_TK, L)
    nk = L // tk
    unroll = min(ATTN_UNROLL, nk)
    assert unroll % 2 == 0 and nk % unroll == 0, "key tiles alternate between two score buffers"
    nq = L // tq if nk == unroll else 1
    tqb = nq * tq
    nh = ATTN_HEADS_PER_STEP
    return pl.pallas_call(
        functools.partial(_attn_kernel, tq=tq, tk=tk, nq=nq, nk=nk, unroll=unroll, nh=nh),
        grid=(B, N_HEADS // nh, L // tqb),
        scratch_shapes=[pltpu.VMEM((nh, 2, tk, tq), f32), pltpu.VMEM((nh, 2, 1, tq), f32),
                        pltpu.VMEM((nh, 1, tq), f32), pltpu.VMEM((nh, V_HEAD + BF16_ROWS, tq), f32)],
        in_specs=[
            pl.BlockSpec((None, nh * HEAD_PAD, tqb), lambda b, h, i: (b, h, i)),
            pl.BlockSpec((None, nh, L, HEAD_PAD), lambda b, h, i: (b, h, 0, 0)),
            pl.BlockSpec((None, nh * V_HEAD, L), lambda b, h, i: (b, h, 0)),
        ],
        out_specs=pl.BlockSpec((None, nh * V_HEAD, tqb), lambda b, h, i: (b, h, i)),
        out_shape=jax.ShapeDtypeStruct((B, N_HEADS * V_HEAD, L), bf16),
        name="attn",
        compiler_params=pltpu.CompilerParams(
            dimension_semantics=("parallel", "parallel", "arbitrary"), vmem_limit_bytes=VMEM_LIMIT),
    )(qT, k, vT)


def _split3(v):
    a1 = v.astype(bf16)
    r1 = v - a1.astype(f32)
    a2 = r1.astype(bf16)
    a3 = (r1 - a2.astype(f32)).astype(bf16)
    return a1, a2, a3


def _chunk_scalars(dtT_raw, biasT, aT, triu):
    nh2 = 2 * SSM_HEADS
    dtT = jax.nn.softplus(dtT_raw + biasT)
    stepT = dtT * aT
    cs3 = _dot(jnp.concatenate(_split3(stepT), axis=0), triu)
    csT = cs3[0:nh2] + cs3[nh2:2 * nh2] + cs3[2 * nh2:3 * nh2]
    return dtT, stepT, csT


def _expand_heads(partsT, e_ref):
    n = len(partsT)
    stk = jnp.concatenate(partsT, axis=0)
    hi = stk.astype(bf16).astype(f32)
    pieces = [hi, stk - hi]
    pad = LANES - 2 * stk.shape[0]
    if pad:
        pieces.append(jnp.zeros((pad, stk.shape[1]), f32))
    nat = jnp.transpose(jnp.concatenate(pieces, axis=0)).astype(bf16)
    full = _dot(nat, e_ref[...])
    return [full[:, j * D_SSM:(j + 1) * D_SSM] for j in range(n)]


def _ssd_kernel(act_ref, dtT_ref, z_ref, biasT_ref, aT_ref, dskipT_ref, norm_ref, triu_ref, e4_ref, e2_ref,
                gT_ref, o_ref, hf_ref, hb_ref, hbs_ref, *, nblk, nsub, Q):
    ph = pl.program_id(1)
    c = pl.program_id(2)
    H = SSM_HEADS

    def group_slices(g):
        return slice(g * GROUP_WIDTH, (g + 1) * GROUP_WIDTH), slice(g * D_STATE, (g + 1) * D_STATE)

    dtT, stepT, csT = _chunk_scalars(dtT_ref[...], biasT_ref[...], aT_ref[...], triu_ref[...])
    totT = jnp.concatenate(
        [jnp.broadcast_to(csT[:, (j + 1) * Q - 1:(j + 1) * Q], (2 * H, Q)) for j in range(nsub)], axis=1)
    cbT_b = csT[H:] - stepT[H:]
    Bm = act_ref[:, D_SSM:D_SSM + GN]

    @pl.when(ph == 0)
    def _backward_states():
        @pl.when(c == 0)
        def _():
            hb_ref[...] = jnp.zeros_like(hb_ref)

        blk = nblk - 1 - c
        wb, eb = _expand_heads([jnp.exp2(cbT_b) * dtT[H:], jnp.exp2(totT[H:] - cbT_b)], e2_ref)
        xw = (act_ref[:, :D_SSM].astype(f32) * wb).astype(bf16)
        for j in reversed(range(nsub)):
            rows = slice(j * Q, (j + 1) * Q)
            hbs_ref[blk * nsub + j] = hb_ref[...].astype(bf16)
            for g in range(SSM_GROUPS):
                sl, sn = group_slices(g)
                upd = lax.dot_general(Bm[rows, sn], xw[rows, sl], TN_DIMS, preferred_element_type=f32)
                hb_ref[g] = eb[j * Q:j * Q + 1, sl] * hb_ref[g] + upd

    @pl.when(ph == 1)
    def _outputs():
        @pl.when(c == 0)
        def _():
            hf_ref[...] = jnp.zeros_like(hf_ref)

        ti = lax.broadcasted_iota(jnp.int32, (Q, Q), 0)
        si = lax.broadcasted_iota(jnp.int32, (Q, Q), 1)
        lower = ti >= si
        lane = lax.broadcasted_iota(jnp.int32, (1, LANES), 1)
        first_half = lane < SSM_HEAD_DIM
        gT = gT_ref[...]
        nw = norm_ref[...]

        xs_b = act_ref[:, :D_SSM]
        Cm = act_ref[:, D_SSM + GN:]
        xs = xs_b.astype(f32)
        csT_f = csT[:H]
        l2dt = jnp.log2(dtT)
        rowf = csT_f - l2dt[:H]
        rowb = cbT_b + l2dt[H:]
        cols = jnp.transpose(jnp.concatenate([csT_f, cbT_b], axis=0))
        cb = Cm.astype(f32) * Bm.astype(f32)
        cb_h = cb.astype(bf16)
        cb_l = (cb - cb_h.astype(f32)).astype(bf16)
        diagT = (lax.dot_general(gT, cb_h, NT_DIMS, preferred_element_type=f32)
                 + lax.dot_general(gT, cb_l, NT_DIMS, preferred_element_type=f32))
        ef, eb, wf, coef = _expand_heads(
            [jnp.exp2(csT_f), jnp.exp2(totT[H:] - cbT_b), jnp.exp2(totT[:H] - csT_f) * dtT[:H],
             dskipT_ref[...] + dtT[H:] * diagT], e4_ref)
        xw = (xs * wf).astype(bf16)
        skip = xs * coef

        for j in range(nsub):
            rows = slice(j * Q, (j + 1) * Q)
            last = (j + 1) * Q - 1
            chunk_id = c * nsub + j
            y_groups = []
            for g in range(SSM_GROUPS):
                sl, sn = group_slices(g)
                Cg = Cm[rows, sn]
                Bg = Bm[rows, sn]
                cbm = lax.dot_general(Cg, Bg, NT_DIMS, preferred_element_type=f32)
                y_off = (ef[rows, sl] * _dot(Cg, hf_ref[g].astype(bf16))
                         + eb[rows, sl] * _dot(Cg, hbs_ref[chunk_id, g]))
                pairs = []
                for hp in range(HEADS_PER_GROUP // 2):
                    lo = g * GROUP_WIDTH + hp * LANES
                    x_pair = xs_b[rows, lo:lo + LANES]
                    y_pair = None
                    for k in range(2):
                        hh = g * HEADS_PER_GROUP + 2 * hp + k
                        arg = jnp.where(lower, cols[rows, hh:hh + 1] - rowf[hh:hh + 1, rows],
                                        rowb[hh:hh + 1, rows] - cols[rows, H + hh:H + hh + 1])
                        mat = (cbm * jnp.exp2(arg)).astype(bf16)
                        keep = first_half if k == 0 else jnp.logical_not(first_half)
                        contrib = _dot(mat, jnp.where(keep, x_pair, jnp.zeros_like(x_pair)))
                        y_pair = contrib if y_pair is None else y_pair + contrib
                    pairs.append(y_pair)
                y_groups.append(jnp.concatenate(pairs, axis=1) + y_off)
                upd = lax.dot_general(Bg, xw[rows, sl], TN_DIMS, preferred_element_type=f32)
                hf_ref[g] = ef[last:last + 1, sl] * hf_ref[g] + upd

            y = jnp.concatenate(y_groups, axis=1) + skip[rows]
            y = y * _silu(z_ref[rows, :].astype(f32))
            for g in range(SSM_GROUPS):
                sl, _ = group_slices(g)
                o_ref[rows, sl] = _rms(y[:, sl], nw[:, sl]).astype(o_ref.dtype)


def _ssd_call(act, dtT, z, biasT, aT, dskipT, norm, triu, e4, e2, gT):
    B, L, _ = act.shape
    blk = min(SSD_BLOCK, L)
    Q = min(SSD_Q, blk)
    nsub = blk // Q
    nblk = L // blk

    def bidx(p, c):
        return jnp.where(p == 0, nblk - 1 - c, c)

    consts = (biasT, aT, dskipT, norm, triu, e4, e2, gT)
    in_specs = [
        pl.BlockSpec((None, blk, D_XBC), lambda b, p, c: (b, bidx(p, c), 0)),
        pl.BlockSpec((None, 2 * SSM_HEADS, blk), lambda b, p, c: (b, 0, bidx(p, c))),
        pl.BlockSpec((None, blk, D_SSM), lambda b, p, c: (b, p * c, 0)),
    ] + [_const_spec(a.shape) for a in consts]
    state = (SSM_GROUPS, D_STATE, GROUP_WIDTH)
    return pl.pallas_call(
        functools.partial(_ssd_kernel, nblk=nblk, nsub=nsub, Q=Q),
        grid=(B, 2, nblk),
        in_specs=in_specs,
        out_specs=pl.BlockSpec((None, blk, D_SSM), lambda b, p, c: (b, p * c, 0)),
        out_shape=jax.ShapeDtypeStruct((B, L, D_SSM), bf16),
        scratch_shapes=[pltpu.VMEM(state, f32), pltpu.VMEM(state, f32),
                        pltpu.VMEM((nblk * nsub,) + state, bf16)],
        name="ssd",
        compiler_params=pltpu.CompilerParams(
            dimension_semantics=("parallel", "arbitrary", "arbitrary"), vmem_limit_bytes=VMEM_LIMIT),
    )(act, dtT, z, *consts)


def _oproj_kernel(x_ref, aT_ref, s_ref, wa_ref, ws_ref, o_ref):
    o_ref[...] = (x_ref[...]
                  + lax.dot_general(aT_ref[...], wa_ref[...], TN_DIMS, preferred_element_type=f32)
                  + _dot(s_ref[...], ws_ref[...]))


def _oproj_call(x, attnT, ssm, wa, ws):
    B, L, _ = x.shape
    tm = min(OPROJ_TM, L)
    tok = lambda w: pl.BlockSpec((None, tm, w), lambda b, i: (b, i, 0))
    return pl.pallas_call(
        _oproj_kernel, grid=(B, L // tm),
        in_specs=[tok(D_MODEL), pl.BlockSpec((None, N_HEADS * V_HEAD, tm), lambda b, i: (b, 0, i)),
                  tok(D_SSM), _const_spec(wa.shape), _const_spec(ws.shape)],
        out_specs=tok(D_MODEL),
        out_shape=jax.ShapeDtypeStruct((B, L, D_MODEL), f32),
        name="oproj",
        compiler_params=pltpu.CompilerParams(
            dimension_semantics=("parallel", "parallel"), vmem_limit_bytes=VMEM_LIMIT),
    )(x, attnT, ssm, wa, ws)


def _ffn_kernel(x_ref, xp_ref, xn_ref, n2_ref, wg_ref, wu_ref, cw_ref, cb_ref, wd_ref, fn_ref,
                o_ref, h_scr, act_scr, *, nfc):
    i = pl.program_id(1)
    nt = pl.num_programs(1)
    tm = x_ref.shape[0]
    n2 = n2_ref[...]
    h_scr[0:tm, :] = _rms(x_ref[...], n2).astype(bf16)
    halo = jnp.concatenate([xp_ref[...], xn_ref[...]], axis=0)
    h_scr[tm:tm + 2 * F32_ROWS, :] = _rms(halo, n2).astype(bf16)
    has_prev = i > 0
    has_next = i < nt - 1

    def chunk(cf):
        col = cf * FFN_FC
        g_ext = _dot(h_scr[...], wg_ref[:, pl.ds(col, FFN_FC)])
        before = jnp.where(has_prev, g_ext[tm + F32_ROWS - 1:tm + F32_ROWS], 0.0)
        after = jnp.where(has_next, g_ext[tm + F32_ROWS:tm + F32_ROWS + 1], 0.0)
        gc = _conv3_rows(g_ext[:tm], before, after, cw_ref[:, pl.ds(col, FFN_FC)], cb_ref[:, pl.ds(col, FFN_FC)])
        u = _dot(h_scr[0:tm, :], wu_ref[:, pl.ds(col, FFN_FC)])
        act_scr[:, pl.ds(col, FFN_FC)] = (_silu(gc) * u).astype(bf16)

    for cf in range(nfc):
        chunk(cf)
    y = x_ref[...] + _dot(act_scr[...], wd_ref[...])
    o_ref[...] = _rms(y, fn_ref[...])


def _ffn_call(x, n2, wg, wu, cw, cb, wd, fn):
    B, L, _ = x.shape
    tm = min(FFN_TM, L)
    tok = pl.BlockSpec((None, tm, D_MODEL), lambda b, i: (b, i, 0))
    consts = (n2, wg, wu, cw, cb, wd, fn)
    return pl.pallas_call(
        functools.partial(_ffn_kernel, nfc=D_FF // FFN_FC),
        grid=(B, L // tm),
        in_specs=[tok] + _halo_specs(tm, L, D_MODEL) + [_const_spec(a.shape) for a in consts],
        out_specs=tok,
        out_shape=jax.ShapeDtypeStruct((B, L, D_MODEL), f32),
        scratch_shapes=[pltpu.VMEM((tm + 2 * F32_ROWS, D_MODEL), bf16), pltpu.VMEM((tm, D_FF), bf16)],
        name="ffn",
        compiler_params=pltpu.CompilerParams(
            dimension_semantics=("parallel", "parallel"), vmem_limit_bytes=VMEM_LIMIT),
    )(x, x, x, *consts)


def _head_lane_sources():
    src = np.full((HEAD_PAD,), QK_DIM, np.int32)
    half = HEAD_PAD // 2
    src[0:HALF_ROPE] = QK_NOPE + np.arange(HALF_ROPE)
    src[HALF_ROPE:half] = np.arange(half - HALF_ROPE)
    src[half:half + HALF_ROPE] = QK_NOPE + HALF_ROPE + np.arange(HALF_ROPE)
    n_rest = QK_NOPE - (half - HALF_ROPE)
    src[half + HALF_ROPE:half + HALF_ROPE + n_rest] = (half - HALF_ROPE) + np.arange(n_rest)
    return src


def _rope_tables(L):
    inv = ROPE_THETA ** (-jnp.arange(0, QK_ROPE, 2, dtype=f32) / QK_ROPE)
    ang = jnp.arange(L, dtype=f32)[:, None] * inv[None, :]
    cos, sin = jnp.cos(ang), jnp.sin(ang)
    half = HEAD_PAD // 2
    c_tab = jnp.ones((L, HEAD_PAD), f32)
    c_tab = c_tab.at[:, 0:HALF_ROPE].set(cos).at[:, half:half + HALF_ROPE].set(cos)
    s_tab = jnp.zeros((L, HEAD_PAD), f32)
    s_tab = s_tab.at[:, 0:HALF_ROPE].set(-sin).at[:, half:half + HALF_ROPE].set(sin)
    return c_tab, s_tab


def _expand_matrix(n):
    m = np.zeros((LANES, n * D_SSM), np.float32)
    for part in range(2):
        for j in range(n):
            for h in range(SSM_HEADS):
                r = part * n * SSM_HEADS + j * SSM_HEADS + h
                m[r, j * D_SSM + h * SSM_HEAD_DIM:j * D_SSM + (h + 1) * SSM_HEAD_DIM] = 1.0
    return jnp.asarray(m, bf16)


def _prepare_weights(norm1, w_in, q_a_norm, kv_a_norm, w_q_b, w_kv_b, conv_w, conv_b,
                     dt_bias_f, dt_bias_b, a_log_f, a_log_b, d_skip, ssm_norm, w_out,
                     norm2, w_gate, w_up, ffn_conv_w, ffn_conv_b, w_down, final_norm):
    half = HEAD_PAD // 2
    o_kr = Q_LORA + KV_LORA
    o_z = o_kr + QK_ROPE
    o_dt = o_z + D_SSM + D_XBC
    kr_blk = jnp.zeros((D_MODEL, HEAD_PAD), f32)
    kr_blk = kr_blk.at[:, 0:HALF_ROPE].set(w_in[:, o_kr:o_kr + HALF_ROPE])
    kr_blk = kr_blk.at[:, half:half + HALF_ROPE].set(w_in[:, o_kr + HALF_ROPE:o_kr + QK_ROPE])
    win = jnp.concatenate([w_in[:, :o_kr], kr_blk, w_in[:, o_z:o_dt]], axis=1).astype(bf16)
    w_dt = w_in[:, o_dt:o_dt + 2 * SSM_HEADS]

    src = _head_lane_sources()
    wq = w_q_b.reshape(Q_LORA, N_HEADS, QK_DIM)
    wq = jnp.concatenate([wq, jnp.zeros((Q_LORA, N_HEADS, 1), f32)], axis=-1)[:, :, src]
    wqT = wq.reshape(Q_LORA, N_HEADS * HEAD_PAD).T.astype(bf16)
    wkv = w_kv_b.reshape(KV_LORA, N_HEADS, QK_NOPE + V_HEAD)
    src_k = np.where(src < QK_NOPE, src, QK_NOPE)
    wk = jnp.concatenate([wkv[:, :, :QK_NOPE], jnp.zeros((KV_LORA, N_HEADS, 1), f32)], axis=-1)[:, :, src_k]
    wk = wk.reshape(KV_LORA, N_HEADS * HEAD_PAD).astype(bf16)
    wvT = wkv[:, :, QK_NOPE:].reshape(KV_LORA, N_HEADS * V_HEAD).T.astype(bf16)

    row = lambda v: v.reshape(1, -1).astype(f32)
    col = lambda v: v.reshape(-1, 1).astype(f32)
    a_neg = -jnp.exp(jnp.concatenate([a_log_f, a_log_b]).astype(f32)) * LOG2E
    group_of_lane = np.arange(GN) // D_STATE
    group_of_head = np.arange(SSM_HEADS) // HEADS_PER_GROUP
    gT = jnp.asarray(group_of_head[:, None] == group_of_lane[None, :], bf16)
    return dict(
        n1=row(norm1), win=win, qan=row(q_a_norm), kvan=row(kv_a_norm), wqT=wqT, wk=wk, wvT=wvT,
        wdtT=w_dt.T.astype(bf16), cw=conv_w.astype(f32), cb=row(conv_b),
        biasT=col(jnp.concatenate([dt_bias_f, dt_bias_b])), aT=col(a_neg), dskipT=col(d_skip),
        ssm_norm=row(ssm_norm), e4=_expand_matrix(4), e2=_expand_matrix(2), gT=gT,
        wa=w_out[:N_HEADS * V_HEAD].astype(bf16), ws=w_out[N_HEADS * V_HEAD:].astype(bf16),
        n2=row(norm2), wg=w_gate.astype(bf16), wu=w_up.astype(bf16), fcw=ffn_conv_w.astype(f32),
        fcb=row(ffn_conv_b), wd=w_down.astype(bf16), fn=row(final_norm),
    )


def _encoder(x, w):
    B, L, _ = x.shape
    c_tab, s_tab = _rope_tables(L)
    scale = QK_DIM ** -0.5 * LOG2E
    qT, k, vT, z, act, dtT = _proj_call(
        x, w["n1"], w["win"], w["qan"], w["kvan"], w["wqT"], w["wk"], w["wvT"], w["wdtT"], w["cw"], w["cb"],
        c_tab, s_tab, (c_tab * scale).T, (s_tab * scale).T)
    attnT = _attn_call(qT, k, vT)
    blk = min(SSD_BLOCK, L)
    Q = min(SSD_Q, blk)
    triu = jnp.kron(jnp.eye(blk // Q, dtype=f32), jnp.triu(jnp.ones((Q, Q), f32))).astype(bf16)
    ssm = _ssd_call(act, dtT, z, w["biasT"], w["aT"], w["dskipT"], w["ssm_norm"], triu, w["e4"], w["e2"], w["gT"])
    x1 = _oproj_call(x, attnT, ssm, w["wa"], w["ws"])
    return _ffn_call(x1, w["n2"], w["wg"], w["wu"], w["fcw"], w["fcb"], w["wd"], w["fn"])


def kernel(x_prompt, x_sample, norm1, w_in, q_a_norm, kv_a_norm, w_q_b, w_kv_b, conv_w, conv_b,
           dt_bias_f, dt_bias_b, a_log_f, a_log_b, d_skip, ssm_norm, w_out, norm2, w_gate, w_up,
           ffn_conv_w, ffn_conv_b, w_down, final_norm):
    w = _prepare_weights(norm1[0], w_in[0], q_a_norm[0], kv_a_norm[0], w_q_b[0], w_kv_b[0], conv_w[0],
                         conv_b[0], dt_bias_f[0], dt_bias_b[0], a_log_f[0], a_log_b[0], d_skip[0],
                         ssm_norm[0], w_out[0], norm2[0], w_gate[0], w_up[0], ffn_conv_w[0],
                         ffn_conv_b[0], w_down[0], final_norm)
    return (_encoder(x_prompt, w), _encoder(x_sample, w))
```

```python
import functools

import numpy as np
import jax
import jax.numpy as jnp
from jax import lax
from jax.experimental import pallas as pl
from jax.experimental.pallas import tpu as pltpu

D_MODEL = 1024
N_HEADS = 16
QK_NOPE = 64
QK_ROPE = 32
HALF_ROPE = QK_ROPE // 2
QK_DIM = QK_NOPE + QK_ROPE
V_HEAD = 64
Q_LORA = 384
KV_LORA = 256
ROPE_THETA = 10000.0
SSM_HEADS = 16
SSM_HEAD_DIM = 64
D_SSM = SSM_HEADS * SSM_HEAD_DIM
SSM_GROUPS = 2
HEADS_PER_GROUP = SSM_HEADS // SSM_GROUPS
GROUP_WIDTH = D_SSM // SSM_GROUPS
D_STATE = 64
GN = SSM_GROUPS * D_STATE
D_XBC = D_SSM + 2 * GN
D_FF = 2816
EPS = 1e-6
LOG2E = float(np.log2(np.e))

LANES = 128
HEAD_PAD = LANES
BF16_ROWS = 16
F32_ROWS = 8
VMEM_LIMIT = 56 * 1024 * 1024

OFF_Q = 0
OFF_CKV = OFF_Q + Q_LORA
OFF_KR = OFF_CKV + KV_LORA
OFF_Z = OFF_KR + HEAD_PAD
OFF_XBC = OFF_Z + D_SSM
D_IN_PAD = OFF_XBC + D_XBC

PROJ_TM = 512
ATTN_TQ = 512
ATTN_TK = 512
ATTN_UNROLL = 4
ATTN_HEADS_PER_STEP = 2
SHIFT_ROW = QK_DIM
ATTN_BOUND_MARGIN = 1.0 + 2.0 ** -6
ATTN_DENOM_FLOOR = 2.0 ** -80
SSD_Q = 128
SSD_BLOCK = 512
OPROJ_TM = 512
FFN_TM = 512
FFN_FC = 256

NT_DIMS = (((1,), (1,)), ((), ()))
TN_DIMS = (((0,), (0,)), ((), ()))

f32 = jnp.float32
bf16 = jnp.bfloat16


def _rms(x, w):
    return x * lax.rsqrt(jnp.mean(x * x, axis=-1, keepdims=True) + EPS) * w


def _dot(a, b):
    return jnp.dot(a, b, preferred_element_type=f32)


def _silu(x):
    h = 0.5 * x
    return h * jnp.tanh(h) + h


def _conv3_rows(x, before, after, cw, cb):
    n = x.shape[0]
    sub = lax.broadcasted_iota(jnp.int32, (F32_ROWS, 1), 0)
    down = pltpu.roll(x, 1, axis=0)
    up = pltpu.roll(x, n - 1, axis=0)
    x_prev = jnp.concatenate([jnp.where(sub == 0, before, down[:F32_ROWS]), down[F32_ROWS:]], axis=0)
    x_next = jnp.concatenate([up[:n - F32_ROWS],
                              jnp.where(sub == F32_ROWS - 1, after, up[n - F32_ROWS:])], axis=0)
    return x_prev * cw[0:1] + x * cw[1:2] + x_next * cw[2:3] + cb


def _halo_specs(tm, L, width):
    hb = tm // F32_ROWS
    last = L // F32_ROWS - 1
    return [pl.BlockSpec((None, F32_ROWS, width), lambda b, i: (b, jnp.maximum(i * hb - 1, 0), 0)),
            pl.BlockSpec((None, F32_ROWS, width), lambda b, i: (b, jnp.minimum((i + 1) * hb, last), 0))]


def _const_spec(shape):
    nd = len(shape)
    return pl.BlockSpec(shape, lambda *_: (0,) * nd, pipeline_mode=pl.Buffered(1))


def _proj_kernel(x_ref, xp_ref, xn_ref, n1_ref, win_ref, qan_ref, kvan_ref, wqT_ref, wk_ref, wvT_ref, wdtT_ref,
                 cw_ref, cb_ref, c_ref, s_ref, cT_ref, sT_ref,
                 qT_out, k_out, vT_out, z_out, act_out, dtT_out, kn2_out, h_scr):
    i = pl.program_id(1)
    tm = x_ref.shape[0]
    n1 = n1_ref[...]
    h_scr[0:tm, :] = _rms(x_ref[...], n1).astype(bf16)
    halo = jnp.concatenate([xp_ref[...], xn_ref[...]], axis=0)
    h_scr[tm:tm + 2 * F32_ROWS, :] = _rms(halo, n1).astype(bf16)
    h = h_scr[0:tm, :]
    half = HEAD_PAD // 2
    lat = _dot(h, win_ref[:, OFF_Q:OFF_Z])
    xbc = _dot(h_scr[...], win_ref[:, OFF_XBC:OFF_XBC + D_XBC])
    hq = _rms(lat[:, OFF_Q:OFF_Q + Q_LORA], qan_ref[...]).astype(bf16)
    hc = _rms(lat[:, OFF_CKV:OFF_CKV + KV_LORA], kvan_ref[...]).astype(bf16)
    qT = lax.dot_general(wqT_ref[...], hq, NT_DIMS, preferred_element_type=f32)

    before = jnp.where(i > 0, xbc[tm + F32_ROWS - 1:tm + F32_ROWS], 0.0)
    after = jnp.where(i < pl.num_programs(1) - 1, xbc[tm + F32_ROWS:tm + F32_ROWS + 1], 0.0)
    act_out[...] = _silu(_conv3_rows(xbc[:tm], before, after, cw_ref[...], cb_ref[...])).astype(bf16)

    kn = _dot(hc, wk_ref[...])
    cT = cT_ref[...]
    sT = sT_ref[...]
    for hh in range(N_HEADS):
        blk = qT[hh * HEAD_PAD:(hh + 1) * HEAD_PAD, :]
        rot = jnp.concatenate([blk[half:], blk[:half]], axis=0)
        qT_out[hh * HEAD_PAD:(hh + 1) * HEAD_PAD, :] = (blk * cT + rot * sT).astype(bf16)

    z_out[...] = _dot(h, win_ref[:, OFF_Z:OFF_Z + D_SSM]).astype(bf16)
    kr = lat[:, OFF_KR:OFF_KR + HEAD_PAD]
    krf = kr * c_ref[...] + pltpu.roll(kr, half, axis=1) * s_ref[...]
    lane = lax.broadcasted_iota(jnp.int32, (1, HEAD_PAD), 1)
    shift_lanes = (lane == SHIFT_ROW) | (lane == SHIFT_ROW + 1)
    norms = []
    for hh in range(N_HEADS):
        kb = (kn[:, hh * HEAD_PAD:(hh + 1) * HEAD_PAD] + krf).astype(bf16)
        kf = kb.astype(f32)
        n2 = jnp.max(jnp.sum(kf * kf, axis=1, keepdims=True), axis=0, keepdims=True)
        norms.append(jnp.broadcast_to(n2, (1, HEAD_PAD)))
        k_out[hh] = jnp.where(shift_lanes, jnp.ones_like(kb), kb)
    kn2_out[...] = jnp.concatenate(norms, axis=0)
    vT_out[...] = lax.dot_general(wvT_ref[...], hc, NT_DIMS, preferred_element_type=f32).astype(bf16)
    dtT_out[...] = lax.dot_general(wdtT_ref[...], h, NT_DIMS, preferred_element_type=f32)


def _proj_call(x, n1, win, qan, kvan, wqT, wk, wvT, wdtT, cw, cb, c_tab, s_tab, cT_tab, sT_tab):
    B, L, _ = x.shape
    tm = min(PROJ_TM, L)
    grid = (B, L // tm)
    tok = lambda w: pl.BlockSpec((None, tm, w), lambda b, i: (b, i, 0))
    tokT = lambda w: pl.BlockSpec((None, w, tm), lambda b, i: (b, 0, i))
    consts = (n1, win, qan, kvan, wqT, wk, wvT, wdtT, cw, cb)
    in_specs = [tok(D_MODEL)] + _halo_specs(tm, L, D_MODEL) + [_const_spec(a.shape) for a in consts] + [
        pl.BlockSpec((tm, HEAD_PAD), lambda b, i: (i, 0)),
        pl.BlockSpec((tm, HEAD_PAD), lambda b, i: (i, 0)),
        pl.BlockSpec((HEAD_PAD, tm), lambda b, i: (0, i)),
        pl.BlockSpec((HEAD_PAD, tm), lambda b, i: (0, i)),
    ]
    out_shape = [
        jax.ShapeDtypeStruct((B, N_HEADS * HEAD_PAD, L), bf16),
        jax.ShapeDtypeStruct((B, N_HEADS, L, HEAD_PAD), bf16),
        jax.ShapeDtypeStruct((B, N_HEADS * V_HEAD, L), bf16),
        jax.ShapeDtypeStruct((B, L, D_SSM), bf16),
        jax.ShapeDtypeStruct((B, L, D_XBC), bf16),
        jax.ShapeDtypeStruct((B, 2 * SSM_HEADS, L), f32),
        jax.ShapeDtypeStruct((B, L // tm, N_HEADS, HEAD_PAD), f32),
    ]
    k_spec = pl.BlockSpec((None, N_HEADS, tm, HEAD_PAD), lambda b, i: (b, 0, i, 0))
    out_specs = [tokT(N_HEADS * HEAD_PAD), k_spec, tokT(N_HEADS * V_HEAD),
                 tok(D_SSM), tok(D_XBC), tokT(2 * SSM_HEADS),
                 pl.BlockSpec((None, None, N_HEADS, HEAD_PAD), lambda b, i: (b, i, 0, 0))]
    return pl.pallas_call(
        _proj_kernel, grid=grid, in_specs=in_specs, out_specs=out_specs, out_shape=out_shape,
        scratch_shapes=[pltpu.VMEM((tm + 2 * F32_ROWS, D_MODEL), bf16)],
        name="proj",
        compiler_params=pltpu.CompilerParams(
            dimension_semantics=("parallel", "parallel"), vmem_limit_bytes=VMEM_LIMIT),
    )(x, x, x, *consts, c_tab, s_tab, cT_tab, sT_tab)


def _attn_kernel(kmax_ref, qT_ref, k_ref, vT_ref, o_ref, q_scr, p_scr, acc_ref, dmin_ref, *, tq, tk, nq, nk, unroll, nh):
    ones = jnp.ones((BF16_ROWS, tk), bf16)
    ng = nk // unroll
    heads = range(nh)

    def q_cols(qi):
        return pl.ds(pl.multiple_of(qi * tq, tq), tq)

    def head_rows(hd, width):
        return slice(hd * width, (hd + 1) * width)

    def v_tile(hd, j):
        start = pl.multiple_of(j * tk, tk)
        return jnp.concatenate([vT_ref[head_rows(hd, V_HEAD), pl.ds(start, tk)], ones], axis=0)

    def k_tile(hd, j):
        return k_ref[hd, pl.ds(pl.multiple_of(j * tk, tk), tk), :]

    def shift_queries(qi, qslot):
        row = lax.broadcasted_iota(jnp.int32, (HEAD_PAD, 1), 0)
        for hd in heads:
            q = qT_ref[head_rows(hd, HEAD_PAD), q_cols(qi)]
            qf = q.astype(f32)
            bound = jnp.sqrt(jnp.sum(qf * qf, axis=0, keepdims=True)) * kmax_ref[hd] * ATTN_BOUND_MARGIN
            hi = bound.astype(bf16)
            lo = (bound - hi.astype(f32)).astype(bf16)
            q_scr[qslot, hd] = jnp.where(row == SHIFT_ROW, -hi, jnp.where(row == SHIFT_ROW + 1, -lo, q))

    def produce(hd, qslot, j, slot):
        p_scr[hd, slot] = jnp.exp2(_dot(k_tile(hd, j), q_scr[qslot, hd])).astype(bf16)

    def group(g, qslot, next_qslot):
        acc = [acc_ref[hd] for hd in heads]
        for u in range(unroll):
            j = g * unroll + u
            nxt = (u + 1) % 2
            for hd in heads:
                if next_qslot is False or u < unroll - 1:
                    produce(hd, qslot, j + 1, nxt)
                elif next_qslot is not None:
                    produce(hd, next_qslot, 0, nxt)
            for hd in heads:
                acc[hd] = acc[hd] + _dot(v_tile(hd, j), p_scr[hd, u % 2])
        for hd in heads:
            acc_ref[hd] = acc[hd]

    def recompute_exact(hd, qi):
        q = qT_ref[head_rows(hd, HEAD_PAD), q_cols(qi)]

        def body(j, carry):
            m, acc = carry
            s = _dot(k_tile(hd, j), q)
            m_new = jnp.maximum(m, jnp.max(s, axis=0, keepdims=True))
            p = jnp.exp2(s - m_new).astype(bf16)
            return m_new, jnp.exp2(m - m_new) * acc + _dot(v_tile(hd, j), p)

        m0 = jnp.full((1, tq), -jnp.inf, f32)
        _, acc = lax.fori_loop(0, nk, body, (m0, jnp.zeros((V_HEAD + BF16_ROWS, tq), f32)))
        write_output(hd, qi, acc)

    def write_output(hd, qi, acc):
        o_ref[head_rows(hd, V_HEAD), q_cols(qi)] = (acc[:V_HEAD] / acc[V_HEAD:V_HEAD + 1]).astype(o_ref.dtype)

    def query_tile(qi, qslot, next_qslot):
        if next_qslot is not None:
            shift_queries(qi + 1, next_qslot)
        acc_ref[...] = jnp.zeros(acc_ref.shape, f32)
        if ng > 1:
            def body(g, carry):
                group(g, qslot, False)
                return carry
            lax.fori_loop(0, ng - 1, body, 0)
        group(ng - 1, qslot, next_qslot)
        for hd in heads:
            acc = acc_ref[hd]
            write_output(hd, qi, acc)
            dmin_ref[hd] = jnp.minimum(dmin_ref[hd], acc[V_HEAD:V_HEAD + 1])

    dmin_ref[...] = jnp.full(dmin_ref.shape, jnp.inf, f32)
    shift_queries(0, 0)
    for hd in heads:
        produce(hd, 0, 0, 0)
    if nq > 1:
        def q_body(qi, carry):
            query_tile(qi, qi % 2, (qi + 1) % 2)
            return carry
        lax.fori_loop(0, nq - 1, q_body, 0)
    query_tile(nq - 1, (nq - 1) % 2, None)

    for hd in heads:
        @pl.when(jnp.logical_not(jnp.min(dmin_ref[hd]) >= ATTN_DENOM_FLOOR))
        def _(hd=hd):
            def redo(qi, carry):
                recompute_exact(hd, qi)
                return carry
            lax.fori_loop(0, nq, redo, 0)


def _attn_call(qT, k, vT, kmax):
    B, _, L = qT.shape
    tq = min(ATTN_TQ, L)
    tk = min(ATTN_TK, L)
    nk = L // tk
    unroll = min(ATTN_UNROLL, nk)
    assert unroll % 2 == 0 and nk % unroll == 0, "key tiles alternate between two probability buffers"
    nq = L // tq if nk == unroll else 1
    tqb = nq * tq
    nh = ATTN_HEADS_PER_STEP
    return pl.pallas_call(
        functools.partial(_attn_kernel, tq=tq, tk=tk, nq=nq, nk=nk, unroll=unroll, nh=nh),
        grid=(B, N_HEADS // nh, L // tqb),
        scratch_shapes=[pltpu.VMEM((2, nh, HEAD_PAD, tq), bf16), pltpu.VMEM((nh, 2, tk, tq), bf16),
                        pltpu.VMEM((nh, V_HEAD + BF16_ROWS, tq), f32), pltpu.VMEM((nh, 1, tq), f32)],
        in_specs=[
            pl.BlockSpec((None, nh, 1, tq), lambda b, h, i: (b, h, 0, 0)),
            pl.BlockSpec((None, nh * HEAD_PAD, tqb), lambda b, h, i: (b, h, i)),
            pl.BlockSpec((None, nh, L, HEAD_PAD), lambda b, h, i: (b, h, 0, 0)),
            pl.BlockSpec((None, nh * V_HEAD, L), lambda b, h, i: (b, h, 0)),
        ],
        out_specs=pl.BlockSpec((None, nh * V_HEAD, tqb), lambda b, h, i: (b, h, i)),
        out_shape=jax.ShapeDtypeStruct((B, N_HEADS * V_HEAD, L), bf16),
        name="attn",
        compiler_params=pltpu.CompilerParams(
            dimension_semantics=("parallel", "parallel", "arbitrary"), vmem_limit_bytes=VMEM_LIMIT),
    )(kmax, qT, k, vT)


def _split3(v):
    a1 = v.astype(bf16)
    r1 = v - a1.astype(f32)
    a2 = r1.astype(bf16)
    a3 = (r1 - a2.astype(f32)).astype(bf16)
    return a1, a2, a3


def _chunk_scalars(dtT_raw, biasT, aT, triu):
    nh2 = 2 * SSM_HEADS
    dtT = jax.nn.softplus(dtT_raw + biasT)
    stepT = dtT * aT
    cs3 = _dot(jnp.concatenate(_split3(stepT), axis=0), triu)
    csT = cs3[0:nh2] + cs3[nh2:2 * nh2] + cs3[2 * nh2:3 * nh2]
    return dtT, stepT, csT


def _expand_heads(partsT, e_ref):
    n = len(partsT)
    stk = jnp.concatenate(partsT, axis=0)
    hi = stk.astype(bf16).astype(f32)
    pieces = [hi, stk - hi]
    pad = LANES - 2 * stk.shape[0]
    if pad:
        pieces.append(jnp.zeros((pad, stk.shape[1]), f32))
    nat = jnp.transpose(jnp.concatenate(pieces, axis=0)).astype(bf16)
    full = _dot(nat, e_ref[...])
    return [full[:, j * D_SSM:(j + 1) * D_SSM] for j in range(n)]


def _ssd_kernel(act_ref, dtT_ref, z_ref, biasT_ref, aT_ref, dskipT_ref, norm_ref, triu_ref, e4_ref, e2_ref,
                gT_ref, o_ref, hf_ref, hb_ref, hbs_ref, *, nblk, nsub, Q):
    ph = pl.program_id(1)
    c = pl.program_id(2)
    H = SSM_HEADS

    def group_slices(g):
        return slice(g * GROUP_WIDTH, (g + 1) * GROUP_WIDTH), slice(g * D_STATE, (g + 1) * D_STATE)

    dtT, stepT, csT = _chunk_scalars(dtT_ref[...], biasT_ref[...], aT_ref[...], triu_ref[...])
    totT = jnp.concatenate(
        [jnp.broadcast_to(csT[:, (j + 1) * Q - 1:(j + 1) * Q], (2 * H, Q)) for j in range(nsub)], axis=1)
    cbT_b = csT[H:] - stepT[H:]
    Bm = act_ref[:, D_SSM:D_SSM + GN]

    @pl.when(ph == 0)
    def _backward_states():
        @pl.when(c == 0)
        def _():
            hb_ref[...] = jnp.zeros_like(hb_ref)

        blk = nblk - 1 - c
        wb, eb = _expand_heads([jnp.exp2(cbT_b) * dtT[H:], jnp.exp2(totT[H:] - cbT_b)], e2_ref)
        xw = (act_ref[:, :D_SSM].astype(f32) * wb).astype(bf16)
        for j in reversed(range(nsub)):
            rows = slice(j * Q, (j + 1) * Q)
            hbs_ref[blk * nsub + j] = hb_ref[...].astype(bf16)
            for g in range(SSM_GROUPS):
                sl, sn = group_slices(g)
                upd = lax.dot_general(Bm[rows, sn], xw[rows, sl], TN_DIMS, preferred_element_type=f32)
                hb_ref[g] = eb[j * Q:j * Q + 1, sl] * hb_ref[g] + upd

    @pl.when(ph == 1)
    def _outputs():
        @pl.when(c == 0)
        def _():
            hf_ref[...] = jnp.zeros_like(hf_ref)

        ti = lax.broadcasted_iota(jnp.int32, (Q, Q), 0)
        si = lax.broadcasted_iota(jnp.int32, (Q, Q), 1)
        lower = ti >= si
        lane = lax.broadcasted_iota(jnp.int32, (1, LANES), 1)
        first_half = lane < SSM_HEAD_DIM
        gT = gT_ref[...]
        nw = norm_ref[...]

        xs_b = act_ref[:, :D_SSM]
        Cm = act_ref[:, D_SSM + GN:]
        xs = xs_b.astype(f32)
        csT_f = csT[:H]
        l2dt = jnp.log2(dtT)
        rowf = csT_f - l2dt[:H]
        rowb = cbT_b + l2dt[H:]
        cols = jnp.transpose(jnp.concatenate([csT_f, cbT_b], axis=0))
        cb = Cm.astype(f32) * Bm.astype(f32)
        cb_h = cb.astype(bf16)
        cb_l = (cb - cb_h.astype(f32)).astype(bf16)
        diagT = (lax.dot_general(gT, cb_h, NT_DIMS, preferred_element_type=f32)
                 + lax.dot_general(gT, cb_l, NT_DIMS, preferred_element_type=f32))
        ef, eb, wf, coef = _expand_heads(
            [jnp.exp2(csT_f), jnp.exp2(totT[H:] - cbT_b), jnp.exp2(totT[:H] - csT_f) * dtT[:H],
             dskipT_ref[...] + dtT[H:] * diagT], e4_ref)
        xw = (xs * wf).astype(bf16)
        skip = xs * coef

        for j in range(nsub):
            rows = slice(j * Q, (j + 1) * Q)
            last = (j + 1) * Q - 1
            chunk_id = c * nsub + j
            y_groups = []
            for g in range(SSM_GROUPS):
                sl, sn = group_slices(g)
                Cg = Cm[rows, sn]
                Bg = Bm[rows, sn]
                cbm = lax.dot_general(Cg, Bg, NT_DIMS, preferred_element_type=f32)
                y_off = (ef[rows, sl] * _dot(Cg, hf_ref[g].astype(bf16))
                         + eb[rows, sl] * _dot(Cg, hbs_ref[chunk_id, g]))
                pairs = []
                for hp in range(HEADS_PER_GROUP // 2):
                    lo = g * GROUP_WIDTH + hp * LANES
                    x_pair = xs_b[rows, lo:lo + LANES]
                    y_pair = None
                    for k in range(2):
                        hh = g * HEADS_PER_GROUP + 2 * hp + k
                        arg = jnp.where(lower, cols[rows, hh:hh + 1] - rowf[hh:hh + 1, rows],
                                        rowb[hh:hh + 1, rows] - cols[rows, H + hh:H + hh + 1])
                        mat = (cbm * jnp.exp2(arg)).astype(bf16)
                        keep = first_half if k == 0 else jnp.logical_not(first_half)
                        contrib = _dot(mat, jnp.where(keep, x_pair, jnp.zeros_like(x_pair)))
                        y_pair = contrib if y_pair is None else y_pair + contrib
                    pairs.append(y_pair)
                y_groups.append(jnp.concatenate(pairs, axis=1) + y_off)
                upd = lax.dot_general(Bg, xw[rows, sl], TN_DIMS, preferred_element_type=f32)
                hf_ref[g] = ef[last:last + 1, sl] * hf_ref[g] + upd

            y = jnp.concatenate(y_groups, axis=1) + skip[rows]
            y = y * _silu(z_ref[rows, :].astype(f32))
            for g in range(SSM_GROUPS):
                sl, _ = group_slices(g)
                o_ref[rows, sl] = _rms(y[:, sl], nw[:, sl]).astype(o_ref.dtype)


def _ssd_call(act, dtT, z, biasT, aT, dskipT, norm, triu, e4, e2, gT):
    B, L, _ = act.shape
    blk = min(SSD_BLOCK, L)
    Q = min(SSD_Q, blk)
    nsub = blk // Q
    nblk = L // blk

    def bidx(p, c):
        return jnp.where(p == 0, nblk - 1 - c, c)

    consts = (biasT, aT, dskipT, norm, triu, e4, e2, gT)
    in_specs = [
        pl.BlockSpec((None, blk, D_XBC), lambda b, p, c: (b, bidx(p, c), 0)),
        pl.BlockSpec((None, 2 * SSM_HEADS, blk), lambda b, p, c: (b, 0, bidx(p, c))),
        pl.BlockSpec((None, blk, D_SSM), lambda b, p, c: (b, p * c, 0)),
    ] + [_const_spec(a.shape) for a in consts]
    state = (SSM_GROUPS, D_STATE, GROUP_WIDTH)
    return pl.pallas_call(
        functools.partial(_ssd_kernel, nblk=nblk, nsub=nsub, Q=Q),
        grid=(B, 2, nblk),
        in_specs=in_specs,
        out_specs=pl.BlockSpec((None, blk, D_SSM), lambda b, p, c: (b, p * c, 0)),
        out_shape=jax.ShapeDtypeStruct((B, L, D_SSM), bf16),
        scratch_shapes=[pltpu.VMEM(state, f32), pltpu.VMEM(state, f32),
                        pltpu.VMEM((nblk * nsub,) + state, bf16)],
        name="ssd",
        compiler_params=pltpu.CompilerParams(
            dimension_semantics=("parallel", "arbitrary", "arbitrary"), vmem_limit_bytes=VMEM_LIMIT),
    )(act, dtT, z, *consts)


def _oproj_kernel(x_ref, aT_ref, s_ref, wa_ref, ws_ref, o_ref):
    o_ref[...] = (x_ref[...]
                  + lax.dot_general(aT_ref[...], wa_ref[...], TN_DIMS, preferred_element_type=f32)
                  + _dot(s_ref[...], ws_ref[...]))


def _oproj_call(x, attnT, ssm, wa, ws):
    B, L, _ = x.shape
    tm = min(OPROJ_TM, L)
    tok = lambda w: pl.BlockSpec((None, tm, w), lambda b, i: (b, i, 0))
    return pl.pallas_call(
        _oproj_kernel, grid=(B, L // tm),
        in_specs=[tok(D_MODEL), pl.BlockSpec((None, N_HEADS * V_HEAD, tm), lambda b, i: (b, 0, i)),
                  tok(D_SSM), _const_spec(wa.shape), _const_spec(ws.shape)],
        out_specs=tok(D_MODEL),
        out_shape=jax.ShapeDtypeStruct((B, L, D_MODEL), f32),
        name="oproj",
        compiler_params=pltpu.CompilerParams(
            dimension_semantics=("parallel", "parallel"), vmem_limit_bytes=VMEM_LIMIT),
    )(x, attnT, ssm, wa, ws)


def _ffn_kernel(x_ref, xp_ref, xn_ref, n2_ref, wg_ref, wu_ref, cw_ref, cb_ref, wd_ref, fn_ref,
                o_ref, h_scr, act_scr, *, nfc):
    i = pl.program_id(1)
    nt = pl.num_programs(1)
    tm = x_ref.shape[0]
    n2 = n2_ref[...]
    h_scr[0:tm, :] = _rms(x_ref[...], n2).astype(bf16)
    halo = jnp.concatenate([xp_ref[...], xn_ref[...]], axis=0)
    h_scr[tm:tm + 2 * F32_ROWS, :] = _rms(halo, n2).astype(bf16)
    has_prev = i > 0
    has_next = i < nt - 1

    def chunk(cf):
        col = cf * FFN_FC
        g_ext = _dot(h_scr[...], wg_ref[:, pl.ds(col, FFN_FC)])
        before = jnp.where(has_prev, g_ext[tm + F32_ROWS - 1:tm + F32_ROWS], 0.0)
        after = jnp.where(has_next, g_ext[tm + F32_ROWS:tm + F32_ROWS + 1], 0.0)
        gc = _conv3_rows(g_ext[:tm], before, after, cw_ref[:, pl.ds(col, FFN_FC)], cb_ref[:, pl.ds(col, FFN_FC)])
        u = _dot(h_scr[0:tm, :], wu_ref[:, pl.ds(col, FFN_FC)])
        act_scr[:, pl.ds(col, FFN_FC)] = (_silu(gc) * u).astype(bf16)

    for cf in range(nfc):
        chunk(cf)
    y = x_ref[...] + _dot(act_scr[...], wd_ref[...])
    o_ref[...] = _rms(y, fn_ref[...])


def _ffn_call(x, n2, wg, wu, cw, cb, wd, fn):
    B, L, _ = x.shape
    tm = min(FFN_TM, L)
    tok = pl.BlockSpec((None, tm, D_MODEL), lambda b, i: (b, i, 0))
    consts = (n2, wg, wu, cw, cb, wd, fn)
    return pl.pallas_call(
        functools.partial(_ffn_kernel, nfc=D_FF // FFN_FC),
        grid=(B, L // tm),
        in_specs=[tok] + _halo_specs(tm, L, D_MODEL) + [_const_spec(a.shape) for a in consts],
        out_specs=tok,
        out_shape=jax.ShapeDtypeStruct((B, L, D_MODEL), f32),
        scratch_shapes=[pltpu.VMEM((tm + 2 * F32_ROWS, D_MODEL), bf16), pltpu.VMEM((tm, D_FF), bf16)],
        name="ffn",
        compiler_params=pltpu.CompilerParams(
            dimension_semantics=("parallel", "parallel"), vmem_limit_bytes=VMEM_LIMIT),
    )(x, x, x, *consts)


def _head_lane_sources():
    src = np.full((HEAD_PAD,), QK_DIM, np.int32)
    half = HEAD_PAD // 2
    src[0:HALF_ROPE] = QK_NOPE + np.arange(HALF_ROPE)
    src[HALF_ROPE:half] = np.arange(half - HALF_ROPE)
    src[half:half + HALF_ROPE] = QK_NOPE + HALF_ROPE + np.arange(HALF_ROPE)
    n_rest = QK_NOPE - (half - HALF_ROPE)
    src[half + HALF_ROPE:half + HALF_ROPE + n_rest] = (half - HALF_ROPE) + np.arange(n_rest)
    return src


def _rope_tables(L):
    inv = ROPE_THETA ** (-jnp.arange(0, QK_ROPE, 2, dtype=f32) / QK_ROPE)
    ang = jnp.arange(L, dtype=f32)[:, None] * inv[None, :]
    cos, sin = jnp.cos(ang), jnp.sin(ang)
    half = HEAD_PAD // 2
    c_tab = jnp.ones((L, HEAD_PAD), f32)
    c_tab = c_tab.at[:, 0:HALF_ROPE].set(cos).at[:, half:half + HALF_ROPE].set(cos)
    s_tab = jnp.zeros((L, HEAD_PAD), f32)
    s_tab = s_tab.at[:, 0:HALF_ROPE].set(-sin).at[:, half:half + HALF_ROPE].set(sin)
    return c_tab, s_tab


def _expand_matrix(n):
    m = np.zeros((LANES, n * D_SSM), np.float32)
    for part in range(2):
        for j in range(n):
            for h in range(SSM_HEADS):
                r = part * n * SSM_HEADS + j * SSM_HEADS + h
                m[r, j * D_SSM + h * SSM_HEAD_DIM:j * D_SSM + (h + 1) * SSM_HEAD_DIM] = 1.0
    return jnp.asarray(m, bf16)


def _prepare_weights(norm1, w_in, q_a_norm, kv_a_norm, w_q_b, w_kv_b, conv_w, conv_b,
                     dt_bias_f, dt_bias_b, a_log_f, a_log_b, d_skip, ssm_norm, w_out,
                     norm2, w_gate, w_up, ffn_conv_w, ffn_conv_b, w_down, final_norm):
    half = HEAD_PAD // 2
    o_kr = Q_LORA + KV_LORA
    o_z = o_kr + QK_ROPE
    o_dt = o_z + D_SSM + D_XBC
    kr_blk = jnp.zeros((D_MODEL, HEAD_PAD), f32)
    kr_blk = kr_blk.at[:, 0:HALF_ROPE].set(w_in[:, o_kr:o_kr + HALF_ROPE])
    kr_blk = kr_blk.at[:, half:half + HALF_ROPE].set(w_in[:, o_kr + HALF_ROPE:o_kr + QK_ROPE])
    win = jnp.concatenate([w_in[:, :o_kr], kr_blk, w_in[:, o_z:o_dt]], axis=1).astype(bf16)
    w_dt = w_in[:, o_dt:o_dt + 2 * SSM_HEADS]

    src = _head_lane_sources()
    wq = w_q_b.reshape(Q_LORA, N_HEADS, QK_DIM)
    wq = jnp.concatenate([wq, jnp.zeros((Q_LORA, N_HEADS, 1), f32)], axis=-1)[:, :, src]
    wqT = wq.reshape(Q_LORA, N_HEADS * HEAD_PAD).T.astype(bf16)
    wkv = w_kv_b.reshape(KV_LORA, N_HEADS, QK_NOPE + V_HEAD)
    src_k = np.where(src < QK_NOPE, src, QK_NOPE)
    wk = jnp.concatenate([wkv[:, :, :QK_NOPE], jnp.zeros((KV_LORA, N_HEADS, 1), f32)], axis=-1)[:, :, src_k]
    wk = wk.reshape(KV_LORA, N_HEADS * HEAD_PAD).astype(bf16)
    wvT = wkv[:, :, QK_NOPE:].reshape(KV_LORA, N_HEADS * V_HEAD).T.astype(bf16)

    row = lambda v: v.reshape(1, -1).astype(f32)
    col = lambda v: v.reshape(-1, 1).astype(f32)
    a_neg = -jnp.exp(jnp.concatenate([a_log_f, a_log_b]).astype(f32)) * LOG2E
    group_of_lane = np.arange(GN) // D_STATE
    group_of_head = np.arange(SSM_HEADS) // HEADS_PER_GROUP
    gT = jnp.asarray(group_of_head[:, None] == group_of_lane[None, :], bf16)
    return dict(
        n1=row(norm1), win=win, qan=row(q_a_norm), kvan=row(kv_a_norm), wqT=wqT, wk=wk, wvT=wvT,
        wdtT=w_dt.T.astype(bf16), cw=conv_w.astype(f32), cb=row(conv_b),
        biasT=col(jnp.concatenate([dt_bias_f, dt_bias_b])), aT=col(a_neg), dskipT=col(d_skip),
        ssm_norm=row(ssm_norm), e4=_expand_matrix(4), e2=_expand_matrix(2), gT=gT,
        wa=w_out[:N_HEADS * V_HEAD].astype(bf16), ws=w_out[N_HEADS * V_HEAD:].astype(bf16),
        n2=row(norm2), wg=w_gate.astype(bf16), wu=w_up.astype(bf16), fcw=ffn_conv_w.astype(f32),
        fcb=row(ffn_conv_b), wd=w_down.astype(bf16), fn=row(final_norm),
    )


def _encoder(x, w):
    B, L, _ = x.shape
    c_tab, s_tab = _rope_tables(L)
    scale = QK_DIM ** -0.5 * LOG2E
    qT, k, vT, z, act, dtT, kn2 = _proj_call(
        x, w["n1"], w["win"], w["qan"], w["kvan"], w["wqT"], w["wk"], w["wvT"], w["wdtT"], w["cw"], w["cb"],
        c_tab, s_tab, (c_tab * scale).T, (s_tab * scale).T)
    kmax = jnp.sqrt(jnp.max(kn2, axis=1))[:, :, :1]
    kmax = jnp.broadcast_to(kmax[..., None], (B, N_HEADS, 1, min(ATTN_TQ, L)))
    attnT = _attn_call(qT, k, vT, kmax)
    blk = min(SSD_BLOCK, L)
    Q = min(SSD_Q, blk)
    triu = jnp.kron(jnp.eye(blk // Q, dtype=f32), jnp.triu(jnp.ones((Q, Q), f32))).astype(bf16)
    ssm = _ssd_call(act, dtT, z, w["biasT"], w["aT"], w["dskipT"], w["ssm_norm"], triu, w["e4"], w["e2"], w["gT"])
    x1 = _oproj_call(x, attnT, ssm, w["wa"], w["ws"])
    return _ffn_call(x1, w["n2"], w["wg"], w["wu"], w["fcw"], w["fcb"], w["wd"], w["fn"])


def kernel(x_prompt, x_sample, norm1, w_in, q_a_norm, kv_a_norm, w_q_b, w_kv_b, conv_w, conv_b,
           dt_bias_f, dt_bias_b, a_log_f, a_log_b, d_skip, ssm_norm, w_out, norm2, w_gate, w_up,
           ffn_conv_w, ffn_conv_b, w_down, final_norm):
    w = _prepare_weights(norm1[0], w_in[0], q_a_norm[0], kv_a_norm[0], w_q_b[0], w_kv_b[0], conv_w[0],
                         conv_b[0], dt_bias_f[0], dt_bias_b[0], a_log_f[0], a_log_b[0], d_skip[0],
                         ssm_norm[0], w_out[0], norm2[0], w_gate[0], w_up[0], ffn_conv_w[0],
                         ffn_conv_b[0], w_down[0], final_norm)
    return (_encoder(x_prompt, w), _encoder(x_sample, w))
```

```python
import functools

import numpy as np
import jax
import jax.numpy as jnp
from jax import lax
from jax.experimental import pallas as pl
from jax.experimental.pallas import tpu as pltpu

D_MODEL = 1024
N_HEADS = 16
QK_NOPE = 64
QK_ROPE = 32
HALF_ROPE = QK_ROPE // 2
QK_DIM = QK_NOPE + QK_ROPE
V_HEAD = 64
Q_LORA = 384
KV_LORA = 256
ROPE_THETA = 10000.0
SSM_HEADS = 16
SSM_HEAD_DIM = 64
D_SSM = SSM_HEADS * SSM_HEAD_DIM
SSM_GROUPS = 2
HEADS_PER_GROUP = SSM_HEADS // SSM_GROUPS
GROUP_WIDTH = D_SSM // SSM_GROUPS
D_STATE = 64
GN = SSM_GROUPS * D_STATE
D_XBC = D_SSM + 2 * GN
D_FF = 2816
EPS = 1e-6
LOG2E = float(np.log2(np.e))

LANES = 128
HEAD_PAD = LANES
BF16_ROWS = 16
F32_ROWS = 8
VMEM_LIMIT = 56 * 1024 * 1024

OFF_Q = 0
OFF_CKV = OFF_Q + Q_LORA
OFF_KR = OFF_CKV + KV_LORA
OFF_Z = OFF_KR + HEAD_PAD
OFF_XBC = OFF_Z + D_SSM
D_IN_PAD = OFF_XBC + D_XBC

PROJ_TM = 512
ATTN_TQ = 512
ATTN_TK = 1024
ATTN_UNROLL = 4
ATTN_HEADS_PER_STEP = 2
SHIFT_ROW = QK_DIM
ATTN_BOUND_MARGIN = 1.0 + 2.0 ** -6
ATTN_DENOM_FLOOR = 2.0 ** -80
SSD_Q = 128
SSD_BLOCK = 512
OPROJ_TM = 512
FFN_TM = 512
FFN_FC = 256

NT_DIMS = (((1,), (1,)), ((), ()))
TN_DIMS = (((0,), (0,)), ((), ()))

f32 = jnp.float32
bf16 = jnp.bfloat16


def _rms(x, w):
    return x * lax.rsqrt(jnp.mean(x * x, axis=-1, keepdims=True) + EPS) * w


def _dot(a, b):
    return jnp.dot(a, b, preferred_element_type=f32)


def _silu(x):
    h = 0.5 * x
    return h * jnp.tanh(h) + h


def _conv3_rows(x, before, after, cw, cb):
    n = x.shape[0]
    sub = lax.broadcasted_iota(jnp.int32, (F32_ROWS, 1), 0)
    down = pltpu.roll(x, 1, axis=0)
    up = pltpu.roll(x, n - 1, axis=0)
    x_prev = jnp.concatenate([jnp.where(sub == 0, before, down[:F32_ROWS]), down[F32_ROWS:]], axis=0)
    x_next = jnp.concatenate([up[:n - F32_ROWS],
                              jnp.where(sub == F32_ROWS - 1, after, up[n - F32_ROWS:])], axis=0)
    return x_prev * cw[0:1] + x * cw[1:2] + x_next * cw[2:3] + cb


def _halo_specs(tm, L, width):
    hb = tm // F32_ROWS
    last = L // F32_ROWS - 1
    return [pl.BlockSpec((None, F32_ROWS, width), lambda b, i: (b, jnp.maximum(i * hb - 1, 0), 0)),
            pl.BlockSpec((None, F32_ROWS, width), lambda b, i: (b, jnp.minimum((i + 1) * hb, last), 0))]


def _const_spec(shape):
    nd = len(shape)
    return pl.BlockSpec(shape, lambda *_: (0,) * nd, pipeline_mode=pl.Buffered(1))


def _proj_kernel(x_ref, xp_ref, xn_ref, n1_ref, win_ref, qan_ref, kvan_ref, wqT_ref, wk_ref, wvT_ref, wdtT_ref,
                 cw_ref, cb_ref, c_ref, s_ref, cT_ref, sT_ref,
                 qT_out, k_out, vT_out, z_out, act_out, dtT_out, kn2_out, h_scr):
    i = pl.program_id(1)
    tm = x_ref.shape[0]
    n1 = n1_ref[...]
    h_scr[0:tm, :] = _rms(x_ref[...], n1).astype(bf16)
    halo = jnp.concatenate([xp_ref[...], xn_ref[...]], axis=0)
    h_scr[tm:tm + 2 * F32_ROWS, :] = _rms(halo, n1).astype(bf16)
    h = h_scr[0:tm, :]
    half = HEAD_PAD // 2
    lat = _dot(h, win_ref[:, OFF_Q:OFF_Z])
    xbc = _dot(h_scr[...], win_ref[:, OFF_XBC:OFF_XBC + D_XBC])
    hq = _rms(lat[:, OFF_Q:OFF_Q + Q_LORA], qan_ref[...]).astype(bf16)
    hc = _rms(lat[:, OFF_CKV:OFF_CKV + KV_LORA], kvan_ref[...]).astype(bf16)
    qT = lax.dot_general(wqT_ref[...], hq, NT_DIMS, preferred_element_type=f32)

    before = jnp.where(i > 0, xbc[tm + F32_ROWS - 1:tm + F32_ROWS], 0.0)
    after = jnp.where(i < pl.num_programs(1) - 1, xbc[tm + F32_ROWS:tm + F32_ROWS + 1], 0.0)
    act_out[...] = _silu(_conv3_rows(xbc[:tm], before, after, cw_ref[...], cb_ref[...])).astype(bf16)

    kn = _dot(hc, wk_ref[...])
    cT = cT_ref[...]
    sT = sT_ref[...]
    for hh in range(N_HEADS):
        blk = qT[hh * HEAD_PAD:(hh + 1) * HEAD_PAD, :]
        rot = jnp.concatenate([blk[half:], blk[:half]], axis=0)
        qT_out[hh * HEAD_PAD:(hh + 1) * HEAD_PAD, :] = (blk * cT + rot * sT).astype(bf16)

    z_out[...] = _dot(h, win_ref[:, OFF_Z:OFF_Z + D_SSM]).astype(bf16)
    kr = lat[:, OFF_KR:OFF_KR + HEAD_PAD]
    krf = kr * c_ref[...] + pltpu.roll(kr, half, axis=1) * s_ref[...]
    lane = lax.broadcasted_iota(jnp.int32, (1, HEAD_PAD), 1)
    shift_lanes = (lane == SHIFT_ROW) | (lane == SHIFT_ROW + 1)
    norms = []
    for hh in range(N_HEADS):
        kb = (kn[:, hh * HEAD_PAD:(hh + 1) * HEAD_PAD] + krf).astype(bf16)
        kf = kb.astype(f32)
        n2 = jnp.max(jnp.sum(kf * kf, axis=1, keepdims=True), axis=0, keepdims=True)
        norms.append(jnp.broadcast_to(n2, (1, HEAD_PAD)))
        k_out[hh] = jnp.where(shift_lanes, jnp.ones_like(kb), kb)
    kn2_out[...] = jnp.concatenate(norms, axis=0)
    vT_out[...] = lax.dot_general(wvT_ref[...], hc, NT_DIMS, preferred_element_type=f32).astype(bf16)
    dtT_out[...] = lax.dot_general(wdtT_ref[...], h, NT_DIMS, preferred_element_type=f32)


def _proj_call(x, n1, win, qan, kvan, wqT, wk, wvT, wdtT, cw, cb, c_tab, s_tab, cT_tab, sT_tab):
    B, L, _ = x.shape
    tm = min(PROJ_TM, L)
    grid = (B, L // tm)
    tok = lambda w: pl.BlockSpec((None, tm, w), lambda b, i: (b, i, 0))
    tokT = lambda w: pl.BlockSpec((None, w, tm), lambda b, i: (b, 0, i))
    consts = (n1, win, qan, kvan, wqT, wk, wvT, wdtT, cw, cb)
    in_specs = [tok(D_MODEL)] + _halo_specs(tm, L, D_MODEL) + [_const_spec(a.shape) for a in consts] + [
        pl.BlockSpec((tm, HEAD_PAD), lambda b, i: (i, 0)),
        pl.BlockSpec((tm, HEAD_PAD), lambda b, i: (i, 0)),
        pl.BlockSpec((HEAD_PAD, tm), lambda b, i: (0, i)),
        pl.BlockSpec((HEAD_PAD, tm), lambda b, i: (0, i)),
    ]
    out_shape = [
        jax.ShapeDtypeStruct((B, N_HEADS * HEAD_PAD, L), bf16),
        jax.ShapeDtypeStruct((B, N_HEADS, L, HEAD_PAD), bf16),
        jax.ShapeDtypeStruct((B, N_HEADS * V_HEAD, L), bf16),
        jax.ShapeDtypeStruct((B, L, D_SSM), bf16),
        jax.ShapeDtypeStruct((B, L, D_XBC), bf16),
        jax.ShapeDtypeStruct((B, 2 * SSM_HEADS, L), f32),
        jax.ShapeDtypeStruct((B, L // tm, N_HEADS, HEAD_PAD), f32),
    ]
    k_spec = pl.BlockSpec((None, N_HEADS, tm, HEAD_PAD), lambda b, i: (b, 0, i, 0))
    out_specs = [tokT(N_HEADS * HEAD_PAD), k_spec, tokT(N_HEADS * V_HEAD),
                 tok(D_SSM), tok(D_XBC), tokT(2 * SSM_HEADS),
                 pl.BlockSpec((None, None, N_HEADS, HEAD_PAD), lambda b, i: (b, i, 0, 0))]
    return pl.pallas_call(
        _proj_kernel, grid=grid, in_specs=in_specs, out_specs=out_specs, out_shape=out_shape,
        scratch_shapes=[pltpu.VMEM((tm + 2 * F32_ROWS, D_MODEL), bf16)],
        name="proj",
        compiler_params=pltpu.CompilerParams(
            dimension_semantics=("parallel", "parallel"), vmem_limit_bytes=VMEM_LIMIT),
    )(x, x, x, *consts, c_tab, s_tab, cT_tab, sT_tab)


def _attn_kernel(kmax_ref, qT_ref, k_ref, vT_ref, o_ref, q_scr, p_scr, acc_ref, dmin_ref, *, tq, tk, nq, nk, unroll, nh):
    ones = jnp.ones((BF16_ROWS, tk), bf16)
    ng = nk // unroll
    heads = range(nh)

    def q_cols(qi):
        return pl.ds(pl.multiple_of(qi * tq, tq), tq)

    def head_rows(hd, width):
        return slice(hd * width, (hd + 1) * width)

    def v_tile(hd, j):
        start = pl.multiple_of(j * tk, tk)
        return jnp.concatenate([vT_ref[head_rows(hd, V_HEAD), pl.ds(start, tk)], ones], axis=0)

    def k_tile(hd, j):
        return k_ref[hd, pl.ds(pl.multiple_of(j * tk, tk), tk), :]

    def shift_queries(qi, qslot):
        row = lax.broadcasted_iota(jnp.int32, (HEAD_PAD, 1), 0)
        for hd in heads:
            q = qT_ref[head_rows(hd, HEAD_PAD), q_cols(qi)]
            qf = q.astype(f32)
            bound = jnp.sqrt(jnp.sum(qf * qf, axis=0, keepdims=True)) * kmax_ref[hd] * ATTN_BOUND_MARGIN
            hi = bound.astype(bf16)
            lo = (bound - hi.astype(f32)).astype(bf16)
            q_scr[qslot, hd] = jnp.where(row == SHIFT_ROW, -hi, jnp.where(row == SHIFT_ROW + 1, -lo, q))

    def produce(hd, qslot, j, slot):
        p_scr[hd, slot] = jnp.exp2(_dot(k_tile(hd, j), q_scr[qslot, hd])).astype(bf16)

    def group(g, qslot, next_qslot):
        acc = [acc_ref[hd] for hd in heads]
        for u in range(unroll):
            j = g * unroll + u
            nxt = (u + 1) % 2
            for hd in heads:
                if next_qslot is False or u < unroll - 1:
                    produce(hd, qslot, j + 1, nxt)
                elif next_qslot is not None:
                    produce(hd, next_qslot, 0, nxt)
            for hd in heads:
                acc[hd] = acc[hd] + _dot(v_tile(hd, j), p_scr[hd, u % 2])
        for hd in heads:
            acc_ref[hd] = acc[hd]

    def recompute_exact(hd, qi):
        q = qT_ref[head_rows(hd, HEAD_PAD), q_cols(qi)]

        def body(j, carry):
            m, acc = carry
            s = _dot(k_tile(hd, j), q)
            m_new = jnp.maximum(m, jnp.max(s, axis=0, keepdims=True))
            p = jnp.exp2(s - m_new).astype(bf16)
            return m_new, jnp.exp2(m - m_new) * acc + _dot(v_tile(hd, j), p)

        m0 = jnp.full((1, tq), -jnp.inf, f32)
        _, acc = lax.fori_loop(0, nk, body, (m0, jnp.zeros((V_HEAD + BF16_ROWS, tq), f32)))
        write_output(hd, qi, acc)

    def write_output(hd, qi, acc):
        o_ref[head_rows(hd, V_HEAD), q_cols(qi)] = (acc[:V_HEAD] / acc[V_HEAD:V_HEAD + 1]).astype(o_ref.dtype)

    def query_tile(qi, qslot, next_qslot):
        if next_qslot is not None:
            shift_queries(qi + 1, next_qslot)
        acc_ref[...] = jnp.zeros(acc_ref.shape, f32)
        if ng > 1:
            def body(g, carry):
                group(g, qslot, False)
                return carry
            lax.fori_loop(0, ng - 1, body, 0)
        group(ng - 1, qslot, next_qslot)
        for hd in heads:
            acc = acc_ref[hd]
            write_output(hd, qi, acc)
            dmin_ref[hd] = jnp.minimum(dmin_ref[hd], acc[V_HEAD:V_HEAD + 1])

    dmin_ref[...] = jnp.full(dmin_ref.shape, jnp.inf, f32)
    shift_queries(0, 0)
    for hd in heads:
        produce(hd, 0, 0, 0)
    if nq > 1:
        def q_body(qi, carry):
            query_tile(qi, qi % 2, (qi + 1) % 2)
            return carry
        lax.fori_loop(0, nq - 1, q_body, 0)
    query_tile(nq - 1, (nq - 1) % 2, None)

    for hd in heads:
        @pl.when(jnp.logical_not(jnp.min(dmin_ref[hd]) >= ATTN_DENOM_FLOOR))
        def _(hd=hd):
            def redo(qi, carry):
                recompute_exact(hd, qi)
                return carry
            lax.fori_loop(0, nq, redo, 0)


def _attn_call(qT, k, vT, kmax):
    B, _, L = qT.shape
    tq = min(ATTN_TQ, L)
    tk = min(ATTN_TK, L)
    nk = L // tk
    unroll = min(ATTN_UNROLL, nk)
    assert unroll % 2 == 0 and nk % unroll == 0, "key tiles alternate between two probability buffers"
    nq = L // tq if nk == unroll else 1
    tqb = nq * tq
    nh = ATTN_HEADS_PER_STEP
    return pl.pallas_call(
        functools.partial(_attn_kernel, tq=tq, tk=tk, nq=nq, nk=nk, unroll=unroll, nh=nh),
        grid=(B, N_HEADS // nh, L // tqb),
        scratch_shapes=[pltpu.VMEM((2, nh, HEAD_PAD, tq), bf16), pltpu.VMEM((nh, 2, tk, tq), bf16),
                        pltpu.VMEM((nh, V_HEAD + BF16_ROWS, tq), f32), pltpu.VMEM((nh, 1, tq), f32)],
        in_specs=[
            pl.BlockSpec((None, nh, 1, tq), lambda b, h, i: (b, h, 0, 0)),
            pl.BlockSpec((None, nh * HEAD_PAD, tqb), lambda b, h, i: (b, h, i)),
            pl.BlockSpec((None, nh, L, HEAD_PAD), lambda b, h, i: (b, h, 0, 0)),
            pl.BlockSpec((None, nh * V_HEAD, L), lambda b, h, i: (b, h, 0)),
        ],
        out_specs=pl.BlockSpec((None, nh * V_HEAD, tqb), lambda b, h, i: (b, h, i)),
        out_shape=jax.ShapeDtypeStruct((B, N_HEADS * V_HEAD, L), bf16),
        name="attn",
        compiler_params=pltpu.CompilerParams(
            dimension_semantics=("parallel", "parallel", "arbitrary"), vmem_limit_bytes=VMEM_LIMIT),
    )(kmax, qT, k, vT)


def _split3(v):
    a1 = v.astype(bf16)
    r1 = v - a1.astype(f32)
    a2 = r1.astype(bf16)
    a3 = (r1 - a2.astype(f32)).astype(bf16)
    return a1, a2, a3


def _chunk_scalars(dtT_raw, biasT, aT, triu):
    nh2 = 2 * SSM_HEADS
    dtT = jax.nn.softplus(dtT_raw + biasT)
    stepT = dtT * aT
    cs3 = _dot(jnp.concatenate(_split3(stepT), axis=0), triu)
    csT = cs3[0:nh2] + cs3[nh2:2 * nh2] + cs3[2 * nh2:3 * nh2]
    return dtT, stepT, csT


def _expand_heads(partsT, e_ref):
    n = len(partsT)
    stk = jnp.concatenate(partsT, axis=0)
    hi = stk.astype(bf16).astype(f32)
    pieces = [hi, stk - hi]
    pad = LANES - 2 * stk.shape[0]
    if pad:
        pieces.append(jnp.zeros((pad, stk.shape[1]), f32))
    nat = jnp.transpose(jnp.concatenate(pieces, axis=0)).astype(bf16)
    full = _dot(nat, e_ref[...])
    return [full[:, j * D_SSM:(j + 1) * D_SSM] for j in range(n)]


def _ssd_kernel(act_ref, dtT_ref, z_ref, biasT_ref, aT_ref, dskipT_ref, norm_ref, triu_ref, e4_ref, e2_ref,
                gT_ref, o_ref, hf_ref, hb_ref, hbs_ref, *, nblk, nsub, Q):
    ph = pl.program_id(1)
    c = pl.program_id(2)
    H = SSM_HEADS

    def group_slices(g):
        return slice(g * GROUP_WIDTH, (g + 1) * GROUP_WIDTH), slice(g * D_STATE, (g + 1) * D_STATE)

    dtT, stepT, csT = _chunk_scalars(dtT_ref[...], biasT_ref[...], aT_ref[...], triu_ref[...])
    totT = jnp.concatenate(
        [jnp.broadcast_to(csT[:, (j + 1) * Q - 1:(j + 1) * Q], (2 * H, Q)) for j in range(nsub)], axis=1)
    cbT_b = csT[H:] - stepT[H:]
    Bm = act_ref[:, D_SSM:D_SSM + GN]

    @pl.when(ph == 0)
    def _backward_states():
        @pl.when(c == 0)
        def _():
            hb_ref[...] = jnp.zeros_like(hb_ref)

        blk = nblk - 1 - c
        wb, eb = _expand_heads([jnp.exp2(cbT_b) * dtT[H:], jnp.exp2(totT[H:] - cbT_b)], e2_ref)
        xw = (act_ref[:, :D_SSM].astype(f32) * wb).astype(bf16)
        for j in reversed(range(nsub)):
            rows = slice(j * Q, (j + 1) * Q)
            hbs_ref[blk * nsub + j] = hb_ref[...].astype(bf16)
            for g in range(SSM_GROUPS):
                sl, sn = group_slices(g)
                upd = lax.dot_general(Bm[rows, sn], xw[rows, sl], TN_DIMS, preferred_element_type=f32)
                hb_ref[g] = eb[j * Q:j * Q + 1, sl] * hb_ref[g] + upd

    @pl.when(ph == 1)
    def _outputs():
        @pl.when(c == 0)
        def _():
            hf_ref[...] = jnp.zeros_like(hf_ref)

        ti = lax.broadcasted_iota(jnp.int32, (Q, Q), 0)
        si = lax.broadcasted_iota(jnp.int32, (Q, Q), 1)
        lower = ti >= si
        lane = lax.broadcasted_iota(jnp.int32, (1, LANES), 1)
        first_half = lane < SSM_HEAD_DIM
        gT = gT_ref[...]
        nw = norm_ref[...]

        xs_b = act_ref[:, :D_SSM]
        Cm = act_ref[:, D_SSM + GN:]
        xs = xs_b.astype(f32)
        csT_f = csT[:H]
        l2dt = jnp.log2(dtT)
        rowf = csT_f - l2dt[:H]
        rowb = cbT_b + l2dt[H:]
        cols = jnp.transpose(jnp.concatenate([csT_f, cbT_b], axis=0))
        cb = Cm.astype(f32) * Bm.astype(f32)
        cb_h = cb.astype(bf16)
        cb_l = (cb - cb_h.astype(f32)).astype(bf16)
        diagT = (lax.dot_general(gT, cb_h, NT_DIMS, preferred_element_type=f32)
                 + lax.dot_general(gT, cb_l, NT_DIMS, preferred_element_type=f32))
        partsT = [jnp.exp2(csT_f), jnp.exp2(totT[H:] - cbT_b), jnp.exp2(totT[:H] - csT_f) * dtT[:H],
                  dskipT_ref[...] + dtT[H:] * diagT]

        def expand_chunk(j):
            return _expand_heads([p[:, j * Q:(j + 1) * Q] for p in partsT], e4_ref)

        expanded = expand_chunk(0)
        for j in range(nsub):
            rows = slice(j * Q, (j + 1) * Q)
            chunk_id = c * nsub + j
            ef, eb, wf, coef = expanded
            if j + 1 < nsub:
                expanded = expand_chunk(j + 1)
            y_groups = []
            for g in range(SSM_GROUPS):
                sl, sn = group_slices(g)
                Cg = Cm[rows, sn]
                Bg = Bm[rows, sn]
                cbm = lax.dot_general(Cg, Bg, NT_DIMS, preferred_element_type=f32)
                y_off = (ef[:, sl] * _dot(Cg, hf_ref[g].astype(bf16))
                         + eb[:, sl] * _dot(Cg, hbs_ref[chunk_id, g]))
                pairs = []
                for hp in range(HEADS_PER_GROUP // 2):
                    lo = g * GROUP_WIDTH + hp * LANES
                    x_pair = xs_b[rows, lo:lo + LANES]
                    y_pair = None
                    for k in range(2):
                        hh = g * HEADS_PER_GROUP + 2 * hp + k
                        arg = jnp.where(lower, cols[rows, hh:hh + 1] - rowf[hh:hh + 1, rows],
                                        rowb[hh:hh + 1, rows] - cols[rows, H + hh:H + hh + 1])
                        mat = (cbm * jnp.exp2(arg)).astype(bf16)
                        keep = first_half if k == 0 else jnp.logical_not(first_half)
                        contrib = _dot(mat, jnp.where(keep, x_pair, jnp.zeros_like(x_pair)))
                        y_pair = contrib if y_pair is None else y_pair + contrib
                    pairs.append(y_pair)
                y_groups.append(jnp.concatenate(pairs, axis=1) + y_off)
                xw = (xs[rows, sl] * wf[:, sl]).astype(bf16)
                upd = lax.dot_general(Bg, xw, TN_DIMS, preferred_element_type=f32)
                hf_ref[g] = ef[Q - 1:Q, sl] * hf_ref[g] + upd

            y = jnp.concatenate(y_groups, axis=1) + xs[rows] * coef
            y = y * _silu(z_ref[rows, :].astype(f32))
            for g in range(SSM_GROUPS):
                sl, _ = group_slices(g)
                o_ref[rows, sl] = _rms(y[:, sl], nw[:, sl]).astype(o_ref.dtype)


def _ssd_call(act, dtT, z, biasT, aT, dskipT, norm, triu, e4, e2, gT):
    B, L, _ = act.shape
    blk = min(SSD_BLOCK, L)
    Q = min(SSD_Q, blk)
    nsub = blk // Q
    nblk = L // blk

    def bidx(p, c):
        return jnp.where(p == 0, nblk - 1 - c, c)

    consts = (biasT, aT, dskipT, norm, triu, e4, e2, gT)
    in_specs = [
        pl.BlockSpec((None, blk, D_XBC), lambda b, p, c: (b, bidx(p, c), 0)),
        pl.BlockSpec((None, 2 * SSM_HEADS, blk), lambda b, p, c: (b, 0, bidx(p, c))),
        pl.BlockSpec((None, blk, D_SSM), lambda b, p, c: (b, p * c, 0)),
    ] + [_const_spec(a.shape) for a in consts]
    state = (SSM_GROUPS, D_STATE, GROUP_WIDTH)
    return pl.pallas_call(
        functools.partial(_ssd_kernel, nblk=nblk, nsub=nsub, Q=Q),
        grid=(B, 2, nblk),
        in_specs=in_specs,
        out_specs=pl.BlockSpec((None, blk, D_SSM), lambda b, p, c: (b, p * c, 0)),
        out_shape=jax.ShapeDtypeStruct((B, L, D_SSM), bf16),
        scratch_shapes=[pltpu.VMEM(state, f32), pltpu.VMEM(state, f32),
                        pltpu.VMEM((nblk * nsub,) + state, bf16)],
        name="ssd",
        compiler_params=pltpu.CompilerParams(
            dimension_semantics=("parallel", "arbitrary", "arbitrary"), vmem_limit_bytes=VMEM_LIMIT),
    )(act, dtT, z, *consts)


def _oproj_kernel(x_ref, aT_ref, s_ref, wa_ref, ws_ref, o_ref):
    o_ref[...] = (x_ref[...]
                  + lax.dot_general(aT_ref[...], wa_ref[...], TN_DIMS, preferred_element_type=f32)
                  + _dot(s_ref[...], ws_ref[...]))


def _oproj_call(x, attnT, ssm, wa, ws):
    B, L, _ = x.shape
    tm = min(OPROJ_TM, L)
    tok = lambda w: pl.BlockSpec((None, tm, w), lambda b, i: (b, i, 0))
    return pl.pallas_call(
        _oproj_kernel, grid=(B, L // tm),
        in_specs=[tok(D_MODEL), pl.BlockSpec((None, N_HEADS * V_HEAD, tm), lambda b, i: (b, 0, i)),
                  tok(D_SSM), _const_spec(wa.shape), _const_spec(ws.shape)],
        out_specs=tok(D_MODEL),
        out_shape=jax.ShapeDtypeStruct((B, L, D_MODEL), f32),
        name="oproj",
        compiler_params=pltpu.CompilerParams(
            dimension_semantics=("parallel", "parallel"), vmem_limit_bytes=VMEM_LIMIT),
    )(x, attnT, ssm, wa, ws)


def _ffn_kernel(x_ref, xp_ref, xn_ref, n2_ref, wg_ref, wu_ref, cw_ref, cb_ref, wd_ref, fn_ref,
                o_ref, h_scr, act_scr, *, nfc):
    i = pl.program_id(1)
    nt = pl.num_programs(1)
    tm = x_ref.shape[0]
    n2 = n2_ref[...]
    h_scr[0:tm, :] = _rms(x_ref[...], n2).astype(bf16)
    halo = jnp.concatenate([xp_ref[...], xn_ref[...]], axis=0)
    h_scr[tm:tm + 2 * F32_ROWS, :] = _rms(halo, n2).astype(bf16)
    has_prev = i > 0
    has_next = i < nt - 1

    def chunk(cf):
        col = cf * FFN_FC
        g_ext = _dot(h_scr[...], wg_ref[:, pl.ds(col, FFN_FC)])
        before = jnp.where(has_prev, g_ext[tm + F32_ROWS - 1:tm + F32_ROWS], 0.0)
        after = jnp.where(has_next, g_ext[tm + F32_ROWS:tm + F32_ROWS + 1], 0.0)
        gc = _conv3_rows(g_ext[:tm], before, after, cw_ref[:, pl.ds(col, FFN_FC)], cb_ref[:, pl.ds(col, FFN_FC)])
        u = _dot(h_scr[0:tm, :], wu_ref[:, pl.ds(col, FFN_FC)])
        act_scr[:, pl.ds(col, FFN_FC)] = (_silu(gc) * u).astype(bf16)

    for cf in range(nfc):
        chunk(cf)
    y = x_ref[...] + _dot(act_scr[...], wd_ref[...])
    o_ref[...] = _rms(y, fn_ref[...])


def _ffn_call(x, n2, wg, wu, cw, cb, wd, fn):
    B, L, _ = x.shape
    tm = min(FFN_TM, L)
    tok = pl.BlockSpec((None, tm, D_MODEL), lambda b, i: (b, i, 0))
    consts = (n2, wg, wu, cw, cb, wd, fn)
    return pl.pallas_call(
        functools.partial(_ffn_kernel, nfc=D_FF // FFN_FC),
        grid=(B, L // tm),
        in_specs=[tok] + _halo_specs(tm, L, D_MODEL) + [_const_spec(a.shape) for a in consts],
        out_specs=tok,
        out_shape=jax.ShapeDtypeStruct((B, L, D_MODEL), f32),
        scratch_shapes=[pltpu.VMEM((tm + 2 * F32_ROWS, D_MODEL), bf16), pltpu.VMEM((tm, D_FF), bf16)],
        name="ffn",
        compiler_params=pltpu.CompilerParams(
            dimension_semantics=("parallel", "parallel"), vmem_limit_bytes=VMEM_LIMIT),
    )(x, x, x, *consts)


def _head_lane_sources():
    src = np.full((HEAD_PAD,), QK_DIM, np.int32)
    half = HEAD_PAD // 2
    src[0:HALF_ROPE] = QK_NOPE + np.arange(HALF_ROPE)
    src[HALF_ROPE:half] = np.arange(half - HALF_ROPE)
    src[half:half + HALF_ROPE] = QK_NOPE + HALF_ROPE + np.arange(HALF_ROPE)
    n_rest = QK_NOPE - (half - HALF_ROPE)
    src[half + HALF_ROPE:half + HALF_ROPE + n_rest] = (half - HALF_ROPE) + np.arange(n_rest)
    return src


def _rope_tables(L):
    inv = ROPE_THETA ** (-jnp.arange(0, QK_ROPE, 2, dtype=f32) / QK_ROPE)
    ang = jnp.arange(L, dtype=f32)[:, None] * inv[None, :]
    cos, sin = jnp.cos(ang), jnp.sin(ang)
    half = HEAD_PAD // 2
    c_tab = jnp.ones((L, HEAD_PAD), f32)
    c_tab = c_tab.at[:, 0:HALF_ROPE].set(cos).at[:, half:half + HALF_ROPE].set(cos)
    s_tab = jnp.zeros((L, HEAD_PAD), f32)
    s_tab = s_tab.at[:, 0:HALF_ROPE].set(-sin).at[:, half:half + HALF_ROPE].set(sin)
    return c_tab, s_tab


def _expand_matrix(n):
    m = np.zeros((LANES, n * D_SSM), np.float32)
    for part in range(2):
        for j in range(n):
            for h in range(SSM_HEADS):
                r = part * n * SSM_HEADS + j * SSM_HEADS + h
                m[r, j * D_SSM + h * SSM_HEAD_DIM:j * D_SSM + (h + 1) * SSM_HEAD_DIM] = 1.0
    return jnp.asarray(m, bf16)


def _prepare_weights(norm1, w_in, q_a_norm, kv_a_norm, w_q_b, w_kv_b, conv_w, conv_b,
                     dt_bias_f, dt_bias_b, a_log_f, a_log_b, d_skip, ssm_norm, w_out,
                     norm2, w_gate, w_up, ffn_conv_w, ffn_conv_b, w_down, final_norm):
    half = HEAD_PAD // 2
    o_kr = Q_LORA + KV_LORA
    o_z = o_kr + QK_ROPE
    o_dt = o_z + D_SSM + D_XBC
    kr_blk = jnp.zeros((D_MODEL, HEAD_PAD), f32)
    kr_blk = kr_blk.at[:, 0:HALF_ROPE].set(w_in[:, o_kr:o_kr + HALF_ROPE])
    kr_blk = kr_blk.at[:, half:half + HALF_ROPE].set(w_in[:, o_kr + HALF_ROPE:o_kr + QK_ROPE])
    win = jnp.concatenate([w_in[:, :o_kr], kr_blk, w_in[:, o_z:o_dt]], axis=1).astype(bf16)
    w_dt = w_in[:, o_dt:o_dt + 2 * SSM_HEADS]

    src = _head_lane_sources()
    wq = w_q_b.reshape(Q_LORA, N_HEADS, QK_DIM)
    wq = jnp.concatenate([wq, jnp.zeros((Q_LORA, N_HEADS, 1), f32)], axis=-1)[:, :, src]
    wqT = wq.reshape(Q_LORA, N_HEADS * HEAD_PAD).T.astype(bf16)
    wkv = w_kv_b.reshape(KV_LORA, N_HEADS, QK_NOPE + V_HEAD)
    src_k = np.where(src < QK_NOPE, src, QK_NOPE)
    wk = jnp.concatenate([wkv[:, :, :QK_NOPE], jnp.zeros((KV_LORA, N_HEADS, 1), f32)], axis=-1)[:, :, src_k]
    wk = wk.reshape(KV_LORA, N_HEADS * HEAD_PAD).astype(bf16)
    wvT = wkv[:, :, QK_NOPE:].reshape(KV_LORA, N_HEADS * V_HEAD).T.astype(bf16)

    row = lambda v: v.reshape(1, -1).astype(f32)
    col = lambda v: v.reshape(-1, 1).astype(f32)
    a_neg = -jnp.exp(jnp.concatenate([a_log_f, a_log_b]).astype(f32)) * LOG2E
    group_of_lane = np.arange(GN) // D_STATE
    group_of_head = np.arange(SSM_HEADS) // HEADS_PER_GROUP
    gT = jnp.asarray(group_of_head[:, None] == group_of_lane[None, :], bf16)
    return dict(
        n1=row(norm1), win=win, qan=row(q_a_norm), kvan=row(kv_a_norm), wqT=wqT, wk=wk, wvT=wvT,
        wdtT=w_dt.T.astype(bf16), cw=conv_w.astype(f32), cb=row(conv_b),
        biasT=col(jnp.concatenate([dt_bias_f, dt_bias_b])), aT=col(a_neg), dskipT=col(d_skip),
        ssm_norm=row(ssm_norm), e4=_expand_matrix(4), e2=_expand_matrix(2), gT=gT,
        wa=w_out[:N_HEADS * V_HEAD].astype(bf16), ws=w_out[N_HEADS * V_HEAD:].astype(bf16),
        n2=row(norm2), wg=w_gate.astype(bf16), wu=w_up.astype(bf16), fcw=ffn_conv_w.astype(f32),
        fcb=row(ffn_conv_b), wd=w_down.astype(bf16), fn=row(final_norm),
    )


def _encoder(x, w):
    B, L, _ = x.shape
    c_tab, s_tab = _rope_tables(L)
    scale = QK_DIM ** -0.5 * LOG2E
    qT, k, vT, z, act, dtT, kn2 = _proj_call(
        x, w["n1"], w["win"], w["qan"], w["kvan"], w["wqT"], w["wk"], w["wvT"], w["wdtT"], w["cw"], w["cb"],
        c_tab, s_tab, (c_tab * scale).T, (s_tab * scale).T)
    kmax = jnp.sqrt(jnp.max(kn2, axis=1))[:, :, :1]
    kmax = jnp.broadcast_to(kmax[..., None], (B, N_HEADS, 1, min(ATTN_TQ, L)))
    attnT = _attn_call(qT, k, vT, kmax)
    blk = min(SSD_BLOCK, L)
    Q = min(SSD_Q, blk)
    triu = jnp.kron(jnp.eye(blk // Q, dtype=f32), jnp.triu(jnp.ones((Q, Q), f32))).astype(bf16)
    ssm = _ssd_call(act, dtT, z, w["biasT"], w["aT"], w["dskipT"], w["ssm_norm"], triu, w["e4"], w["e2"], w["gT"])
    x1 = _oproj_call(x, attnT, ssm, w["wa"], w["ws"])
    return _ffn_call(x1, w["n2"], w["wg"], w["wu"], w["fcw"], w["fcb"], w["wd"], w["fn"])


def kernel(x_prompt, x_sample, norm1, w_in, q_a_norm, kv_a_norm, w_q_b, w_kv_b, conv_w, conv_b,
           dt_bias_f, dt_bias_b, a_log_f, a_log_b, d_skip, ssm_norm, w_out, norm2, w_gate, w_up,
           ffn_conv_w, ffn_conv_b, w_down, final_norm):
    w = _prepare_weights(norm1[0], w_in[0], q_a_norm[0], kv_a_norm[0], w_q_b[0], w_kv_b[0], conv_w[0],
                         conv_b[0], dt_bias_f[0], dt_bias_b[0], a_log_f[0], a_log_b[0], d_skip[0],
                         ssm_norm[0], w_out[0], norm2[0], w_gate[0], w_up[0], ffn_conv_w[0],
                         ffn_conv_b[0], w_down[0], final_norm)
    return (_encoder(x_prompt, w), _encoder(x_sample, w))
```

```python
import functools

import numpy as np
import jax
import jax.numpy as jnp
from jax import lax
from jax.experimental import pallas as pl
from jax.experimental.pallas import tpu as pltpu

D_MODEL = 1024
N_HEADS = 16
QK_NOPE = 64
QK_ROPE = 32
HALF_ROPE = QK_ROPE // 2
QK_DIM = QK_NOPE + QK_ROPE
V_HEAD = 64
Q_LORA = 384
KV_LORA = 256
ROPE_THETA = 10000.0
SSM_HEADS = 16
SSM_HEAD_DIM = 64
D_SSM = SSM_HEADS * SSM_HEAD_DIM
SSM_GROUPS = 2
HEADS_PER_GROUP = SSM_HEADS // SSM_GROUPS
GROUP_WIDTH = D_SSM // SSM_GROUPS
D_STATE = 64
GN = SSM_GROUPS * D_STATE
D_XBC = D_SSM + 2 * GN
D_FF = 2816
EPS = 1e-6
LOG2E = float(np.log2(np.e))

LANES = 128
HEAD_PAD = LANES
BF16_ROWS = 16
F32_ROWS = 8
VMEM_LIMIT = 56 * 1024 * 1024

OFF_Q = 0
OFF_CKV = OFF_Q + Q_LORA
OFF_KR = OFF_CKV + KV_LORA
OFF_Z = OFF_KR + HEAD_PAD
OFF_XBC = OFF_Z + D_SSM
D_IN_PAD = OFF_XBC + D_XBC

PROJ_TM = 512
ATTN_TQ = 512
ATTN_TK = 2048
ATTN_UNROLL = 4
ATTN_HEADS_PER_STEP = 2
SHIFT_ROW = QK_DIM
ATTN_BOUND_MARGIN = 1.0 + 2.0 ** -6
ATTN_DENOM_FLOOR = 2.0 ** -80
SSD_Q = 128
SSD_BLOCK = 512
OPROJ_TM = 512
FFN_TM = 512
FFN_FC = 256

NT_DIMS = (((1,), (1,)), ((), ()))
TN_DIMS = (((0,), (0,)), ((), ()))

f32 = jnp.float32
bf16 = jnp.bfloat16


def _rms(x, w):
    return x * lax.rsqrt(jnp.mean(x * x, axis=-1, keepdims=True) + EPS) * w


def _dot(a, b):
    return jnp.dot(a, b, preferred_element_type=f32)


def _silu(x):
    h = 0.5 * x
    return h * jnp.tanh(h) + h


def _conv3_rows(x, before, after, cw, cb):
    n = x.shape[0]
    sub = lax.broadcasted_iota(jnp.int32, (F32_ROWS, 1), 0)
    down = pltpu.roll(x, 1, axis=0)
    up = pltpu.roll(x, n - 1, axis=0)
    x_prev = jnp.concatenate([jnp.where(sub == 0, before, down[:F32_ROWS]), down[F32_ROWS:]], axis=0)
    x_next = jnp.concatenate([up[:n - F32_ROWS],
                              jnp.where(sub == F32_ROWS - 1, after, up[n - F32_ROWS:])], axis=0)
    return x_prev * cw[0:1] + x * cw[1:2] + x_next * cw[2:3] + cb


def _halo_specs(tm, L, width):
    hb = tm // F32_ROWS
    last = L // F32_ROWS - 1
    return [pl.BlockSpec((None, F32_ROWS, width), lambda b, i: (b, jnp.maximum(i * hb - 1, 0), 0)),
            pl.BlockSpec((None, F32_ROWS, width), lambda b, i: (b, jnp.minimum((i + 1) * hb, last), 0))]


def _const_spec(shape):
    nd = len(shape)
    return pl.BlockSpec(shape, lambda *_: (0,) * nd, pipeline_mode=pl.Buffered(1))


def _proj_kernel(x_ref, xp_ref, xn_ref, n1_ref, win_ref, qan_ref, kvan_ref, wqT_ref, wk_ref, wvT_ref, wdtT_ref,
                 cw_ref, cb_ref, c_ref, s_ref, cT_ref, sT_ref,
                 qT_out, k_out, vT_out, z_out, act_out, dtT_out, kn2_out, h_scr):
    i = pl.program_id(1)
    tm = x_ref.shape[0]
    n1 = n1_ref[...]
    h_scr[0:tm, :] = _rms(x_ref[...], n1).astype(bf16)
    halo = jnp.concatenate([xp_ref[...], xn_ref[...]], axis=0)
    h_scr[tm:tm + 2 * F32_ROWS, :] = _rms(halo, n1).astype(bf16)
    h = h_scr[0:tm, :]
    half = HEAD_PAD // 2
    lat = _dot(h, win_ref[:, OFF_Q:OFF_Z])
    xbc = _dot(h_scr[...], win_ref[:, OFF_XBC:OFF_XBC + D_XBC])
    hq = _rms(lat[:, OFF_Q:OFF_Q + Q_LORA], qan_ref[...]).astype(bf16)
    hc = _rms(lat[:, OFF_CKV:OFF_CKV + KV_LORA], kvan_ref[...]).astype(bf16)
    qT = lax.dot_general(wqT_ref[...], hq, NT_DIMS, preferred_element_type=f32)

    before = jnp.where(i > 0, xbc[tm + F32_ROWS - 1:tm + F32_ROWS], 0.0)
    after = jnp.where(i < pl.num_programs(1) - 1, xbc[tm + F32_ROWS:tm + F32_ROWS + 1], 0.0)
    act_out[...] = _silu(_conv3_rows(xbc[:tm], before, after, cw_ref[...], cb_ref[...])).astype(bf16)

    kn = _dot(hc, wk_ref[...])
    cT = cT_ref[...]
    sT = sT_ref[...]
    for hh in range(N_HEADS):
        blk = qT[hh * HEAD_PAD:(hh + 1) * HEAD_PAD, :]
        rot = jnp.concatenate([blk[half:], blk[:half]], axis=0)
        qT_out[hh * HEAD_PAD:(hh + 1) * HEAD_PAD, :] = (blk * cT + rot * sT).astype(bf16)

    z_out[...] = _dot(h, win_ref[:, OFF_Z:OFF_Z + D_SSM]).astype(bf16)
    kr = lat[:, OFF_KR:OFF_KR + HEAD_PAD]
    krf = kr * c_ref[...] + pltpu.roll(kr, half, axis=1) * s_ref[...]
    lane = lax.broadcasted_iota(jnp.int32, (1, HEAD_PAD), 1)
    shift_lanes = (lane == SHIFT_ROW) | (lane == SHIFT_ROW + 1)
    norms = []
    for hh in range(N_HEADS):
        kb = (kn[:, hh * HEAD_PAD:(hh + 1) * HEAD_PAD] + krf).astype(bf16)
        kf = kb.astype(f32)
        n2 = jnp.max(jnp.sum(kf * kf, axis=1, keepdims=True), axis=0, keepdims=True)
        norms.append(jnp.broadcast_to(n2, (1, HEAD_PAD)))
        k_out[hh] = jnp.where(shift_lanes, jnp.ones_like(kb), kb)
    kn2_out[...] = jnp.concatenate(norms, axis=0)
    vT_out[...] = lax.dot_general(wvT_ref[...], hc, NT_DIMS, preferred_element_type=f32).astype(bf16)
    dtT_out[...] = lax.dot_general(wdtT_ref[...], h, NT_DIMS, preferred_element_type=f32)


def _proj_call(x, n1, win, qan, kvan, wqT, wk, wvT, wdtT, cw, cb, c_tab, s_tab, cT_tab, sT_tab):
    B, L, _ = x.shape
    tm = min(PROJ_TM, L)
    grid = (B, L // tm)
    tok = lambda w: pl.BlockSpec((None, tm, w), lambda b, i: (b, i, 0))
    tokT = lambda w: pl.BlockSpec((None, w, tm), lambda b, i: (b, 0, i))
    consts = (n1, win, qan, kvan, wqT, wk, wvT, wdtT, cw, cb)
    in_specs = [tok(D_MODEL)] + _halo_specs(tm, L, D_MODEL) + [_const_spec(a.shape) for a in consts] + [
        pl.BlockSpec((tm, HEAD_PAD), lambda b, i: (i, 0)),
        pl.BlockSpec((tm, HEAD_PAD), lambda b, i: (i, 0)),
        pl.BlockSpec((HEAD_PAD, tm), lambda b, i: (0, i)),
        pl.BlockSpec((HEAD_PAD, tm), lambda b, i: (0, i)),
    ]
    out_shape = [
        jax.ShapeDtypeStruct((B, N_HEADS * HEAD_PAD, L), bf16),
        jax.ShapeDtypeStruct((B, N_HEADS, L, HEAD_PAD), bf16),
        jax.ShapeDtypeStruct((B, N_HEADS * V_HEAD, L), bf16),
        jax.ShapeDtypeStruct((B, L, D_SSM), bf16),
        jax.ShapeDtypeStruct((B, L, D_XBC), bf16),
        jax.ShapeDtypeStruct((B, 2 * SSM_HEADS, L), f32),
        jax.ShapeDtypeStruct((B, L // tm, N_HEADS, HEAD_PAD), f32),
    ]
    k_spec = pl.BlockSpec((None, N_HEADS, tm, HEAD_PAD), lambda b, i: (b, 0, i, 0))
    out_specs = [tokT(N_HEADS * HEAD_PAD), k_spec, tokT(N_HEADS * V_HEAD),
                 tok(D_SSM), tok(D_XBC), tokT(2 * SSM_HEADS),
                 pl.BlockSpec((None, None, N_HEADS, HEAD_PAD), lambda b, i: (b, i, 0, 0))]
    return pl.pallas_call(
        _proj_kernel, grid=grid, in_specs=in_specs, out_specs=out_specs, out_shape=out_shape,
        scratch_shapes=[pltpu.VMEM((tm + 2 * F32_ROWS, D_MODEL), bf16)],
        name="proj",
        compiler_params=pltpu.CompilerParams(
            dimension_semantics=("parallel", "parallel"), vmem_limit_bytes=VMEM_LIMIT),
    )(x, x, x, *consts, c_tab, s_tab, cT_tab, sT_tab)


def _attn_kernel(kmax_ref, qT_ref, k_ref, vT_ref, o_ref, q_scr, p_scr, acc_ref, dmin_ref, *, tq, tk, nq, nk, unroll, nh):
    ones = jnp.ones((BF16_ROWS, tk), bf16)
    ng = nk // unroll
    heads = range(nh)

    def q_cols(qi):
        return pl.ds(pl.multiple_of(qi * tq, tq), tq)

    def head_rows(hd, width):
        return slice(hd * width, (hd + 1) * width)

    def v_tile(hd, j):
        start = pl.multiple_of(j * tk, tk)
        return jnp.concatenate([vT_ref[head_rows(hd, V_HEAD), pl.ds(start, tk)], ones], axis=0)

    def k_tile(hd, j):
        return k_ref[hd, pl.ds(pl.multiple_of(j * tk, tk), tk), :]

    def shift_queries(qi, qslot):
        row = lax.broadcasted_iota(jnp.int32, (HEAD_PAD, 1), 0)
        for hd in heads:
            q = qT_ref[head_rows(hd, HEAD_PAD), q_cols(qi)]
            qf = q.astype(f32)
            bound = jnp.sqrt(jnp.sum(qf * qf, axis=0, keepdims=True)) * kmax_ref[hd] * ATTN_BOUND_MARGIN
            hi = bound.astype(bf16)
            lo = (bound - hi.astype(f32)).astype(bf16)
            q_scr[qslot, hd] = jnp.where(row == SHIFT_ROW, -hi, jnp.where(row == SHIFT_ROW + 1, -lo, q))

    def produce(hd, qslot, j, slot):
        p_scr[hd, slot] = jnp.exp2(_dot(k_tile(hd, j), q_scr[qslot, hd])).astype(bf16)

    def group(g, qslot, next_qslot):
        acc = [acc_ref[hd] for hd in heads]
        for u in range(unroll):
            j = g * unroll + u
            nxt = (u + 1) % 2
            for hd in heads:
                if next_qslot is False or u < unroll - 1:
                    produce(hd, qslot, j + 1, nxt)
                elif next_qslot is not None:
                    produce(hd, next_qslot, 0, nxt)
            for hd in heads:
                acc[hd] = acc[hd] + _dot(v_tile(hd, j), p_scr[hd, u % 2])
        for hd in heads:
            acc_ref[hd] = acc[hd]

    def recompute_exact(hd, qi):
        q = qT_ref[head_rows(hd, HEAD_PAD), q_cols(qi)]

        def body(j, carry):
            m, acc = carry
            s = _dot(k_tile(hd, j), q)
            m_new = jnp.maximum(m, jnp.max(s, axis=0, keepdims=True))
            p = jnp.exp2(s - m_new).astype(bf16)
            return m_new, jnp.exp2(m - m_new) * acc + _dot(v_tile(hd, j), p)

        m0 = jnp.full((1, tq), -jnp.inf, f32)
        _, acc = lax.fori_loop(0, nk, body, (m0, jnp.zeros((V_HEAD + BF16_ROWS, tq), f32)))
        write_output(hd, qi, acc)

    def write_output(hd, qi, acc):
        o_ref[head_rows(hd, V_HEAD), q_cols(qi)] = (acc[:V_HEAD] / acc[V_HEAD:V_HEAD + 1]).astype(o_ref.dtype)

    def query_tile(qi, qslot, next_qslot):
        if next_qslot is not None:
            shift_queries(qi + 1, next_qslot)
        acc_ref[...] = jnp.zeros(acc_ref.shape, f32)
        if ng > 1:
            def body(g, carry):
                group(g, qslot, False)
                return carry
            lax.fori_loop(0, ng - 1, body, 0)
        group(ng - 1, qslot, next_qslot)
        for hd in heads:
            acc = acc_ref[hd]
            write_output(hd, qi, acc)
            dmin_ref[hd] = jnp.minimum(dmin_ref[hd], acc[V_HEAD:V_HEAD + 1])

    dmin_ref[...] = jnp.full(dmin_ref.shape, jnp.inf, f32)
    shift_queries(0, 0)
    for hd in heads:
        produce(hd, 0, 0, 0)
    if nq > 1:
        def q_body(qi, carry):
            query_tile(qi, qi % 2, (qi + 1) % 2)
            return carry
        lax.fori_loop(0, nq - 1, q_body, 0)
    query_tile(nq - 1, (nq - 1) % 2, None)

    for hd in heads:
        @pl.when(jnp.logical_not(jnp.min(dmin_ref[hd]) >= ATTN_DENOM_FLOOR))
        def _(hd=hd):
            def redo(qi, carry):
                recompute_exact(hd, qi)
                return carry
            lax.fori_loop(0, nq, redo, 0)


def _attn_call(qT, k, vT, kmax):
    B, _, L = qT.shape
    tq = min(ATTN_TQ, L)
    tk = min(ATTN_TK, L // 2)
    nk = L // tk
    unroll = min(ATTN_UNROLL, nk)
    assert unroll % 2 == 0 and nk % unroll == 0, "key tiles alternate between two probability buffers"
    nq = L // tq if nk == unroll else 1
    tqb = nq * tq
    nh = ATTN_HEADS_PER_STEP
    return pl.pallas_call(
        functools.partial(_attn_kernel, tq=tq, tk=tk, nq=nq, nk=nk, unroll=unroll, nh=nh),
        grid=(B, N_HEADS // nh, L // tqb),
        scratch_shapes=[pltpu.VMEM((2, nh, HEAD_PAD, tq), bf16), pltpu.VMEM((nh, 2, tk, tq), bf16),
                        pltpu.VMEM((nh, V_HEAD + BF16_ROWS, tq), f32), pltpu.VMEM((nh, 1, tq), f32)],
        in_specs=[
            pl.BlockSpec((None, nh, 1, tq), lambda b, h, i: (b, h, 0, 0)),
            pl.BlockSpec((None, nh * HEAD_PAD, tqb), lambda b, h, i: (b, h, i)),
            pl.BlockSpec((None, nh, L, HEAD_PAD), lambda b, h, i: (b, h, 0, 0)),
            pl.BlockSpec((None, nh * V_HEAD, L), lambda b, h, i: (b, h, 0)),
        ],
        out_specs=pl.BlockSpec((None, nh * V_HEAD, tqb), lambda b, h, i: (b, h, i)),
        out_shape=jax.ShapeDtypeStruct((B, N_HEADS * V_HEAD, L), bf16),
        name="attn",
        compiler_params=pltpu.CompilerParams(
            dimension_semantics=("parallel", "parallel", "arbitrary"), vmem_limit_bytes=VMEM_LIMIT),
    )(kmax, qT, k, vT)


def _split3(v):
    a1 = v.astype(bf16)
    r1 = v - a1.astype(f32)
    a2 = r1.astype(bf16)
    a3 = (r1 - a2.astype(f32)).astype(bf16)
    return a1, a2, a3


def _chunk_scalars(dtT_raw, biasT, aT, triu):
    nh2 = 2 * SSM_HEADS
    dtT = jax.nn.softplus(dtT_raw + biasT)
    stepT = dtT * aT
    cs3 = _dot(jnp.concatenate(_split3(stepT), axis=0), triu)
    csT = cs3[0:nh2] + cs3[nh2:2 * nh2] + cs3[2 * nh2:3 * nh2]
    return dtT, stepT, csT


def _expand_heads(partsT, e_ref):
    n = len(partsT)
    stk = jnp.concatenate(partsT, axis=0)
    hi = stk.astype(bf16).astype(f32)
    pieces = [hi, stk - hi]
    pad = LANES - 2 * stk.shape[0]
    if pad:
        pieces.append(jnp.zeros((pad, stk.shape[1]), f32))
    nat = jnp.transpose(jnp.concatenate(pieces, axis=0)).astype(bf16)
    full = _dot(nat, e_ref[...])
    return [full[:, j * D_SSM:(j + 1) * D_SSM] for j in range(n)]


def _ssd_kernel(act_ref, dtT_ref, z_ref, biasT_ref, aT_ref, dskipT_ref, norm_ref, triu_ref, e4_ref, e2_ref,
                gT_ref, o_ref, hf_ref, hb_ref, hbs_ref, *, nblk, nsub, Q):
    ph = pl.program_id(1)
    c = pl.program_id(2)
    H = SSM_HEADS

    def group_slices(g):
        return slice(g * GROUP_WIDTH, (g + 1) * GROUP_WIDTH), slice(g * D_STATE, (g + 1) * D_STATE)

    dtT, stepT, csT = _chunk_scalars(dtT_ref[...], biasT_ref[...], aT_ref[...], triu_ref[...])
    totT = jnp.concatenate(
        [jnp.broadcast_to(csT[:, (j + 1) * Q - 1:(j + 1) * Q], (2 * H, Q)) for j in range(nsub)], axis=1)
    cbT_b = csT[H:] - stepT[H:]
    Bm = act_ref[:, D_SSM:D_SSM + GN]

    @pl.when(ph == 0)
    def _backward_states():
        @pl.when(c == 0)
        def _():
            hb_ref[...] = jnp.zeros_like(hb_ref)

        blk = nblk - 1 - c
        wb, eb = _expand_heads([jnp.exp2(cbT_b) * dtT[H:], jnp.exp2(totT[H:] - cbT_b)], e2_ref)
        xw = (act_ref[:, :D_SSM].astype(f32) * wb).astype(bf16)
        for j in reversed(range(nsub)):
            rows = slice(j * Q, (j + 1) * Q)
            hbs_ref[blk * nsub + j] = hb_ref[...].astype(bf16)
            for g in range(SSM_GROUPS):
                sl, sn = group_slices(g)
                upd = lax.dot_general(Bm[rows, sn], xw[rows, sl], TN_DIMS, preferred_element_type=f32)
                hb_ref[g] = eb[j * Q:j * Q + 1, sl] * hb_ref[g] + upd

    @pl.when(ph == 1)
    def _outputs():
        @pl.when(c == 0)
        def _():
            hf_ref[...] = jnp.zeros_like(hf_ref)

        ti = lax.broadcasted_iota(jnp.int32, (Q, Q), 0)
        si = lax.broadcasted_iota(jnp.int32, (Q, Q), 1)
        lower = ti >= si
        lane = lax.broadcasted_iota(jnp.int32, (1, LANES), 1)
        first_half = lane < SSM_HEAD_DIM
        gT = gT_ref[...]
        nw = norm_ref[...]

        xs_b = act_ref[:, :D_SSM]
        Cm = act_ref[:, D_SSM + GN:]
        xs = xs_b.astype(f32)
        csT_f = csT[:H]
        l2dt = jnp.log2(dtT)
        rowf = csT_f - l2dt[:H]
        rowb = cbT_b + l2dt[H:]
        cols = jnp.transpose(jnp.concatenate([csT_f, cbT_b], axis=0))
        cb = Cm.astype(f32) * Bm.astype(f32)
        cb_h = cb.astype(bf16)
        cb_l = (cb - cb_h.astype(f32)).astype(bf16)
        diagT = (lax.dot_general(gT, cb_h, NT_DIMS, preferred_element_type=f32)
                 + lax.dot_general(gT, cb_l, NT_DIMS, preferred_element_type=f32))
        partsT = [jnp.exp2(csT_f), jnp.exp2(totT[H:] - cbT_b), jnp.exp2(totT[:H] - csT_f) * dtT[:H],
                  dskipT_ref[...] + dtT[H:] * diagT]

        def expand_chunk(j):
            return _expand_heads([p[:, j * Q:(j + 1) * Q] for p in partsT], e4_ref)

        expanded = expand_chunk(0)
        for j in range(nsub):
            rows = slice(j * Q, (j + 1) * Q)
            chunk_id = c * nsub + j
            ef, eb, wf, coef = expanded
            if j + 1 < nsub:
                expanded = expand_chunk(j + 1)
            y_groups = []
            for g in range(SSM_GROUPS):
                sl, sn = group_slices(g)
                Cg = Cm[rows, sn]
                Bg = Bm[rows, sn]
                cbm = lax.dot_general(Cg, Bg, NT_DIMS, preferred_element_type=f32)
                y_off = (ef[:, sl] * _dot(Cg, hf_ref[g].astype(bf16))
                         + eb[:, sl] * _dot(Cg, hbs_ref[chunk_id, g]))
                pairs = []
                for hp in range(HEADS_PER_GROUP // 2):
                    lo = g * GROUP_WIDTH + hp * LANES
                    x_pair = xs_b[rows, lo:lo + LANES]
                    y_pair = None
                    for k in range(2):
                        hh = g * HEADS_PER_GROUP + 2 * hp + k
                        arg = jnp.where(lower, cols[rows, hh:hh + 1] - rowf[hh:hh + 1, rows],
                                        rowb[hh:hh + 1, rows] - cols[rows, H + hh:H + hh + 1])
                        mat = (cbm * jnp.exp2(arg)).astype(bf16)
                        keep = first_half if k == 0 else jnp.logical_not(first_half)
                        contrib = _dot(mat, jnp.where(keep, x_pair, jnp.zeros_like(x_pair)))
                        y_pair = contrib if y_pair is None else y_pair + contrib
                    pairs.append(y_pair)
                y_groups.append(jnp.concatenate(pairs, axis=1) + y_off)
                xw = (xs[rows, sl] * wf[:, sl]).astype(bf16)
                upd = lax.dot_general(Bg, xw, TN_DIMS, preferred_element_type=f32)
                hf_ref[g] = ef[Q - 1:Q, sl] * hf_ref[g] + upd

            y = jnp.concatenate(y_groups, axis=1) + xs[rows] * coef
            y = y * _silu(z_ref[rows, :].astype(f32))
            for g in range(SSM_GROUPS):
                sl, _ = group_slices(g)
                o_ref[rows, sl] = _rms(y[:, sl], nw[:, sl]).astype(o_ref.dtype)


def _ssd_call(act, dtT, z, biasT, aT, dskipT, norm, triu, e4, e2, gT):
    B, L, _ = act.shape
    blk = min(SSD_BLOCK, L)
    Q = min(SSD_Q, blk)
    nsub = blk // Q
    nblk = L // blk

    def bidx(p, c):
        return jnp.where(p == 0, nblk - 1 - c, c)

    consts = (biasT, aT, dskipT, norm, triu, e4, e2, gT)
    in_specs = [
        pl.BlockSpec((None, blk, D_XBC), lambda b, p, c: (b, bidx(p, c), 0)),
        pl.BlockSpec((None, 2 * SSM_HEADS, blk), lambda b, p, c: (b, 0, bidx(p, c))),
        pl.BlockSpec((None, blk, D_SSM), lambda b, p, c: (b, p * c, 0)),
    ] + [_const_spec(a.shape) for a in consts]
    state = (SSM_GROUPS, D_STATE, GROUP_WIDTH)
    return pl.pallas_call(
        functools.partial(_ssd_kernel, nblk=nblk, nsub=nsub, Q=Q),
        grid=(B, 2, nblk),
        in_specs=in_specs,
        out_specs=pl.BlockSpec((None, blk, D_SSM), lambda b, p, c: (b, p * c, 0)),
        out_shape=jax.ShapeDtypeStruct((B, L, D_SSM), bf16),
        scratch_shapes=[pltpu.VMEM(state, f32), pltpu.VMEM(state, f32),
                        pltpu.VMEM((nblk * nsub,) + state, bf16)],
        name="ssd",
        compiler_params=pltpu.CompilerParams(
            dimension_semantics=("parallel", "arbitrary", "arbitrary"), vmem_limit_bytes=VMEM_LIMIT),
    )(act, dtT, z, *consts)


def _oproj_kernel(x_ref, aT_ref, s_ref, wa_ref, ws_ref, o_ref):
    o_ref[...] = (x_ref[...]
                  + lax.dot_general(aT_ref[...], wa_ref[...], TN_DIMS, preferred_element_type=f32)
                  + _dot(s_ref[...], ws_ref[...]))


def _oproj_call(x, attnT, ssm, wa, ws):
    B, L, _ = x.shape
    tm = min(OPROJ_TM, L)
    tok = lambda w: pl.BlockSpec((None, tm, w), lambda b, i: (b, i, 0))
    return pl.pallas_call(
        _oproj_kernel, grid=(B, L // tm),
        in_specs=[tok(D_MODEL), pl.BlockSpec((None, N_HEADS * V_HEAD, tm), lambda b, i: (b, 0, i)),
                  tok(D_SSM), _const_spec(wa.shape), _const_spec(ws.shape)],
        out_specs=tok(D_MODEL),
        out_shape=jax.ShapeDtypeStruct((B, L, D_MODEL), f32),
        name="oproj",
        compiler_params=pltpu.CompilerParams(
            dimension_semantics=("parallel", "parallel"), vmem_limit_bytes=VMEM_LIMIT),
    )(x, attnT, ssm, wa, ws)


def _ffn_kernel(x_ref, xp_ref, xn_ref, n2_ref, wg_ref, wu_ref, cw_ref, cb_ref, wd_ref, fn_ref,
                o_ref, h_scr, act_scr, *, nfc):
    i = pl.program_id(1)
    nt = pl.num_programs(1)
    tm = x_ref.shape[0]
    n2 = n2_ref[...]
    h_scr[0:tm, :] = _rms(x_ref[...], n2).astype(bf16)
    halo = jnp.concatenate([xp_ref[...], xn_ref[...]], axis=0)
    h_scr[tm:tm + 2 * F32_ROWS, :] = _rms(halo, n2).astype(bf16)
    has_prev = i > 0
    has_next = i < nt - 1

    def chunk(cf):
        col = cf * FFN_FC
        g_ext = _dot(h_scr[...], wg_ref[:, pl.ds(col, FFN_FC)])
        before = jnp.where(has_prev, g_ext[tm + F32_ROWS - 1:tm + F32_ROWS], 0.0)
        after = jnp.where(has_next, g_ext[tm + F32_ROWS:tm + F32_ROWS + 1], 0.0)
        gc = _conv3_rows(g_ext[:tm], before, after, cw_ref[:, pl.ds(col, FFN_FC)], cb_ref[:, pl.ds(col, FFN_FC)])
        u = _dot(h_scr[0:tm, :], wu_ref[:, pl.ds(col, FFN_FC)])
        act_scr[:, pl.ds(col, FFN_FC)] = (_silu(gc) * u).astype(bf16)

    for cf in range(nfc):
        chunk(cf)
    y = x_ref[...] + _dot(act_scr[...], wd_ref[...])
    o_ref[...] = _rms(y, fn_ref[...])


def _ffn_call(x, n2, wg, wu, cw, cb, wd, fn):
    B, L, _ = x.shape
    tm = min(FFN_TM, L)
    tok = pl.BlockSpec((None, tm, D_MODEL), lambda b, i: (b, i, 0))
    consts = (n2, wg, wu, cw, cb, wd, fn)
    return pl.pallas_call(
        functools.partial(_ffn_kernel, nfc=D_FF // FFN_FC),
        grid=(B, L // tm),
        in_specs=[tok] + _halo_specs(tm, L, D_MODEL) + [_const_spec(a.shape) for a in consts],
        out_specs=tok,
        out_shape=jax.ShapeDtypeStruct((B, L, D_MODEL), f32),
        scratch_shapes=[pltpu.VMEM((tm + 2 * F32_ROWS, D_MODEL), bf16), pltpu.VMEM((tm, D_FF), bf16)],
        name="ffn",
        compiler_params=pltpu.CompilerParams(
            dimension_semantics=("parallel", "parallel"), vmem_limit_bytes=VMEM_LIMIT),
    )(x, x, x, *consts)


def _head_lane_sources():
    src = np.full((HEAD_PAD,), QK_DIM, np.int32)
    half = HEAD_PAD // 2
    src[0:HALF_ROPE] = QK_NOPE + np.arange(HALF_ROPE)
    src[HALF_ROPE:half] = np.arange(half - HALF_ROPE)
    src[half:half + HALF_ROPE] = QK_NOPE + HALF_ROPE + np.arange(HALF_ROPE)
    n_rest = QK_NOPE - (half - HALF_ROPE)
    src[half + HALF_ROPE:half + HALF_ROPE + n_rest] = (half - HALF_ROPE) + np.arange(n_rest)
    return src


def _rope_tables(L):
    inv = ROPE_THETA ** (-jnp.arange(0, QK_ROPE, 2, dtype=f32) / QK_ROPE)
    ang = jnp.arange(L, dtype=f32)[:, None] * inv[None, :]
    cos, sin = jnp.cos(ang), jnp.sin(ang)
    half = HEAD_PAD // 2
    c_tab = jnp.ones((L, HEAD_PAD), f32)
    c_tab = c_tab.at[:, 0:HALF_ROPE].set(cos).at[:, half:half + HALF_ROPE].set(cos)
    s_tab = jnp.zeros((L, HEAD_PAD), f32)
    s_tab = s_tab.at[:, 0:HALF_ROPE].set(-sin).at[:, half:half + HALF_ROPE].set(sin)
    return c_tab, s_tab


def _expand_matrix(n):
    m = np.zeros((LANES, n * D_SSM), np.float32)
    for part in range(2):
        for j in range(n):
            for h in range(SSM_HEADS):
                r = part * n * SSM_HEADS + j * SSM_HEADS + h
                m[r, j * D_SSM + h * SSM_HEAD_DIM:j * D_SSM + (h + 1) * SSM_HEAD_DIM] = 1.0
    return jnp.asarray(m, bf16)


def _prepare_weights(norm1, w_in, q_a_norm, kv_a_norm, w_q_b, w_kv_b, conv_w, conv_b,
                     dt_bias_f, dt_bias_b, a_log_f, a_log_b, d_skip, ssm_norm, w_out,
                     norm2, w_gate, w_up, ffn_conv_w, ffn_conv_b, w_down, final_norm):
    half = HEAD_PAD // 2
    o_kr = Q_LORA + KV_LORA
    o_z = o_kr + QK_ROPE
    o_dt = o_z + D_SSM + D_XBC
    kr_blk = jnp.zeros((D_MODEL, HEAD_PAD), f32)
    kr_blk = kr_blk.at[:, 0:HALF_ROPE].set(w_in[:, o_kr:o_kr + HALF_ROPE])
    kr_blk = kr_blk.at[:, half:half + HALF_ROPE].set(w_in[:, o_kr + HALF_ROPE:o_kr + QK_ROPE])
    win = jnp.concatenate([w_in[:, :o_kr], kr_blk, w_in[:, o_z:o_dt]], axis=1).astype(bf16)
    w_dt = w_in[:, o_dt:o_dt + 2 * SSM_HEADS]

    src = _head_lane_sources()
    wq = w_q_b.reshape(Q_LORA, N_HEADS, QK_DIM)
    wq = jnp.concatenate([wq, jnp.zeros((Q_LORA, N_HEADS, 1), f32)], axis=-1)[:, :, src]
    wqT = wq.reshape(Q_LORA, N_HEADS * HEAD_PAD).T.astype(bf16)
    wkv = w_kv_b.reshape(KV_LORA, N_HEADS, QK_NOPE + V_HEAD)
    src_k = np.where(src < QK_NOPE, src, QK_NOPE)
    wk = jnp.concatenate([wkv[:, :, :QK_NOPE], jnp.zeros((KV_LORA, N_HEADS, 1), f32)], axis=-1)[:, :, src_k]
    wk = wk.reshape(KV_LORA, N_HEADS * HEAD_PAD).astype(bf16)
    wvT = wkv[:, :, QK_NOPE:].reshape(KV_LORA, N_HEADS * V_HEAD).T.astype(bf16)

    row = lambda v: v.reshape(1, -1).astype(f32)
    col = lambda v: v.reshape(-1, 1).astype(f32)
    a_neg = -jnp.exp(jnp.concatenate([a_log_f, a_log_b]).astype(f32)) * LOG2E
    group_of_lane = np.arange(GN) // D_STATE
    group_of_head = np.arange(SSM_HEADS) // HEADS_PER_GROUP
    gT = jnp.asarray(group_of_head[:, None] == group_of_lane[None, :], bf16)
    return dict(
        n1=row(norm1), win=win, qan=row(q_a_norm), kvan=row(kv_a_norm), wqT=wqT, wk=wk, wvT=wvT,
        wdtT=w_dt.T.astype(bf16), cw=conv_w.astype(f32), cb=row(conv_b),
        biasT=col(jnp.concatenate([dt_bias_f, dt_bias_b])), aT=col(a_neg), dskipT=col(d_skip),
        ssm_norm=row(ssm_norm), e4=_expand_matrix(4), e2=_expand_matrix(2), gT=gT,
        wa=w_out[:N_HEADS * V_HEAD].astype(bf16), ws=w_out[N_HEADS * V_HEAD:].astype(bf16),
        n2=row(norm2), wg=w_gate.astype(bf16), wu=w_up.astype(bf16), fcw=ffn_conv_w.astype(f32),
        fcb=row(ffn_conv_b), wd=w_down.astype(bf16), fn=row(final_norm),
    )


def _encoder(x, w):
    B, L, _ = x.shape
    c_tab, s_tab = _rope_tables(L)
    scale = QK_DIM ** -0.5 * LOG2E
    qT, k, vT, z, act, dtT, kn2 = _proj_call(
        x, w["n1"], w["win"], w["qan"], w["kvan"], w["wqT"], w["wk"], w["wvT"], w["wdtT"], w["cw"], w["cb"],
        c_tab, s_tab, (c_tab * scale).T, (s_tab * scale).T)
    kmax = jnp.sqrt(jnp.max(kn2, axis=1))[:, :, :1]
    kmax = jnp.broadcast_to(kmax[..., None], (B, N_HEADS, 1, min(ATTN_TQ, L)))
    attnT = _attn_call(qT, k, vT, kmax)
    blk = min(SSD_BLOCK, L)
    Q = min(SSD_Q, blk)
    triu = jnp.kron(jnp.eye(blk // Q, dtype=f32), jnp.triu(jnp.ones((Q, Q), f32))).astype(bf16)
    ssm = _ssd_call(act, dtT, z, w["biasT"], w["aT"], w["dskipT"], w["ssm_norm"], triu, w["e4"], w["e2"], w["gT"])
    x1 = _oproj_call(x, attnT, ssm, w["wa"], w["ws"])
    return _ffn_call(x1, w["n2"], w["wg"], w["wu"], w["fcw"], w["fcb"], w["wd"], w["fn"])


def kernel(x_prompt, x_sample, norm1, w_in, q_a_norm, kv_a_norm, w_q_b, w_kv_b, conv_w, conv_b,
           dt_bias_f, dt_bias_b, a_log_f, a_log_b, d_skip, ssm_norm, w_out, norm2, w_gate, w_up,
           ffn_conv_w, ffn_conv_b, w_down, final_norm):
    w = _prepare_weights(norm1[0], w_in[0], q_a_norm[0], kv_a_norm[0], w_q_b[0], w_kv_b[0], conv_w[0],
                         conv_b[0], dt_bias_f[0], dt_bias_b[0], a_log_f[0], a_log_b[0], d_skip[0],
                         ssm_norm[0], w_out[0], norm2[0], w_gate[0], w_up[0], ffn_conv_w[0],
                         ffn_conv_b[0], w_down[0], final_norm)
    return (_encoder(x_prompt, w), _encoder(x_sample, w))
```

```python
import functools

import numpy as np
import jax
import jax.numpy as jnp
from jax import lax
from jax.experimental import pallas as pl
from jax.experimental.pallas import tpu as pltpu

D_MODEL = 1024
N_HEADS = 16
QK_NOPE = 64
QK_ROPE = 32
HALF_ROPE = QK_ROPE // 2
QK_DIM = QK_NOPE + QK_ROPE
V_HEAD = 64
Q_LORA = 384
KV_LORA = 256
ROPE_THETA = 10000.0
SSM_HEADS = 16
SSM_HEAD_DIM = 64
D_SSM = SSM_HEADS * SSM_HEAD_DIM
SSM_GROUPS = 2
HEADS_PER_GROUP = SSM_HEADS // SSM_GROUPS
GROUP_WIDTH = D_SSM // SSM_GROUPS
D_STATE = 64
GN = SSM_GROUPS * D_STATE
D_XBC = D_SSM + 2 * GN
D_FF = 2816
EPS = 1e-6
LOG2E = float(np.log2(np.e))

LANES = 128
HEAD_PAD = LANES
BF16_ROWS = 16
F32_ROWS = 8
VMEM_LIMIT = 56 * 1024 * 1024

OFF_Q = 0
OFF_CKV = OFF_Q + Q_LORA
OFF_KR = OFF_CKV + KV_LORA
OFF_Z = OFF_KR + HEAD_PAD
OFF_XBC = OFF_Z + D_SSM
D_IN_PAD = OFF_XBC + D_XBC

PROJ_TM = 512
ATTN_VMEM_BUDGET = 34 * 1024 * 1024
ATTN_TK = 2048
ATTN_UNROLL = 4
ATTN_HEADS_PER_STEP = 2
SHIFT_ROW = QK_DIM
ATTN_BOUND_MARGIN = 1.0 + 2.0 ** -6
ATTN_DENOM_FLOOR = 2.0 ** -80
SSD_Q = 128
SSD_BLOCK = 512
OPROJ_TM = 512
FFN_TM = 512
FFN_FC = 256

NT_DIMS = (((1,), (1,)), ((), ()))
TN_DIMS = (((0,), (0,)), ((), ()))

f32 = jnp.float32
bf16 = jnp.bfloat16


def _rms(x, w):
    return x * lax.rsqrt(jnp.mean(x * x, axis=-1, keepdims=True) + EPS) * w


def _dot(a, b):
    return jnp.dot(a, b, preferred_element_type=f32)


def _silu(x):
    h = 0.5 * x
    return h * jnp.tanh(h) + h


def _conv3_rows(x, before, after, cw, cb):
    n = x.shape[0]
    sub = lax.broadcasted_iota(jnp.int32, (F32_ROWS, 1), 0)
    down = pltpu.roll(x, 1, axis=0)
    up = pltpu.roll(x, n - 1, axis=0)
    x_prev = jnp.concatenate([jnp.where(sub == 0, before, down[:F32_ROWS]), down[F32_ROWS:]], axis=0)
    x_next = jnp.concatenate([up[:n - F32_ROWS],
                              jnp.where(sub == F32_ROWS - 1, after, up[n - F32_ROWS:])], axis=0)
    return x_prev * cw[0:1] + x * cw[1:2] + x_next * cw[2:3] + cb


def _halo_specs(tm, L, width):
    hb = tm // F32_ROWS
    last = L // F32_ROWS - 1
    return [pl.BlockSpec((None, F32_ROWS, width), lambda b, i: (b, jnp.maximum(i * hb - 1, 0), 0)),
            pl.BlockSpec((None, F32_ROWS, width), lambda b, i: (b, jnp.minimum((i + 1) * hb, last), 0))]


def _const_spec(shape):
    nd = len(shape)
    return pl.BlockSpec(shape, lambda *_: (0,) * nd, pipeline_mode=pl.Buffered(1))


def _proj_kernel(x_ref, xp_ref, xn_ref, n1_ref, win_ref, qan_ref, kvan_ref, wqT_ref, wk_ref, wvT_ref, wdtT_ref,
                 cw_ref, cb_ref, c_ref, s_ref, cT_ref, sT_ref,
                 qT_out, k_out, vT_out, z_out, act_out, dtT_out, kn2_out, h_scr):
    i = pl.program_id(1)
    tm = x_ref.shape[0]
    n1 = n1_ref[...]
    h_scr[0:tm, :] = _rms(x_ref[...], n1).astype(bf16)
    halo = jnp.concatenate([xp_ref[...], xn_ref[...]], axis=0)
    h_scr[tm:tm + 2 * F32_ROWS, :] = _rms(halo, n1).astype(bf16)
    h = h_scr[0:tm, :]
    half = HEAD_PAD // 2
    lat = _dot(h, win_ref[:, OFF_Q:OFF_Z])
    xbc = _dot(h_scr[...], win_ref[:, OFF_XBC:OFF_XBC + D_XBC])
    hq = _rms(lat[:, OFF_Q:OFF_Q + Q_LORA], qan_ref[...]).astype(bf16)
    hc = _rms(lat[:, OFF_CKV:OFF_CKV + KV_LORA], kvan_ref[...]).astype(bf16)
    qT = lax.dot_general(wqT_ref[...], hq, NT_DIMS, preferred_element_type=f32)

    before = jnp.where(i > 0, xbc[tm + F32_ROWS - 1:tm + F32_ROWS], 0.0)
    after = jnp.where(i < pl.num_programs(1) - 1, xbc[tm + F32_ROWS:tm + F32_ROWS + 1], 0.0)
    act_out[...] = _silu(_conv3_rows(xbc[:tm], before, after, cw_ref[...], cb_ref[...])).astype(bf16)

    kn = _dot(hc, wk_ref[...])
    cT = cT_ref[...]
    sT = sT_ref[...]
    for hh in range(N_HEADS):
        blk = qT[hh * HEAD_PAD:(hh + 1) * HEAD_PAD, :]
        rot = jnp.concatenate([blk[half:], blk[:half]], axis=0)
        qT_out[hh * HEAD_PAD:(hh + 1) * HEAD_PAD, :] = (blk * cT + rot * sT).astype(bf16)

    z_out[...] = _dot(h, win_ref[:, OFF_Z:OFF_Z + D_SSM]).astype(bf16)
    kr = lat[:, OFF_KR:OFF_KR + HEAD_PAD]
    krf = kr * c_ref[...] + pltpu.roll(kr, half, axis=1) * s_ref[...]
    lane = lax.broadcasted_iota(jnp.int32, (1, HEAD_PAD), 1)
    shift_lanes = (lane == SHIFT_ROW) | (lane == SHIFT_ROW + 1)
    norms = []
    for hh in range(N_HEADS):
        kb = (kn[:, hh * HEAD_PAD:(hh + 1) * HEAD_PAD] + krf).astype(bf16)
        kf = kb.astype(f32)
        n2 = jnp.max(jnp.sum(kf * kf, axis=1, keepdims=True), axis=0, keepdims=True)
        norms.append(jnp.broadcast_to(n2, (1, HEAD_PAD)))
        k_out[hh] = jnp.where(shift_lanes, jnp.ones_like(kb), kb)
    kn2_out[...] = jnp.concatenate(norms, axis=0)
    vT_out[...] = lax.dot_general(wvT_ref[...], hc, NT_DIMS, preferred_element_type=f32).astype(bf16)
    dtT_out[...] = lax.dot_general(wdtT_ref[...], h, NT_DIMS, preferred_element_type=f32)


def _proj_call(x, n1, win, qan, kvan, wqT, wk, wvT, wdtT, cw, cb, c_tab, s_tab, cT_tab, sT_tab):
    B, L, _ = x.shape
    tm = min(PROJ_TM, L)
    grid = (B, L // tm)
    tok = lambda w: pl.BlockSpec((None, tm, w), lambda b, i: (b, i, 0))
    tokT = lambda w: pl.BlockSpec((None, w, tm), lambda b, i: (b, 0, i))
    consts = (n1, win, qan, kvan, wqT, wk, wvT, wdtT, cw, cb)
    in_specs = [tok(D_MODEL)] + _halo_specs(tm, L, D_MODEL) + [_const_spec(a.shape) for a in consts] + [
        pl.BlockSpec((tm, HEAD_PAD), lambda b, i: (i, 0)),
        pl.BlockSpec((tm, HEAD_PAD), lambda b, i: (i, 0)),
        pl.BlockSpec((HEAD_PAD, tm), lambda b, i: (0, i)),
        pl.BlockSpec((HEAD_PAD, tm), lambda b, i: (0, i)),
    ]
    out_shape = [
        jax.ShapeDtypeStruct((B, N_HEADS * HEAD_PAD, L), bf16),
        jax.ShapeDtypeStruct((B, N_HEADS, L, HEAD_PAD), bf16),
        jax.ShapeDtypeStruct((B, N_HEADS * V_HEAD, L), bf16),
        jax.ShapeDtypeStruct((B, L, D_SSM), bf16),
        jax.ShapeDtypeStruct((B, L, D_XBC), bf16),
        jax.ShapeDtypeStruct((B, 2 * SSM_HEADS, L), f32),
        jax.ShapeDtypeStruct((B, L // tm, N_HEADS, HEAD_PAD), f32),
    ]
    k_spec = pl.BlockSpec((None, N_HEADS, tm, HEAD_PAD), lambda b, i: (b, 0, i, 0))
    out_specs = [tokT(N_HEADS * HEAD_PAD), k_spec, tokT(N_HEADS * V_HEAD),
                 tok(D_SSM), tok(D_XBC), tokT(2 * SSM_HEADS),
                 pl.BlockSpec((None, None, N_HEADS, HEAD_PAD), lambda b, i: (b, i, 0, 0))]
    return pl.pallas_call(
        _proj_kernel, grid=grid, in_specs=in_specs, out_specs=out_specs, out_shape=out_shape,
        scratch_shapes=[pltpu.VMEM((tm + 2 * F32_ROWS, D_MODEL), bf16)],
        name="proj",
        compiler_params=pltpu.CompilerParams(
            dimension_semantics=("parallel", "parallel"), vmem_limit_bytes=VMEM_LIMIT),
    )(x, x, x, *consts, c_tab, s_tab, cT_tab, sT_tab)


def _attn_kernel(kmax_ref, qT_ref, k_ref, vT_ref, o_ref, q_scr, p_scr, acc_ref, dmin_ref, *, tq, tk, nq, nk, unroll, nh):
    ones = jnp.ones((BF16_ROWS, tk), bf16)
    ng = nk // unroll
    heads = range(nh)

    def q_cols(qi):
        return pl.ds(pl.multiple_of(qi * tq, tq), tq)

    def head_rows(hd, width):
        return slice(hd * width, (hd + 1) * width)

    def v_tile(hd, j):
        start = pl.multiple_of(j * tk, tk)
        return jnp.concatenate([vT_ref[head_rows(hd, V_HEAD), pl.ds(start, tk)], ones], axis=0)

    def k_tile(hd, j):
        return k_ref[hd, pl.ds(pl.multiple_of(j * tk, tk), tk), :]

    def shift_queries(qi, qslot):
        row = lax.broadcasted_iota(jnp.int32, (HEAD_PAD, 1), 0)
        for hd in heads:
            q = qT_ref[head_rows(hd, HEAD_PAD), q_cols(qi)]
            qf = q.astype(f32)
            bound = jnp.sqrt(jnp.sum(qf * qf, axis=0, keepdims=True)) * kmax_ref[hd] * ATTN_BOUND_MARGIN
            hi = bound.astype(bf16)
            lo = (bound - hi.astype(f32)).astype(bf16)
            q_scr[qslot, hd] = jnp.where(row == SHIFT_ROW, -hi, jnp.where(row == SHIFT_ROW + 1, -lo, q))

    def produce(hd, qslot, j, slot):
        p_scr[hd, slot] = jnp.exp2(_dot(k_tile(hd, j), q_scr[qslot, hd])).astype(bf16)

    def group(g, qslot, next_qslot):
        acc = [acc_ref[hd] for hd in heads]
        for u in range(unroll):
            j = g * unroll + u
            nxt = (u + 1) % 2
            for hd in heads:
                if next_qslot is False or u < unroll - 1:
                    produce(hd, qslot, j + 1, nxt)
                elif next_qslot is not None:
                    produce(hd, next_qslot, 0, nxt)
            for hd in heads:
                acc[hd] = acc[hd] + _dot(v_tile(hd, j), p_scr[hd, u % 2])
        for hd in heads:
            acc_ref[hd] = acc[hd]

    def recompute_exact(hd, qi):
        q = qT_ref[head_rows(hd, HEAD_PAD), q_cols(qi)]

        def body(j, carry):
            m, acc = carry
            s = _dot(k_tile(hd, j), q)
            m_new = jnp.maximum(m, jnp.max(s, axis=0, keepdims=True))
            p = jnp.exp2(s - m_new).astype(bf16)
            return m_new, jnp.exp2(m - m_new) * acc + _dot(v_tile(hd, j), p)

        m0 = jnp.full((1, tq), -jnp.inf, f32)
        _, acc = lax.fori_loop(0, nk, body, (m0, jnp.zeros((V_HEAD + BF16_ROWS, tq), f32)))
        write_output(hd, qi, acc)

    def write_output(hd, qi, acc):
        o_ref[head_rows(hd, V_HEAD), q_cols(qi)] = (acc[:V_HEAD] / acc[V_HEAD:V_HEAD + 1]).astype(o_ref.dtype)

    def query_tile(qi, qslot, next_qslot):
        if next_qslot is not None:
            shift_queries(qi + 1, next_qslot)
        acc_ref[...] = jnp.zeros(acc_ref.shape, f32)
        if ng > 1:
            def body(g, carry):
                group(g, qslot, False)
                return carry
            lax.fori_loop(0, ng - 1, body, 0)
        group(ng - 1, qslot, next_qslot)
        for hd in heads:
            acc = acc_ref[hd]
            write_output(hd, qi, acc)
            dmin_ref[hd] = jnp.minimum(dmin_ref[hd], acc[V_HEAD:V_HEAD + 1])

    dmin_ref[...] = jnp.full(dmin_ref.shape, jnp.inf, f32)
    shift_queries(0, 0)
    for hd in heads:
        produce(hd, 0, 0, 0)
    if nq > 1:
        def q_body(qi, carry):
            query_tile(qi, qi % 2, (qi + 1) % 2)
            return carry
        lax.fori_loop(0, nq - 1, q_body, 0)
    query_tile(nq - 1, (nq - 1) % 2, None)

    for hd in heads:
        @pl.when(jnp.logical_not(jnp.min(dmin_ref[hd]) >= ATTN_DENOM_FLOOR))
        def _(hd=hd):
            def redo(qi, carry):
                recompute_exact(hd, qi)
                return carry
            lax.fori_loop(0, nq, redo, 0)


def _attn_tiles(L):
    nh = ATTN_HEADS_PER_STEP
    tk = min(ATTN_TK, L // 2)
    kv_bytes = 2 * nh * L * (HEAD_PAD + V_HEAD) * 2
    tq = L
    while nh * 2 * tk * tq * 2 > ATTN_VMEM_BUDGET - kv_bytes and tq > LANES:
        tq //= 2
    return tq, tk


def _attn_call(qT, k, vT, kmax):
    B, _, L = qT.shape
    tq, tk = _attn_tiles(L)
    nk = L // tk
    unroll = min(ATTN_UNROLL, nk)
    assert unroll % 2 == 0 and nk % unroll == 0, "key tiles alternate between two probability buffers"
    nq = L // tq if nk == unroll else 1
    tqb = nq * tq
    nh = ATTN_HEADS_PER_STEP
    return pl.pallas_call(
        functools.partial(_attn_kernel, tq=tq, tk=tk, nq=nq, nk=nk, unroll=unroll, nh=nh),
        grid=(B, N_HEADS // nh, L // tqb),
        scratch_shapes=[pltpu.VMEM((2, nh, HEAD_PAD, tq), bf16), pltpu.VMEM((nh, 2, tk, tq), bf16),
                        pltpu.VMEM((nh, V_HEAD + BF16_ROWS, tq), f32), pltpu.VMEM((nh, 1, tq), f32)],
        in_specs=[
            pl.BlockSpec((None, nh, 1, tq), lambda b, h, i: (b, h, 0, 0)),
            pl.BlockSpec((None, nh * HEAD_PAD, tqb), lambda b, h, i: (b, h, i)),
            pl.BlockSpec((None, nh, L, HEAD_PAD), lambda b, h, i: (b, h, 0, 0)),
            pl.BlockSpec((None, nh * V_HEAD, L), lambda b, h, i: (b, h, 0)),
        ],
        out_specs=pl.BlockSpec((None, nh * V_HEAD, tqb), lambda b, h, i: (b, h, i)),
        out_shape=jax.ShapeDtypeStruct((B, N_HEADS * V_HEAD, L), bf16),
        name="attn",
        compiler_params=pltpu.CompilerParams(
            dimension_semantics=("parallel", "parallel", "arbitrary"), vmem_limit_bytes=VMEM_LIMIT),
    )(kmax, qT, k, vT)


def _split3(v):
    a1 = v.astype(bf16)
    r1 = v - a1.astype(f32)
    a2 = r1.astype(bf16)
    a3 = (r1 - a2.astype(f32)).astype(bf16)
    return a1, a2, a3


def _chunk_scalars(dtT_raw, biasT, aT, triu):
    nh2 = 2 * SSM_HEADS
    dtT = jax.nn.softplus(dtT_raw + biasT)
    stepT = dtT * aT
    cs3 = _dot(jnp.concatenate(_split3(stepT), axis=0), triu)
    csT = cs3[0:nh2] + cs3[nh2:2 * nh2] + cs3[2 * nh2:3 * nh2]
    return dtT, stepT, csT


def _expand_heads(partsT, e_ref):
    n = len(partsT)
    stk = jnp.concatenate(partsT, axis=0)
    hi = stk.astype(bf16).astype(f32)
    pieces = [hi, stk - hi]
    pad = LANES - 2 * stk.shape[0]
    if pad:
        pieces.append(jnp.zeros((pad, stk.shape[1]), f32))
    nat = jnp.transpose(jnp.concatenate(pieces, axis=0)).astype(bf16)
    full = _dot(nat, e_ref[...])
    return [full[:, j * D_SSM:(j + 1) * D_SSM] for j in range(n)]


def _ssd_kernel(act_ref, dtT_ref, z_ref, biasT_ref, aT_ref, dskipT_ref, norm_ref, triu_ref, e4_ref, e2_ref,
                gT_ref, o_ref, hf_ref, hb_ref, hbs_ref, *, nblk, nsub, Q):
    ph = pl.program_id(1)
    c = pl.program_id(2)
    H = SSM_HEADS

    def group_slices(g):
        return slice(g * GROUP_WIDTH, (g + 1) * GROUP_WIDTH), slice(g * D_STATE, (g + 1) * D_STATE)

    dtT, stepT, csT = _chunk_scalars(dtT_ref[...], biasT_ref[...], aT_ref[...], triu_ref[...])
    totT = jnp.concatenate(
        [jnp.broadcast_to(csT[:, (j + 1) * Q - 1:(j + 1) * Q], (2 * H, Q)) for j in range(nsub)], axis=1)
    cbT_b = csT[H:] - stepT[H:]
    Bm = act_ref[:, D_SSM:D_SSM + GN]

    @pl.when(ph == 0)
    def _backward_states():
        @pl.when(c == 0)
        def _():
            hb_ref[...] = jnp.zeros_like(hb_ref)

        blk = nblk - 1 - c
        wb, eb = _expand_heads([jnp.exp2(cbT_b) * dtT[H:], jnp.exp2(totT[H:] - cbT_b)], e2_ref)
        xw = (act_ref[:, :D_SSM].astype(f32) * wb).astype(bf16)
        for j in reversed(range(nsub)):
            rows = slice(j * Q, (j + 1) * Q)
            hbs_ref[blk * nsub + j] = hb_ref[...].astype(bf16)
            for g in range(SSM_GROUPS):
                sl, sn = group_slices(g)
                upd = lax.dot_general(Bm[rows, sn], xw[rows, sl], TN_DIMS, preferred_element_type=f32)
                hb_ref[g] = eb[j * Q:j * Q + 1, sl] * hb_ref[g] + upd

    @pl.when(ph == 1)
    def _outputs():
        @pl.when(c == 0)
        def _():
            hf_ref[...] = jnp.zeros_like(hf_ref)

        ti = lax.broadcasted_iota(jnp.int32, (Q, Q), 0)
        si = lax.broadcasted_iota(jnp.int32, (Q, Q), 1)
        lower = ti >= si
        lane = lax.broadcasted_iota(jnp.int32, (1, LANES), 1)
        first_half = lane < SSM_HEAD_DIM
        gT = gT_ref[...]
        nw = norm_ref[...]

        xs_b = act_ref[:, :D_SSM]
        Cm = act_ref[:, D_SSM + GN:]
        xs = xs_b.astype(f32)
        csT_f = csT[:H]
        l2dt = jnp.log2(dtT)
        rowf = csT_f - l2dt[:H]
        rowb = cbT_b + l2dt[H:]
        cols = jnp.transpose(jnp.concatenate([csT_f, cbT_b], axis=0))
        cb = Cm.astype(f32) * Bm.astype(f32)
        cb_h = cb.astype(bf16)
        cb_l = (cb - cb_h.astype(f32)).astype(bf16)
        diagT = (lax.dot_general(gT, cb_h, NT_DIMS, preferred_element_type=f32)
                 + lax.dot_general(gT, cb_l, NT_DIMS, preferred_element_type=f32))
        partsT = [jnp.exp2(csT_f), jnp.exp2(totT[H:] - cbT_b), jnp.exp2(totT[:H] - csT_f) * dtT[:H],
                  dskipT_ref[...] + dtT[H:] * diagT]

        def expand_chunk(j):
            return _expand_heads([p[:, j * Q:(j + 1) * Q] for p in partsT], e4_ref)

        expanded = expand_chunk(0)
        for j in range(nsub):
            rows = slice(j * Q, (j + 1) * Q)
            chunk_id = c * nsub + j
            ef, eb, wf, coef = expanded
            if j + 1 < nsub:
                expanded = expand_chunk(j + 1)
            y_groups = []
            for g in range(SSM_GROUPS):
                sl, sn = group_slices(g)
                Cg = Cm[rows, sn]
                Bg = Bm[rows, sn]
                cbm = lax.dot_general(Cg, Bg, NT_DIMS, preferred_element_type=f32)
                y_off = (ef[:, sl] * _dot(Cg, hf_ref[g].astype(bf16))
                         + eb[:, sl] * _dot(Cg, hbs_ref[chunk_id, g]))
                pairs = []
                for hp in range(HEADS_PER_GROUP // 2):
                    lo = g * GROUP_WIDTH + hp * LANES
                    x_pair = xs_b[rows, lo:lo + LANES]
                    y_pair = None
                    for k in range(2):
                        hh = g * HEADS_PER_GROUP + 2 * hp + k
                        arg = jnp.where(lower, cols[rows, hh:hh + 1] - rowf[hh:hh + 1, rows],
                                        rowb[hh:hh + 1, rows] - cols[rows, H + hh:H + hh + 1])
                        mat = (cbm * jnp.exp2(arg)).astype(bf16)
                        keep = first_half if k == 0 else jnp.logical_not(first_half)
                        contrib = _dot(mat, jnp.where(keep, x_pair, jnp.zeros_like(x_pair)))
                        y_pair = contrib if y_pair is None else y_pair + contrib
                    pairs.append(y_pair)
                y_groups.append(jnp.concatenate(pairs, axis=1) + y_off)
                xw = (xs[rows, sl] * wf[:, sl]).astype(bf16)
                upd = lax.dot_general(Bg, xw, TN_DIMS, preferred_element_type=f32)
                hf_ref[g] = ef[Q - 1:Q, sl] * hf_ref[g] + upd

            y = jnp.concatenate(y_groups, axis=1) + xs[rows] * coef
            y = y * _silu(z_ref[rows, :].astype(f32))
            for g in range(SSM_GROUPS):
                sl, _ = group_slices(g)
                o_ref[rows, sl] = _rms(y[:, sl], nw[:, sl]).astype(o_ref.dtype)


def _ssd_call(act, dtT, z, biasT, aT, dskipT, norm, triu, e4, e2, gT):
    B, L, _ = act.shape
    blk = min(SSD_BLOCK, L)
    Q = min(SSD_Q, blk)
    nsub = blk // Q
    nblk = L // blk

    def bidx(p, c):
        return jnp.where(p == 0, nblk - 1 - c, c)

    consts = (biasT, aT, dskipT, norm, triu, e4, e2, gT)
    in_specs = [
        pl.BlockSpec((None, blk, D_XBC), lambda b, p, c: (b, bidx(p, c), 0)),
        pl.BlockSpec((None, 2 * SSM_HEADS, blk), lambda b, p, c: (b, 0, bidx(p, c))),
        pl.BlockSpec((None, blk, D_SSM), lambda b, p, c: (b, p * c, 0)),
    ] + [_const_spec(a.shape) for a in consts]
    state = (SSM_GROUPS, D_STATE, GROUP_WIDTH)
    return pl.pallas_call(
        functools.partial(_ssd_kernel, nblk=nblk, nsub=nsub, Q=Q),
        grid=(B, 2, nblk),
        in_specs=in_specs,
        out_specs=pl.BlockSpec((None, blk, D_SSM), lambda b, p, c: (b, p * c, 0)),
        out_shape=jax.ShapeDtypeStruct((B, L, D_SSM), bf16),
        scratch_shapes=[pltpu.VMEM(state, f32), pltpu.VMEM(state, f32),
                        pltpu.VMEM((nblk * nsub,) + state, bf16)],
        name="ssd",
        compiler_params=pltpu.CompilerParams(
            dimension_semantics=("parallel", "arbitrary", "arbitrary"), vmem_limit_bytes=VMEM_LIMIT),
    )(act, dtT, z, *consts)


def _oproj_kernel(x_ref, aT_ref, s_ref, wa_ref, ws_ref, o_ref):
    o_ref[...] = (x_ref[...]
                  + lax.dot_general(aT_ref[...], wa_ref[...], TN_DIMS, preferred_element_type=f32)
                  + _dot(s_ref[...], ws_ref[...]))


def _oproj_call(x, attnT, ssm, wa, ws):
    B, L, _ = x.shape
    tm = min(OPROJ_TM, L)
    tok = lambda w: pl.BlockSpec((None, tm, w), lambda b, i: (b, i, 0))
    return pl.pallas_call(
        _oproj_kernel, grid=(B, L // tm),
        in_specs=[tok(D_MODEL), pl.BlockSpec((None, N_HEADS * V_HEAD, tm), lambda b, i: (b, 0, i)),
                  tok(D_SSM), _const_spec(wa.shape), _const_spec(ws.shape)],
        out_specs=tok(D_MODEL),
        out_shape=jax.ShapeDtypeStruct((B, L, D_MODEL), f32),
        name="oproj",
        compiler_params=pltpu.CompilerParams(
            dimension_semantics=("parallel", "parallel"), vmem_limit_bytes=VMEM_LIMIT),
    )(x, attnT, ssm, wa, ws)


def _ffn_kernel(x_ref, xp_ref, xn_ref, n2_ref, wg_ref, wu_ref, cw_ref, cb_ref, wd_ref, fn_ref,
                o_ref, h_scr, act_scr, *, nfc):
    i = pl.program_id(1)
    nt = pl.num_programs(1)
    tm = x_ref.shape[0]
    n2 = n2_ref[...]
    h_scr[0:tm, :] = _rms(x_ref[...], n2).astype(bf16)
    halo = jnp.concatenate([xp_ref[...], xn_ref[...]], axis=0)
    h_scr[tm:tm + 2 * F32_ROWS, :] = _rms(halo, n2).astype(bf16)
    has_prev = i > 0
    has_next = i < nt - 1

    def chunk(cf):
        col = cf * FFN_FC
        g_ext = _dot(h_scr[...], wg_ref[:, pl.ds(col, FFN_FC)])
        before = jnp.where(has_prev, g_ext[tm + F32_ROWS - 1:tm + F32_ROWS], 0.0)
        after = jnp.where(has_next, g_ext[tm + F32_ROWS:tm + F32_ROWS + 1], 0.0)
        gc = _conv3_rows(g_ext[:tm], before, after, cw_ref[:, pl.ds(col, FFN_FC)], cb_ref[:, pl.ds(col, FFN_FC)])
        u = _dot(h_scr[0:tm, :], wu_ref[:, pl.ds(col, FFN_FC)])
        act_scr[:, pl.ds(col, FFN_FC)] = (_silu(gc) * u).astype(bf16)

    for cf in range(nfc):
        chunk(cf)
    y = x_ref[...] + _dot(act_scr[...], wd_ref[...])
    o_ref[...] = _rms(y, fn_ref[...])


def _ffn_call(x, n2, wg, wu, cw, cb, wd, fn):
    B, L, _ = x.shape
    tm = min(FFN_TM, L)
    tok = pl.BlockSpec((None, tm, D_MODEL), lambda b, i: (b, i, 0))
    consts = (n2, wg, wu, cw, cb, wd, fn)
    return pl.pallas_call(
        functools.partial(_ffn_kernel, nfc=D_FF // FFN_FC),
        grid=(B, L // tm),
        in_specs=[tok] + _halo_specs(tm, L, D_MODEL) + [_const_spec(a.shape) for a in consts],
        out_specs=tok,
        out_shape=jax.ShapeDtypeStruct((B, L, D_MODEL), f32),
        scratch_shapes=[pltpu.VMEM((tm + 2 * F32_ROWS, D_MODEL), bf16), pltpu.VMEM((tm, D_FF), bf16)],
        name="ffn",
        compiler_params=pltpu.CompilerParams(
            dimension_semantics=("parallel", "parallel"), vmem_limit_bytes=VMEM_LIMIT),
    )(x, x, x, *consts)


def _head_lane_sources():
    src = np.full((HEAD_PAD,), QK_DIM, np.int32)
    half = HEAD_PAD // 2
    src[0:HALF_ROPE] = QK_NOPE + np.arange(HALF_ROPE)
    src[HALF_ROPE:half] = np.arange(half - HALF_ROPE)
    src[half:half + HALF_ROPE] = QK_NOPE + HALF_ROPE + np.arange(HALF_ROPE)
    n_rest = QK_NOPE - (half - HALF_ROPE)
    src[half + HALF_ROPE:half + HALF_ROPE + n_rest] = (half - HALF_ROPE) + np.arange(n_rest)
    return src


def _rope_tables(L):
    inv = ROPE_THETA ** (-jnp.arange(0, QK_ROPE, 2, dtype=f32) / QK_ROPE)
    ang = jnp.arange(L, dtype=f32)[:, None] * inv[None, :]
    cos, sin = jnp.cos(ang), jnp.sin(ang)
    half = HEAD_PAD // 2
    c_tab = jnp.ones((L, HEAD_PAD), f32)
    c_tab = c_tab.at[:, 0:HALF_ROPE].set(cos).at[:, half:half + HALF_ROPE].set(cos)
    s_tab = jnp.zeros((L, HEAD_PAD), f32)
    s_tab = s_tab.at[:, 0:HALF_ROPE].set(-sin).at[:, half:half + HALF_ROPE].set(sin)
    return c_tab, s_tab


def _expand_matrix(n):
    m = np.zeros((LANES, n * D_SSM), np.float32)
    for part in range(2):
        for j in range(n):
            for h in range(SSM_HEADS):
                r = part * n * SSM_HEADS + j * SSM_HEADS + h
                m[r, j * D_SSM + h * SSM_HEAD_DIM:j * D_SSM + (h + 1) * SSM_HEAD_DIM] = 1.0
    return jnp.asarray(m, bf16)


def _prepare_weights(norm1, w_in, q_a_norm, kv_a_norm, w_q_b, w_kv_b, conv_w, conv_b,
                     dt_bias_f, dt_bias_b, a_log_f, a_log_b, d_skip, ssm_norm, w_out,
                     norm2, w_gate, w_up, ffn_conv_w, ffn_conv_b, w_down, final_norm):
    half = HEAD_PAD // 2
    o_kr = Q_LORA + KV_LORA
    o_z = o_kr + QK_ROPE
    o_dt = o_z + D_SSM + D_XBC
    kr_blk = jnp.zeros((D_MODEL, HEAD_PAD), f32)
    kr_blk = kr_blk.at[:, 0:HALF_ROPE].set(w_in[:, o_kr:o_kr + HALF_ROPE])
    kr_blk = kr_blk.at[:, half:half + HALF_ROPE].set(w_in[:, o_kr + HALF_ROPE:o_kr + QK_ROPE])
    win = jnp.concatenate([w_in[:, :o_kr], kr_blk, w_in[:, o_z:o_dt]], axis=1).astype(bf16)
    w_dt = w_in[:, o_dt:o_dt + 2 * SSM_HEADS]

    src = _head_lane_sources()
    wq = w_q_b.reshape(Q_LORA, N_HEADS, QK_DIM)
    wq = jnp.concatenate([wq, jnp.zeros((Q_LORA, N_HEADS, 1), f32)], axis=-1)[:, :, src]
    wqT = wq.reshape(Q_LORA, N_HEADS * HEAD_PAD).T.astype(bf16)
    wkv = w_kv_b.reshape(KV_LORA, N_HEADS, QK_NOPE + V_HEAD)
    src_k = np.where(src < QK_NOPE, src, QK_NOPE)
    wk = jnp.concatenate([wkv[:, :, :QK_NOPE], jnp.zeros((KV_LORA, N_HEADS, 1), f32)], axis=-1)[:, :, src_k]
    wk = wk.reshape(KV_LORA, N_HEADS * HEAD_PAD).astype(bf16)
    wvT = wkv[:, :, QK_NOPE:].reshape(KV_LORA, N_HEADS * V_HEAD).T.astype(bf16)

    row = lambda v: v.reshape(1, -1).astype(f32)
    col = lambda v: v.reshape(-1, 1).astype(f32)
    a_neg = -jnp.exp(jnp.concatenate([a_log_f, a_log_b]).astype(f32)) * LOG2E
    group_of_lane = np.arange(GN) // D_STATE
    group_of_head = np.arange(SSM_HEADS) // HEADS_PER_GROUP
    gT = jnp.asarray(group_of_head[:, None] == group_of_lane[None, :], bf16)
    return dict(
        n1=row(norm1), win=win, qan=row(q_a_norm), kvan=row(kv_a_norm), wqT=wqT, wk=wk, wvT=wvT,
        wdtT=w_dt.T.astype(bf16), cw=conv_w.astype(f32), cb=row(conv_b),
        biasT=col(jnp.concatenate([dt_bias_f, dt_bias_b])), aT=col(a_neg), dskipT=col(d_skip),
        ssm_norm=row(ssm_norm), e4=_expand_matrix(4), e2=_expand_matrix(2), gT=gT,
        wa=w_out[:N_HEADS * V_HEAD].astype(bf16), ws=w_out[N_HEADS * V_HEAD:].astype(bf16),
        n2=row(norm2), wg=w_gate.astype(bf16), wu=w_up.astype(bf16), fcw=ffn_conv_w.astype(f32),
        fcb=row(ffn_conv_b), wd=w_down.astype(bf16), fn=row(final_norm),
    )


def _encoder(x, w):
    B, L, _ = x.shape
    c_tab, s_tab = _rope_tables(L)
    scale = QK_DIM ** -0.5 * LOG2E
    qT, k, vT, z, act, dtT, kn2 = _proj_call(
        x, w["n1"], w["win"], w["qan"], w["kvan"], w["wqT"], w["wk"], w["wvT"], w["wdtT"], w["cw"], w["cb"],
        c_tab, s_tab, (c_tab * scale).T, (s_tab * scale).T)
    kmax = jnp.sqrt(jnp.max(kn2, axis=1))[:, :, :1]
    kmax = jnp.broadcast_to(kmax[..., None], (B, N_HEADS, 1, _attn_tiles(L)[0]))
    attnT = _attn_call(qT, k, vT, kmax)
    blk = min(SSD_BLOCK, L)
    Q = min(SSD_Q, blk)
    triu = jnp.kron(jnp.eye(blk // Q, dtype=f32), jnp.triu(jnp.ones((Q, Q), f32))).astype(bf16)
    ssm = _ssd_call(act, dtT, z, w["biasT"], w["aT"], w["dskipT"], w["ssm_norm"], triu, w["e4"], w["e2"], w["gT"])
    x1 = _oproj_call(x, attnT, ssm, w["wa"], w["ws"])
    return _ffn_call(x1, w["n2"], w["wg"], w["wu"], w["fcw"], w["fcb"], w["wd"], w["fn"])


def kernel(x_prompt, x_sample, norm1, w_in, q_a_norm, kv_a_norm, w_q_b, w_kv_b, conv_w, conv_b,
           dt_bias_f, dt_bias_b, a_log_f, a_log_b, d_skip, ssm_norm, w_out, norm2, w_gate, w_up,
           ffn_conv_w, ffn_conv_b, w_down, final_norm):
    w = _prepare_weights(norm1[0], w_in[0], q_a_norm[0], kv_a_norm[0], w_q_b[0], w_kv_b[0], conv_w[0],
                         conv_b[0], dt_bias_f[0], dt_bias_b[0], a_log_f[0], a_log_b[0], d_skip[0],
                         ssm_norm[0], w_out[0], norm2[0], w_gate[0], w_up[0], ffn_conv_w[0],
                         ffn_conv_b[0], w_down[0], final_norm)
    return (_encoder(x_prompt, w), _encoder(x_sample, w))
```

```python
import functools

import numpy as np
import jax
import jax.numpy as jnp
from jax import lax
from jax.experimental import pallas as pl
from jax.experimental.pallas import tpu as pltpu

D_MODEL = 1024
N_HEADS = 16
QK_NOPE = 64
QK_ROPE = 32
HALF_ROPE = QK_ROPE // 2
QK_DIM = QK_NOPE + QK_ROPE
V_HEAD = 64
Q_LORA = 384
KV_LORA = 256
ROPE_THETA = 10000.0
SSM_HEADS = 16
SSM_HEAD_DIM = 64
D_SSM = SSM_HEADS * SSM_HEAD_DIM
SSM_GROUPS = 2
HEADS_PER_GROUP = SSM_HEADS // SSM_GROUPS
GROUP_WIDTH = D_SSM // SSM_GROUPS
D_STATE = 64
GN = SSM_GROUPS * D_STATE
D_XBC = D_SSM + 2 * GN
D_FF = 2816
EPS = 1e-6
LOG2E = float(np.log2(np.e))

LANES = 128
HEAD_PAD = LANES
BF16_ROWS = 16
F32_ROWS = 8
VMEM_LIMIT = 56 * 1024 * 1024

OFF_Q = 0
OFF_CKV = OFF_Q + Q_LORA
OFF_KR = OFF_CKV + KV_LORA
OFF_Z = OFF_KR + HEAD_PAD
OFF_XBC = OFF_Z + D_SSM
D_IN_PAD = OFF_XBC + D_XBC

PROJ_TM = 512
PROJ_CHUNK = 256
ATTN_VMEM_BUDGET = 34 * 1024 * 1024
ATTN_TK = 2048
ATTN_UNROLL = 4
ATTN_HEADS_PER_STEP = 2
SHIFT_ROW = QK_DIM
ATTN_BOUND_MARGIN = 1.0 + 2.0 ** -6
ATTN_DENOM_FLOOR = 2.0 ** -80
SSD_Q = 128
SSD_BLOCK = 512
OPROJ_TM = 1024
FFN_TM = 1024
FFN_FC = 256

NT_DIMS = (((1,), (1,)), ((), ()))
TN_DIMS = (((0,), (0,)), ((), ()))

f32 = jnp.float32
bf16 = jnp.bfloat16


def _rms(x, w):
    return x * lax.rsqrt(jnp.mean(x * x, axis=-1, keepdims=True) + EPS) * w


def _dot(a, b):
    return jnp.dot(a, b, preferred_element_type=f32)


def _silu(x):
    h = 0.5 * x
    return h * jnp.tanh(h) + h


def _conv3_rows(x, before, after, cw, cb):
    n = x.shape[0]
    sub = lax.broadcasted_iota(jnp.int32, (F32_ROWS, 1), 0)
    down = pltpu.roll(x, 1, axis=0)
    up = pltpu.roll(x, n - 1, axis=0)
    x_prev = jnp.concatenate([jnp.where(sub == 0, before, down[:F32_ROWS]), down[F32_ROWS:]], axis=0)
    x_next = jnp.concatenate([up[:n - F32_ROWS],
                              jnp.where(sub == F32_ROWS - 1, after, up[n - F32_ROWS:])], axis=0)
    return x_prev * cw[0:1] + x * cw[1:2] + x_next * cw[2:3] + cb


def _halo_specs(tm, L, width):
    hb = tm // F32_ROWS
    last = L // F32_ROWS - 1
    return [pl.BlockSpec((None, F32_ROWS, width), lambda b, i: (b, jnp.maximum(i * hb - 1, 0), 0)),
            pl.BlockSpec((None, F32_ROWS, width), lambda b, i: (b, jnp.minimum((i + 1) * hb, last), 0))]


def _const_spec(shape):
    nd = len(shape)
    return pl.BlockSpec(shape, lambda *_: (0,) * nd, pipeline_mode=pl.Buffered(1))


def _proj_kernel(x_ref, xp_ref, xn_ref, n1_ref, win_ref, qan_ref, kvan_ref, wqT_ref, wk_ref, wvT_ref, wdtT_ref,
                 cw_ref, cb_ref, c_ref, s_ref, cT_ref, sT_ref,
                 qT_out, k_out, vT_out, z_out, act_out, dtT_out, kn2_out, h_scr):
    i = pl.program_id(1)
    tm = x_ref.shape[0]
    n1 = n1_ref[...]
    h_scr[0:tm, :] = _rms(x_ref[...], n1).astype(bf16)
    halo = jnp.concatenate([xp_ref[...], xn_ref[...]], axis=0)
    h_scr[tm:tm + 2 * F32_ROWS, :] = _rms(halo, n1).astype(bf16)
    h = h_scr[0:tm, :]
    half = HEAD_PAD // 2
    lat = _dot(h, win_ref[:, OFF_Q:OFF_Z])
    hq = _rms(lat[:, OFF_Q:OFF_Q + Q_LORA], qan_ref[...]).astype(bf16)
    hc = _rms(lat[:, OFF_CKV:OFF_CKV + KV_LORA], kvan_ref[...]).astype(bf16)
    kr = lat[:, OFF_KR:OFF_KR + HEAD_PAD]
    krf = kr * c_ref[...] + pltpu.roll(kr, half, axis=1) * s_ref[...]

    has_prev = i > 0
    has_next = i < pl.num_programs(1) - 1
    cT = cT_ref[...]
    sT = sT_ref[...]
    lane = lax.broadcasted_iota(jnp.int32, (1, HEAD_PAD), 1)
    shift_lanes = (lane == SHIFT_ROW) | (lane == SHIFT_ROW + 1)
    heads_per_chunk = PROJ_CHUNK // HEAD_PAD
    norms = [None] * N_HEADS

    def ssd_chunk(col):
        xbc = _dot(h_scr[...], win_ref[:, OFF_XBC + col:OFF_XBC + col + PROJ_CHUNK])
        before = jnp.where(has_prev, xbc[tm + F32_ROWS - 1:tm + F32_ROWS], 0.0)
        after = jnp.where(has_next, xbc[tm + F32_ROWS:tm + F32_ROWS + 1], 0.0)
        conv = _conv3_rows(xbc[:tm], before, after, cw_ref[:, col:col + PROJ_CHUNK], cb_ref[:, col:col + PROJ_CHUNK])
        act_out[:, col:col + PROJ_CHUNK] = _silu(conv).astype(bf16)

    def qk_chunk(h0):
        rows = slice(h0 * HEAD_PAD, (h0 + heads_per_chunk) * HEAD_PAD)
        qT = lax.dot_general(wqT_ref[rows, :], hq, NT_DIMS, preferred_element_type=f32)
        kn = _dot(hc, wk_ref[:, rows])
        for hh in range(heads_per_chunk):
            blk = qT[hh * HEAD_PAD:(hh + 1) * HEAD_PAD, :]
            rot = jnp.concatenate([blk[half:], blk[:half]], axis=0)
            qT_out[(h0 + hh) * HEAD_PAD:(h0 + hh + 1) * HEAD_PAD, :] = (blk * cT + rot * sT).astype(bf16)
            kb = (kn[:, hh * HEAD_PAD:(hh + 1) * HEAD_PAD] + krf).astype(bf16)
            kf = kb.astype(f32)
            n2 = jnp.max(jnp.sum(kf * kf, axis=1, keepdims=True), axis=0, keepdims=True)
            norms[h0 + hh] = jnp.broadcast_to(n2, (1, HEAD_PAD))
            k_out[h0 + hh] = jnp.where(shift_lanes, jnp.ones_like(kb), kb)

    def zv_chunk(col):
        z_out[:, col:col + PROJ_CHUNK] = _dot(h, win_ref[:, OFF_Z + col:OFF_Z + col + PROJ_CHUNK]).astype(bf16)
        vT_out[col:col + PROJ_CHUNK, :] = lax.dot_general(
            wvT_ref[col:col + PROJ_CHUNK, :], hc, NT_DIMS, preferred_element_type=f32).astype(bf16)

    light = ([functools.partial(qk_chunk, h0) for h0 in range(0, N_HEADS, heads_per_chunk)]
             + [functools.partial(zv_chunk, col) for col in range(0, D_SSM, PROJ_CHUNK)])
    heavy = [functools.partial(ssd_chunk, col) for col in range(0, D_XBC, PROJ_CHUNK)]
    per_heavy = -(-len(light) // len(heavy))
    for n, chunk in enumerate(heavy):
        chunk()
        for other in light[n * per_heavy:(n + 1) * per_heavy]:
            other()
    kn2_out[...] = jnp.concatenate(norms, axis=0)
    dtT_out[...] = lax.dot_general(wdtT_ref[...], h, NT_DIMS, preferred_element_type=f32)


def _proj_call(x, n1, win, qan, kvan, wqT, wk, wvT, wdtT, cw, cb, c_tab, s_tab, cT_tab, sT_tab):
    B, L, _ = x.shape
    tm = min(PROJ_TM, L)
    grid = (B, L // tm)
    tok = lambda w: pl.BlockSpec((None, tm, w), lambda b, i: (b, i, 0))
    tokT = lambda w: pl.BlockSpec((None, w, tm), lambda b, i: (b, 0, i))
    consts = (n1, win, qan, kvan, wqT, wk, wvT, wdtT, cw, cb)
    in_specs = [tok(D_MODEL)] + _halo_specs(tm, L, D_MODEL) + [_const_spec(a.shape) for a in consts] + [
        pl.BlockSpec((tm, HEAD_PAD), lambda b, i: (i, 0)),
        pl.BlockSpec((tm, HEAD_PAD), lambda b, i: (i, 0)),
        pl.BlockSpec((HEAD_PAD, tm), lambda b, i: (0, i)),
        pl.BlockSpec((HEAD_PAD, tm), lambda b, i: (0, i)),
    ]
    out_shape = [
        jax.ShapeDtypeStruct((B, N_HEADS * HEAD_PAD, L), bf16),
        jax.ShapeDtypeStruct((B, N_HEADS, L, HEAD_PAD), bf16),
        jax.ShapeDtypeStruct((B, N_HEADS * V_HEAD, L), bf16),
        jax.ShapeDtypeStruct((B, L, D_SSM), bf16),
        jax.ShapeDtypeStruct((B, L, D_XBC), bf16),
        jax.ShapeDtypeStruct((B, 2 * SSM_HEADS, L), f32),
        jax.ShapeDtypeStruct((B, L // tm, N_HEADS, HEAD_PAD), f32),
    ]
    k_spec = pl.BlockSpec((None, N_HEADS, tm, HEAD_PAD), lambda b, i: (b, 0, i, 0))
    out_specs = [tokT(N_HEADS * HEAD_PAD), k_spec, tokT(N_HEADS * V_HEAD),
                 tok(D_SSM), tok(D_XBC), tokT(2 * SSM_HEADS),
                 pl.BlockSpec((None, None, N_HEADS, HEAD_PAD), lambda b, i: (b, i, 0, 0))]
    return pl.pallas_call(
        _proj_kernel, grid=grid, in_specs=in_specs, out_specs=out_specs, out_shape=out_shape,
        scratch_shapes=[pltpu.VMEM((tm + 2 * F32_ROWS, D_MODEL), bf16)],
        name="proj",
        compiler_params=pltpu.CompilerParams(
            dimension_semantics=("parallel", "parallel"), vmem_limit_bytes=VMEM_LIMIT),
    )(x, x, x, *consts, c_tab, s_tab, cT_tab, sT_tab)


def _attn_kernel(kmax_ref, qT_ref, k_ref, vT_ref, o_ref, q_scr, p_scr, psum_scr, acc_ref, den_ref, dmin_ref, *, tq, tk, nq, nk, unroll, nh):
    ng = nk // unroll
    heads = range(nh)

    def q_cols(qi):
        return pl.ds(pl.multiple_of(qi * tq, tq), tq)

    def head_rows(hd, width):
        return slice(hd * width, (hd + 1) * width)

    def v_tile(hd, j):
        return vT_ref[head_rows(hd, V_HEAD), pl.ds(pl.multiple_of(j * tk, tk), tk)]

    def k_tile(hd, j):
        return k_ref[hd, pl.ds(pl.multiple_of(j * tk, tk), tk), :]

    def shift_queries(qi, qslot):
        row = lax.broadcasted_iota(jnp.int32, (HEAD_PAD, 1), 0)
        for hd in heads:
            q = qT_ref[head_rows(hd, HEAD_PAD), q_cols(qi)]
            qf = q.astype(f32)
            bound = jnp.sqrt(jnp.sum(qf * qf, axis=0, keepdims=True)) * kmax_ref[hd] * ATTN_BOUND_MARGIN
            hi = bound.astype(bf16)
            lo = (bound - hi.astype(f32)).astype(bf16)
            q_scr[qslot, hd] = jnp.where(row == SHIFT_ROW, -hi, jnp.where(row == SHIFT_ROW + 1, -lo, q))

    def produce(hd, qslot, j, slot):
        p = jnp.exp2(_dot(k_tile(hd, j), q_scr[qslot, hd]))
        p_scr[hd, slot] = p.astype(bf16)
        psum_scr[hd, slot] = jnp.sum(p, axis=0, keepdims=True)

    def group(g, qslot, next_qslot):
        acc = [acc_ref[hd] for hd in heads]
        den = [den_ref[hd] for hd in heads]
        for u in range(unroll):
            j = g * unroll + u
            nxt = (u + 1) % 2
            for hd in heads:
                if next_qslot is False or u < unroll - 1:
                    produce(hd, qslot, j + 1, nxt)
                elif next_qslot is not None:
                    produce(hd, next_qslot, 0, nxt)
            for hd in heads:
                acc[hd] = acc[hd] + _dot(v_tile(hd, j), p_scr[hd, u % 2])
                den[hd] = den[hd] + psum_scr[hd, u % 2]
        for hd in heads:
            acc_ref[hd] = acc[hd]
            den_ref[hd] = den[hd]

    def recompute_exact(hd, qi):
        q = qT_ref[head_rows(hd, HEAD_PAD), q_cols(qi)]

        def body(j, carry):
            m, den, acc = carry
            s = _dot(k_tile(hd, j), q)
            m_new = jnp.maximum(m, jnp.max(s, axis=0, keepdims=True))
            alpha = jnp.exp2(m - m_new)
            p = jnp.exp2(s - m_new)
            return (m_new, alpha * den + jnp.sum(p, axis=0, keepdims=True),
                    alpha * acc + _dot(v_tile(hd, j), p.astype(bf16)))

        init = (jnp.full((1, tq), -jnp.inf, f32), jnp.zeros((1, tq), f32), jnp.zeros((V_HEAD, tq), f32))
        _, den, acc = lax.fori_loop(0, nk, body, init)
        write_output(hd, qi, acc, den)

    def write_output(hd, qi, acc, den):
        o_ref[head_rows(hd, V_HEAD), q_cols(qi)] = (acc / den).astype(o_ref.dtype)

    def query_tile(qi, qslot, next_qslot):
        if next_qslot is not None:
            shift_queries(qi + 1, next_qslot)
        acc_ref[...] = jnp.zeros(acc_ref.shape, f32)
        den_ref[...] = jnp.zeros(den_ref.shape, f32)
        if ng > 1:
            def body(g, carry):
                group(g, qslot, False)
                return carry
            lax.fori_loop(0, ng - 1, body, 0)
        group(ng - 1, qslot, next_qslot)
        for hd in heads:
            den = den_ref[hd]
            write_output(hd, qi, acc_ref[hd], den)
            dmin_ref[hd] = jnp.minimum(dmin_ref[hd], den)

    dmin_ref[...] = jnp.full(dmin_ref.shape, jnp.inf, f32)
    shift_queries(0, 0)
    for hd in heads:
        produce(hd, 0, 0, 0)
    if nq > 1:
        def q_body(qi, carry):
            query_tile(qi, qi % 2, (qi + 1) % 2)
            return carry
        lax.fori_loop(0, nq - 1, q_body, 0)
    query_tile(nq - 1, (nq - 1) % 2, None)

    for hd in heads:
        @pl.when(jnp.logical_not(jnp.min(dmin_ref[hd]) >= ATTN_DENOM_FLOOR))
        def _(hd=hd):
            def redo(qi, carry):
                recompute_exact(hd, qi)
                return carry
            lax.fori_loop(0, nq, redo, 0)


def _attn_tiles(L):
    nh = ATTN_HEADS_PER_STEP
    tk = min(ATTN_TK, L // 2)
    kv_bytes = 2 * nh * L * (HEAD_PAD + V_HEAD) * 2
    tq = L
    while nh * 2 * tk * tq * 2 > ATTN_VMEM_BUDGET - kv_bytes and tq > LANES:
        tq //= 2
    return tq, tk


def _attn_call(qT, k, vT, kmax):
    B, _, L = qT.shape
    tq, tk = _attn_tiles(L)
    nk = L // tk
    unroll = min(ATTN_UNROLL, nk)
    assert unroll % 2 == 0 and nk % unroll == 0, "key tiles alternate between two probability buffers"
    nq = L // tq if nk == unroll else 1
    tqb = nq * tq
    nh = ATTN_HEADS_PER_STEP
    return pl.pallas_call(
        functools.partial(_attn_kernel, tq=tq, tk=tk, nq=nq, nk=nk, unroll=unroll, nh=nh),
        grid=(B, N_HEADS // nh, L // tqb),
        scratch_shapes=[pltpu.VMEM((2, nh, HEAD_PAD, tq), bf16), pltpu.VMEM((nh, 2, tk, tq), bf16),
                        pltpu.VMEM((nh, 2, 1, tq), f32), pltpu.VMEM((nh, V_HEAD, tq), f32),
                        pltpu.VMEM((nh, 1, tq), f32), pltpu.VMEM((nh, 1, tq), f32)],
        in_specs=[
            pl.BlockSpec((None, nh, 1, tq), lambda b, h, i: (b, h, 0, 0)),
            pl.BlockSpec((None, nh * HEAD_PAD, tqb), lambda b, h, i: (b, h, i)),
            pl.BlockSpec((None, nh, L, HEAD_PAD), lambda b, h, i: (b, h, 0, 0)),
            pl.BlockSpec((None, nh * V_HEAD, L), lambda b, h, i: (b, h, 0)),
        ],
        out_specs=pl.BlockSpec((None, nh * V_HEAD, tqb), lambda b, h, i: (b, h, i)),
        out_shape=jax.ShapeDtypeStruct((B, N_HEADS * V_HEAD, L), bf16),
        name="attn",
        compiler_params=pltpu.CompilerParams(
            dimension_semantics=("parallel", "parallel", "arbitrary"), vmem_limit_bytes=VMEM_LIMIT),
    )(kmax, qT, k, vT)


def _split3(v):
    a1 = v.astype(bf16)
    r1 = v - a1.astype(f32)
    a2 = r1.astype(bf16)
    a3 = (r1 - a2.astype(f32)).astype(bf16)
    return a1, a2, a3


def _chunk_scalars(dtT_raw, biasT, aT, triu):
    nh2 = 2 * SSM_HEADS
    dtT = jax.nn.softplus(dtT_raw + biasT)
    stepT = dtT * aT
    cs3 = _dot(jnp.concatenate(_split3(stepT), axis=0), triu)
    csT = cs3[0:nh2] + cs3[nh2:2 * nh2] + cs3[2 * nh2:3 * nh2]
    return dtT, stepT, csT


def _expand_heads(partsT, e_ref):
    n = len(partsT)
    stk = jnp.concatenate(partsT, axis=0)
    hi = stk.astype(bf16).astype(f32)
    pieces = [hi, stk - hi]
    pad = LANES - 2 * stk.shape[0]
    if pad:
        pieces.append(jnp.zeros((pad, stk.shape[1]), f32))
    nat = jnp.transpose(jnp.concatenate(pieces, axis=0)).astype(bf16)
    full = _dot(nat, e_ref[...])
    return [full[:, j * D_SSM:(j + 1) * D_SSM] for j in range(n)]


def _ssd_kernel(act_ref, dtT_ref, z_ref, biasT_ref, aT_ref, dskipT_ref, norm_ref, triu_ref, e4_ref, e2_ref,
                gT_ref, o_ref, hf_ref, hb_ref, hbs_ref, *, nblk, nsub, Q):
    ph = pl.program_id(1)
    c = pl.program_id(2)
    H = SSM_HEADS

    def group_slices(g):
        return slice(g * GROUP_WIDTH, (g + 1) * GROUP_WIDTH), slice(g * D_STATE, (g + 1) * D_STATE)

    dtT, stepT, csT = _chunk_scalars(dtT_ref[...], biasT_ref[...], aT_ref[...], triu_ref[...])
    totT = jnp.concatenate(
        [jnp.broadcast_to(csT[:, (j + 1) * Q - 1:(j + 1) * Q], (2 * H, Q)) for j in range(nsub)], axis=1)
    cbT_b = csT[H:] - stepT[H:]
    Bm = act_ref[:, D_SSM:D_SSM + GN]

    @pl.when(ph == 0)
    def _backward_states():
        @pl.when(c == 0)
        def _():
            hb_ref[...] = jnp.zeros_like(hb_ref)

        blk = nblk - 1 - c
        wb, eb = _expand_heads([jnp.exp2(cbT_b) * dtT[H:], jnp.exp2(totT[H:] - cbT_b)], e2_ref)
        xw = (act_ref[:, :D_SSM].astype(f32) * wb).astype(bf16)
        for j in reversed(range(nsub)):
            rows = slice(j * Q, (j + 1) * Q)
            hbs_ref[blk * nsub + j] = hb_ref[...].astype(bf16)
            for g in range(SSM_GROUPS):
                sl, sn = group_slices(g)
                upd = lax.dot_general(Bm[rows, sn], xw[rows, sl], TN_DIMS, preferred_element_type=f32)
                hb_ref[g] = eb[j * Q:j * Q + 1, sl] * hb_ref[g] + upd

    @pl.when(ph == 1)
    def _outputs():
        @pl.when(c == 0)
        def _():
            hf_ref[...] = jnp.zeros_like(hf_ref)

        ti = lax.broadcasted_iota(jnp.int32, (Q, Q), 0)
        si = lax.broadcasted_iota(jnp.int32, (Q, Q), 1)
        lower = ti >= si
        lane = lax.broadcasted_iota(jnp.int32, (1, LANES), 1)
        first_half = lane < SSM_HEAD_DIM
        gT = gT_ref[...]
        nw = norm_ref[...]

        xs_b = act_ref[:, :D_SSM]
        Cm = act_ref[:, D_SSM + GN:]
        xs = xs_b.astype(f32)
        csT_f = csT[:H]
        l2dt = jnp.log2(dtT)
        rowf = csT_f - l2dt[:H]
        rowb = cbT_b + l2dt[H:]
        cols = jnp.transpose(jnp.concatenate([csT_f, cbT_b], axis=0))
        cb = Cm.astype(f32) * Bm.astype(f32)
        cb_h = cb.astype(bf16)
        cb_l = (cb - cb_h.astype(f32)).astype(bf16)
        diagT = (lax.dot_general(gT, cb_h, NT_DIMS, preferred_element_type=f32)
                 + lax.dot_general(gT, cb_l, NT_DIMS, preferred_element_type=f32))
        partsT = [jnp.exp2(csT_f), jnp.exp2(totT[H:] - cbT_b), jnp.exp2(totT[:H] - csT_f) * dtT[:H],
                  dskipT_ref[...] + dtT[H:] * diagT]

        def expand_chunk(j):
            return _expand_heads([p[:, j * Q:(j + 1) * Q] for p in partsT], e4_ref)

        expanded = expand_chunk(0)
        for j in range(nsub):
            rows = slice(j * Q, (j + 1) * Q)
            chunk_id = c * nsub + j
            ef, eb, wf, coef = expanded
            if j + 1 < nsub:
                expanded = expand_chunk(j + 1)
            y_groups = []
            for g in range(SSM_GROUPS):
                sl, sn = group_slices(g)
                Cg = Cm[rows, sn]
                Bg = Bm[rows, sn]
                cbm = lax.dot_general(Cg, Bg, NT_DIMS, preferred_element_type=f32)
                y_off = (ef[:, sl] * _dot(Cg, hf_ref[g].astype(bf16))
                         + eb[:, sl] * _dot(Cg, hbs_ref[chunk_id, g]))
                pairs = []
                for hp in range(HEADS_PER_GROUP // 2):
                    lo = g * GROUP_WIDTH + hp * LANES
                    x_pair = xs_b[rows, lo:lo + LANES]
                    y_pair = None
                    for k in range(2):
                        hh = g * HEADS_PER_GROUP + 2 * hp + k
                        arg = jnp.where(lower, cols[rows, hh:hh + 1] - rowf[hh:hh + 1, rows],
                                        rowb[hh:hh + 1, rows] - cols[rows, H + hh:H + hh + 1])
                        mat = (cbm * jnp.exp2(arg)).astype(bf16)
                        keep = first_half if k == 0 else jnp.logical_not(first_half)
                        contrib = _dot(mat, jnp.where(keep, x_pair, jnp.zeros_like(x_pair)))
                        y_pair = contrib if y_pair is None else y_pair + contrib
                    pairs.append(y_pair)
                y_groups.append(jnp.concatenate(pairs, axis=1) + y_off)
                xw = (xs[rows, sl] * wf[:, sl]).astype(bf16)
                upd = lax.dot_general(Bg, xw, TN_DIMS, preferred_element_type=f32)
                hf_ref[g] = ef[Q - 1:Q, sl] * hf_ref[g] + upd

            y = jnp.concatenate(y_groups, axis=1) + xs[rows] * coef
            y = y * _silu(z_ref[rows, :].astype(f32))
            for g in range(SSM_GROUPS):
                sl, _ = group_slices(g)
                o_ref[rows, sl] = _rms(y[:, sl], nw[:, sl]).astype(o_ref.dtype)


def _ssd_call(act, dtT, z, biasT, aT, dskipT, norm, triu, e4, e2, gT):
    B, L, _ = act.shape
    blk = min(SSD_BLOCK, L)
    Q = min(SSD_Q, blk)
    nsub = blk // Q
    nblk = L // blk

    def bidx(p, c):
        return jnp.where(p == 0, nblk - 1 - c, c)

    consts = (biasT, aT, dskipT, norm, triu, e4, e2, gT)
    in_specs = [
        pl.BlockSpec((None, blk, D_XBC), lambda b, p, c: (b, bidx(p, c), 0)),
        pl.BlockSpec((None, 2 * SSM_HEADS, blk), lambda b, p, c: (b, 0, bidx(p, c))),
        pl.BlockSpec((None, blk, D_SSM), lambda b, p, c: (b, p * c, 0)),
    ] + [_const_spec(a.shape) for a in consts]
    state = (SSM_GROUPS, D_STATE, GROUP_WIDTH)
    return pl.pallas_call(
        functools.partial(_ssd_kernel, nblk=nblk, nsub=nsub, Q=Q),
        grid=(B, 2, nblk),
        in_specs=in_specs,
        out_specs=pl.BlockSpec((None, blk, D_SSM), lambda b, p, c: (b, p * c, 0)),
        out_shape=jax.ShapeDtypeStruct((B, L, D_SSM), bf16),
        scratch_shapes=[pltpu.VMEM(state, f32), pltpu.VMEM(state, f32),
                        pltpu.VMEM((nblk * nsub,) + state, bf16)],
        name="ssd",
        compiler_params=pltpu.CompilerParams(
            dimension_semantics=("parallel", "arbitrary", "arbitrary"), vmem_limit_bytes=VMEM_LIMIT),
    )(act, dtT, z, *consts)


def _oproj_kernel(x_ref, aT_ref, s_ref, wa_ref, ws_ref, o_ref):
    o_ref[...] = (x_ref[...]
                  + lax.dot_general(aT_ref[...], wa_ref[...], TN_DIMS, preferred_element_type=f32)
                  + _dot(s_ref[...], ws_ref[...]))


def _oproj_call(x, attnT, ssm, wa, ws):
    B, L, _ = x.shape
    tm = min(OPROJ_TM, L)
    tok = lambda w: pl.BlockSpec((None, tm, w), lambda b, i: (b, i, 0))
    return pl.pallas_call(
        _oproj_kernel, grid=(B, L // tm),
        in_specs=[tok(D_MODEL), pl.BlockSpec((None, N_HEADS * V_HEAD, tm), lambda b, i: (b, 0, i)),
                  tok(D_SSM), _const_spec(wa.shape), _const_spec(ws.shape)],
        out_specs=tok(D_MODEL),
        out_shape=jax.ShapeDtypeStruct((B, L, D_MODEL), f32),
        name="oproj",
        compiler_params=pltpu.CompilerParams(
            dimension_semantics=("parallel", "parallel"), vmem_limit_bytes=VMEM_LIMIT),
    )(x, attnT, ssm, wa, ws)


def _ffn_kernel(x_ref, xp_ref, xn_ref, n2_ref, wg_ref, wu_ref, cw_ref, cb_ref, wd_ref, fn_ref,
                o_ref, h_scr, act_scr, *, nfc):
    i = pl.program_id(1)
    nt = pl.num_programs(1)
    tm = x_ref.shape[0]
    n2 = n2_ref[...]
    h_scr[0:tm, :] = _rms(x_ref[...], n2).astype(bf16)
    halo = jnp.concatenate([xp_ref[...], xn_ref[...]], axis=0)
    h_scr[tm:tm + 2 * F32_ROWS, :] = _rms(halo, n2).astype(bf16)
    has_prev = i > 0
    has_next = i < nt - 1

    def chunk(cf):
        col = cf * FFN_FC
        g_ext = _dot(h_scr[...], wg_ref[:, pl.ds(col, FFN_FC)])
        before = jnp.where(has_prev, g_ext[tm + F32_ROWS - 1:tm + F32_ROWS], 0.0)
        after = jnp.where(has_next, g_ext[tm + F32_ROWS:tm + F32_ROWS + 1], 0.0)
        gc = _conv3_rows(g_ext[:tm], before, after, cw_ref[:, pl.ds(col, FFN_FC)], cb_ref[:, pl.ds(col, FFN_FC)])
        u = _dot(h_scr[0:tm, :], wu_ref[:, pl.ds(col, FFN_FC)])
        act_scr[:, pl.ds(col, FFN_FC)] = (_silu(gc) * u).astype(bf16)

    for cf in range(nfc):
        chunk(cf)
    y = x_ref[...] + _dot(act_scr[...], wd_ref[...])
    o_ref[...] = _rms(y, fn_ref[...])


def _ffn_call(x, n2, wg, wu, cw, cb, wd, fn):
    B, L, _ = x.shape
    tm = min(FFN_TM, L)
    tok = pl.BlockSpec((None, tm, D_MODEL), lambda b, i: (b, i, 0))
    consts = (n2, wg, wu, cw, cb, wd, fn)
    return pl.pallas_call(
        functools.partial(_ffn_kernel, nfc=D_FF // FFN_FC),
        grid=(B, L // tm),
        in_specs=[tok] + _halo_specs(tm, L, D_MODEL) + [_const_spec(a.shape) for a in consts],
        out_specs=tok,
        out_shape=jax.ShapeDtypeStruct((B, L, D_MODEL), f32),
        scratch_shapes=[pltpu.VMEM((tm + 2 * F32_ROWS, D_MODEL), bf16), pltpu.VMEM((tm, D_FF), bf16)],
        name="ffn",
        compiler_params=pltpu.CompilerParams(
            dimension_semantics=("parallel", "parallel"), vmem_limit_bytes=VMEM_LIMIT),
    )(x, x, x, *consts)


def _head_lane_sources():
    src = np.full((HEAD_PAD,), QK_DIM, np.int32)
    half = HEAD_PAD // 2
    src[0:HALF_ROPE] = QK_NOPE + np.arange(HALF_ROPE)
    src[HALF_ROPE:half] = np.arange(half - HALF_ROPE)
    src[half:half + HALF_ROPE] = QK_NOPE + HALF_ROPE + np.arange(HALF_ROPE)
    n_rest = QK_NOPE - (half - HALF_ROPE)
    src[half + HALF_ROPE:half + HALF_ROPE + n_rest] = (half - HALF_ROPE) + np.arange(n_rest)
    return src


def _rope_tables(L):
    inv = ROPE_THETA ** (-jnp.arange(0, QK_ROPE, 2, dtype=f32) / QK_ROPE)
    ang = jnp.arange(L, dtype=f32)[:, None] * inv[None, :]
    cos, sin = jnp.cos(ang), jnp.sin(ang)
    half = HEAD_PAD // 2
    c_tab = jnp.ones((L, HEAD_PAD), f32)
    c_tab = c_tab.at[:, 0:HALF_ROPE].set(cos).at[:, half:half + HALF_ROPE].set(cos)
    s_tab = jnp.zeros((L, HEAD_PAD), f32)
    s_tab = s_tab.at[:, 0:HALF_ROPE].set(-sin).at[:, half:half + HALF_ROPE].set(sin)
    return c_tab, s_tab


def _expand_matrix(n):
    m = np.zeros((LANES, n * D_SSM), np.float32)
    for part in range(2):
        for j in range(n):
            for h in range(SSM_HEADS):
                r = part * n * SSM_HEADS + j * SSM_HEADS + h
                m[r, j * D_SSM + h * SSM_HEAD_DIM:j * D_SSM + (h + 1) * SSM_HEAD_DIM] = 1.0
    return jnp.asarray(m, bf16)


def _prepare_weights(norm1, w_in, q_a_norm, kv_a_norm, w_q_b, w_kv_b, conv_w, conv_b,
                     dt_bias_f, dt_bias_b, a_log_f, a_log_b, d_skip, ssm_norm, w_out,
                     norm2, w_gate, w_up, ffn_conv_w, ffn_conv_b, w_down, final_norm):
    half = HEAD_PAD // 2
    o_kr = Q_LORA + KV_LORA
    o_z = o_kr + QK_ROPE
    o_dt = o_z + D_SSM + D_XBC
    kr_blk = jnp.zeros((D_MODEL, HEAD_PAD), f32)
    kr_blk = kr_blk.at[:, 0:HALF_ROPE].set(w_in[:, o_kr:o_kr + HALF_ROPE])
    kr_blk = kr_blk.at[:, half:half + HALF_ROPE].set(w_in[:, o_kr + HALF_ROPE:o_kr + QK_ROPE])
    win = jnp.concatenate([w_in[:, :o_kr], kr_blk, w_in[:, o_z:o_dt]], axis=1).astype(bf16)
    w_dt = w_in[:, o_dt:o_dt + 2 * SSM_HEADS]

    src = _head_lane_sources()
    wq = w_q_b.reshape(Q_LORA, N_HEADS, QK_DIM)
    wq = jnp.concatenate([wq, jnp.zeros((Q_LORA, N_HEADS, 1), f32)], axis=-1)[:, :, src]
    wqT = wq.reshape(Q_LORA, N_HEADS * HEAD_PAD).T.astype(bf16)
    wkv = w_kv_b.reshape(KV_LORA, N_HEADS, QK_NOPE + V_HEAD)
    src_k = np.where(src < QK_NOPE, src, QK_NOPE)
    wk = jnp.concatenate([wkv[:, :, :QK_NOPE], jnp.zeros((KV_LORA, N_HEADS, 1), f32)], axis=-1)[:, :, src_k]
    wk = wk.reshape(KV_LORA, N_HEADS * HEAD_PAD).astype(bf16)
    wvT = wkv[:, :, QK_NOPE:].reshape(KV_LORA, N_HEADS * V_HEAD).T.astype(bf16)

    row = lambda v: v.reshape(1, -1).astype(f32)
    col = lambda v: v.reshape(-1, 1).astype(f32)
    a_neg = -jnp.exp(jnp.concatenate([a_log_f, a_log_b]).astype(f32)) * LOG2E
    group_of_lane = np.arange(GN) // D_STATE
    group_of_head = np.arange(SSM_HEADS) // HEADS_PER_GROUP
    gT = jnp.asarray(group_of_head[:, None] == group_of_lane[None, :], bf16)
    return dict(
        n1=row(norm1), win=win, qan=row(q_a_norm), kvan=row(kv_a_norm), wqT=wqT, wk=wk, wvT=wvT,
        wdtT=w_dt.T.astype(bf16), cw=conv_w.astype(f32), cb=row(conv_b),
        biasT=col(jnp.concatenate([dt_bias_f, dt_bias_b])), aT=col(a_neg), dskipT=col(d_skip),
        ssm_norm=row(ssm_norm), e4=_expand_matrix(4), e2=_expand_matrix(2), gT=gT,
        wa=w_out[:N_HEADS * V_HEAD].astype(bf16), ws=w_out[N_HEADS * V_HEAD:].astype(bf16),
        n2=row(norm2), wg=w_gate.astype(bf16), wu=w_up.astype(bf16), fcw=ffn_conv_w.astype(f32),
        fcb=row(ffn_conv_b), wd=w_down.astype(bf16), fn=row(final_norm),
    )


def _encoder(x, w):
    B, L, _ = x.shape
    c_tab, s_tab = _rope_tables(L)
    scale = QK_DIM ** -0.5 * LOG2E
    qT, k, vT, z, act, dtT, kn2 = _proj_call(
        x, w["n1"], w["win"], w["qan"], w["kvan"], w["wqT"], w["wk"], w["wvT"], w["wdtT"], w["cw"], w["cb"],
        c_tab, s_tab, (c_tab * scale).T, (s_tab * scale).T)
    kmax = jnp.sqrt(jnp.max(kn2, axis=1))[:, :, :1]
    kmax = jnp.broadcast_to(kmax[..., None], (B, N_HEADS, 1, _attn_tiles(L)[0]))
    attnT = _attn_call(qT, k, vT, kmax)
    blk = min(SSD_BLOCK, L)
    Q = min(SSD_Q, blk)
    triu = jnp.kron(jnp.eye(blk // Q, dtype=f32), jnp.triu(jnp.ones((Q, Q), f32))).astype(bf16)
    ssm = _ssd_call(act, dtT, z, w["biasT"], w["aT"], w["dskipT"], w["ssm_norm"], triu, w["e4"], w["e2"], w["gT"])
    x1 = _oproj_call(x, attnT, ssm, w["wa"], w["ws"])
    return _ffn_call(x1, w["n2"], w["wg"], w["wu"], w["fcw"], w["fcb"], w["wd"], w["fn"])


def kernel(x_prompt, x_sample, norm1, w_in, q_a_norm, kv_a_norm, w_q_b, w_kv_b, conv_w, conv_b,
           dt_bias_f, dt_bias_b, a_log_f, a_log_b, d_skip, ssm_norm, w_out, norm2, w_gate, w_up,
           ffn_conv_w, ffn_conv_b, w_down, final_norm):
    w = _prepare_weights(norm1[0], w_in[0], q_a_norm[0], kv_a_norm[0], w_q_b[0], w_kv_b[0], conv_w[0],
                         conv_b[0], dt_bias_f[0], dt_bias_b[0], a_log_f[0], a_log_b[0], d_skip[0],
                         ssm_norm[0], w_out[0], norm2[0], w_gate[0], w_up[0], ffn_conv_w[0],
                         ffn_conv_b[0], w_down[0], final_norm)
    return (_encoder(x_prompt, w), _encoder(x_sample, w))
```

```python
import functools

import numpy as np
import jax
import jax.numpy as jnp
from jax import lax
from jax.experimental import pallas as pl
from jax.experimental.pallas import tpu as pltpu

D_MODEL = 1024
N_HEADS = 16
QK_NOPE = 64
QK_ROPE = 32
HALF_ROPE = QK_ROPE // 2
QK_DIM = QK_NOPE + QK_ROPE
V_HEAD = 64
Q_LORA = 384
KV_LORA = 256
ROPE_THETA = 10000.0
SSM_HEADS = 16
SSM_HEAD_DIM = 64
D_SSM = SSM_HEADS * SSM_HEAD_DIM
SSM_GROUPS = 2
HEADS_PER_GROUP = SSM_HEADS // SSM_GROUPS
GROUP_WIDTH = D_SSM // SSM_GROUPS
D_STATE = 64
GN = SSM_GROUPS * D_STATE
D_XBC = D_SSM + 2 * GN
D_FF = 2816
EPS = 1e-6
LOG2E = float(np.log2(np.e))

LANES = 128
HEAD_PAD = LANES
BF16_ROWS = 16
F32_ROWS = 8
VMEM_LIMIT = 56 * 1024 * 1024

OFF_Q = 0
OFF_CKV = OFF_Q + Q_LORA
OFF_KR = OFF_CKV + KV_LORA
OFF_Z = OFF_KR + HEAD_PAD
OFF_XBC = OFF_Z + D_SSM
D_IN_PAD = OFF_XBC + D_XBC

PROJ_TM = 512
PROJ_CHUNK = 256
ATTN_VMEM_BUDGET = 42 * 1024 * 1024
ATTN_TK = 2048
ATTN_UNROLL = 4
ATTN_HEADS_PER_STEP = 2
SHIFT_ROW = QK_DIM
ATTN_BOUND_MARGIN = 1.0 + 2.0 ** -6
ATTN_DENOM_FLOOR = 2.0 ** -80
SSD_Q = 128
SSD_BLOCK = 512
OPROJ_TM = 1024
FFN_TM = 1024
FFN_FC = 256

NT_DIMS = (((1,), (1,)), ((), ()))
TN_DIMS = (((0,), (0,)), ((), ()))

f32 = jnp.float32
bf16 = jnp.bfloat16


def _rms(x, w):
    return x * lax.rsqrt(jnp.mean(x * x, axis=-1, keepdims=True) + EPS) * w


def _dot(a, b):
    return jnp.dot(a, b, preferred_element_type=f32)


def _silu(x):
    h = 0.5 * x
    return h * jnp.tanh(h) + h


def _conv3_rows(x, before, after, cw, cb):
    n = x.shape[0]
    sub = lax.broadcasted_iota(jnp.int32, (F32_ROWS, 1), 0)
    down = pltpu.roll(x, 1, axis=0)
    up = pltpu.roll(x, n - 1, axis=0)
    x_prev = jnp.concatenate([jnp.where(sub == 0, before, down[:F32_ROWS]), down[F32_ROWS:]], axis=0)
    x_next = jnp.concatenate([up[:n - F32_ROWS],
                              jnp.where(sub == F32_ROWS - 1, after, up[n - F32_ROWS:])], axis=0)
    return x_prev * cw[0:1] + x * cw[1:2] + x_next * cw[2:3] + cb


def _halo_specs(tm, L, width):
    hb = tm // F32_ROWS
    last = L // F32_ROWS - 1
    return [pl.BlockSpec((None, F32_ROWS, width), lambda b, i: (b, jnp.maximum(i * hb - 1, 0), 0)),
            pl.BlockSpec((None, F32_ROWS, width), lambda b, i: (b, jnp.minimum((i + 1) * hb, last), 0))]


def _const_spec(shape):
    nd = len(shape)
    return pl.BlockSpec(shape, lambda *_: (0,) * nd, pipeline_mode=pl.Buffered(1))


def _proj_kernel(x_ref, xp_ref, xn_ref, n1_ref, win_ref, qan_ref, kvan_ref, wqT_ref, wk_ref, wvT_ref, wdtT_ref,
                 cw_ref, cb_ref, c_ref, s_ref, cT_ref, sT_ref,
                 qT_out, k_out, vT_out, z_out, act_out, dtT_out, kn2_out, h_scr):
    i = pl.program_id(1)
    tm = x_ref.shape[0]
    n1 = n1_ref[...]
    h_scr[0:tm, :] = _rms(x_ref[...], n1).astype(bf16)
    halo = jnp.concatenate([xp_ref[...], xn_ref[...]], axis=0)
    h_scr[tm:tm + 2 * F32_ROWS, :] = _rms(halo, n1).astype(bf16)
    h = h_scr[0:tm, :]
    half = HEAD_PAD // 2
    lat = _dot(h, win_ref[:, OFF_Q:OFF_Z])
    hq = _rms(lat[:, OFF_Q:OFF_Q + Q_LORA], qan_ref[...]).astype(bf16)
    hc = _rms(lat[:, OFF_CKV:OFF_CKV + KV_LORA], kvan_ref[...]).astype(bf16)
    kr = lat[:, OFF_KR:OFF_KR + HEAD_PAD]
    krf = kr * c_ref[...] + pltpu.roll(kr, half, axis=1) * s_ref[...]

    has_prev = i > 0
    has_next = i < pl.num_programs(1) - 1
    cT = cT_ref[...]
    sT = sT_ref[...]
    lane = lax.broadcasted_iota(jnp.int32, (1, HEAD_PAD), 1)
    shift_lanes = (lane == SHIFT_ROW) | (lane == SHIFT_ROW + 1)
    heads_per_chunk = PROJ_CHUNK // HEAD_PAD
    norms = [None] * N_HEADS

    def ssd_chunk(col):
        xbc = _dot(h_scr[...], win_ref[:, OFF_XBC + col:OFF_XBC + col + PROJ_CHUNK])
        before = jnp.where(has_prev, xbc[tm + F32_ROWS - 1:tm + F32_ROWS], 0.0)
        after = jnp.where(has_next, xbc[tm + F32_ROWS:tm + F32_ROWS + 1], 0.0)
        conv = _conv3_rows(xbc[:tm], before, after, cw_ref[:, col:col + PROJ_CHUNK], cb_ref[:, col:col + PROJ_CHUNK])
        act_out[:, col:col + PROJ_CHUNK] = _silu(conv).astype(bf16)

    def qk_chunk(h0):
        rows = slice(h0 * HEAD_PAD, (h0 + heads_per_chunk) * HEAD_PAD)
        qT = lax.dot_general(wqT_ref[rows, :], hq, NT_DIMS, preferred_element_type=f32)
        kn = _dot(hc, wk_ref[:, rows])
        for hh in range(heads_per_chunk):
            blk = qT[hh * HEAD_PAD:(hh + 1) * HEAD_PAD, :]
            rot = jnp.concatenate([blk[half:], blk[:half]], axis=0)
            qT_out[(h0 + hh) * HEAD_PAD:(h0 + hh + 1) * HEAD_PAD, :] = (blk * cT + rot * sT).astype(bf16)
            kb = (kn[:, hh * HEAD_PAD:(hh + 1) * HEAD_PAD] + krf).astype(bf16)
            kf = kb.astype(f32)
            n2 = jnp.max(jnp.sum(kf * kf, axis=1, keepdims=True), axis=0, keepdims=True)
            norms[h0 + hh] = jnp.broadcast_to(n2, (1, HEAD_PAD))
            k_out[h0 + hh] = jnp.where(shift_lanes, jnp.ones_like(kb), kb)

    def zv_chunk(col):
        z_out[:, col:col + PROJ_CHUNK] = _dot(h, win_ref[:, OFF_Z + col:OFF_Z + col + PROJ_CHUNK]).astype(bf16)
        vT_out[col:col + PROJ_CHUNK, :] = lax.dot_general(
            wvT_ref[col:col + PROJ_CHUNK, :], hc, NT_DIMS, preferred_element_type=f32).astype(bf16)

    light = ([functools.partial(qk_chunk, h0) for h0 in range(0, N_HEADS, heads_per_chunk)]
             + [functools.partial(zv_chunk, col) for col in range(0, D_SSM, PROJ_CHUNK)])
    heavy = [functools.partial(ssd_chunk, col) for col in range(0, D_XBC, PROJ_CHUNK)]
    per_heavy = -(-len(light) // len(heavy))
    for n, chunk in enumerate(heavy):
        chunk()
        for other in light[n * per_heavy:(n + 1) * per_heavy]:
            other()
    kn2_out[...] = jnp.concatenate(norms, axis=0)
    dtT_out[...] = lax.dot_general(wdtT_ref[...], h, NT_DIMS, preferred_element_type=f32)


def _proj_call(x, n1, win, qan, kvan, wqT, wk, wvT, wdtT, cw, cb, c_tab, s_tab, cT_tab, sT_tab):
    B, L, _ = x.shape
    tm = min(PROJ_TM, L)
    grid = (B, L // tm)
    tok = lambda w: pl.BlockSpec((None, tm, w), lambda b, i: (b, i, 0))
    tokT = lambda w: pl.BlockSpec((None, w, tm), lambda b, i: (b, 0, i))
    consts = (n1, win, qan, kvan, wqT, wk, wvT, wdtT, cw, cb)
    in_specs = [tok(D_MODEL)] + _halo_specs(tm, L, D_MODEL) + [_const_spec(a.shape) for a in consts] + [
        pl.BlockSpec((tm, HEAD_PAD), lambda b, i: (i, 0)),
        pl.BlockSpec((tm, HEAD_PAD), lambda b, i: (i, 0)),
        pl.BlockSpec((HEAD_PAD, tm), lambda b, i: (0, i)),
        pl.BlockSpec((HEAD_PAD, tm), lambda b, i: (0, i)),
    ]
    out_shape = [
        jax.ShapeDtypeStruct((B, N_HEADS * HEAD_PAD, L), bf16),
        jax.ShapeDtypeStruct((B, N_HEADS, L, HEAD_PAD), bf16),
        jax.ShapeDtypeStruct((B, N_HEADS * V_HEAD, L), bf16),
        jax.ShapeDtypeStruct((B, L, D_SSM), bf16),
        jax.ShapeDtypeStruct((B, L, D_XBC), bf16),
        jax.ShapeDtypeStruct((B, 2 * SSM_HEADS, L), f32),
        jax.ShapeDtypeStruct((B, L // tm, N_HEADS, HEAD_PAD), f32),
    ]
    k_spec = pl.BlockSpec((None, N_HEADS, tm, HEAD_PAD), lambda b, i: (b, 0, i, 0))
    out_specs = [tokT(N_HEADS * HEAD_PAD), k_spec, tokT(N_HEADS * V_HEAD),
                 tok(D_SSM), tok(D_XBC), tokT(2 * SSM_HEADS),
                 pl.BlockSpec((None, None, N_HEADS, HEAD_PAD), lambda b, i: (b, i, 0, 0))]
    return pl.pallas_call(
        _proj_kernel, grid=grid, in_specs=in_specs, out_specs=out_specs, out_shape=out_shape,
        scratch_shapes=[pltpu.VMEM((tm + 2 * F32_ROWS, D_MODEL), bf16)],
        name="proj",
        compiler_params=pltpu.CompilerParams(
            dimension_semantics=("parallel", "parallel"), vmem_limit_bytes=VMEM_LIMIT),
    )(x, x, x, *consts, c_tab, s_tab, cT_tab, sT_tab)


def _attn_kernel(kmax_ref, qT_ref, k_ref, vT_ref, o_ref, q_scr, p_scr, psum_scr, acc_ref, den_ref, dmin_ref, *, tq, tk, nq, nk, unroll, nh):
    ng = nk // unroll
    heads = range(nh)

    def q_cols(qi):
        return pl.ds(pl.multiple_of(qi * tq, tq), tq)

    def head_rows(hd, width):
        return slice(hd * width, (hd + 1) * width)

    def v_tile(hd, j):
        return vT_ref[head_rows(hd, V_HEAD), pl.ds(pl.multiple_of(j * tk, tk), tk)]

    def k_tile(hd, j):
        return k_ref[hd, pl.ds(pl.multiple_of(j * tk, tk), tk), :]

    def shift_queries(qi, qslot):
        row = lax.broadcasted_iota(jnp.int32, (HEAD_PAD, 1), 0)
        for hd in heads:
            q = qT_ref[head_rows(hd, HEAD_PAD), q_cols(qi)]
            qf = q.astype(f32)
            bound = jnp.sqrt(jnp.sum(qf * qf, axis=0, keepdims=True)) * kmax_ref[hd] * ATTN_BOUND_MARGIN
            hi = bound.astype(bf16)
            lo = (bound - hi.astype(f32)).astype(bf16)
            q_scr[qslot, hd] = jnp.where(row == SHIFT_ROW, -hi, jnp.where(row == SHIFT_ROW + 1, -lo, q))

    def produce(hd, qslot, j, slot):
        p = jnp.exp2(_dot(k_tile(hd, j), q_scr[qslot, hd]))
        p_scr[hd, slot] = p.astype(bf16)
        psum_scr[hd, slot] = jnp.sum(p, axis=0, keepdims=True)

    def group(g, qslot, next_qslot):
        acc = [acc_ref[hd] for hd in heads]
        den = [den_ref[hd] for hd in heads]
        for u in range(unroll):
            j = g * unroll + u
            nxt = (u + 1) % 2
            for hd in heads:
                if next_qslot is False or u < unroll - 1:
                    produce(hd, qslot, j + 1, nxt)
                elif next_qslot is not None:
                    produce(hd, next_qslot, 0, nxt)
            for hd in heads:
                acc[hd] = acc[hd] + _dot(v_tile(hd, j), p_scr[hd, u % 2])
                den[hd] = den[hd] + psum_scr[hd, u % 2]
        for hd in heads:
            acc_ref[hd] = acc[hd]
            den_ref[hd] = den[hd]

    def recompute_exact(hd, qi):
        q = qT_ref[head_rows(hd, HEAD_PAD), q_cols(qi)]

        def body(j, carry):
            m, den, acc = carry
            s = _dot(k_tile(hd, j), q)
            m_new = jnp.maximum(m, jnp.max(s, axis=0, keepdims=True))
            alpha = jnp.exp2(m - m_new)
            p = jnp.exp2(s - m_new)
            return (m_new, alpha * den + jnp.sum(p, axis=0, keepdims=True),
                    alpha * acc + _dot(v_tile(hd, j), p.astype(bf16)))

        init = (jnp.full((1, tq), -jnp.inf, f32), jnp.zeros((1, tq), f32), jnp.zeros((V_HEAD, tq), f32))
        _, den, acc = lax.fori_loop(0, nk, body, init)
        write_output(hd, qi, acc, den)

    def write_output(hd, qi, acc, den):
        o_ref[head_rows(hd, V_HEAD), q_cols(qi)] = (acc / den).astype(o_ref.dtype)

    def query_tile(qi, qslot, next_qslot):
        if next_qslot is not None:
            shift_queries(qi + 1, next_qslot)
        acc_ref[...] = jnp.zeros(acc_ref.shape, f32)
        den_ref[...] = jnp.zeros(den_ref.shape, f32)
        if ng > 1:
            def body(g, carry):
                group(g, qslot, False)
                return carry
            lax.fori_loop(0, ng - 1, body, 0)
        group(ng - 1, qslot, next_qslot)
        for hd in heads:
            den = den_ref[hd]
            write_output(hd, qi, acc_ref[hd], den)
            dmin_ref[hd] = jnp.minimum(dmin_ref[hd], den)

    dmin_ref[...] = jnp.full(dmin_ref.shape, jnp.inf, f32)
    shift_queries(0, 0)
    for hd in heads:
        produce(hd, 0, 0, 0)
    if nq > 1:
        def q_body(qi, carry):
            query_tile(qi, qi % 2, (qi + 1) % 2)
            return carry
        lax.fori_loop(0, nq - 1, q_body, 0)
    query_tile(nq - 1, (nq - 1) % 2, None)

    for hd in heads:
        @pl.when(jnp.logical_not(jnp.min(dmin_ref[hd]) >= ATTN_DENOM_FLOOR))
        def _(hd=hd):
            def redo(qi, carry):
                recompute_exact(hd, qi)
                return carry
            lax.fori_loop(0, nq, redo, 0)


def _attn_tiles(L):
    nh = ATTN_HEADS_PER_STEP
    tk = min(ATTN_TK, L // 2)
    kv_buffers = 1 if L // tk > 2 else 2
    kv_bytes = kv_buffers * nh * L * (HEAD_PAD + V_HEAD) * 2
    tq = L
    while nh * 2 * tk * tq * 2 > ATTN_VMEM_BUDGET - kv_bytes and tq > LANES:
        tq //= 2
    return tq, tk, kv_buffers


def _attn_call(qT, k, vT, kmax):
    B, _, L = qT.shape
    tq, tk, kv_buffers = _attn_tiles(L)
    kv_mode = pl.Buffered(kv_buffers)
    nk = L // tk
    unroll = min(ATTN_UNROLL, nk)
    assert unroll % 2 == 0 and nk % unroll == 0, "key tiles alternate between two probability buffers"
    nq = L // tq if nk == unroll else 1
    tqb = nq * tq
    nh = ATTN_HEADS_PER_STEP
    return pl.pallas_call(
        functools.partial(_attn_kernel, tq=tq, tk=tk, nq=nq, nk=nk, unroll=unroll, nh=nh),
        grid=(B, N_HEADS // nh, L // tqb),
        scratch_shapes=[pltpu.VMEM((2, nh, HEAD_PAD, tq), bf16), pltpu.VMEM((nh, 2, tk, tq), bf16),
                        pltpu.VMEM((nh, 2, 1, tq), f32), pltpu.VMEM((nh, V_HEAD, tq), f32),
                        pltpu.VMEM((nh, 1, tq), f32), pltpu.VMEM((nh, 1, tq), f32)],
        in_specs=[
            pl.BlockSpec((None, nh, 1, tq), lambda b, h, i: (b, h, 0, 0)),
            pl.BlockSpec((None, nh * HEAD_PAD, tqb), lambda b, h, i: (b, h, i)),
            pl.BlockSpec((None, nh, L, HEAD_PAD), lambda b, h, i: (b, h, 0, 0), pipeline_mode=kv_mode),
            pl.BlockSpec((None, nh * V_HEAD, L), lambda b, h, i: (b, h, 0), pipeline_mode=kv_mode),
        ],
        out_specs=pl.BlockSpec((None, nh * V_HEAD, tqb), lambda b, h, i: (b, h, i)),
        out_shape=jax.ShapeDtypeStruct((B, N_HEADS * V_HEAD, L), bf16),
        name="attn",
        compiler_params=pltpu.CompilerParams(
            dimension_semantics=("parallel", "parallel", "arbitrary"), vmem_limit_bytes=VMEM_LIMIT),
    )(kmax, qT, k, vT)


def _split3(v):
    a1 = v.astype(bf16)
    r1 = v - a1.astype(f32)
    a2 = r1.astype(bf16)
    a3 = (r1 - a2.astype(f32)).astype(bf16)
    return a1, a2, a3


def _chunk_scalars(dtT_raw, biasT, aT, triu):
    nh2 = 2 * SSM_HEADS
    dtT = jax.nn.softplus(dtT_raw + biasT)
    stepT = dtT * aT
    cs3 = _dot(jnp.concatenate(_split3(stepT), axis=0), triu)
    csT = cs3[0:nh2] + cs3[nh2:2 * nh2] + cs3[2 * nh2:3 * nh2]
    return dtT, stepT, csT


def _expand_heads(partsT, e_ref):
    n = len(partsT)
    stk = jnp.concatenate(partsT, axis=0)
    hi = stk.astype(bf16).astype(f32)
    pieces = [hi, stk - hi]
    pad = LANES - 2 * stk.shape[0]
    if pad:
        pieces.append(jnp.zeros((pad, stk.shape[1]), f32))
    nat = jnp.transpose(jnp.concatenate(pieces, axis=0)).astype(bf16)
    full = _dot(nat, e_ref[...])
    return [full[:, j * D_SSM:(j + 1) * D_SSM] for j in range(n)]


def _ssd_kernel(act_ref, dtT_ref, z_ref, biasT_ref, aT_ref, dskipT_ref, norm_ref, triu_ref, e4_ref, e2_ref,
                gT_ref, o_ref, hf_ref, hb_ref, hbs_ref, *, nblk, nsub, Q):
    ph = pl.program_id(1)
    c = pl.program_id(2)
    H = SSM_HEADS

    def group_slices(g):
        return slice(g * GROUP_WIDTH, (g + 1) * GROUP_WIDTH), slice(g * D_STATE, (g + 1) * D_STATE)

    dtT, stepT, csT = _chunk_scalars(dtT_ref[...], biasT_ref[...], aT_ref[...], triu_ref[...])
    totT = jnp.concatenate(
        [jnp.broadcast_to(csT[:, (j + 1) * Q - 1:(j + 1) * Q], (2 * H, Q)) for j in range(nsub)], axis=1)
    cbT_b = csT[H:] - stepT[H:]
    Bm = act_ref[:, D_SSM:D_SSM + GN]

    @pl.when(ph == 0)
    def _backward_states():
        @pl.when(c == 0)
        def _():
            hb_ref[...] = jnp.zeros_like(hb_ref)

        blk = nblk - 1 - c
        wb, eb = _expand_heads([jnp.exp2(cbT_b) * dtT[H:], jnp.exp2(totT[H:] - cbT_b)], e2_ref)
        xw = (act_ref[:, :D_SSM].astype(f32) * wb).astype(bf16)
        for j in reversed(range(nsub)):
            rows = slice(j * Q, (j + 1) * Q)
            hbs_ref[blk * nsub + j] = hb_ref[...].astype(bf16)
            for g in range(SSM_GROUPS):
                sl, sn = group_slices(g)
                upd = lax.dot_general(Bm[rows, sn], xw[rows, sl], TN_DIMS, preferred_element_type=f32)
                hb_ref[g] = eb[j * Q:j * Q + 1, sl] * hb_ref[g] + upd

    @pl.when(ph == 1)
    def _outputs():
        @pl.when(c == 0)
        def _():
            hf_ref[...] = jnp.zeros_like(hf_ref)

        ti = lax.broadcasted_iota(jnp.int32, (Q, Q), 0)
        si = lax.broadcasted_iota(jnp.int32, (Q, Q), 1)
        lower = ti >= si
        lane = lax.broadcasted_iota(jnp.int32, (1, LANES), 1)
        first_half = lane < SSM_HEAD_DIM
        gT = gT_ref[...]
        nw = norm_ref[...]

        xs_b = act_ref[:, :D_SSM]
        Cm = act_ref[:, D_SSM + GN:]
        xs = xs_b.astype(f32)
        csT_f = csT[:H]
        l2dt = jnp.log2(dtT)
        rowf = csT_f - l2dt[:H]
        rowb = cbT_b + l2dt[H:]
        cols = jnp.transpose(jnp.concatenate([csT_f, cbT_b], axis=0))
        cb = Cm.astype(f32) * Bm.astype(f32)
        cb_h = cb.astype(bf16)
        cb_l = (cb - cb_h.astype(f32)).astype(bf16)
        diagT = (lax.dot_general(gT, cb_h, NT_DIMS, preferred_element_type=f32)
                 + lax.dot_general(gT, cb_l, NT_DIMS, preferred_element_type=f32))
        partsT = [jnp.exp2(csT_f), jnp.exp2(totT[H:] - cbT_b), jnp.exp2(totT[:H] - csT_f) * dtT[:H],
                  dskipT_ref[...] + dtT[H:] * diagT]

        def expand_chunk(j):
            return _expand_heads([p[:, j * Q:(j + 1) * Q] for p in partsT], e4_ref)

        expanded = expand_chunk(0)
        for j in range(nsub):
            rows = slice(j * Q, (j + 1) * Q)
            chunk_id = c * nsub + j
            ef, eb, wf, coef = expanded
            if j + 1 < nsub:
                expanded = expand_chunk(j + 1)
            y_groups = []
            for g in range(SSM_GROUPS):
                sl, sn = group_slices(g)
                Cg = Cm[rows, sn]
                Bg = Bm[rows, sn]
                cbm = lax.dot_general(Cg, Bg, NT_DIMS, preferred_element_type=f32)
                y_off = (ef[:, sl] * _dot(Cg, hf_ref[g].astype(bf16))
                         + eb[:, sl] * _dot(Cg, hbs_ref[chunk_id, g]))
                pairs = []
                for hp in range(HEADS_PER_GROUP // 2):
                    lo = g * GROUP_WIDTH + hp * LANES
                    x_pair = xs_b[rows, lo:lo + LANES]
                    y_pair = None
                    for k in range(2):
                        hh = g * HEADS_PER_GROUP + 2 * hp + k
                        arg = jnp.where(lower, cols[rows, hh:hh + 1] - rowf[hh:hh + 1, rows],
                                        rowb[hh:hh + 1, rows] - cols[rows, H + hh:H + hh + 1])
                        mat = (cbm * jnp.exp2(arg)).astype(bf16)
                        keep = first_half if k == 0 else jnp.logical_not(first_half)
                        contrib = _dot(mat, jnp.where(keep, x_pair, jnp.zeros_like(x_pair)))
                        y_pair = contrib if y_pair is None else y_pair + contrib
                    pairs.append(y_pair)
                y_groups.append(jnp.concatenate(pairs, axis=1) + y_off)
                xw = (xs[rows, sl] * wf[:, sl]).astype(bf16)
                upd = lax.dot_general(Bg, xw, TN_DIMS, preferred_element_type=f32)
                hf_ref[g] = ef[Q - 1:Q, sl] * hf_ref[g] + upd

            y = jnp.concatenate(y_groups, axis=1) + xs[rows] * coef
            y = y * _silu(z_ref[rows, :].astype(f32))
            for g in range(SSM_GROUPS):
                sl, _ = group_slices(g)
                o_ref[rows, sl] = _rms(y[:, sl], nw[:, sl]).astype(o_ref.dtype)


def _ssd_call(act, dtT, z, biasT, aT, dskipT, norm, triu, e4, e2, gT):
    B, L, _ = act.shape
    blk = min(SSD_BLOCK, L)
    Q = min(SSD_Q, blk)
    nsub = blk // Q
    nblk = L // blk

    def bidx(p, c):
        return jnp.where(p == 0, nblk - 1 - c, c)

    consts = (biasT, aT, dskipT, norm, triu, e4, e2, gT)
    in_specs = [
        pl.BlockSpec((None, blk, D_XBC), lambda b, p, c: (b, bidx(p, c), 0)),
        pl.BlockSpec((None, 2 * SSM_HEADS, blk), lambda b, p, c: (b, 0, bidx(p, c))),
        pl.BlockSpec((None, blk, D_SSM), lambda b, p, c: (b, p * c, 0)),
    ] + [_const_spec(a.shape) for a in consts]
    state = (SSM_GROUPS, D_STATE, GROUP_WIDTH)
    return pl.pallas_call(
        functools.partial(_ssd_kernel, nblk=nblk, nsub=nsub, Q=Q),
        grid=(B, 2, nblk),
        in_specs=in_specs,
        out_specs=pl.BlockSpec((None, blk, D_SSM), lambda b, p, c: (b, p * c, 0)),
        out_shape=jax.ShapeDtypeStruct((B, L, D_SSM), bf16),
        scratch_shapes=[pltpu.VMEM(state, f32), pltpu.VMEM(state, f32),
                        pltpu.VMEM((nblk * nsub,) + state, bf16)],
        name="ssd",
        compiler_params=pltpu.CompilerParams(
            dimension_semantics=("parallel", "arbitrary", "arbitrary"), vmem_limit_bytes=VMEM_LIMIT),
    )(act, dtT, z, *consts)


def _oproj_kernel(x_ref, aT_ref, s_ref, wa_ref, ws_ref, o_ref):
    o_ref[...] = (x_ref[...]
                  + lax.dot_general(aT_ref[...], wa_ref[...], TN_DIMS, preferred_element_type=f32)
                  + _dot(s_ref[...], ws_ref[...]))


def _oproj_call(x, attnT, ssm, wa, ws):
    B, L, _ = x.shape
    tm = min(OPROJ_TM, L)
    tok = lambda w: pl.BlockSpec((None, tm, w), lambda b, i: (b, i, 0))
    return pl.pallas_call(
        _oproj_kernel, grid=(B, L // tm),
        in_specs=[tok(D_MODEL), pl.BlockSpec((None, N_HEADS * V_HEAD, tm), lambda b, i: (b, 0, i)),
                  tok(D_SSM), _const_spec(wa.shape), _const_spec(ws.shape)],
        out_specs=tok(D_MODEL),
        out_shape=jax.ShapeDtypeStruct((B, L, D_MODEL), f32),
        name="oproj",
        compiler_params=pltpu.CompilerParams(
            dimension_semantics=("parallel", "parallel"), vmem_limit_bytes=VMEM_LIMIT),
    )(x, attnT, ssm, wa, ws)


def _ffn_kernel(x_ref, xp_ref, xn_ref, n2_ref, wg_ref, wu_ref, cw_ref, cb_ref, wd_ref, fn_ref,
                o_ref, h_scr, act_scr, *, nfc):
    i = pl.program_id(1)
    nt = pl.num_programs(1)
    tm = x_ref.shape[0]
    n2 = n2_ref[...]
    h_scr[0:tm, :] = _rms(x_ref[...], n2).astype(bf16)
    halo = jnp.concatenate([xp_ref[...], xn_ref[...]], axis=0)
    h_scr[tm:tm + 2 * F32_ROWS, :] = _rms(halo, n2).astype(bf16)
    has_prev = i > 0
    has_next = i < nt - 1

    def chunk(cf):
        col = cf * FFN_FC
        g_ext = _dot(h_scr[...], wg_ref[:, pl.ds(col, FFN_FC)])
        before = jnp.where(has_prev, g_ext[tm + F32_ROWS - 1:tm + F32_ROWS], 0.0)
        after = jnp.where(has_next, g_ext[tm + F32_ROWS:tm + F32_ROWS + 1], 0.0)
        gc = _conv3_rows(g_ext[:tm], before, after, cw_ref[:, pl.ds(col, FFN_FC)], cb_ref[:, pl.ds(col, FFN_FC)])
        u = _dot(h_scr[0:tm, :], wu_ref[:, pl.ds(col, FFN_FC)])
        act_scr[:, pl.ds(col, FFN_FC)] = (_silu(gc) * u).astype(bf16)

    for cf in range(nfc):
        chunk(cf)
    y = x_ref[...] + _dot(act_scr[...], wd_ref[...])
    o_ref[...] = _rms(y, fn_ref[...])


def _ffn_call(x, n2, wg, wu, cw, cb, wd, fn):
    B, L, _ = x.shape
    tm = min(FFN_TM, L)
    tok = pl.BlockSpec((None, tm, D_MODEL), lambda b, i: (b, i, 0))
    consts = (n2, wg, wu, cw, cb, wd, fn)
    return pl.pallas_call(
        functools.partial(_ffn_kernel, nfc=D_FF // FFN_FC),
        grid=(B, L // tm),
        in_specs=[tok] + _halo_specs(tm, L, D_MODEL) + [_const_spec(a.shape) for a in consts],
        out_specs=tok,
        out_shape=jax.ShapeDtypeStruct((B, L, D_MODEL), f32),
        scratch_shapes=[pltpu.VMEM((tm + 2 * F32_ROWS, D_MODEL), bf16), pltpu.VMEM((tm, D_FF), bf16)],
        name="ffn",
        compiler_params=pltpu.CompilerParams(
            dimension_semantics=("parallel", "parallel"), vmem_limit_bytes=VMEM_LIMIT),
    )(x, x, x, *consts)


def _head_lane_sources():
    src = np.full((HEAD_PAD,), QK_DIM, np.int32)
    half = HEAD_PAD // 2
    src[0:HALF_ROPE] = QK_NOPE + np.arange(HALF_ROPE)
    src[HALF_ROPE:half] = np.arange(half - HALF_ROPE)
    src[half:half + HALF_ROPE] = QK_NOPE + HALF_ROPE + np.arange(HALF_ROPE)
    n_rest = QK_NOPE - (half - HALF_ROPE)
    src[half + HALF_ROPE:half + HALF_ROPE + n_rest] = (half - HALF_ROPE) + np.arange(n_rest)
    return src


def _rope_tables(L):
    inv = ROPE_THETA ** (-jnp.arange(0, QK_ROPE, 2, dtype=f32) / QK_ROPE)
    ang = jnp.arange(L, dtype=f32)[:, None] * inv[None, :]
    cos, sin = jnp.cos(ang), jnp.sin(ang)
    half = HEAD_PAD // 2
    c_tab = jnp.ones((L, HEAD_PAD), f32)
    c_tab = c_tab.at[:, 0:HALF_ROPE].set(cos).at[:, half:half + HALF_ROPE].set(cos)
    s_tab = jnp.zeros((L, HEAD_PAD), f32)
    s_tab = s_tab.at[:, 0:HALF_ROPE].set(-sin).at[:, half:half + HALF_ROPE].set(sin)
    return c_tab, s_tab


def _expand_matrix(n):
    m = np.zeros((LANES, n * D_SSM), np.float32)
    for part in range(2):
        for j in range(n):
            for h in range(SSM_HEADS):
                r = part * n * SSM_HEADS + j * SSM_HEADS + h
                m[r, j * D_SSM + h * SSM_HEAD_DIM:j * D_SSM + (h + 1) * SSM_HEAD_DIM] = 1.0
    return jnp.asarray(m, bf16)


def _prepare_weights(norm1, w_in, q_a_norm, kv_a_norm, w_q_b, w_kv_b, conv_w, conv_b,
                     dt_bias_f, dt_bias_b, a_log_f, a_log_b, d_skip, ssm_norm, w_out,
                     norm2, w_gate, w_up, ffn_conv_w, ffn_conv_b, w_down, final_norm):
    half = HEAD_PAD // 2
    o_kr = Q_LORA + KV_LORA
    o_z = o_kr + QK_ROPE
    o_dt = o_z + D_SSM + D_XBC
    kr_blk = jnp.zeros((D_MODEL, HEAD_PAD), f32)
    kr_blk = kr_blk.at[:, 0:HALF_ROPE].set(w_in[:, o_kr:o_kr + HALF_ROPE])
    kr_blk = kr_blk.at[:, half:half + HALF_ROPE].set(w_in[:, o_kr + HALF_ROPE:o_kr + QK_ROPE])
    win = jnp.concatenate([w_in[:, :o_kr], kr_blk, w_in[:, o_z:o_dt]], axis=1).astype(bf16)
    w_dt = w_in[:, o_dt:o_dt + 2 * SSM_HEADS]

    src = _head_lane_sources()
    wq = w_q_b.reshape(Q_LORA, N_HEADS, QK_DIM)
    wq = jnp.concatenate([wq, jnp.zeros((Q_LORA, N_HEADS, 1), f32)], axis=-1)[:, :, src]
    wqT = wq.reshape(Q_LORA, N_HEADS * HEAD_PAD).T.astype(bf16)
    wkv = w_kv_b.reshape(KV_LORA, N_HEADS, QK_NOPE + V_HEAD)
    src_k = np.where(src < QK_NOPE, src, QK_NOPE)
    wk = jnp.concatenate([wkv[:, :, :QK_NOPE], jnp.zeros((KV_LORA, N_HEADS, 1), f32)], axis=-1)[:, :, src_k]
    wk = wk.reshape(KV_LORA, N_HEADS * HEAD_PAD).astype(bf16)
    wvT = wkv[:, :, QK_NOPE:].reshape(KV_LORA, N_HEADS * V_HEAD).T.astype(bf16)

    row = lambda v: v.reshape(1, -1).astype(f32)
    col = lambda v: v.reshape(-1, 1).astype(f32)
    a_neg = -jnp.exp(jnp.concatenate([a_log_f, a_log_b]).astype(f32)) * LOG2E
    group_of_lane = np.arange(GN) // D_STATE
    group_of_head = np.arange(SSM_HEADS) // HEADS_PER_GROUP
    gT = jnp.asarray(group_of_head[:, None] == group_of_lane[None, :], bf16)
    return dict(
        n1=row(norm1), win=win, qan=row(q_a_norm), kvan=row(kv_a_norm), wqT=wqT, wk=wk, wvT=wvT,
        wdtT=w_dt.T.astype(bf16), cw=conv_w.astype(f32), cb=row(conv_b),
        biasT=col(jnp.concatenate([dt_bias_f, dt_bias_b])), aT=col(a_neg), dskipT=col(d_skip),
        ssm_norm=row(ssm_norm), e4=_expand_matrix(4), e2=_expand_matrix(2), gT=gT,
        wa=w_out[:N_HEADS * V_HEAD].astype(bf16), ws=w_out[N_HEADS * V_HEAD:].astype(bf16),
        n2=row(norm2), wg=w_gate.astype(bf16), wu=w_up.astype(bf16), fcw=ffn_conv_w.astype(f32),
        fcb=row(ffn_conv_b), wd=w_down.astype(bf16), fn=row(final_norm),
    )


def _encoder(x, w):
    B, L, _ = x.shape
    c_tab, s_tab = _rope_tables(L)
    scale = QK_DIM ** -0.5 * LOG2E
    qT, k, vT, z, act, dtT, kn2 = _proj_call(
        x, w["n1"], w["win"], w["qan"], w["kvan"], w["wqT"], w["wk"], w["wvT"], w["wdtT"], w["cw"], w["cb"],
        c_tab, s_tab, (c_tab * scale).T, (s_tab * scale).T)
    kmax = jnp.sqrt(jnp.max(kn2, axis=1))[:, :, :1]
    kmax = jnp.broadcast_to(kmax[..., None], (B, N_HEADS, 1, _attn_tiles(L)[0]))
    attnT = _attn_call(qT, k, vT, kmax)
    blk = min(SSD_BLOCK, L)
    Q = min(SSD_Q, blk)
    triu = jnp.kron(jnp.eye(blk // Q, dtype=f32), jnp.triu(jnp.ones((Q, Q), f32))).astype(bf16)
    ssm = _ssd_call(act, dtT, z, w["biasT"], w["aT"], w["dskipT"], w["ssm_norm"], triu, w["e4"], w["e2"], w["gT"])
    x1 = _oproj_call(x, attnT, ssm, w["wa"], w["ws"])
    return _ffn_call(x1, w["n2"], w["wg"], w["wu"], w["fcw"], w["fcb"], w["wd"], w["fn"])


def kernel(x_prompt, x_sample, norm1, w_in, q_a_norm, kv_a_norm, w_q_b, w_kv_b, conv_w, conv_b,
           dt_bias_f, dt_bias_b, a_log_f, a_log_b, d_skip, ssm_norm, w_out, norm2, w_gate, w_up,
           ffn_conv_w, ffn_conv_b, w_down, final_norm):
    w = _prepare_weights(norm1[0], w_in[0], q_a_norm[0], kv_a_norm[0], w_q_b[0], w_kv_b[0], conv_w[0],
                         conv_b[0], dt_bias_f[0], dt_bias_b[0], a_log_f[0], a_log_b[0], d_skip[0],
                         ssm_norm[0], w_out[0], norm2[0], w_gate[0], w_up[0], ffn_conv_w[0],
                         ffn_conv_b[0], w_down[0], final_norm)
    return (_encoder(x_prompt, w), _encoder(x_sample, w))
```

```python
import functools

import numpy as np
import jax
import jax.numpy as jnp
from jax import lax
from jax.experimental import pallas as pl
from jax.experimental.pallas import tpu as pltpu

D_MODEL = 1024
N_HEADS = 16
QK_NOPE = 64
QK_ROPE = 32
HALF_ROPE = QK_ROPE // 2
QK_DIM = QK_NOPE + QK_ROPE
V_HEAD = 64
Q_LORA = 384
KV_LORA = 256
ROPE_THETA = 10000.0
SSM_HEADS = 16
SSM_HEAD_DIM = 64
D_SSM = SSM_HEADS * SSM_HEAD_DIM
SSM_GROUPS = 2
HEADS_PER_GROUP = SSM_HEADS // SSM_GROUPS
GROUP_WIDTH = D_SSM // SSM_GROUPS
D_STATE = 64
GN = SSM_GROUPS * D_STATE
D_XBC = D_SSM + 2 * GN
D_FF = 2816
EPS = 1e-6
LOG2E = float(np.log2(np.e))

LANES = 128
HEAD_PAD = LANES
F32_ROWS = 8
VMEM_LIMIT = 56 * 1024 * 1024

OFF_Q = 0
OFF_CKV = OFF_Q + Q_LORA
OFF_KR = OFF_CKV + KV_LORA
OFF_Z = OFF_KR + HEAD_PAD
OFF_XBC = OFF_Z + D_SSM

PROJ_TM = 512
PROJ_CHUNK = 256
ATTN_VMEM_BUDGET = 42 * 1024 * 1024
ATTN_TK = 2048
ATTN_UNROLL = 4
ATTN_HEADS_PER_STEP = 2
SHIFT_ROW = QK_DIM
ATTN_BOUND_MARGIN = 1.0 + 2.0 ** -6
ATTN_DENOM_FLOOR = 2.0 ** -80
SSD_Q = 128
SSD_BLOCK = 512
OPROJ_TM = 1024
FFN_TM = 1024
FFN_FC = 256

NT_DIMS = (((1,), (1,)), ((), ()))
TN_DIMS = (((0,), (0,)), ((), ()))

f32 = jnp.float32
bf16 = jnp.bfloat16


def _rms(x, w):
    return x * lax.rsqrt(jnp.mean(x * x, axis=-1, keepdims=True) + EPS) * w


def _dot(a, b):
    return jnp.dot(a, b, preferred_element_type=f32)


def _silu(x):
    h = 0.5 * x
    return h * jnp.tanh(h) + h


def _conv3_rows(x, before, after, cw, cb):
    n = x.shape[0]
    sub = lax.broadcasted_iota(jnp.int32, (F32_ROWS, 1), 0)
    down = pltpu.roll(x, 1, axis=0)
    up = pltpu.roll(x, n - 1, axis=0)
    x_prev = jnp.concatenate([jnp.where(sub == 0, before, down[:F32_ROWS]), down[F32_ROWS:]], axis=0)
    x_next = jnp.concatenate([up[:n - F32_ROWS],
                              jnp.where(sub == F32_ROWS - 1, after, up[n - F32_ROWS:])], axis=0)
    return x_prev * cw[0:1] + x * cw[1:2] + x_next * cw[2:3] + cb


def _halo_specs(tm, L, width):
    hb = tm // F32_ROWS
    last = L // F32_ROWS - 1
    return [pl.BlockSpec((None, F32_ROWS, width), lambda b, i: (b, jnp.maximum(i * hb - 1, 0), 0)),
            pl.BlockSpec((None, F32_ROWS, width), lambda b, i: (b, jnp.minimum((i + 1) * hb, last), 0))]


def _const_spec(shape):
    nd = len(shape)
    return pl.BlockSpec(shape, lambda *_: (0,) * nd, pipeline_mode=pl.Buffered(1))


def _proj_kernel(x_ref, xp_ref, xn_ref, n1_ref, win_ref, qan_ref, kvan_ref, wqT_ref, wk_ref, wvT_ref, wdtT_ref,
                 cw_ref, cb_ref, c_ref, s_ref, cT_ref, sT_ref,
                 qT_out, k_out, vT_out, z_out, act_out, dtT_out, kn2_out, h_scr):
    i = pl.program_id(1)
    tm = x_ref.shape[0]
    n1 = n1_ref[...]
    h_scr[0:tm, :] = _rms(x_ref[...], n1).astype(bf16)
    halo = jnp.concatenate([xp_ref[...], xn_ref[...]], axis=0)
    h_scr[tm:tm + 2 * F32_ROWS, :] = _rms(halo, n1).astype(bf16)
    h = h_scr[0:tm, :]
    half = HEAD_PAD // 2
    lat = _dot(h, win_ref[:, OFF_Q:OFF_Z])
    hq = _rms(lat[:, OFF_Q:OFF_Q + Q_LORA], qan_ref[...]).astype(bf16)
    hc = _rms(lat[:, OFF_CKV:OFF_CKV + KV_LORA], kvan_ref[...]).astype(bf16)
    kr = lat[:, OFF_KR:OFF_KR + HEAD_PAD]
    krf = kr * c_ref[...] + pltpu.roll(kr, half, axis=1) * s_ref[...]

    has_prev = i > 0
    has_next = i < pl.num_programs(1) - 1
    cT = cT_ref[...]
    sT = sT_ref[...]
    lane = lax.broadcasted_iota(jnp.int32, (1, HEAD_PAD), 1)
    shift_lanes = (lane == SHIFT_ROW) | (lane == SHIFT_ROW + 1)
    heads_per_chunk = PROJ_CHUNK // HEAD_PAD
    norms = [None] * N_HEADS

    def ssd_chunk(col):
        xbc = _dot(h_scr[...], win_ref[:, OFF_XBC + col:OFF_XBC + col + PROJ_CHUNK])
        before = jnp.where(has_prev, xbc[tm + F32_ROWS - 1:tm + F32_ROWS], 0.0)
        after = jnp.where(has_next, xbc[tm + F32_ROWS:tm + F32_ROWS + 1], 0.0)
        conv = _conv3_rows(xbc[:tm], before, after, cw_ref[:, col:col + PROJ_CHUNK], cb_ref[:, col:col + PROJ_CHUNK])
        act_out[:, col:col + PROJ_CHUNK] = _silu(conv).astype(bf16)

    def qk_chunk(h0):
        rows = slice(h0 * HEAD_PAD, (h0 + heads_per_chunk) * HEAD_PAD)
        qT = lax.dot_general(wqT_ref[rows, :], hq, NT_DIMS, preferred_element_type=f32)
        kn = _dot(hc, wk_ref[:, rows])
        for hh in range(heads_per_chunk):
            blk = qT[hh * HEAD_PAD:(hh + 1) * HEAD_PAD, :]
            rot = jnp.concatenate([blk[half:], blk[:half]], axis=0)
            qT_out[(h0 + hh) * HEAD_PAD:(h0 + hh + 1) * HEAD_PAD, :] = (blk * cT + rot * sT).astype(bf16)
            kb = (kn[:, hh * HEAD_PAD:(hh + 1) * HEAD_PAD] + krf).astype(bf16)
            kf = kb.astype(f32)
            n2 = jnp.max(jnp.sum(kf * kf, axis=1, keepdims=True), axis=0, keepdims=True)
            norms[h0 + hh] = jnp.broadcast_to(n2, (1, HEAD_PAD))
            k_out[h0 + hh] = jnp.where(shift_lanes, jnp.ones_like(kb), kb)

    def zv_chunk(col):
        z_out[:, col:col + PROJ_CHUNK] = _dot(h, win_ref[:, OFF_Z + col:OFF_Z + col + PROJ_CHUNK]).astype(bf16)
        vT_out[col:col + PROJ_CHUNK, :] = lax.dot_general(
            wvT_ref[col:col + PROJ_CHUNK, :], hc, NT_DIMS, preferred_element_type=f32).astype(bf16)

    light = ([functools.partial(qk_chunk, h0) for h0 in range(0, N_HEADS, heads_per_chunk)]
             + [functools.partial(zv_chunk, col) for col in range(0, D_SSM, PROJ_CHUNK)])
    heavy = [functools.partial(ssd_chunk, col) for col in range(0, D_XBC, PROJ_CHUNK)]
    per_heavy = -(-len(light) // len(heavy))
    for n, chunk in enumerate(heavy):
        chunk()
        for other in light[n * per_heavy:(n + 1) * per_heavy]:
            other()
    kn2_out[...] = jnp.concatenate(norms, axis=0)
    dtT_out[...] = lax.dot_general(wdtT_ref[...], h, NT_DIMS, preferred_element_type=f32)


def _proj_call(x, n1, win, qan, kvan, wqT, wk, wvT, wdtT, cw, cb, c_tab, s_tab, cT_tab, sT_tab):
    B, L, _ = x.shape
    tm = min(PROJ_TM, L)
    grid = (B, L // tm)
    tok = lambda w: pl.BlockSpec((None, tm, w), lambda b, i: (b, i, 0))
    tokT = lambda w: pl.BlockSpec((None, w, tm), lambda b, i: (b, 0, i))
    consts = (n1, win, qan, kvan, wqT, wk, wvT, wdtT, cw, cb)
    in_specs = [tok(D_MODEL)] + _halo_specs(tm, L, D_MODEL) + [_const_spec(a.shape) for a in consts] + [
        pl.BlockSpec((tm, HEAD_PAD), lambda b, i: (i, 0)),
        pl.BlockSpec((tm, HEAD_PAD), lambda b, i: (i, 0)),
        pl.BlockSpec((HEAD_PAD, tm), lambda b, i: (0, i)),
        pl.BlockSpec((HEAD_PAD, tm), lambda b, i: (0, i)),
    ]
    out_shape = [
        jax.ShapeDtypeStruct((B, N_HEADS * HEAD_PAD, L), bf16),
        jax.ShapeDtypeStruct((B, N_HEADS, L, HEAD_PAD), bf16),
        jax.ShapeDtypeStruct((B, N_HEADS * V_HEAD, L), bf16),
        jax.ShapeDtypeStruct((B, L, D_SSM), bf16),
        jax.ShapeDtypeStruct((B, L, D_XBC), bf16),
        jax.ShapeDtypeStruct((B, 2 * SSM_HEADS, L), f32),
        jax.ShapeDtypeStruct((B, L // tm, N_HEADS, HEAD_PAD), f32),
    ]
    k_spec = pl.BlockSpec((None, N_HEADS, tm, HEAD_PAD), lambda b, i: (b, 0, i, 0))
    out_specs = [tokT(N_HEADS * HEAD_PAD), k_spec, tokT(N_HEADS * V_HEAD),
                 tok(D_SSM), tok(D_XBC), tokT(2 * SSM_HEADS),
                 pl.BlockSpec((None, None, N_HEADS, HEAD_PAD), lambda b, i: (b, i, 0, 0))]
    return pl.pallas_call(
        _proj_kernel, grid=grid, in_specs=in_specs, out_specs=out_specs, out_shape=out_shape,
        scratch_shapes=[pltpu.VMEM((tm + 2 * F32_ROWS, D_MODEL), bf16)],
        name="proj",
        compiler_params=pltpu.CompilerParams(
            dimension_semantics=("parallel", "parallel"), vmem_limit_bytes=VMEM_LIMIT),
    )(x, x, x, *consts, c_tab, s_tab, cT_tab, sT_tab)


def _attn_kernel(kmax_ref, qT_ref, k_ref, vT_ref, o_ref, q_scr, p_scr, psum_scr, acc_ref, den_ref, dmin_ref, *, tq, tk, nq, nk, unroll, nh):
    ng = nk // unroll
    heads = range(nh)

    def q_cols(qi):
        return pl.ds(pl.multiple_of(qi * tq, tq), tq)

    def head_rows(hd, width):
        return slice(hd * width, (hd + 1) * width)

    def v_tile(hd, j):
        return vT_ref[head_rows(hd, V_HEAD), pl.ds(pl.multiple_of(j * tk, tk), tk)]

    def k_tile(hd, j):
        return k_ref[hd, pl.ds(pl.multiple_of(j * tk, tk), tk), :]

    def shift_queries(qi, qslot):
        row = lax.broadcasted_iota(jnp.int32, (HEAD_PAD, 1), 0)
        for hd in heads:
            q = qT_ref[head_rows(hd, HEAD_PAD), q_cols(qi)]
            qf = q.astype(f32)
            bound = jnp.sqrt(jnp.sum(qf * qf, axis=0, keepdims=True)) * kmax_ref[hd] * ATTN_BOUND_MARGIN
            hi = bound.astype(bf16)
            lo = (bound - hi.astype(f32)).astype(bf16)
            q_scr[qslot, hd] = jnp.where(row == SHIFT_ROW, -hi, jnp.where(row == SHIFT_ROW + 1, -lo, q))

    def produce(hd, qslot, j, slot):
        p = jnp.exp2(_dot(k_tile(hd, j), q_scr[qslot, hd]))
        p_scr[hd, slot] = p.astype(bf16)
        psum_scr[hd, slot] = jnp.sum(p, axis=0, keepdims=True)

    def group(g, qslot, next_qslot):
        acc = [acc_ref[hd] for hd in heads]
        den = [den_ref[hd] for hd in heads]
        for u in range(unroll):
            j = g * unroll + u
            nxt = (u + 1) % 2
            for hd in heads:
                if next_qslot is False or u < unroll - 1:
                    produce(hd, qslot, j + 1, nxt)
                elif next_qslot is not None:
                    produce(hd, next_qslot, 0, nxt)
            for hd in heads:
                acc[hd] = acc[hd] + _dot(v_tile(hd, j), p_scr[hd, u % 2])
                den[hd] = den[hd] + psum_scr[hd, u % 2]
        for hd in heads:
            acc_ref[hd] = acc[hd]
            den_ref[hd] = den[hd]

    def recompute_exact(hd, qi):
        q = qT_ref[head_rows(hd, HEAD_PAD), q_cols(qi)]

        def body(j, carry):
            m, den, acc = carry
            s = _dot(k_tile(hd, j), q)
            m_new = jnp.maximum(m, jnp.max(s, axis=0, keepdims=True))
            alpha = jnp.exp2(m - m_new)
            p = jnp.exp2(s - m_new)
            return (m_new, alpha * den + jnp.sum(p, axis=0, keepdims=True),
                    alpha * acc + _dot(v_tile(hd, j), p.astype(bf16)))

        init = (jnp.full((1, tq), -jnp.inf, f32), jnp.zeros((1, tq), f32), jnp.zeros((V_HEAD, tq), f32))
        _, den, acc = lax.fori_loop(0, nk, body, init)
        write_output(hd, qi, acc, den)

    def write_output(hd, qi, acc, den):
        o_ref[head_rows(hd, V_HEAD), q_cols(qi)] = (acc / den).astype(o_ref.dtype)

    def query_tile(qi, qslot, next_qslot):
        if next_qslot is not None:
            shift_queries(qi + 1, next_qslot)
        acc_ref[...] = jnp.zeros(acc_ref.shape, f32)
        den_ref[...] = jnp.zeros(den_ref.shape, f32)
        if ng > 1:
            def body(g, carry):
                group(g, qslot, False)
                return carry
            lax.fori_loop(0, ng - 1, body, 0)
        group(ng - 1, qslot, next_qslot)
        for hd in heads:
            den = den_ref[hd]
            write_output(hd, qi, acc_ref[hd], den)
            dmin_ref[hd] = jnp.minimum(dmin_ref[hd], den)

    dmin_ref[...] = jnp.full(dmin_ref.shape, jnp.inf, f32)
    shift_queries(0, 0)
    for hd in heads:
        produce(hd, 0, 0, 0)
    if nq > 1:
        def q_body(qi, carry):
            query_tile(qi, qi % 2, (qi + 1) % 2)
            return carry
        lax.fori_loop(0, nq - 1, q_body, 0)
    query_tile(nq - 1, (nq - 1) % 2, None)

    for hd in heads:
        @pl.when(jnp.logical_not(jnp.min(dmin_ref[hd]) >= ATTN_DENOM_FLOOR))
        def _(hd=hd):
            def redo(qi, carry):
                recompute_exact(hd, qi)
                return carry
            lax.fori_loop(0, nq, redo, 0)


def _attn_tiles(L):
    nh = ATTN_HEADS_PER_STEP
    tk = min(ATTN_TK, L // 2)
    kv_buffers = 1 if L // tk > 2 else 2
    kv_bytes = kv_buffers * nh * L * (HEAD_PAD + V_HEAD) * 2
    tq = L
    while nh * 2 * tk * tq * 2 > ATTN_VMEM_BUDGET - kv_bytes and tq > LANES:
        tq //= 2
    return tq, tk, kv_buffers


def _attn_call(qT, k, vT, kmax):
    B, _, L = qT.shape
    tq, tk, kv_buffers = _attn_tiles(L)
    kv_mode = pl.Buffered(kv_buffers)
    nk = L // tk
    unroll = min(ATTN_UNROLL, nk)
    assert unroll % 2 == 0 and nk % unroll == 0, "key tiles alternate between two probability buffers"
    nq = L // tq if nk == unroll else 1
    tqb = nq * tq
    nh = ATTN_HEADS_PER_STEP
    return pl.pallas_call(
        functools.partial(_attn_kernel, tq=tq, tk=tk, nq=nq, nk=nk, unroll=unroll, nh=nh),
        grid=(B, N_HEADS // nh, L // tqb),
        scratch_shapes=[pltpu.VMEM((2, nh, HEAD_PAD, tq), bf16), pltpu.VMEM((nh, 2, tk, tq), bf16),
                        pltpu.VMEM((nh, 2, 1, tq), f32), pltpu.VMEM((nh, V_HEAD, tq), f32),
                        pltpu.VMEM((nh, 1, tq), f32), pltpu.VMEM((nh, 1, tq), f32)],
        in_specs=[
            pl.BlockSpec((None, nh, 1, tq), lambda b, h, i: (b, h, 0, 0)),
            pl.BlockSpec((None, nh * HEAD_PAD, tqb), lambda b, h, i: (b, h, i)),
            pl.BlockSpec((None, nh, L, HEAD_PAD), lambda b, h, i: (b, h, 0, 0), pipeline_mode=kv_mode),
            pl.BlockSpec((None, nh * V_HEAD, L), lambda b, h, i: (b, h, 0), pipeline_mode=kv_mode),
        ],
        out_specs=pl.BlockSpec((None, nh * V_HEAD, tqb), lambda b, h, i: (b, h, i)),
        out_shape=jax.ShapeDtypeStruct((B, N_HEADS * V_HEAD, L), bf16),
        name="attn",
        compiler_params=pltpu.CompilerParams(
            dimension_semantics=("parallel", "parallel", "arbitrary"), vmem_limit_bytes=VMEM_LIMIT),
    )(kmax, qT, k, vT)


def _split3(v):
    a1 = v.astype(bf16)
    r1 = v - a1.astype(f32)
    a2 = r1.astype(bf16)
    a3 = (r1 - a2.astype(f32)).astype(bf16)
    return a1, a2, a3


def _chunk_scalars(dtT_raw, biasT, aT, triu):
    nh2 = 2 * SSM_HEADS
    dtT = jax.nn.softplus(dtT_raw + biasT)
    stepT = dtT * aT
    cs3 = _dot(jnp.concatenate(_split3(stepT), axis=0), triu)
    csT = cs3[0:nh2] + cs3[nh2:2 * nh2] + cs3[2 * nh2:3 * nh2]
    return dtT, stepT, csT


def _expand_heads(partsT, e_ref):
    n = len(partsT)
    stk = jnp.concatenate(partsT, axis=0)
    hi = stk.astype(bf16).astype(f32)
    pieces = [hi, stk - hi]
    pad = LANES - 2 * stk.shape[0]
    if pad:
        pieces.append(jnp.zeros((pad, stk.shape[1]), f32))
    nat = jnp.transpose(jnp.concatenate(pieces, axis=0)).astype(bf16)
    full = _dot(nat, e_ref[...])
    return [full[:, j * D_SSM:(j + 1) * D_SSM] for j in range(n)]


def _ssd_kernel(act_ref, dtT_ref, z_ref, biasT_ref, aT_ref, dskipT_ref, norm_ref, triu_ref, e4_ref, e2_ref,
                gT_ref, o_ref, hf_ref, hb_ref, hbs_ref, *, nblk, nsub, Q):
    ph = pl.program_id(1)
    c = pl.program_id(2)
    H = SSM_HEADS

    def group_slices(g):
        return slice(g * GROUP_WIDTH, (g + 1) * GROUP_WIDTH), slice(g * D_STATE, (g + 1) * D_STATE)

    dtT, stepT, csT = _chunk_scalars(dtT_ref[...], biasT_ref[...], aT_ref[...], triu_ref[...])
    totT = jnp.concatenate(
        [jnp.broadcast_to(csT[:, (j + 1) * Q - 1:(j + 1) * Q], (2 * H, Q)) for j in range(nsub)], axis=1)
    cbT_b = csT[H:] - stepT[H:]
    Bm = act_ref[:, D_SSM:D_SSM + GN]

    @pl.when(ph == 0)
    def _backward_states():
        @pl.when(c == 0)
        def _():
            hb_ref[...] = jnp.zeros_like(hb_ref)

        blk = nblk - 1 - c
        wb, eb = _expand_heads([jnp.exp2(cbT_b) * dtT[H:], jnp.exp2(totT[H:] - cbT_b)], e2_ref)
        xw = (act_ref[:, :D_SSM].astype(f32) * wb).astype(bf16)
        for j in reversed(range(nsub)):
            rows = slice(j * Q, (j + 1) * Q)
            hbs_ref[blk * nsub + j] = hb_ref[...].astype(bf16)
            for g in range(SSM_GROUPS):
                sl, sn = group_slices(g)
                upd = lax.dot_general(Bm[rows, sn], xw[rows, sl], TN_DIMS, preferred_element_type=f32)
                hb_ref[g] = eb[j * Q:j * Q + 1, sl] * hb_ref[g] + upd

    @pl.when(ph == 1)
    def _outputs():
        @pl.when(c == 0)
        def _():
            hf_ref[...] = jnp.zeros_like(hf_ref)

        ti = lax.broadcasted_iota(jnp.int32, (Q, Q), 0)
        si = lax.broadcasted_iota(jnp.int32, (Q, Q), 1)
        lower = ti >= si
        lane = lax.broadcasted_iota(jnp.int32, (1, LANES), 1)
        first_half = lane < SSM_HEAD_DIM
        gT = gT_ref[...]
        nw = norm_ref[...]

        xs_b = act_ref[:, :D_SSM]
        Cm = act_ref[:, D_SSM + GN:]
        xs = xs_b.astype(f32)
        csT_f = csT[:H]
        l2dt = jnp.log2(dtT)
        rowf = csT_f - l2dt[:H]
        rowb = cbT_b + l2dt[H:]
        cols = jnp.transpose(jnp.concatenate([csT_f, cbT_b], axis=0))
        cb = Cm.astype(f32) * Bm.astype(f32)
        cb_h = cb.astype(bf16)
        cb_l = (cb - cb_h.astype(f32)).astype(bf16)
        diagT = (lax.dot_general(gT, cb_h, NT_DIMS, preferred_element_type=f32)
                 + lax.dot_general(gT, cb_l, NT_DIMS, preferred_element_type=f32))
        partsT = [jnp.exp2(csT_f), jnp.exp2(totT[H:] - cbT_b), jnp.exp2(totT[:H] - csT_f) * dtT[:H],
                  dskipT_ref[...] + dtT[H:] * diagT]

        def expand_chunk(j):
            return _expand_heads([p[:, j * Q:(j + 1) * Q] for p in partsT], e4_ref)

        expanded = expand_chunk(0)
        for j in range(nsub):
            rows = slice(j * Q, (j + 1) * Q)
            chunk_id = c * nsub + j
            ef, eb, wf, coef = expanded
            if j + 1 < nsub:
                expanded = expand_chunk(j + 1)
            y_groups = []
            for g in range(SSM_GROUPS):
                sl, sn = group_slices(g)
                Cg = Cm[rows, sn]
                Bg = Bm[rows, sn]
                cbm = lax.dot_general(Cg, Bg, NT_DIMS, preferred_element_type=f32)
                y_off = (ef[:, sl] * _dot(Cg, hf_ref[g].astype(bf16))
                         + eb[:, sl] * _dot(Cg, hbs_ref[chunk_id, g]))
                pairs = []
                for hp in range(HEADS_PER_GROUP // 2):
                    lo = g * GROUP_WIDTH + hp * LANES
                    x_pair = xs_b[rows, lo:lo + LANES]
                    y_pair = None
                    for k in range(2):
                        hh = g * HEADS_PER_GROUP + 2 * hp + k
                        arg = jnp.where(lower, cols[rows, hh:hh + 1] - rowf[hh:hh + 1, rows],
                                        rowb[hh:hh + 1, rows] - cols[rows, H + hh:H + hh + 1])
                        mat = (cbm * jnp.exp2(arg)).astype(bf16)
                        keep = first_half if k == 0 else jnp.logical_not(first_half)
                        contrib = _dot(mat, jnp.where(keep, x_pair, jnp.zeros_like(x_pair)))
                        y_pair = contrib if y_pair is None else y_pair + contrib
                    pairs.append(y_pair)
                y_groups.append(jnp.concatenate(pairs, axis=1) + y_off)
                xw = (xs[rows, sl] * wf[:, sl]).astype(bf16)
                upd = lax.dot_general(Bg, xw, TN_DIMS, preferred_element_type=f32)
                hf_ref[g] = ef[Q - 1:Q, sl] * hf_ref[g] + upd

            y = jnp.concatenate(y_groups, axis=1) + xs[rows] * coef
            y = y * _silu(z_ref[rows, :].astype(f32))
            for g in range(SSM_GROUPS):
                sl, _ = group_slices(g)
                o_ref[rows, sl] = _rms(y[:, sl], nw[:, sl]).astype(o_ref.dtype)


def _ssd_call(act, dtT, z, biasT, aT, dskipT, norm, triu, e4, e2, gT):
    B, L, _ = act.shape
    blk = min(SSD_BLOCK, L)
    Q = min(SSD_Q, blk)
    nsub = blk // Q
    nblk = L // blk

    def bidx(p, c):
        return jnp.where(p == 0, nblk - 1 - c, c)

    consts = (biasT, aT, dskipT, norm, triu, e4, e2, gT)
    in_specs = [
        pl.BlockSpec((None, blk, D_XBC), lambda b, p, c: (b, bidx(p, c), 0)),
        pl.BlockSpec((None, 2 * SSM_HEADS, blk), lambda b, p, c: (b, 0, bidx(p, c))),
        pl.BlockSpec((None, blk, D_SSM), lambda b, p, c: (b, p * c, 0)),
    ] + [_const_spec(a.shape) for a in consts]
    state = (SSM_GROUPS, D_STATE, GROUP_WIDTH)
    return pl.pallas_call(
        functools.partial(_ssd_kernel, nblk=nblk, nsub=nsub, Q=Q),
        grid=(B, 2, nblk),
        in_specs=in_specs,
        out_specs=pl.BlockSpec((None, blk, D_SSM), lambda b, p, c: (b, p * c, 0)),
        out_shape=jax.ShapeDtypeStruct((B, L, D_SSM), bf16),
        scratch_shapes=[pltpu.VMEM(state, f32), pltpu.VMEM(state, f32),
                        pltpu.VMEM((nblk * nsub,) + state, bf16)],
        name="ssd",
        compiler_params=pltpu.CompilerParams(
            dimension_semantics=("parallel", "arbitrary", "arbitrary"), vmem_limit_bytes=VMEM_LIMIT),
    )(act, dtT, z, *consts)


def _oproj_kernel(x_ref, aT_ref, s_ref, wa_ref, ws_ref, o_ref):
    o_ref[...] = (x_ref[...]
                  + lax.dot_general(aT_ref[...], wa_ref[...], TN_DIMS, preferred_element_type=f32)
                  + _dot(s_ref[...], ws_ref[...]))


def _oproj_call(x, attnT, ssm, wa, ws):
    B, L, _ = x.shape
    tm = min(OPROJ_TM, L)
    tok = lambda w: pl.BlockSpec((None, tm, w), lambda b, i: (b, i, 0))
    return pl.pallas_call(
        _oproj_kernel, grid=(B, L // tm),
        in_specs=[tok(D_MODEL), pl.BlockSpec((None, N_HEADS * V_HEAD, tm), lambda b, i: (b, 0, i)),
                  tok(D_SSM), _const_spec(wa.shape), _const_spec(ws.shape)],
        out_specs=tok(D_MODEL),
        out_shape=jax.ShapeDtypeStruct((B, L, D_MODEL), f32),
        name="oproj",
        compiler_params=pltpu.CompilerParams(
            dimension_semantics=("parallel", "parallel"), vmem_limit_bytes=VMEM_LIMIT),
    )(x, attnT, ssm, wa, ws)


def _ffn_kernel(x_ref, xp_ref, xn_ref, n2_ref, wg_ref, wu_ref, cw_ref, cb_ref, wd_ref, fn_ref,
                o_ref, h_scr, act_scr, *, nfc):
    i = pl.program_id(1)
    nt = pl.num_programs(1)
    tm = x_ref.shape[0]
    n2 = n2_ref[...]
    h_scr[0:tm, :] = _rms(x_ref[...], n2).astype(bf16)
    halo = jnp.concatenate([xp_ref[...], xn_ref[...]], axis=0)
    h_scr[tm:tm + 2 * F32_ROWS, :] = _rms(halo, n2).astype(bf16)
    has_prev = i > 0
    has_next = i < nt - 1

    def chunk(cf):
        col = cf * FFN_FC
        g_ext = _dot(h_scr[...], wg_ref[:, pl.ds(col, FFN_FC)])
        before = jnp.where(has_prev, g_ext[tm + F32_ROWS - 1:tm + F32_ROWS], 0.0)
        after = jnp.where(has_next, g_ext[tm + F32_ROWS:tm + F32_ROWS + 1], 0.0)
        gc = _conv3_rows(g_ext[:tm], before, after, cw_ref[:, pl.ds(col, FFN_FC)], cb_ref[:, pl.ds(col, FFN_FC)])
        u = _dot(h_scr[0:tm, :], wu_ref[:, pl.ds(col, FFN_FC)])
        act_scr[:, pl.ds(col, FFN_FC)] = (_silu(gc) * u).astype(bf16)

    for cf in range(nfc):
        chunk(cf)
    y = x_ref[...] + _dot(act_scr[...], wd_ref[...])
    o_ref[...] = _rms(y, fn_ref[...])


def _ffn_call(x, n2, wg, wu, cw, cb, wd, fn):
    B, L, _ = x.shape
    tm = min(FFN_TM, L)
    tok = pl.BlockSpec((None, tm, D_MODEL), lambda b, i: (b, i, 0))
    consts = (n2, wg, wu, cw, cb, wd, fn)
    return pl.pallas_call(
        functools.partial(_ffn_kernel, nfc=D_FF // FFN_FC),
        grid=(B, L // tm),
        in_specs=[tok] + _halo_specs(tm, L, D_MODEL) + [_const_spec(a.shape) for a in consts],
        out_specs=tok,
        out_shape=jax.ShapeDtypeStruct((B, L, D_MODEL), f32),
        scratch_shapes=[pltpu.VMEM((tm + 2 * F32_ROWS, D_MODEL), bf16), pltpu.VMEM((tm, D_FF), bf16)],
        name="ffn",
        compiler_params=pltpu.CompilerParams(
            dimension_semantics=("parallel", "parallel"), vmem_limit_bytes=VMEM_LIMIT),
    )(x, x, x, *consts)


def _head_lane_sources():
    src = np.full((HEAD_PAD,), QK_DIM, np.int32)
    half = HEAD_PAD // 2
    src[0:HALF_ROPE] = QK_NOPE + np.arange(HALF_ROPE)
    src[HALF_ROPE:half] = np.arange(half - HALF_ROPE)
    src[half:half + HALF_ROPE] = QK_NOPE + HALF_ROPE + np.arange(HALF_ROPE)
    n_rest = QK_NOPE - (half - HALF_ROPE)
    src[half + HALF_ROPE:half + HALF_ROPE + n_rest] = (half - HALF_ROPE) + np.arange(n_rest)
    return src


def _rope_tables(L):
    inv = ROPE_THETA ** (-jnp.arange(0, QK_ROPE, 2, dtype=f32) / QK_ROPE)
    ang = jnp.arange(L, dtype=f32)[:, None] * inv[None, :]
    cos, sin = jnp.cos(ang), jnp.sin(ang)
    half = HEAD_PAD // 2
    c_tab = jnp.ones((L, HEAD_PAD), f32)
    c_tab = c_tab.at[:, 0:HALF_ROPE].set(cos).at[:, half:half + HALF_ROPE].set(cos)
    s_tab = jnp.zeros((L, HEAD_PAD), f32)
    s_tab = s_tab.at[:, 0:HALF_ROPE].set(-sin).at[:, half:half + HALF_ROPE].set(sin)
    return c_tab, s_tab


def _expand_matrix(n):
    m = np.zeros((LANES, n * D_SSM), np.float32)
    for part in range(2):
        for j in range(n):
            for h in range(SSM_HEADS):
                r = part * n * SSM_HEADS + j * SSM_HEADS + h
                m[r, j * D_SSM + h * SSM_HEAD_DIM:j * D_SSM + (h + 1) * SSM_HEAD_DIM] = 1.0
    return jnp.asarray(m, bf16)


def _prepare_weights(norm1, w_in, q_a_norm, kv_a_norm, w_q_b, w_kv_b, conv_w, conv_b,
                     dt_bias_f, dt_bias_b, a_log_f, a_log_b, d_skip, ssm_norm, w_out,
                     norm2, w_gate, w_up, ffn_conv_w, ffn_conv_b, w_down, final_norm):
    half = HEAD_PAD // 2
    o_kr = Q_LORA + KV_LORA
    o_z = o_kr + QK_ROPE
    o_dt = o_z + D_SSM + D_XBC
    kr_blk = jnp.zeros((D_MODEL, HEAD_PAD), f32)
    kr_blk = kr_blk.at[:, 0:HALF_ROPE].set(w_in[:, o_kr:o_kr + HALF_ROPE])
    kr_blk = kr_blk.at[:, half:half + HALF_ROPE].set(w_in[:, o_kr + HALF_ROPE:o_kr + QK_ROPE])
    win = jnp.concatenate([w_in[:, :o_kr], kr_blk, w_in[:, o_z:o_dt]], axis=1).astype(bf16)
    w_dt = w_in[:, o_dt:o_dt + 2 * SSM_HEADS]

    src = _head_lane_sources()
    wq = w_q_b.reshape(Q_LORA, N_HEADS, QK_DIM)
    wq = jnp.concatenate([wq, jnp.zeros((Q_LORA, N_HEADS, 1), f32)], axis=-1)[:, :, src]
    wqT = wq.reshape(Q_LORA, N_HEADS * HEAD_PAD).T.astype(bf16)
    wkv = w_kv_b.reshape(KV_LORA, N_HEADS, QK_NOPE + V_HEAD)
    src_k = np.where(src < QK_NOPE, src, QK_NOPE)
    wk = jnp.concatenate([wkv[:, :, :QK_NOPE], jnp.zeros((KV_LORA, N_HEADS, 1), f32)], axis=-1)[:, :, src_k]
    wk = wk.reshape(KV_LORA, N_HEADS * HEAD_PAD).astype(bf16)
    wvT = wkv[:, :, QK_NOPE:].reshape(KV_LORA, N_HEADS * V_HEAD).T.astype(bf16)

    row = lambda v: v.reshape(1, -1).astype(f32)
    col = lambda v: v.reshape(-1, 1).astype(f32)
    a_neg = -jnp.exp(jnp.concatenate([a_log_f, a_log_b]).astype(f32)) * LOG2E
    group_of_lane = np.arange(GN) // D_STATE
    group_of_head = np.arange(SSM_HEADS) // HEADS_PER_GROUP
    gT = jnp.asarray(group_of_head[:, None] == group_of_lane[None, :], bf16)
    return dict(
        n1=row(norm1), win=win, qan=row(q_a_norm), kvan=row(kv_a_norm), wqT=wqT, wk=wk, wvT=wvT,
        wdtT=w_dt.T.astype(bf16), cw=conv_w.astype(f32), cb=row(conv_b),
        biasT=col(jnp.concatenate([dt_bias_f, dt_bias_b])), aT=col(a_neg), dskipT=col(d_skip),
        ssm_norm=row(ssm_norm), e4=_expand_matrix(4), e2=_expand_matrix(2), gT=gT,
        wa=w_out[:N_HEADS * V_HEAD].astype(bf16), ws=w_out[N_HEADS * V_HEAD:].astype(bf16),
        n2=row(norm2), wg=w_gate.astype(bf16), wu=w_up.astype(bf16), fcw=ffn_conv_w.astype(f32),
        fcb=row(ffn_conv_b), wd=w_down.astype(bf16), fn=row(final_norm),
    )


def _encoder(x, w):
    B, L, _ = x.shape
    c_tab, s_tab = _rope_tables(L)
    scale = QK_DIM ** -0.5 * LOG2E
    qT, k, vT, z, act, dtT, kn2 = _proj_call(
        x, w["n1"], w["win"], w["qan"], w["kvan"], w["wqT"], w["wk"], w["wvT"], w["wdtT"], w["cw"], w["cb"],
        c_tab, s_tab, (c_tab * scale).T, (s_tab * scale).T)
    kmax = jnp.sqrt(jnp.max(kn2, axis=1))[:, :, :1]
    kmax = jnp.broadcast_to(kmax[..., None], (B, N_HEADS, 1, _attn_tiles(L)[0]))
    attnT = _attn_call(qT, k, vT, kmax)
    blk = min(SSD_BLOCK, L)
    Q = min(SSD_Q, blk)
    triu = jnp.kron(jnp.eye(blk // Q, dtype=f32), jnp.triu(jnp.ones((Q, Q), f32))).astype(bf16)
    ssm = _ssd_call(act, dtT, z, w["biasT"], w["aT"], w["dskipT"], w["ssm_norm"], triu, w["e4"], w["e2"], w["gT"])
    x1 = _oproj_call(x, attnT, ssm, w["wa"], w["ws"])
    return _ffn_call(x1, w["n2"], w["wg"], w["wu"], w["fcw"], w["fcb"], w["wd"], w["fn"])


def kernel(x_prompt, x_sample, norm1, w_in, q_a_norm, kv_a_norm, w_q_b, w_kv_b, conv_w, conv_b,
           dt_bias_f, dt_bias_b, a_log_f, a_log_b, d_skip, ssm_norm, w_out, norm2, w_gate, w_up,
           ffn_conv_w, ffn_conv_b, w_down, final_norm):
    w = _prepare_weights(norm1[0], w_in[0], q_a_norm[0], kv_a_norm[0], w_q_b[0], w_kv_b[0], conv_w[0],
                         conv_b[0], dt_bias_f[0], dt_bias_b[0], a_log_f[0], a_log_b[0], d_skip[0],
                         ssm_norm[0], w_out[0], norm2[0], w_gate[0], w_up[0], ffn_conv_w[0],
                         ffn_conv_b[0], w_down[0], final_norm)
    return (_encoder(x_prompt, w), _encoder(x_sample, w))
```

```python
import functools

import numpy as np
import jax
import jax.numpy as jnp
from jax import lax
from jax.experimental import pallas as pl
from jax.experimental.pallas import tpu as pltpu

D_MODEL = 1024
N_HEADS = 16
QK_NOPE = 64
QK_ROPE = 32
HALF_ROPE = QK_ROPE // 2
QK_DIM = QK_NOPE + QK_ROPE
V_HEAD = 64
Q_LORA = 384
KV_LORA = 256
ROPE_THETA = 10000.0
SSM_HEADS = 16
SSM_HEAD_DIM = 64
D_SSM = SSM_HEADS * SSM_HEAD_DIM
SSM_GROUPS = 2
HEADS_PER_GROUP = SSM_HEADS // SSM_GROUPS
GROUP_WIDTH = D_SSM // SSM_GROUPS
D_STATE = 64
GN = SSM_GROUPS * D_STATE
D_XBC = D_SSM + 2 * GN
D_FF = 2816
EPS = 1e-6
LOG2E = float(np.log2(np.e))

LANES = 128
HEAD_PAD = LANES
F32_ROWS = 8
VMEM_LIMIT = 56 * 1024 * 1024

OFF_Q = 0
OFF_CKV = OFF_Q + Q_LORA
OFF_KR = OFF_CKV + KV_LORA
OFF_Z = OFF_KR + HEAD_PAD
OFF_XBC = OFF_Z + D_SSM

PROJ_TM = 1024
PROJ_CHUNK = 256
ATTN_VMEM_BUDGET = 42 * 1024 * 1024
ATTN_TK = 2048
ATTN_UNROLL = 4
ATTN_HEADS_PER_STEP = 2
SHIFT_ROW = QK_DIM
ATTN_BOUND_MARGIN = 1.0 + 2.0 ** -6
ATTN_DENOM_FLOOR = 2.0 ** -80
SSD_Q = 128
SSD_BLOCK = 512
OPROJ_TM = 1024
FFN_TM = 1024
FFN_FC = 256

NT_DIMS = (((1,), (1,)), ((), ()))
TN_DIMS = (((0,), (0,)), ((), ()))

f32 = jnp.float32
bf16 = jnp.bfloat16


def _rms(x, w):
    return x * lax.rsqrt(jnp.mean(x * x, axis=-1, keepdims=True) + EPS) * w


def _dot(a, b):
    return jnp.dot(a, b, preferred_element_type=f32)


def _silu(x):
    h = 0.5 * x
    return h * jnp.tanh(h) + h


def _conv3_rows(x, before, after, cw, cb):
    n = x.shape[0]
    sub = lax.broadcasted_iota(jnp.int32, (F32_ROWS, 1), 0)
    down = pltpu.roll(x, 1, axis=0)
    up = pltpu.roll(x, n - 1, axis=0)
    x_prev = jnp.concatenate([jnp.where(sub == 0, before, down[:F32_ROWS]), down[F32_ROWS:]], axis=0)
    x_next = jnp.concatenate([up[:n - F32_ROWS],
                              jnp.where(sub == F32_ROWS - 1, after, up[n - F32_ROWS:])], axis=0)
    return x_prev * cw[0:1] + x * cw[1:2] + x_next * cw[2:3] + cb


def _halo_specs(tm, L, width):
    hb = tm // F32_ROWS
    last = L // F32_ROWS - 1
    return [pl.BlockSpec((None, F32_ROWS, width), lambda b, i: (b, jnp.maximum(i * hb - 1, 0), 0)),
            pl.BlockSpec((None, F32_ROWS, width), lambda b, i: (b, jnp.minimum((i + 1) * hb, last), 0))]


def _const_spec(shape):
    nd = len(shape)
    return pl.BlockSpec(shape, lambda *_: (0,) * nd, pipeline_mode=pl.Buffered(1))


def _proj_kernel(x_ref, xp_ref, xn_ref, n1_ref, win_ref, qan_ref, kvan_ref, wqT_ref, wk_ref, wvT_ref, wdtT_ref,
                 cw_ref, cb_ref, c_ref, s_ref, cT_ref, sT_ref,
                 qT_out, k_out, vT_out, z_out, act_out, dtT_out, kn2_out, h_scr):
    i = pl.program_id(1)
    tm = x_ref.shape[0]
    n1 = n1_ref[...]
    h_scr[0:tm, :] = _rms(x_ref[...], n1).astype(bf16)
    halo = jnp.concatenate([xp_ref[...], xn_ref[...]], axis=0)
    h_scr[tm:tm + 2 * F32_ROWS, :] = _rms(halo, n1).astype(bf16)
    h = h_scr[0:tm, :]
    half = HEAD_PAD // 2
    lat = _dot(h, win_ref[:, OFF_Q:OFF_Z])
    hq = _rms(lat[:, OFF_Q:OFF_Q + Q_LORA], qan_ref[...]).astype(bf16)
    hc = _rms(lat[:, OFF_CKV:OFF_CKV + KV_LORA], kvan_ref[...]).astype(bf16)
    kr = lat[:, OFF_KR:OFF_KR + HEAD_PAD]
    krf = kr * c_ref[...] + pltpu.roll(kr, half, axis=1) * s_ref[...]

    has_prev = i > 0
    has_next = i < pl.num_programs(1) - 1
    cT = cT_ref[...]
    sT = sT_ref[...]
    lane = lax.broadcasted_iota(jnp.int32, (1, HEAD_PAD), 1)
    shift_lanes = (lane == SHIFT_ROW) | (lane == SHIFT_ROW + 1)
    heads_per_chunk = PROJ_CHUNK // HEAD_PAD
    norms = [None] * N_HEADS

    def ssd_chunk(col):
        xbc = _dot(h_scr[...], win_ref[:, OFF_XBC + col:OFF_XBC + col + PROJ_CHUNK])
        before = jnp.where(has_prev, xbc[tm + F32_ROWS - 1:tm + F32_ROWS], 0.0)
        after = jnp.where(has_next, xbc[tm + F32_ROWS:tm + F32_ROWS + 1], 0.0)
        conv = _conv3_rows(xbc[:tm], before, after, cw_ref[:, col:col + PROJ_CHUNK], cb_ref[:, col:col + PROJ_CHUNK])
        act_out[:, col:col + PROJ_CHUNK] = _silu(conv).astype(bf16)

    def qk_chunk(h0):
        rows = slice(h0 * HEAD_PAD, (h0 + heads_per_chunk) * HEAD_PAD)
        qT = lax.dot_general(wqT_ref[rows, :], hq, NT_DIMS, preferred_element_type=f32)
        kn = _dot(hc, wk_ref[:, rows])
        for hh in range(heads_per_chunk):
            blk = qT[hh * HEAD_PAD:(hh + 1) * HEAD_PAD, :]
            rot = jnp.concatenate([blk[half:], blk[:half]], axis=0)
            qT_out[(h0 + hh) * HEAD_PAD:(h0 + hh + 1) * HEAD_PAD, :] = (blk * cT + rot * sT).astype(bf16)
            kb = (kn[:, hh * HEAD_PAD:(hh + 1) * HEAD_PAD] + krf).astype(bf16)
            kf = kb.astype(f32)
            n2 = jnp.max(jnp.sum(kf * kf, axis=1, keepdims=True), axis=0, keepdims=True)
            norms[h0 + hh] = jnp.broadcast_to(n2, (1, HEAD_PAD))
            k_out[h0 + hh] = jnp.where(shift_lanes, jnp.ones_like(kb), kb)

    def zv_chunk(col):
        z_out[:, col:col + PROJ_CHUNK] = _dot(h, win_ref[:, OFF_Z + col:OFF_Z + col + PROJ_CHUNK]).astype(bf16)
        vT_out[col:col + PROJ_CHUNK, :] = lax.dot_general(
            wvT_ref[col:col + PROJ_CHUNK, :], hc, NT_DIMS, preferred_element_type=f32).astype(bf16)

    light = ([functools.partial(qk_chunk, h0) for h0 in range(0, N_HEADS, heads_per_chunk)]
             + [functools.partial(zv_chunk, col) for col in range(0, D_SSM, PROJ_CHUNK)])
    heavy = [functools.partial(ssd_chunk, col) for col in range(0, D_XBC, PROJ_CHUNK)]
    per_heavy = -(-len(light) // len(heavy))
    for n, chunk in enumerate(heavy):
        chunk()
        for other in light[n * per_heavy:(n + 1) * per_heavy]:
            other()
    kn2_out[...] = jnp.concatenate(norms, axis=0)
    dtT_out[...] = lax.dot_general(wdtT_ref[...], h, NT_DIMS, preferred_element_type=f32)


def _proj_call(x, n1, win, qan, kvan, wqT, wk, wvT, wdtT, cw, cb, c_tab, s_tab, cT_tab, sT_tab):
    B, L, _ = x.shape
    tm = min(PROJ_TM, L)
    grid = (B, L // tm)
    tok = lambda w: pl.BlockSpec((None, tm, w), lambda b, i: (b, i, 0))
    tokT = lambda w: pl.BlockSpec((None, w, tm), lambda b, i: (b, 0, i))
    consts = (n1, win, qan, kvan, wqT, wk, wvT, wdtT, cw, cb)
    in_specs = [tok(D_MODEL)] + _halo_specs(tm, L, D_MODEL) + [_const_spec(a.shape) for a in consts] + [
        pl.BlockSpec((tm, HEAD_PAD), lambda b, i: (i, 0)),
        pl.BlockSpec((tm, HEAD_PAD), lambda b, i: (i, 0)),
        pl.BlockSpec((HEAD_PAD, tm), lambda b, i: (0, i)),
        pl.BlockSpec((HEAD_PAD, tm), lambda b, i: (0, i)),
    ]
    out_shape = [
        jax.ShapeDtypeStruct((B, N_HEADS * HEAD_PAD, L), bf16),
        jax.ShapeDtypeStruct((B, N_HEADS, L, HEAD_PAD), bf16),
        jax.ShapeDtypeStruct((B, N_HEADS * V_HEAD, L), bf16),
        jax.ShapeDtypeStruct((B, L, D_SSM), bf16),
        jax.ShapeDtypeStruct((B, L, D_XBC), bf16),
        jax.ShapeDtypeStruct((B, 2 * SSM_HEADS, L), f32),
        jax.ShapeDtypeStruct((B, L // tm, N_HEADS, HEAD_PAD), f32),
    ]
    k_spec = pl.BlockSpec((None, N_HEADS, tm, HEAD_PAD), lambda b, i: (b, 0, i, 0))
    out_specs = [tokT(N_HEADS * HEAD_PAD), k_spec, tokT(N_HEADS * V_HEAD),
                 tok(D_SSM), tok(D_XBC), tokT(2 * SSM_HEADS),
                 pl.BlockSpec((None, None, N_HEADS, HEAD_PAD), lambda b, i: (b, i, 0, 0))]
    return pl.pallas_call(
        _proj_kernel, grid=grid, in_specs=in_specs, out_specs=out_specs, out_shape=out_shape,
        scratch_shapes=[pltpu.VMEM((tm + 2 * F32_ROWS, D_MODEL), bf16)],
        name="proj",
        compiler_params=pltpu.CompilerParams(
            dimension_semantics=("parallel", "parallel"), vmem_limit_bytes=VMEM_LIMIT),
    )(x, x, x, *consts, c_tab, s_tab, cT_tab, sT_tab)


def _attn_kernel(kmax_ref, qT_ref, k_ref, vT_ref, o_ref, q_scr, p_scr, psum_scr, acc_ref, den_ref, dmin_ref, *, tq, tk, nq, nk, unroll, nh):
    ng = nk // unroll
    heads = range(nh)

    def q_cols(qi):
        return pl.ds(pl.multiple_of(qi * tq, tq), tq)

    def head_rows(hd, width):
        return slice(hd * width, (hd + 1) * width)

    def v_tile(hd, j):
        return vT_ref[head_rows(hd, V_HEAD), pl.ds(pl.multiple_of(j * tk, tk), tk)]

    def k_tile(hd, j):
        return k_ref[hd, pl.ds(pl.multiple_of(j * tk, tk), tk), :]

    def shift_queries(qi, qslot):
        row = lax.broadcasted_iota(jnp.int32, (HEAD_PAD, 1), 0)
        for hd in heads:
            q = qT_ref[head_rows(hd, HEAD_PAD), q_cols(qi)]
            qf = q.astype(f32)
            bound = jnp.sqrt(jnp.sum(qf * qf, axis=0, keepdims=True)) * kmax_ref[hd] * ATTN_BOUND_MARGIN
            hi = bound.astype(bf16)
            lo = (bound - hi.astype(f32)).astype(bf16)
            q_scr[qslot, hd] = jnp.where(row == SHIFT_ROW, -hi, jnp.where(row == SHIFT_ROW + 1, -lo, q))

    def produce(hd, qslot, j, slot):
        p = jnp.exp2(_dot(k_tile(hd, j), q_scr[qslot, hd]))
        p_scr[hd, slot] = p.astype(bf16)
        psum_scr[hd, slot] = jnp.sum(p, axis=0, keepdims=True)

    def group(g, qslot, next_qslot):
        acc = [acc_ref[hd] for hd in heads]
        den = [den_ref[hd] for hd in heads]
        for u in range(unroll):
            j = g * unroll + u
            nxt = (u + 1) % 2
            for hd in heads:
                if next_qslot is False or u < unroll - 1:
                    produce(hd, qslot, j + 1, nxt)
                elif next_qslot is not None:
                    produce(hd, next_qslot, 0, nxt)
            for hd in heads:
                acc[hd] = acc[hd] + _dot(v_tile(hd, j), p_scr[hd, u % 2])
                den[hd] = den[hd] + psum_scr[hd, u % 2]
        for hd in heads:
            acc_ref[hd] = acc[hd]
            den_ref[hd] = den[hd]

    def recompute_exact(hd, qi):
        q = qT_ref[head_rows(hd, HEAD_PAD), q_cols(qi)]

        def body(j, carry):
            m, den, acc = carry
            s = _dot(k_tile(hd, j), q)
            m_new = jnp.maximum(m, jnp.max(s, axis=0, keepdims=True))
            alpha = jnp.exp2(m - m_new)
            p = jnp.exp2(s - m_new)
            return (m_new, alpha * den + jnp.sum(p, axis=0, keepdims=True),
                    alpha * acc + _dot(v_tile(hd, j), p.astype(bf16)))

        init = (jnp.full((1, tq), -jnp.inf, f32), jnp.zeros((1, tq), f32), jnp.zeros((V_HEAD, tq), f32))
        _, den, acc = lax.fori_loop(0, nk, body, init)
        write_output(hd, qi, acc, den)

    def write_output(hd, qi, acc, den):
        o_ref[head_rows(hd, V_HEAD), q_cols(qi)] = (acc / den).astype(o_ref.dtype)

    def query_tile(qi, qslot, next_qslot):
        if next_qslot is not None:
            shift_queries(qi + 1, next_qslot)
        acc_ref[...] = jnp.zeros(acc_ref.shape, f32)
        den_ref[...] = jnp.zeros(den_ref.shape, f32)
        if ng > 1:
            def body(g, carry):
                group(g, qslot, False)
                return carry
            lax.fori_loop(0, ng - 1, body, 0)
        group(ng - 1, qslot, next_qslot)
        for hd in heads:
            den = den_ref[hd]
            write_output(hd, qi, acc_ref[hd], den)
            dmin_ref[hd] = jnp.minimum(dmin_ref[hd], den)

    dmin_ref[...] = jnp.full(dmin_ref.shape, jnp.inf, f32)
    shift_queries(0, 0)
    for hd in heads:
        produce(hd, 0, 0, 0)
    if nq > 1:
        def q_body(qi, carry):
            query_tile(qi, qi % 2, (qi + 1) % 2)
            return carry
        lax.fori_loop(0, nq - 1, q_body, 0)
    query_tile(nq - 1, (nq - 1) % 2, None)

    for hd in heads:
        @pl.when(jnp.logical_not(jnp.min(dmin_ref[hd]) >= ATTN_DENOM_FLOOR))
        def _(hd=hd):
            def redo(qi, carry):
                recompute_exact(hd, qi)
                return carry
            lax.fori_loop(0, nq, redo, 0)


def _attn_tiles(L):
    nh = ATTN_HEADS_PER_STEP
    tk = min(ATTN_TK, L // 2)
    kv_buffers = 1 if L // tk > 2 else 2
    kv_bytes = kv_buffers * nh * L * (HEAD_PAD + V_HEAD) * 2
    tq = L
    while nh * 2 * tk * tq * 2 > ATTN_VMEM_BUDGET - kv_bytes and tq > LANES:
        tq //= 2
    return tq, tk, kv_buffers


def _attn_call(qT, k, vT, kmax):
    B, _, L = qT.shape
    tq, tk, kv_buffers = _attn_tiles(L)
    kv_mode = pl.Buffered(kv_buffers)
    nk = L // tk
    unroll = min(ATTN_UNROLL, nk)
    assert unroll % 2 == 0 and nk % unroll == 0, "key tiles alternate between two probability buffers"
    nq = L // tq if nk == unroll else 1
    tqb = nq * tq
    nh = ATTN_HEADS_PER_STEP
    return pl.pallas_call(
        functools.partial(_attn_kernel, tq=tq, tk=tk, nq=nq, nk=nk, unroll=unroll, nh=nh),
        grid=(B, N_HEADS // nh, L // tqb),
        scratch_shapes=[pltpu.VMEM((2, nh, HEAD_PAD, tq), bf16), pltpu.VMEM((nh, 2, tk, tq), bf16),
                        pltpu.VMEM((nh, 2, 1, tq), f32), pltpu.VMEM((nh, V_HEAD, tq), f32),
                        pltpu.VMEM((nh, 1, tq), f32), pltpu.VMEM((nh, 1, tq), f32)],
        in_specs=[
            pl.BlockSpec((None, nh, 1, tq), lambda b, h, i: (b, h, 0, 0)),
            pl.BlockSpec((None, nh * HEAD_PAD, tqb), lambda b, h, i: (b, h, i)),
            pl.BlockSpec((None, nh, L, HEAD_PAD), lambda b, h, i: (b, h, 0, 0), pipeline_mode=kv_mode),
            pl.BlockSpec((None, nh * V_HEAD, L), lambda b, h, i: (b, h, 0), pipeline_mode=kv_mode),
        ],
        out_specs=pl.BlockSpec((None, nh * V_HEAD, tqb), lambda b, h, i: (b, h, i)),
        out_shape=jax.ShapeDtypeStruct((B, N_HEADS * V_HEAD, L), bf16),
        name="attn",
        compiler_params=pltpu.CompilerParams(
            dimension_semantics=("parallel", "parallel", "arbitrary"), vmem_limit_bytes=VMEM_LIMIT),
    )(kmax, qT, k, vT)


def _split3(v):
    a1 = v.astype(bf16)
    r1 = v - a1.astype(f32)
    a2 = r1.astype(bf16)
    a3 = (r1 - a2.astype(f32)).astype(bf16)
    return a1, a2, a3


def _chunk_scalars(dtT_raw, biasT, aT, triu):
    nh2 = 2 * SSM_HEADS
    dtT = jax.nn.softplus(dtT_raw + biasT)
    stepT = dtT * aT
    cs3 = _dot(jnp.concatenate(_split3(stepT), axis=0), triu)
    csT = cs3[0:nh2] + cs3[nh2:2 * nh2] + cs3[2 * nh2:3 * nh2]
    return dtT, stepT, csT


def _expand_heads(partsT, e_ref):
    n = len(partsT)
    stk = jnp.concatenate(partsT, axis=0)
    hi = stk.astype(bf16).astype(f32)
    pieces = [hi, stk - hi]
    pad = LANES - 2 * stk.shape[0]
    if pad:
        pieces.append(jnp.zeros((pad, stk.shape[1]), f32))
    nat = jnp.transpose(jnp.concatenate(pieces, axis=0)).astype(bf16)
    full = _dot(nat, e_ref[...])
    return [full[:, j * D_SSM:(j + 1) * D_SSM] for j in range(n)]


def _ssd_kernel(act_ref, dtT_ref, z_ref, biasT_ref, aT_ref, dskipT_ref, norm_ref, triu_ref, e4_ref, e2_ref,
                gT_ref, o_ref, hf_ref, hb_ref, hbs_ref, *, nblk, nsub, Q):
    ph = pl.program_id(1)
    c = pl.program_id(2)
    H = SSM_HEADS

    def group_slices(g):
        return slice(g * GROUP_WIDTH, (g + 1) * GROUP_WIDTH), slice(g * D_STATE, (g + 1) * D_STATE)

    dtT, stepT, csT = _chunk_scalars(dtT_ref[...], biasT_ref[...], aT_ref[...], triu_ref[...])
    totT = jnp.concatenate(
        [jnp.broadcast_to(csT[:, (j + 1) * Q - 1:(j + 1) * Q], (2 * H, Q)) for j in range(nsub)], axis=1)
    cbT_b = csT[H:] - stepT[H:]
    Bm = act_ref[:, D_SSM:D_SSM + GN]

    @pl.when(ph == 0)
    def _backward_states():
        @pl.when(c == 0)
        def _():
            hb_ref[...] = jnp.zeros_like(hb_ref)

        blk = nblk - 1 - c
        wb, eb = _expand_heads([jnp.exp2(cbT_b) * dtT[H:], jnp.exp2(totT[H:] - cbT_b)], e2_ref)
        xw = (act_ref[:, :D_SSM].astype(f32) * wb).astype(bf16)
        for j in reversed(range(nsub)):
            rows = slice(j * Q, (j + 1) * Q)
            hbs_ref[blk * nsub + j] = hb_ref[...].astype(bf16)
            for g in range(SSM_GROUPS):
                sl, sn = group_slices(g)
                upd = lax.dot_general(Bm[rows, sn], xw[rows, sl], TN_DIMS, preferred_element_type=f32)
                hb_ref[g] = eb[j * Q:j * Q + 1, sl] * hb_ref[g] + upd

    @pl.when(ph == 1)
    def _outputs():
        @pl.when(c == 0)
        def _():
            hf_ref[...] = jnp.zeros_like(hf_ref)

        ti = lax.broadcasted_iota(jnp.int32, (Q, Q), 0)
        si = lax.broadcasted_iota(jnp.int32, (Q, Q), 1)
        lower = ti >= si
        lane = lax.broadcasted_iota(jnp.int32, (1, LANES), 1)
        first_half = lane < SSM_HEAD_DIM
        gT = gT_ref[...]
        nw = norm_ref[...]

        xs_b = act_ref[:, :D_SSM]
        Cm = act_ref[:, D_SSM + GN:]
        xs = xs_b.astype(f32)
        csT_f = csT[:H]
        l2dt = jnp.log2(dtT)
        rowf = csT_f - l2dt[:H]
        rowb = cbT_b + l2dt[H:]
        cols = jnp.transpose(jnp.concatenate([csT_f, cbT_b], axis=0))
        cb = Cm.astype(f32) * Bm.astype(f32)
        cb_h = cb.astype(bf16)
        cb_l = (cb - cb_h.astype(f32)).astype(bf16)
        diagT = (lax.dot_general(gT, cb_h, NT_DIMS, preferred_element_type=f32)
                 + lax.dot_general(gT, cb_l, NT_DIMS, preferred_element_type=f32))
        partsT = [jnp.exp2(csT_f), jnp.exp2(totT[H:] - cbT_b), jnp.exp2(totT[:H] - csT_f) * dtT[:H],
                  dskipT_ref[...] + dtT[H:] * diagT]

        def expand_chunk(j):
            return _expand_heads([p[:, j * Q:(j + 1) * Q] for p in partsT], e4_ref)

        expanded = expand_chunk(0)
        for j in range(nsub):
            rows = slice(j * Q, (j + 1) * Q)
            chunk_id = c * nsub + j
            ef, eb, wf, coef = expanded
            if j + 1 < nsub:
                expanded = expand_chunk(j + 1)
            y_groups = []
            for g in range(SSM_GROUPS):
                sl, sn = group_slices(g)
                Cg = Cm[rows, sn]
                Bg = Bm[rows, sn]
                cbm = lax.dot_general(Cg, Bg, NT_DIMS, preferred_element_type=f32)
                y_off = (ef[:, sl] * _dot(Cg, hf_ref[g].astype(bf16))
                         + eb[:, sl] * _dot(Cg, hbs_ref[chunk_id, g]))
                pairs = []
                for hp in range(HEADS_PER_GROUP // 2):
                    lo = g * GROUP_WIDTH + hp * LANES
                    x_pair = xs_b[rows, lo:lo + LANES]
                    y_pair = None
                    for k in range(2):
                        hh = g * HEADS_PER_GROUP + 2 * hp + k
                        arg = jnp.where(lower, cols[rows, hh:hh + 1] - rowf[hh:hh + 1, rows],
                                        rowb[hh:hh + 1, rows] - cols[rows, H + hh:H + hh + 1])
                        mat = (cbm * jnp.exp2(arg)).astype(bf16)
                        keep = first_half if k == 0 else jnp.logical_not(first_half)
                        contrib = _dot(mat, jnp.where(keep, x_pair, jnp.zeros_like(x_pair)))
                        y_pair = contrib if y_pair is None else y_pair + contrib
                    pairs.append(y_pair)
                y_groups.append(jnp.concatenate(pairs, axis=1) + y_off)
                xw = (xs[rows, sl] * wf[:, sl]).astype(bf16)
                upd = lax.dot_general(Bg, xw, TN_DIMS, preferred_element_type=f32)
                hf_ref[g] = ef[Q - 1:Q, sl] * hf_ref[g] + upd

            y = jnp.concatenate(y_groups, axis=1) + xs[rows] * coef
            y = y * _silu(z_ref[rows, :].astype(f32))
            for g in range(SSM_GROUPS):
                sl, _ = group_slices(g)
                o_ref[rows, sl] = _rms(y[:, sl], nw[:, sl]).astype(o_ref.dtype)


def _ssd_call(act, dtT, z, biasT, aT, dskipT, norm, triu, e4, e2, gT):
    B, L, _ = act.shape
    blk = min(SSD_BLOCK, L)
    Q = min(SSD_Q, blk)
    nsub = blk // Q
    nblk = L // blk

    def bidx(p, c):
        return jnp.where(p == 0, nblk - 1 - c, c)

    consts = (biasT, aT, dskipT, norm, triu, e4, e2, gT)
    in_specs = [
        pl.BlockSpec((None, blk, D_XBC), lambda b, p, c: (b, bidx(p, c), 0)),
        pl.BlockSpec((None, 2 * SSM_HEADS, blk), lambda b, p, c: (b, 0, bidx(p, c))),
        pl.BlockSpec((None, blk, D_SSM), lambda b, p, c: (b, p * c, 0)),
    ] + [_const_spec(a.shape) for a in consts]
    state = (SSM_GROUPS, D_STATE, GROUP_WIDTH)
    return pl.pallas_call(
        functools.partial(_ssd_kernel, nblk=nblk, nsub=nsub, Q=Q),
        grid=(B, 2, nblk),
        in_specs=in_specs,
        out_specs=pl.BlockSpec((None, blk, D_SSM), lambda b, p, c: (b, p * c, 0)),
        out_shape=jax.ShapeDtypeStruct((B, L, D_SSM), bf16),
        scratch_shapes=[pltpu.VMEM(state, f32), pltpu.VMEM(state, f32),
                        pltpu.VMEM((nblk * nsub,) + state, bf16)],
        name="ssd",
        compiler_params=pltpu.CompilerParams(
            dimension_semantics=("parallel", "arbitrary", "arbitrary"), vmem_limit_bytes=VMEM_LIMIT),
    )(act, dtT, z, *consts)


def _oproj_kernel(x_ref, aT_ref, s_ref, wa_ref, ws_ref, o_ref):
    o_ref[...] = (x_ref[...]
                  + lax.dot_general(aT_ref[...], wa_ref[...], TN_DIMS, preferred_element_type=f32)
                  + _dot(s_ref[...], ws_ref[...]))


def _oproj_call(x, attnT, ssm, wa, ws):
    B, L, _ = x.shape
    tm = min(OPROJ_TM, L)
    tok = lambda w: pl.BlockSpec((None, tm, w), lambda b, i: (b, i, 0))
    return pl.pallas_call(
        _oproj_kernel, grid=(B, L // tm),
        in_specs=[tok(D_MODEL), pl.BlockSpec((None, N_HEADS * V_HEAD, tm), lambda b, i: (b, 0, i)),
                  tok(D_SSM), _const_spec(wa.shape), _const_spec(ws.shape)],
        out_specs=tok(D_MODEL),
        out_shape=jax.ShapeDtypeStruct((B, L, D_MODEL), f32),
        name="oproj",
        compiler_params=pltpu.CompilerParams(
            dimension_semantics=("parallel", "parallel"), vmem_limit_bytes=VMEM_LIMIT),
    )(x, attnT, ssm, wa, ws)


def _ffn_kernel(x_ref, xp_ref, xn_ref, n2_ref, wg_ref, wu_ref, cw_ref, cb_ref, wd_ref, fn_ref,
                o_ref, h_scr, act_scr, *, nfc):
    i = pl.program_id(1)
    nt = pl.num_programs(1)
    tm = x_ref.shape[0]
    n2 = n2_ref[...]
    h_scr[0:tm, :] = _rms(x_ref[...], n2).astype(bf16)
    halo = jnp.concatenate([xp_ref[...], xn_ref[...]], axis=0)
    h_scr[tm:tm + 2 * F32_ROWS, :] = _rms(halo, n2).astype(bf16)
    has_prev = i > 0
    has_next = i < nt - 1

    def chunk(cf):
        col = cf * FFN_FC
        g_ext = _dot(h_scr[...], wg_ref[:, pl.ds(col, FFN_FC)])
        before = jnp.where(has_prev, g_ext[tm + F32_ROWS - 1:tm + F32_ROWS], 0.0)
        after = jnp.where(has_next, g_ext[tm + F32_ROWS:tm + F32_ROWS + 1], 0.0)
        gc = _conv3_rows(g_ext[:tm], before, after, cw_ref[:, pl.ds(col, FFN_FC)], cb_ref[:, pl.ds(col, FFN_FC)])
        u = _dot(h_scr[0:tm, :], wu_ref[:, pl.ds(col, FFN_FC)])
        act_scr[:, pl.ds(col, FFN_FC)] = (_silu(gc) * u).astype(bf16)

    for cf in range(nfc):
        chunk(cf)
    y = x_ref[...] + _dot(act_scr[...], wd_ref[...])
    o_ref[...] = _rms(y, fn_ref[...])


def _ffn_call(x, n2, wg, wu, cw, cb, wd, fn):
    B, L, _ = x.shape
    tm = min(FFN_TM, L)
    tok = pl.BlockSpec((None, tm, D_MODEL), lambda b, i: (b, i, 0))
    consts = (n2, wg, wu, cw, cb, wd, fn)
    return pl.pallas_call(
        functools.partial(_ffn_kernel, nfc=D_FF // FFN_FC),
        grid=(B, L // tm),
        in_specs=[tok] + _halo_specs(tm, L, D_MODEL) + [_const_spec(a.shape) for a in consts],
        out_specs=tok,
        out_shape=jax.ShapeDtypeStruct((B, L, D_MODEL), f32),
        scratch_shapes=[pltpu.VMEM((tm + 2 * F32_ROWS, D_MODEL), bf16), pltpu.VMEM((tm, D_FF), bf16)],
        name="ffn",
        compiler_params=pltpu.CompilerParams(
            dimension_semantics=("parallel", "parallel"), vmem_limit_bytes=VMEM_LIMIT),
    )(x, x, x, *consts)


def _head_lane_sources():
    src = np.full((HEAD_PAD,), QK_DIM, np.int32)
    half = HEAD_PAD // 2
    src[0:HALF_ROPE] = QK_NOPE + np.arange(HALF_ROPE)
    src[HALF_ROPE:half] = np.arange(half - HALF_ROPE)
    src[half:half + HALF_ROPE] = QK_NOPE + HALF_ROPE + np.arange(HALF_ROPE)
    n_rest = QK_NOPE - (half - HALF_ROPE)
    src[half + HALF_ROPE:half + HALF_ROPE + n_rest] = (half - HALF_ROPE) + np.arange(n_rest)
    return src


def _rope_tables(L):
    inv = ROPE_THETA ** (-jnp.arange(0, QK_ROPE, 2, dtype=f32) / QK_ROPE)
    ang = jnp.arange(L, dtype=f32)[:, None] * inv[None, :]
    cos, sin = jnp.cos(ang), jnp.sin(ang)
    half = HEAD_PAD // 2
    c_tab = jnp.ones((L, HEAD_PAD), f32)
    c_tab = c_tab.at[:, 0:HALF_ROPE].set(cos).at[:, half:half + HALF_ROPE].set(cos)
    s_tab = jnp.zeros((L, HEAD_PAD), f32)
    s_tab = s_tab.at[:, 0:HALF_ROPE].set(-sin).at[:, half:half + HALF_ROPE].set(sin)
    return c_tab, s_tab


def _expand_matrix(n):
    m = np.zeros((LANES, n * D_SSM), np.float32)
    for part in range(2):
        for j in range(n):
            for h in range(SSM_HEADS):
                r = part * n * SSM_HEADS + j * SSM_HEADS + h
                m[r, j * D_SSM + h * SSM_HEAD_DIM:j * D_SSM + (h + 1) * SSM_HEAD_DIM] = 1.0
    return jnp.asarray(m, bf16)


def _prepare_weights(norm1, w_in, q_a_norm, kv_a_norm, w_q_b, w_kv_b, conv_w, conv_b,
                     dt_bias_f, dt_bias_b, a_log_f, a_log_b, d_skip, ssm_norm, w_out,
                     norm2, w_gate, w_up, ffn_conv_w, ffn_conv_b, w_down, final_norm):
    half = HEAD_PAD // 2
    o_kr = Q_LORA + KV_LORA
    o_z = o_kr + QK_ROPE
    o_dt = o_z + D_SSM + D_XBC
    kr_blk = jnp.zeros((D_MODEL, HEAD_PAD), f32)
    kr_blk = kr_blk.at[:, 0:HALF_ROPE].set(w_in[:, o_kr:o_kr + HALF_ROPE])
    kr_blk = kr_blk.at[:, half:half + HALF_ROPE].set(w_in[:, o_kr + HALF_ROPE:o_kr + QK_ROPE])
    win = jnp.concatenate([w_in[:, :o_kr], kr_blk, w_in[:, o_z:o_dt]], axis=1).astype(bf16)
    w_dt = w_in[:, o_dt:o_dt + 2 * SSM_HEADS]

    src = _head_lane_sources()
    wq = w_q_b.reshape(Q_LORA, N_HEADS, QK_DIM)
    wq = jnp.concatenate([wq, jnp.zeros((Q_LORA, N_HEADS, 1), f32)], axis=-1)[:, :, src]
    wqT = wq.reshape(Q_LORA, N_HEADS * HEAD_PAD).T.astype(bf16)
    wkv = w_kv_b.reshape(KV_LORA, N_HEADS, QK_NOPE + V_HEAD)
    src_k = np.where(src < QK_NOPE, src, QK_NOPE)
    wk = jnp.concatenate([wkv[:, :, :QK_NOPE], jnp.zeros((KV_LORA, N_HEADS, 1), f32)], axis=-1)[:, :, src_k]
    wk = wk.reshape(KV_LORA, N_HEADS * HEAD_PAD).astype(bf16)
    wvT = wkv[:, :, QK_NOPE:].reshape(KV_LORA, N_HEADS * V_HEAD).T.astype(bf16)

    row = lambda v: v.reshape(1, -1).astype(f32)
    col = lambda v: v.reshape(-1, 1).astype(f32)
    a_neg = -jnp.exp(jnp.concatenate([a_log_f, a_log_b]).astype(f32)) * LOG2E
    group_of_lane = np.arange(GN) // D_STATE
    group_of_head = np.arange(SSM_HEADS) // HEADS_PER_GROUP
    gT = jnp.asarray(group_of_head[:, None] == group_of_lane[None, :], bf16)
    return dict(
        n1=row(norm1), win=win, qan=row(q_a_norm), kvan=row(kv_a_norm), wqT=wqT, wk=wk, wvT=wvT,
        wdtT=w_dt.T.astype(bf16), cw=conv_w.astype(f32), cb=row(conv_b),
        biasT=col(jnp.concatenate([dt_bias_f, dt_bias_b])), aT=col(a_neg), dskipT=col(d_skip),
        ssm_norm=row(ssm_norm), e4=_expand_matrix(4), e2=_expand_matrix(2), gT=gT,
        wa=w_out[:N_HEADS * V_HEAD].astype(bf16), ws=w_out[N_HEADS * V_HEAD:].astype(bf16),
        n2=row(norm2), wg=w_gate.astype(bf16), wu=w_up.astype(bf16), fcw=ffn_conv_w.astype(f32),
        fcb=row(ffn_conv_b), wd=w_down.astype(bf16), fn=row(final_norm),
    )


def _encoder(x, w):
    B, L, _ = x.shape
    c_tab, s_tab = _rope_tables(L)
    scale = QK_DIM ** -0.5 * LOG2E
    qT, k, vT, z, act, dtT, kn2 = _proj_call(
        x, w["n1"], w["win"], w["qan"], w["kvan"], w["wqT"], w["wk"], w["wvT"], w["wdtT"], w["cw"], w["cb"],
        c_tab, s_tab, (c_tab * scale).T, (s_tab * scale).T)
    kmax = jnp.sqrt(jnp.max(kn2, axis=1))[:, :, :1]
    kmax = jnp.broadcast_to(kmax[..., None], (B, N_HEADS, 1, _attn_tiles(L)[0]))
    attnT = _attn_call(qT, k, vT, kmax)
    blk = min(SSD_BLOCK, L)
    Q = min(SSD_Q, blk)
    triu = jnp.kron(jnp.eye(blk // Q, dtype=f32), jnp.triu(jnp.ones((Q, Q), f32))).astype(bf16)
    ssm = _ssd_call(act, dtT, z, w["biasT"], w["aT"], w["dskipT"], w["ssm_norm"], triu, w["e4"], w["e2"], w["gT"])
    x1 = _oproj_call(x, attnT, ssm, w["wa"], w["ws"])
    return _ffn_call(x1, w["n2"], w["wg"], w["wu"], w["fcw"], w["fcb"], w["wd"], w["fn"])


def kernel(x_prompt, x_sample, norm1, w_in, q_a_norm, kv_a_norm, w_q_b, w_kv_b, conv_w, conv_b,
           dt_bias_f, dt_bias_b, a_log_f, a_log_b, d_skip, ssm_norm, w_out, norm2, w_gate, w_up,
           ffn_conv_w, ffn_conv_b, w_down, final_norm):
    w = _prepare_weights(norm1[0], w_in[0], q_a_norm[0], kv_a_norm[0], w_q_b[0], w_kv_b[0], conv_w[0],
                         conv_b[0], dt_bias_f[0], dt_bias_b[0], a_log_f[0], a_log_b[0], d_skip[0],
                         ssm_norm[0], w_out[0], norm2[0], w_gate[0], w_up[0], ffn_conv_w[0],
                         ffn_conv_b[0], w_down[0], final_norm)
    return (_encoder(x_prompt, w), _encoder(x_sample, w))
```

```python
import functools

import numpy as np
import jax
import jax.numpy as jnp
from jax import lax
from jax.experimental import pallas as pl
from jax.experimental.pallas import tpu as pltpu

D_MODEL = 1024
N_HEADS = 16
QK_NOPE = 64
QK_ROPE = 32
HALF_ROPE = QK_ROPE // 2
QK_DIM = QK_NOPE + QK_ROPE
V_HEAD = 64
Q_LORA = 384
KV_LORA = 256
ROPE_THETA = 10000.0
SSM_HEADS = 16
SSM_HEAD_DIM = 64
D_SSM = SSM_HEADS * SSM_HEAD_DIM
SSM_GROUPS = 2
HEADS_PER_GROUP = SSM_HEADS // SSM_GROUPS
GROUP_WIDTH = D_SSM // SSM_GROUPS
D_STATE = 64
GN = SSM_GROUPS * D_STATE
D_XBC = D_SSM + 2 * GN
D_FF = 2816
EPS = 1e-6
LOG2E = float(np.log2(np.e))

LANES = 128
HEAD_PAD = LANES
F32_ROWS = 8
VMEM_LIMIT = 56 * 1024 * 1024

OFF_Q = 0
OFF_CKV = OFF_Q + Q_LORA
OFF_KR = OFF_CKV + KV_LORA
OFF_Z = OFF_KR + HEAD_PAD
OFF_XBC = OFF_Z + D_SSM

PROJ_TM = 1024
PROJ_CHUNK = 256
ATTN_VMEM_BUDGET = 42 * 1024 * 1024
ATTN_TK = 2048
ATTN_UNROLL = 4
ATTN_HEADS_PER_STEP = 2
SHIFT_ROW = QK_DIM
ATTN_BOUND_MARGIN = 1.0 + 2.0 ** -6
ATTN_DENOM_FLOOR = 2.0 ** -80
SSD_Q = 128
SSD_BLOCK = 1024
OPROJ_TM = 1024
FFN_TM = 1024
FFN_FC = 256

NT_DIMS = (((1,), (1,)), ((), ()))
TN_DIMS = (((0,), (0,)), ((), ()))

f32 = jnp.float32
bf16 = jnp.bfloat16


def _rms(x, w):
    return x * lax.rsqrt(jnp.mean(x * x, axis=-1, keepdims=True) + EPS) * w


def _dot(a, b):
    return jnp.dot(a, b, preferred_element_type=f32)


def _silu(x):
    h = 0.5 * x
    return h * jnp.tanh(h) + h


def _conv3_rows(x, before, after, cw, cb):
    n = x.shape[0]
    sub = lax.broadcasted_iota(jnp.int32, (F32_ROWS, 1), 0)
    down = pltpu.roll(x, 1, axis=0)
    up = pltpu.roll(x, n - 1, axis=0)
    x_prev = jnp.concatenate([jnp.where(sub == 0, before, down[:F32_ROWS]), down[F32_ROWS:]], axis=0)
    x_next = jnp.concatenate([up[:n - F32_ROWS],
                              jnp.where(sub == F32_ROWS - 1, after, up[n - F32_ROWS:])], axis=0)
    return x_prev * cw[0:1] + x * cw[1:2] + x_next * cw[2:3] + cb


def _halo_specs(tm, L, width):
    hb = tm // F32_ROWS
    last = L // F32_ROWS - 1
    return [pl.BlockSpec((None, F32_ROWS, width), lambda b, i: (b, jnp.maximum(i * hb - 1, 0), 0)),
            pl.BlockSpec((None, F32_ROWS, width), lambda b, i: (b, jnp.minimum((i + 1) * hb, last), 0))]


def _const_spec(shape):
    nd = len(shape)
    return pl.BlockSpec(shape, lambda *_: (0,) * nd, pipeline_mode=pl.Buffered(1))


def _proj_kernel(x_ref, xp_ref, xn_ref, n1_ref, win_ref, qan_ref, kvan_ref, wqT_ref, wk_ref, wvT_ref, wdtT_ref,
                 cw_ref, cb_ref, c_ref, s_ref, cT_ref, sT_ref,
                 qT_out, k_out, vT_out, z_out, act_out, dtT_out, kn2_out, h_scr):
    i = pl.program_id(1)
    tm = x_ref.shape[0]
    n1 = n1_ref[...]
    h_scr[0:tm, :] = _rms(x_ref[...], n1).astype(bf16)
    halo = jnp.concatenate([xp_ref[...], xn_ref[...]], axis=0)
    h_scr[tm:tm + 2 * F32_ROWS, :] = _rms(halo, n1).astype(bf16)
    h = h_scr[0:tm, :]
    half = HEAD_PAD // 2
    lat = _dot(h, win_ref[:, OFF_Q:OFF_Z])
    hq = _rms(lat[:, OFF_Q:OFF_Q + Q_LORA], qan_ref[...]).astype(bf16)
    hc = _rms(lat[:, OFF_CKV:OFF_CKV + KV_LORA], kvan_ref[...]).astype(bf16)
    kr = lat[:, OFF_KR:OFF_KR + HEAD_PAD]
    krf = kr * c_ref[...] + pltpu.roll(kr, half, axis=1) * s_ref[...]

    has_prev = i > 0
    has_next = i < pl.num_programs(1) - 1
    cT = cT_ref[...]
    sT = sT_ref[...]
    lane = lax.broadcasted_iota(jnp.int32, (1, HEAD_PAD), 1)
    shift_lanes = (lane == SHIFT_ROW) | (lane == SHIFT_ROW + 1)
    heads_per_chunk = PROJ_CHUNK // HEAD_PAD
    norms = [None] * N_HEADS

    def ssd_chunk(col):
        xbc = _dot(h_scr[...], win_ref[:, OFF_XBC + col:OFF_XBC + col + PROJ_CHUNK])
        before = jnp.where(has_prev, xbc[tm + F32_ROWS - 1:tm + F32_ROWS], 0.0)
        after = jnp.where(has_next, xbc[tm + F32_ROWS:tm + F32_ROWS + 1], 0.0)
        conv = _conv3_rows(xbc[:tm], before, after, cw_ref[:, col:col + PROJ_CHUNK], cb_ref[:, col:col + PROJ_CHUNK])
        act_out[:, col:col + PROJ_CHUNK] = _silu(conv).astype(bf16)

    def qk_chunk(h0):
        rows = slice(h0 * HEAD_PAD, (h0 + heads_per_chunk) * HEAD_PAD)
        qT = lax.dot_general(wqT_ref[rows, :], hq, NT_DIMS, preferred_element_type=f32)
        kn = _dot(hc, wk_ref[:, rows])
        for hh in range(heads_per_chunk):
            blk = qT[hh * HEAD_PAD:(hh + 1) * HEAD_PAD, :]
            rot = jnp.concatenate([blk[half:], blk[:half]], axis=0)
            qT_out[(h0 + hh) * HEAD_PAD:(h0 + hh + 1) * HEAD_PAD, :] = (blk * cT + rot * sT).astype(bf16)
            kb = (kn[:, hh * HEAD_PAD:(hh + 1) * HEAD_PAD] + krf).astype(bf16)
            kf = kb.astype(f32)
            n2 = jnp.max(jnp.sum(kf * kf, axis=1, keepdims=True), axis=0, keepdims=True)
            norms[h0 + hh] = jnp.broadcast_to(n2, (1, HEAD_PAD))
            k_out[h0 + hh] = jnp.where(shift_lanes, jnp.ones_like(kb), kb)

    def zv_chunk(col):
        z_out[:, col:col + PROJ_CHUNK] = _dot(h, win_ref[:, OFF_Z + col:OFF_Z + col + PROJ_CHUNK]).astype(bf16)
        vT_out[col:col + PROJ_CHUNK, :] = lax.dot_general(
            wvT_ref[col:col + PROJ_CHUNK, :], hc, NT_DIMS, preferred_element_type=f32).astype(bf16)

    light = ([functools.partial(qk_chunk, h0) for h0 in range(0, N_HEADS, heads_per_chunk)]
             + [functools.partial(zv_chunk, col) for col in range(0, D_SSM, PROJ_CHUNK)])
    heavy = [functools.partial(ssd_chunk, col) for col in range(0, D_XBC, PROJ_CHUNK)]
    per_heavy = -(-len(light) // len(heavy))
    for n, chunk in enumerate(heavy):
        chunk()
        for other in light[n * per_heavy:(n + 1) * per_heavy]:
            other()
    kn2_out[...] = jnp.concatenate(norms, axis=0)
    dtT_out[...] = lax.dot_general(wdtT_ref[...], h, NT_DIMS, preferred_element_type=f32)


def _proj_call(x, n1, win, qan, kvan, wqT, wk, wvT, wdtT, cw, cb, c_tab, s_tab, cT_tab, sT_tab):
    B, L, _ = x.shape
    tm = min(PROJ_TM, L)
    grid = (B, L // tm)
    tok = lambda w: pl.BlockSpec((None, tm, w), lambda b, i: (b, i, 0))
    tokT = lambda w: pl.BlockSpec((None, w, tm), lambda b, i: (b, 0, i))
    consts = (n1, win, qan, kvan, wqT, wk, wvT, wdtT, cw, cb)
    in_specs = [tok(D_MODEL)] + _halo_specs(tm, L, D_MODEL) + [_const_spec(a.shape) for a in consts] + [
        pl.BlockSpec((tm, HEAD_PAD), lambda b, i: (i, 0)),
        pl.BlockSpec((tm, HEAD_PAD), lambda b, i: (i, 0)),
        pl.BlockSpec((HEAD_PAD, tm), lambda b, i: (0, i)),
        pl.BlockSpec((HEAD_PAD, tm), lambda b, i: (0, i)),
    ]
    out_shape = [
        jax.ShapeDtypeStruct((B, N_HEADS * HEAD_PAD, L), bf16),
        jax.ShapeDtypeStruct((B, N_HEADS, L, HEAD_PAD), bf16),
        jax.ShapeDtypeStruct((B, N_HEADS * V_HEAD, L), bf16),
        jax.ShapeDtypeStruct((B, L, D_SSM), bf16),
        jax.ShapeDtypeStruct((B, L, D_XBC), bf16),
        jax.ShapeDtypeStruct((B, 2 * SSM_HEADS, L), f32),
        jax.ShapeDtypeStruct((B, L // tm, N_HEADS, HEAD_PAD), f32),
    ]
    k_spec = pl.BlockSpec((None, N_HEADS, tm, HEAD_PAD), lambda b, i: (b, 0, i, 0))
    out_specs = [tokT(N_HEADS * HEAD_PAD), k_spec, tokT(N_HEADS * V_HEAD),
                 tok(D_SSM), tok(D_XBC), tokT(2 * SSM_HEADS),
                 pl.BlockSpec((None, None, N_HEADS, HEAD_PAD), lambda b, i: (b, i, 0, 0))]
    return pl.pallas_call(
        _proj_kernel, grid=grid, in_specs=in_specs, out_specs=out_specs, out_shape=out_shape,
        scratch_shapes=[pltpu.VMEM((tm + 2 * F32_ROWS, D_MODEL), bf16)],
        name="proj",
        compiler_params=pltpu.CompilerParams(
            dimension_semantics=("parallel", "parallel"), vmem_limit_bytes=VMEM_LIMIT),
    )(x, x, x, *consts, c_tab, s_tab, cT_tab, sT_tab)


def _attn_kernel(kmax_ref, qT_ref, k_ref, vT_ref, o_ref, q_scr, p_scr, psum_scr, acc_ref, den_ref, dmin_ref, *, tq, tk, nq, nk, unroll, nh):
    ng = nk // unroll
    heads = range(nh)

    def q_cols(qi):
        return pl.ds(pl.multiple_of(qi * tq, tq), tq)

    def head_rows(hd, width):
        return slice(hd * width, (hd + 1) * width)

    def v_tile(hd, j):
        return vT_ref[head_rows(hd, V_HEAD), pl.ds(pl.multiple_of(j * tk, tk), tk)]

    def k_tile(hd, j):
        return k_ref[hd, pl.ds(pl.multiple_of(j * tk, tk), tk), :]

    def shift_queries(qi, qslot):
        row = lax.broadcasted_iota(jnp.int32, (HEAD_PAD, 1), 0)
        for hd in heads:
            q = qT_ref[head_rows(hd, HEAD_PAD), q_cols(qi)]
            qf = q.astype(f32)
            bound = jnp.sqrt(jnp.sum(qf * qf, axis=0, keepdims=True)) * kmax_ref[hd] * ATTN_BOUND_MARGIN
            hi = bound.astype(bf16)
            lo = (bound - hi.astype(f32)).astype(bf16)
            q_scr[qslot, hd] = jnp.where(row == SHIFT_ROW, -hi, jnp.where(row == SHIFT_ROW + 1, -lo, q))

    def produce(hd, qslot, j, slot):
        p = jnp.exp2(_dot(k_tile(hd, j), q_scr[qslot, hd]))
        p_scr[hd, slot] = p.astype(bf16)
        psum_scr[hd, slot] = jnp.sum(p, axis=0, keepdims=True)

    def group(g, qslot, next_qslot):
        acc = [acc_ref[hd] for hd in heads]
        den = [den_ref[hd] for hd in heads]
        for u in range(unroll):
            j = g * unroll + u
            nxt = (u + 1) % 2
            for hd in heads:
                if next_qslot is False or u < unroll - 1:
                    produce(hd, qslot, j + 1, nxt)
                elif next_qslot is not None:
                    produce(hd, next_qslot, 0, nxt)
            for hd in heads:
                acc[hd] = acc[hd] + _dot(v_tile(hd, j), p_scr[hd, u % 2])
                den[hd] = den[hd] + psum_scr[hd, u % 2]
        for hd in heads:
            acc_ref[hd] = acc[hd]
            den_ref[hd] = den[hd]

    def recompute_exact(hd, qi):
        q = qT_ref[head_rows(hd, HEAD_PAD), q_cols(qi)]

        def body(j, carry):
            m, den, acc = carry
            s = _dot(k_tile(hd, j), q)
            m_new = jnp.maximum(m, jnp.max(s, axis=0, keepdims=True))
            alpha = jnp.exp2(m - m_new)
            p = jnp.exp2(s - m_new)
            return (m_new, alpha * den + jnp.sum(p, axis=0, keepdims=True),
                    alpha * acc + _dot(v_tile(hd, j), p.astype(bf16)))

        init = (jnp.full((1, tq), -jnp.inf, f32), jnp.zeros((1, tq), f32), jnp.zeros((V_HEAD, tq), f32))
        _, den, acc = lax.fori_loop(0, nk, body, init)
        write_output(hd, qi, acc, den)

    def write_output(hd, qi, acc, den):
        o_ref[head_rows(hd, V_HEAD), q_cols(qi)] = (acc / den).astype(o_ref.dtype)

    def query_tile(qi, qslot, next_qslot):
        if next_qslot is not None:
            shift_queries(qi + 1, next_qslot)
        acc_ref[...] = jnp.zeros(acc_ref.shape, f32)
        den_ref[...] = jnp.zeros(den_ref.shape, f32)
        if ng > 1:
            def body(g, carry):
                group(g, qslot, False)
                return carry
            lax.fori_loop(0, ng - 1, body, 0)
        group(ng - 1, qslot, next_qslot)
        for hd in heads:
            den = den_ref[hd]
            write_output(hd, qi, acc_ref[hd], den)
            dmin_ref[hd] = jnp.minimum(dmin_ref[hd], den)

    dmin_ref[...] = jnp.full(dmin_ref.shape, jnp.inf, f32)
    shift_queries(0, 0)
    for hd in heads:
        produce(hd, 0, 0, 0)
    if nq > 1:
        def q_body(qi, carry):
            query_tile(qi, qi % 2, (qi + 1) % 2)
            return carry
        lax.fori_loop(0, nq - 1, q_body, 0)
    query_tile(nq - 1, (nq - 1) % 2, None)

    for hd in heads:
        @pl.when(jnp.logical_not(jnp.min(dmin_ref[hd]) >= ATTN_DENOM_FLOOR))
        def _(hd=hd):
            def redo(qi, carry):
                recompute_exact(hd, qi)
                return carry
            lax.fori_loop(0, nq, redo, 0)


def _attn_tiles(L):
    nh = ATTN_HEADS_PER_STEP
    tk = min(ATTN_TK, L // 2)
    kv_buffers = 1 if L // tk > 2 else 2
    kv_bytes = kv_buffers * nh * L * (HEAD_PAD + V_HEAD) * 2
    tq = L
    while nh * 2 * tk * tq * 2 > ATTN_VMEM_BUDGET - kv_bytes and tq > LANES:
        tq //= 2
    return tq, tk, kv_buffers


def _attn_call(qT, k, vT, kmax):
    B, _, L = qT.shape
    tq, tk, kv_buffers = _attn_tiles(L)
    kv_mode = pl.Buffered(kv_buffers)
    nk = L // tk
    unroll = min(ATTN_UNROLL, nk)
    assert unroll % 2 == 0 and nk % unroll == 0, "key tiles alternate between two probability buffers"
    nq = L // tq if nk == unroll else 1
    tqb = nq * tq
    nh = ATTN_HEADS_PER_STEP
    return pl.pallas_call(
        functools.partial(_attn_kernel, tq=tq, tk=tk, nq=nq, nk=nk, unroll=unroll, nh=nh),
        grid=(B, N_HEADS // nh, L // tqb),
        scratch_shapes=[pltpu.VMEM((2, nh, HEAD_PAD, tq), bf16), pltpu.VMEM((nh, 2, tk, tq), bf16),
                        pltpu.VMEM((nh, 2, 1, tq), f32), pltpu.VMEM((nh, V_HEAD, tq), f32),
                        pltpu.VMEM((nh, 1, tq), f32), pltpu.VMEM((nh, 1, tq), f32)],
        in_specs=[
            pl.BlockSpec((None, nh, 1, tq), lambda b, h, i: (b, h, 0, 0)),
            pl.BlockSpec((None, nh * HEAD_PAD, tqb), lambda b, h, i: (b, h, i)),
            pl.BlockSpec((None, nh, L, HEAD_PAD), lambda b, h, i: (b, h, 0, 0), pipeline_mode=kv_mode),
            pl.BlockSpec((None, nh * V_HEAD, L), lambda b, h, i: (b, h, 0), pipeline_mode=kv_mode),
        ],
        out_specs=pl.BlockSpec((None, nh * V_HEAD, tqb), lambda b, h, i: (b, h, i)),
        out_shape=jax.ShapeDtypeStruct((B, N_HEADS * V_HEAD, L), bf16),
        name="attn",
        compiler_params=pltpu.CompilerParams(
            dimension_semantics=("parallel", "parallel", "arbitrary"), vmem_limit_bytes=VMEM_LIMIT),
    )(kmax, qT, k, vT)


def _split3(v):
    a1 = v.astype(bf16)
    r1 = v - a1.astype(f32)
    a2 = r1.astype(bf16)
    a3 = (r1 - a2.astype(f32)).astype(bf16)
    return a1, a2, a3


def _chunk_scalars(dtT_raw, biasT, aT, triu):
    nh2 = 2 * SSM_HEADS
    dtT = jax.nn.softplus(dtT_raw + biasT)
    stepT = dtT * aT
    cs3 = _dot(jnp.concatenate(_split3(stepT), axis=0), triu)
    csT = cs3[0:nh2] + cs3[nh2:2 * nh2] + cs3[2 * nh2:3 * nh2]
    return dtT, stepT, csT


def _expand_heads(partsT, e_ref):
    n = len(partsT)
    stk = jnp.concatenate(partsT, axis=0)
    hi = stk.astype(bf16).astype(f32)
    pieces = [hi, stk - hi]
    pad = LANES - 2 * stk.shape[0]
    if pad:
        pieces.append(jnp.zeros((pad, stk.shape[1]), f32))
    nat = jnp.transpose(jnp.concatenate(pieces, axis=0)).astype(bf16)
    full = _dot(nat, e_ref[...])
    return [full[:, j * D_SSM:(j + 1) * D_SSM] for j in range(n)]


def _ssd_kernel(act_ref, dtT_ref, z_ref, biasT_ref, aT_ref, dskipT_ref, norm_ref, triu_ref, e4_ref, e2_ref,
                gT_ref, o_ref, hf_ref, hb_ref, hbs_ref, *, nblk, nsub, Q):
    ph = pl.program_id(1)
    c = pl.program_id(2)
    H = SSM_HEADS

    def group_slices(g):
        return slice(g * GROUP_WIDTH, (g + 1) * GROUP_WIDTH), slice(g * D_STATE, (g + 1) * D_STATE)

    dtT, stepT, csT = _chunk_scalars(dtT_ref[...], biasT_ref[...], aT_ref[...], triu_ref[...])
    totT = jnp.concatenate(
        [jnp.broadcast_to(csT[:, (j + 1) * Q - 1:(j + 1) * Q], (2 * H, Q)) for j in range(nsub)], axis=1)
    cbT_b = csT[H:] - stepT[H:]
    Bm = act_ref[:, D_SSM:D_SSM + GN]

    @pl.when(ph == 0)
    def _backward_states():
        @pl.when(c == 0)
        def _():
            hb_ref[...] = jnp.zeros_like(hb_ref)

        blk = nblk - 1 - c
        wb, eb = _expand_heads([jnp.exp2(cbT_b) * dtT[H:], jnp.exp2(totT[H:] - cbT_b)], e2_ref)
        xw = (act_ref[:, :D_SSM].astype(f32) * wb).astype(bf16)
        for j in reversed(range(nsub)):
            rows = slice(j * Q, (j + 1) * Q)
            hbs_ref[blk * nsub + j] = hb_ref[...].astype(bf16)
            for g in range(SSM_GROUPS):
                sl, sn = group_slices(g)
                upd = lax.dot_general(Bm[rows, sn], xw[rows, sl], TN_DIMS, preferred_element_type=f32)
                hb_ref[g] = eb[j * Q:j * Q + 1, sl] * hb_ref[g] + upd

    @pl.when(ph == 1)
    def _outputs():
        @pl.when(c == 0)
        def _():
            hf_ref[...] = jnp.zeros_like(hf_ref)

        ti = lax.broadcasted_iota(jnp.int32, (Q, Q), 0)
        si = lax.broadcasted_iota(jnp.int32, (Q, Q), 1)
        lower = ti >= si
        lane = lax.broadcasted_iota(jnp.int32, (1, LANES), 1)
        first_half = lane < SSM_HEAD_DIM
        gT = gT_ref[...]
        nw = norm_ref[...]

        xs_b = act_ref[:, :D_SSM]
        Cm = act_ref[:, D_SSM + GN:]
        xs = xs_b.astype(f32)
        csT_f = csT[:H]
        l2dt = jnp.log2(dtT)
        rowf = csT_f - l2dt[:H]
        rowb = cbT_b + l2dt[H:]
        cols = jnp.transpose(jnp.concatenate([csT_f, cbT_b], axis=0))
        cb = Cm.astype(f32) * Bm.astype(f32)
        cb_h = cb.astype(bf16)
        cb_l = (cb - cb_h.astype(f32)).astype(bf16)
        diagT = (lax.dot_general(gT, cb_h, NT_DIMS, preferred_element_type=f32)
                 + lax.dot_general(gT, cb_l, NT_DIMS, preferred_element_type=f32))
        partsT = [jnp.exp2(csT_f), jnp.exp2(totT[H:] - cbT_b), jnp.exp2(totT[:H] - csT_f) * dtT[:H],
                  dskipT_ref[...] + dtT[H:] * diagT]

        def expand_chunk(j):
            return _expand_heads([p[:, j * Q:(j + 1) * Q] for p in partsT], e4_ref)

        expanded = expand_chunk(0)
        for j in range(nsub):
            rows = slice(j * Q, (j + 1) * Q)
            chunk_id = c * nsub + j
            ef, eb, wf, coef = expanded
            if j + 1 < nsub:
                expanded = expand_chunk(j + 1)
            y_groups = []
            for g in range(SSM_GROUPS):
                sl, sn = group_slices(g)
                Cg = Cm[rows, sn]
                Bg = Bm[rows, sn]
                cbm = lax.dot_general(Cg, Bg, NT_DIMS, preferred_element_type=f32)
                y_off = (ef[:, sl] * _dot(Cg, hf_ref[g].astype(bf16))
                         + eb[:, sl] * _dot(Cg, hbs_ref[chunk_id, g]))
                pairs = []
                for hp in range(HEADS_PER_GROUP // 2):
                    lo = g * GROUP_WIDTH + hp * LANES
                    x_pair = xs_b[rows, lo:lo + LANES]
                    y_pair = None
                    for k in range(2):
                        hh = g * HEADS_PER_GROUP + 2 * hp + k
                        arg = jnp.where(lower, cols[rows, hh:hh + 1] - rowf[hh:hh + 1, rows],
                                        rowb[hh:hh + 1, rows] - cols[rows, H + hh:H + hh + 1])
                        mat = (cbm * jnp.exp2(arg)).astype(bf16)
                        keep = first_half if k == 0 else jnp.logical_not(first_half)
                        contrib = _dot(mat, jnp.where(keep, x_pair, jnp.zeros_like(x_pair)))
                        y_pair = contrib if y_pair is None else y_pair + contrib
                    pairs.append(y_pair)
                y_groups.append(jnp.concatenate(pairs, axis=1) + y_off)
                xw = (xs[rows, sl] * wf[:, sl]).astype(bf16)
                upd = lax.dot_general(Bg, xw, TN_DIMS, preferred_element_type=f32)
                hf_ref[g] = ef[Q - 1:Q, sl] * hf_ref[g] + upd

            y = jnp.concatenate(y_groups, axis=1) + xs[rows] * coef
            y = y * _silu(z_ref[rows, :].astype(f32))
            for g in range(SSM_GROUPS):
                sl, _ = group_slices(g)
                o_ref[rows, sl] = _rms(y[:, sl], nw[:, sl]).astype(o_ref.dtype)


def _ssd_call(act, dtT, z, biasT, aT, dskipT, norm, triu, e4, e2, gT):
    B, L, _ = act.shape
    blk = min(SSD_BLOCK, L)
    Q = min(SSD_Q, blk)
    nsub = blk // Q
    nblk = L // blk

    def bidx(p, c):
        return jnp.where(p == 0, nblk - 1 - c, c)

    consts = (biasT, aT, dskipT, norm, triu, e4, e2, gT)
    in_specs = [
        pl.BlockSpec((None, blk, D_XBC), lambda b, p, c: (b, bidx(p, c), 0)),
        pl.BlockSpec((None, 2 * SSM_HEADS, blk), lambda b, p, c: (b, 0, bidx(p, c))),
        pl.BlockSpec((None, blk, D_SSM), lambda b, p, c: (b, p * c, 0)),
    ] + [_const_spec(a.shape) for a in consts]
    state = (SSM_GROUPS, D_STATE, GROUP_WIDTH)
    return pl.pallas_call(
        functools.partial(_ssd_kernel, nblk=nblk, nsub=nsub, Q=Q),
        grid=(B, 2, nblk),
        in_specs=in_specs,
        out_specs=pl.BlockSpec((None, blk, D_SSM), lambda b, p, c: (b, p * c, 0)),
        out_shape=jax.ShapeDtypeStruct((B, L, D_SSM), bf16),
        scratch_shapes=[pltpu.VMEM(state, f32), pltpu.VMEM(state, f32),
                        pltpu.VMEM((nblk * nsub,) + state, bf16)],
        name="ssd",
        compiler_params=pltpu.CompilerParams(
            dimension_semantics=("parallel", "arbitrary", "arbitrary"), vmem_limit_bytes=VMEM_LIMIT),
    )(act, dtT, z, *consts)


def _oproj_kernel(x_ref, aT_ref, s_ref, wa_ref, ws_ref, o_ref):
    o_ref[...] = (x_ref[...]
                  + lax.dot_general(aT_ref[...], wa_ref[...], TN_DIMS, preferred_element_type=f32)
                  + _dot(s_ref[...], ws_ref[...]))


def _oproj_call(x, attnT, ssm, wa, ws):
    B, L, _ = x.shape
    tm = min(OPROJ_TM, L)
    tok = lambda w: pl.BlockSpec((None, tm, w), lambda b, i: (b, i, 0))
    return pl.pallas_call(
        _oproj_kernel, grid=(B, L // tm),
        in_specs=[tok(D_MODEL), pl.BlockSpec((None, N_HEADS * V_HEAD, tm), lambda b, i: (b, 0, i)),
                  tok(D_SSM), _const_spec(wa.shape), _const_spec(ws.shape)],
        out_specs=tok(D_MODEL),
        out_shape=jax.ShapeDtypeStruct((B, L, D_MODEL), f32),
        name="oproj",
        compiler_params=pltpu.CompilerParams(
            dimension_semantics=("parallel", "parallel"), vmem_limit_bytes=VMEM_LIMIT),
    )(x, attnT, ssm, wa, ws)


def _ffn_kernel(x_ref, xp_ref, xn_ref, n2_ref, wg_ref, wu_ref, cw_ref, cb_ref, wd_ref, fn_ref,
                o_ref, h_scr, act_scr, *, nfc):
    i = pl.program_id(1)
    nt = pl.num_programs(1)
    tm = x_ref.shape[0]
    n2 = n2_ref[...]
    h_scr[0:tm, :] = _rms(x_ref[...], n2).astype(bf16)
    halo = jnp.concatenate([xp_ref[...], xn_ref[...]], axis=0)
    h_scr[tm:tm + 2 * F32_ROWS, :] = _rms(halo, n2).astype(bf16)
    has_prev = i > 0
    has_next = i < nt - 1

    def chunk(cf):
        col = cf * FFN_FC
        g_ext = _dot(h_scr[...], wg_ref[:, pl.ds(col, FFN_FC)])
        before = jnp.where(has_prev, g_ext[tm + F32_ROWS - 1:tm + F32_ROWS], 0.0)
        after = jnp.where(has_next, g_ext[tm + F32_ROWS:tm + F32_ROWS + 1], 0.0)
        gc = _conv3_rows(g_ext[:tm], before, after, cw_ref[:, pl.ds(col, FFN_FC)], cb_ref[:, pl.ds(col, FFN_FC)])
        u = _dot(h_scr[0:tm, :], wu_ref[:, pl.ds(col, FFN_FC)])
        act_scr[:, pl.ds(col, FFN_FC)] = (_silu(gc) * u).astype(bf16)

    for cf in range(nfc):
        chunk(cf)
    y = x_ref[...] + _dot(act_scr[...], wd_ref[...])
    o_ref[...] = _rms(y, fn_ref[...])


def _ffn_call(x, n2, wg, wu, cw, cb, wd, fn):
    B, L, _ = x.shape
    tm = min(FFN_TM, L)
    tok = pl.BlockSpec((None, tm, D_MODEL), lambda b, i: (b, i, 0))
    consts = (n2, wg, wu, cw, cb, wd, fn)
    return pl.pallas_call(
        functools.partial(_ffn_kernel, nfc=D_FF // FFN_FC),
        grid=(B, L // tm),
        in_specs=[tok] + _halo_specs(tm, L, D_MODEL) + [_const_spec(a.shape) for a in consts],
        out_specs=tok,
        out_shape=jax.ShapeDtypeStruct((B, L, D_MODEL), f32),
        scratch_shapes=[pltpu.VMEM((tm + 2 * F32_ROWS, D_MODEL), bf16), pltpu.VMEM((tm, D_FF), bf16)],
        name="ffn",
        compiler_params=pltpu.CompilerParams(
            dimension_semantics=("parallel", "parallel"), vmem_limit_bytes=VMEM_LIMIT),
    )(x, x, x, *consts)


def _head_lane_sources():
    src = np.full((HEAD_PAD,), QK_DIM, np.int32)
    half = HEAD_PAD // 2
    src[0:HALF_ROPE] = QK_NOPE + np.arange(HALF_ROPE)
    src[HALF_ROPE:half] = np.arange(half - HALF_ROPE)
    src[half:half + HALF_ROPE] = QK_NOPE + HALF_ROPE + np.arange(HALF_ROPE)
    n_rest = QK_NOPE - (half - HALF_ROPE)
    src[half + HALF_ROPE:half + HALF_ROPE + n_rest] = (half - HALF_ROPE) + np.arange(n_rest)
    return src


def _rope_tables(L):
    inv = ROPE_THETA ** (-jnp.arange(0, QK_ROPE, 2, dtype=f32) / QK_ROPE)
    ang = jnp.arange(L, dtype=f32)[:, None] * inv[None, :]
    cos, sin = jnp.cos(ang), jnp.sin(ang)
    half = HEAD_PAD // 2
    c_tab = jnp.ones((L, HEAD_PAD), f32)
    c_tab = c_tab.at[:, 0:HALF_ROPE].set(cos).at[:, half:half + HALF_ROPE].set(cos)
    s_tab = jnp.zeros((L, HEAD_PAD), f32)
    s_tab = s_tab.at[:, 0:HALF_ROPE].set(-sin).at[:, half:half + HALF_ROPE].set(sin)
    return c_tab, s_tab


def _expand_matrix(n):
    m = np.zeros((LANES, n * D_SSM), np.float32)
    for part in range(2):
        for j in range(n):
            for h in range(SSM_HEADS):
                r = part * n * SSM_HEADS + j * SSM_HEADS + h
                m[r, j * D_SSM + h * SSM_HEAD_DIM:j * D_SSM + (h + 1) * SSM_HEAD_DIM] = 1.0
    return jnp.asarray(m, bf16)


def _prepare_weights(norm1, w_in, q_a_norm, kv_a_norm, w_q_b, w_kv_b, conv_w, conv_b,
                     dt_bias_f, dt_bias_b, a_log_f, a_log_b, d_skip, ssm_norm, w_out,
                     norm2, w_gate, w_up, ffn_conv_w, ffn_conv_b, w_down, final_norm):
    half = HEAD_PAD // 2
    o_kr = Q_LORA + KV_LORA
    o_z = o_kr + QK_ROPE
    o_dt = o_z + D_SSM + D_XBC
    kr_blk = jnp.zeros((D_MODEL, HEAD_PAD), f32)
    kr_blk = kr_blk.at[:, 0:HALF_ROPE].set(w_in[:, o_kr:o_kr + HALF_ROPE])
    kr_blk = kr_blk.at[:, half:half + HALF_ROPE].set(w_in[:, o_kr + HALF_ROPE:o_kr + QK_ROPE])
    win = jnp.concatenate([w_in[:, :o_kr], kr_blk, w_in[:, o_z:o_dt]], axis=1).astype(bf16)
    w_dt = w_in[:, o_dt:o_dt + 2 * SSM_HEADS]

    src = _head_lane_sources()
    wq = w_q_b.reshape(Q_LORA, N_HEADS, QK_DIM)
    wq = jnp.concatenate([wq, jnp.zeros((Q_LORA, N_HEADS, 1), f32)], axis=-1)[:, :, src]
    wqT = wq.reshape(Q_LORA, N_HEADS * HEAD_PAD).T.astype(bf16)
    wkv = w_kv_b.reshape(KV_LORA, N_HEADS, QK_NOPE + V_HEAD)
    src_k = np.where(src < QK_NOPE, src, QK_NOPE)
    wk = jnp.concatenate([wkv[:, :, :QK_NOPE], jnp.zeros((KV_LORA, N_HEADS, 1), f32)], axis=-1)[:, :, src_k]
    wk = wk.reshape(KV_LORA, N_HEADS * HEAD_PAD).astype(bf16)
    wvT = wkv[:, :, QK_NOPE:].reshape(KV_LORA, N_HEADS * V_HEAD).T.astype(bf16)

    row = lambda v: v.reshape(1, -1).astype(f32)
    col = lambda v: v.reshape(-1, 1).astype(f32)
    a_neg = -jnp.exp(jnp.concatenate([a_log_f, a_log_b]).astype(f32)) * LOG2E
    group_of_lane = np.arange(GN) // D_STATE
    group_of_head = np.arange(SSM_HEADS) // HEADS_PER_GROUP
    gT = jnp.asarray(group_of_head[:, None] == group_of_lane[None, :], bf16)
    return dict(
        n1=row(norm1), win=win, qan=row(q_a_norm), kvan=row(kv_a_norm), wqT=wqT, wk=wk, wvT=wvT,
        wdtT=w_dt.T.astype(bf16), cw=conv_w.astype(f32), cb=row(conv_b),
        biasT=col(jnp.concatenate([dt_bias_f, dt_bias_b])), aT=col(a_neg), dskipT=col(d_skip),
        ssm_norm=row(ssm_norm), e4=_expand_matrix(4), e2=_expand_matrix(2), gT=gT,
        wa=w_out[:N_HEADS * V_HEAD].astype(bf16), ws=w_out[N_HEADS * V_HEAD:].astype(bf16),
        n2=row(norm2), wg=w_gate.astype(bf16), wu=w_up.astype(bf16), fcw=ffn_conv_w.astype(f32),
        fcb=row(ffn_conv_b), wd=w_down.astype(bf16), fn=row(final_norm),
    )


def _encoder(x, w):
    B, L, _ = x.shape
    c_tab, s_tab = _rope_tables(L)
    scale = QK_DIM ** -0.5 * LOG2E
    qT, k, vT, z, act, dtT, kn2 = _proj_call(
        x, w["n1"], w["win"], w["qan"], w["kvan"], w["wqT"], w["wk"], w["wvT"], w["wdtT"], w["cw"], w["cb"],
        c_tab, s_tab, (c_tab * scale).T, (s_tab * scale).T)
    kmax = jnp.sqrt(jnp.max(kn2, axis=1))[:, :, :1]
    kmax = jnp.broadcast_to(kmax[..., None], (B, N_HEADS, 1, _attn_tiles(L)[0]))
    attnT = _attn_call(qT, k, vT, kmax)
    blk = min(SSD_BLOCK, L)
    Q = min(SSD_Q, blk)
    triu = jnp.kron(jnp.eye(blk // Q, dtype=f32), jnp.triu(jnp.ones((Q, Q), f32))).astype(bf16)
    ssm = _ssd_call(act, dtT, z, w["biasT"], w["aT"], w["dskipT"], w["ssm_norm"], triu, w["e4"], w["e2"], w["gT"])
    x1 = _oproj_call(x, attnT, ssm, w["wa"], w["ws"])
    return _ffn_call(x1, w["n2"], w["wg"], w["wu"], w["fcw"], w["fcb"], w["wd"], w["fn"])


def kernel(x_prompt, x_sample, norm1, w_in, q_a_norm, kv_a_norm, w_q_b, w_kv_b, conv_w, conv_b,
           dt_bias_f, dt_bias_b, a_log_f, a_log_b, d_skip, ssm_norm, w_out, norm2, w_gate, w_up,
           ffn_conv_w, ffn_conv_b, w_down, final_norm):
    w = _prepare_weights(norm1[0], w_in[0], q_a_norm[0], kv_a_norm[0], w_q_b[0], w_kv_b[0], conv_w[0],
                         conv_b[0], dt_bias_f[0], dt_bias_b[0], a_log_f[0], a_log_b[0], d_skip[0],
                         ssm_norm[0], w_out[0], norm2[0], w_gate[0], w_up[0], ffn_conv_w[0],
                         ffn_conv_b[0], w_down[0], final_norm)
    return (_encoder(x_prompt, w), _encoder(x_sample, w))
```

```python
import functools

import numpy as np
import jax
import jax.numpy as jnp
from jax import lax
from jax.experimental import pallas as pl
from jax.experimental.pallas import tpu as pltpu

D_MODEL = 1024
N_HEADS = 16
QK_NOPE = 64
QK_ROPE = 32
HALF_ROPE = QK_ROPE // 2
QK_DIM = QK_NOPE + QK_ROPE
V_HEAD = 64
Q_LORA = 384
KV_LORA = 256
ROPE_THETA = 10000.0
SSM_HEADS = 16
SSM_HEAD_DIM = 64
D_SSM = SSM_HEADS * SSM_HEAD_DIM
SSM_GROUPS = 2
HEADS_PER_GROUP = SSM_HEADS // SSM_GROUPS
GROUP_WIDTH = D_SSM // SSM_GROUPS
D_STATE = 64
GN = SSM_GROUPS * D_STATE
D_XBC = D_SSM + 2 * GN
D_FF = 2816
EPS = 1e-6
LOG2E = float(np.log2(np.e))

LANES = 128
HEAD_PAD = LANES
F32_ROWS = 8
VMEM_LIMIT = 56 * 1024 * 1024

OFF_Q = 0
OFF_CKV = OFF_Q + Q_LORA
OFF_KR = OFF_CKV + KV_LORA
OFF_Z = OFF_KR + HEAD_PAD
OFF_XBC = OFF_Z + D_SSM

PROJ_TM = 1024
PROJ_CHUNK = 256
ATTN_VMEM_BUDGET = 42 * 1024 * 1024
ATTN_TK = 2048
ATTN_TQ = 1024
ATTN_UNROLL = 4
ATTN_HEADS_PER_STEP = 2
SHIFT_ROW = QK_DIM
ATTN_BOUND_MARGIN = 1.0 + 2.0 ** -6
ATTN_DENOM_FLOOR = 2.0 ** -80
SSD_Q = 128
SSD_BLOCK = 512
OPROJ_TM = 1024
FFN_TM = 1024
FFN_FC = 256

NT_DIMS = (((1,), (1,)), ((), ()))
TN_DIMS = (((0,), (0,)), ((), ()))

f32 = jnp.float32
bf16 = jnp.bfloat16


def _rms(x, w):
    return x * lax.rsqrt(jnp.mean(x * x, axis=-1, keepdims=True) + EPS) * w


def _dot(a, b):
    return jnp.dot(a, b, preferred_element_type=f32)


def _silu(x):
    h = 0.5 * x
    return h * jnp.tanh(h) + h


def _conv3_rows(x, before, after, cw, cb):
    n = x.shape[0]
    sub = lax.broadcasted_iota(jnp.int32, (F32_ROWS, 1), 0)
    down = pltpu.roll(x, 1, axis=0)
    up = pltpu.roll(x, n - 1, axis=0)
    x_prev = jnp.concatenate([jnp.where(sub == 0, before, down[:F32_ROWS]), down[F32_ROWS:]], axis=0)
    x_next = jnp.concatenate([up[:n - F32_ROWS],
                              jnp.where(sub == F32_ROWS - 1, after, up[n - F32_ROWS:])], axis=0)
    return x_prev * cw[0:1] + x * cw[1:2] + x_next * cw[2:3] + cb


def _halo_specs(tm, L, width):
    hb = tm // F32_ROWS
    last = L // F32_ROWS - 1
    return [pl.BlockSpec((None, F32_ROWS, width), lambda b, i: (b, jnp.maximum(i * hb - 1, 0), 0)),
            pl.BlockSpec((None, F32_ROWS, width), lambda b, i: (b, jnp.minimum((i + 1) * hb, last), 0))]


def _const_spec(shape):
    nd = len(shape)
    return pl.BlockSpec(shape, lambda *_: (0,) * nd, pipeline_mode=pl.Buffered(1))


def _proj_kernel(x_ref, xp_ref, xn_ref, n1_ref, win_ref, qan_ref, kvan_ref, wqT_ref, wk_ref, wvT_ref, wdtT_ref,
                 cw_ref, cb_ref, c_ref, s_ref, cT_ref, sT_ref,
                 qT_out, k_out, vT_out, z_out, act_out, dtT_out, kn2_out, h_scr):
    i = pl.program_id(1)
    tm = x_ref.shape[0]
    n1 = n1_ref[...]
    h_scr[0:tm, :] = _rms(x_ref[...], n1).astype(bf16)
    halo = jnp.concatenate([xp_ref[...], xn_ref[...]], axis=0)
    h_scr[tm:tm + 2 * F32_ROWS, :] = _rms(halo, n1).astype(bf16)
    h = h_scr[0:tm, :]
    half = HEAD_PAD // 2
    lat = _dot(h, win_ref[:, OFF_Q:OFF_Z])
    hq = _rms(lat[:, OFF_Q:OFF_Q + Q_LORA], qan_ref[...]).astype(bf16)
    hc = _rms(lat[:, OFF_CKV:OFF_CKV + KV_LORA], kvan_ref[...]).astype(bf16)
    kr = lat[:, OFF_KR:OFF_KR + HEAD_PAD]
    krf = kr * c_ref[...] + pltpu.roll(kr, half, axis=1) * s_ref[...]

    has_prev = i > 0
    has_next = i < pl.num_programs(1) - 1
    cT = cT_ref[...]
    sT = sT_ref[...]
    lane = lax.broadcasted_iota(jnp.int32, (1, HEAD_PAD), 1)
    shift_lanes = (lane == SHIFT_ROW) | (lane == SHIFT_ROW + 1)
    heads_per_chunk = PROJ_CHUNK // HEAD_PAD
    norms = [None] * N_HEADS

    def ssd_chunk(col):
        xbc = _dot(h_scr[...], win_ref[:, OFF_XBC + col:OFF_XBC + col + PROJ_CHUNK])
        before = jnp.where(has_prev, xbc[tm + F32_ROWS - 1:tm + F32_ROWS], 0.0)
        after = jnp.where(has_next, xbc[tm + F32_ROWS:tm + F32_ROWS + 1], 0.0)
        conv = _conv3_rows(xbc[:tm], before, after, cw_ref[:, col:col + PROJ_CHUNK], cb_ref[:, col:col + PROJ_CHUNK])
        act_out[:, col:col + PROJ_CHUNK] = _silu(conv).astype(bf16)

    def qk_chunk(h0):
        rows = slice(h0 * HEAD_PAD, (h0 + heads_per_chunk) * HEAD_PAD)
        qT = lax.dot_general(wqT_ref[rows, :], hq, NT_DIMS, preferred_element_type=f32)
        kn = _dot(hc, wk_ref[:, rows])
        for hh in range(heads_per_chunk):
            blk = qT[hh * HEAD_PAD:(hh + 1) * HEAD_PAD, :]
            rot = jnp.concatenate([blk[half:], blk[:half]], axis=0)
            qT_out[(h0 + hh) * HEAD_PAD:(h0 + hh + 1) * HEAD_PAD, :] = (blk * cT + rot * sT).astype(bf16)
            kb = (kn[:, hh * HEAD_PAD:(hh + 1) * HEAD_PAD] + krf).astype(bf16)
            kf = kb.astype(f32)
            n2 = jnp.max(jnp.sum(kf * kf, axis=1, keepdims=True), axis=0, keepdims=True)
            norms[h0 + hh] = jnp.broadcast_to(n2, (1, HEAD_PAD))
            k_out[h0 + hh] = jnp.where(shift_lanes, jnp.ones_like(kb), kb)

    def zv_chunk(col):
        z_out[:, col:col + PROJ_CHUNK] = _dot(h, win_ref[:, OFF_Z + col:OFF_Z + col + PROJ_CHUNK]).astype(bf16)
        vT_out[col:col + PROJ_CHUNK, :] = lax.dot_general(
            wvT_ref[col:col + PROJ_CHUNK, :], hc, NT_DIMS, preferred_element_type=f32).astype(bf16)

    light = ([functools.partial(qk_chunk, h0) for h0 in range(0, N_HEADS, heads_per_chunk)]
             + [functools.partial(zv_chunk, col) for col in range(0, D_SSM, PROJ_CHUNK)])
    heavy = [functools.partial(ssd_chunk, col) for col in range(0, D_XBC, PROJ_CHUNK)]
    per_heavy = -(-len(light) // len(heavy))
    for n, chunk in enumerate(heavy):
        chunk()
        for other in light[n * per_heavy:(n + 1) * per_heavy]:
            other()
    kn2_out[...] = jnp.concatenate(norms, axis=0)
    dtT_out[...] = lax.dot_general(wdtT_ref[...], h, NT_DIMS, preferred_element_type=f32)


def _proj_call(x, n1, win, qan, kvan, wqT, wk, wvT, wdtT, cw, cb, c_tab, s_tab, cT_tab, sT_tab):
    B, L, _ = x.shape
    tm = min(PROJ_TM, L)
    grid = (B, L // tm)
    tok = lambda w: pl.BlockSpec((None, tm, w), lambda b, i: (b, i, 0))
    tokT = lambda w: pl.BlockSpec((None, w, tm), lambda b, i: (b, 0, i))
    consts = (n1, win, qan, kvan, wqT, wk, wvT, wdtT, cw, cb)
    in_specs = [tok(D_MODEL)] + _halo_specs(tm, L, D_MODEL) + [_const_spec(a.shape) for a in consts] + [
        pl.BlockSpec((tm, HEAD_PAD), lambda b, i: (i, 0)),
        pl.BlockSpec((tm, HEAD_PAD), lambda b, i: (i, 0)),
        pl.BlockSpec((HEAD_PAD, tm), lambda b, i: (0, i)),
        pl.BlockSpec((HEAD_PAD, tm), lambda b, i: (0, i)),
    ]
    out_shape = [
        jax.ShapeDtypeStruct((B, N_HEADS * HEAD_PAD, L), bf16),
        jax.ShapeDtypeStruct((B, N_HEADS, L, HEAD_PAD), bf16),
        jax.ShapeDtypeStruct((B, N_HEADS * V_HEAD, L), bf16),
        jax.ShapeDtypeStruct((B, L, D_SSM), bf16),
        jax.ShapeDtypeStruct((B, L, D_XBC), bf16),
        jax.ShapeDtypeStruct((B, 2 * SSM_HEADS, L), f32),
        jax.ShapeDtypeStruct((B, L // tm, N_HEADS, HEAD_PAD), f32),
    ]
    k_spec = pl.BlockSpec((None, N_HEADS, tm, HEAD_PAD), lambda b, i: (b, 0, i, 0))
    out_specs = [tokT(N_HEADS * HEAD_PAD), k_spec, tokT(N_HEADS * V_HEAD),
                 tok(D_SSM), tok(D_XBC), tokT(2 * SSM_HEADS),
                 pl.BlockSpec((None, None, N_HEADS, HEAD_PAD), lambda b, i: (b, i, 0, 0))]
    return pl.pallas_call(
        _proj_kernel, grid=grid, in_specs=in_specs, out_specs=out_specs, out_shape=out_shape,
        scratch_shapes=[pltpu.VMEM((tm + 2 * F32_ROWS, D_MODEL), bf16)],
        name="proj",
        compiler_params=pltpu.CompilerParams(
            dimension_semantics=("parallel", "parallel"), vmem_limit_bytes=VMEM_LIMIT),
    )(x, x, x, *consts, c_tab, s_tab, cT_tab, sT_tab)


def _attn_kernel(kmax_ref, qT_ref, k_ref, vT_ref, o_ref, q_scr, p_scr, psum_scr, acc_ref, den_ref, dmin_ref, *, tq, tk, nq, nk, unroll, nh):
    ng = nk // unroll
    heads = range(nh)

    def q_cols(qi):
        return pl.ds(pl.multiple_of(qi * tq, tq), tq)

    def head_rows(hd, width):
        return slice(hd * width, (hd + 1) * width)

    def v_tile(hd, j):
        return vT_ref[head_rows(hd, V_HEAD), pl.ds(pl.multiple_of(j * tk, tk), tk)]

    def k_tile(hd, j):
        return k_ref[hd, pl.ds(pl.multiple_of(j * tk, tk), tk), :]

    def shift_queries(qi, qslot):
        row = lax.broadcasted_iota(jnp.int32, (HEAD_PAD, 1), 0)
        for hd in heads:
            q = qT_ref[head_rows(hd, HEAD_PAD), q_cols(qi)]
            qf = q.astype(f32)
            bound = jnp.sqrt(jnp.sum(qf * qf, axis=0, keepdims=True)) * kmax_ref[hd] * ATTN_BOUND_MARGIN
            hi = bound.astype(bf16)
            lo = (bound - hi.astype(f32)).astype(bf16)
            q_scr[qslot, hd] = jnp.where(row == SHIFT_ROW, -hi, jnp.where(row == SHIFT_ROW + 1, -lo, q))

    def produce(hd, qslot, j, slot):
        p = jnp.exp2(_dot(k_tile(hd, j), q_scr[qslot, hd]))
        p_scr[hd, slot] = p.astype(bf16)
        psum_scr[hd, slot] = jnp.sum(p, axis=0, keepdims=True)

    def group(g, qslot, next_qslot):
        acc = [acc_ref[hd] for hd in heads]
        den = [den_ref[hd] for hd in heads]
        for u in range(unroll):
            j = g * unroll + u
            nxt = (u + 1) % 2
            for hd in heads:
                if next_qslot is False or u < unroll - 1:
                    produce(hd, qslot, j + 1, nxt)
                elif next_qslot is not None:
                    produce(hd, next_qslot, 0, nxt)
            for hd in heads:
                acc[hd] = acc[hd] + _dot(v_tile(hd, j), p_scr[hd, u % 2])
                den[hd] = den[hd] + psum_scr[hd, u % 2]
        for hd in heads:
            acc_ref[hd] = acc[hd]
            den_ref[hd] = den[hd]

    def recompute_exact(hd, qi):
        q = qT_ref[head_rows(hd, HEAD_PAD), q_cols(qi)]

        def body(j, carry):
            m, den, acc = carry
            s = _dot(k_tile(hd, j), q)
            m_new = jnp.maximum(m, jnp.max(s, axis=0, keepdims=True))
            alpha = jnp.exp2(m - m_new)
            p = jnp.exp2(s - m_new)
            return (m_new, alpha * den + jnp.sum(p, axis=0, keepdims=True),
                    alpha * acc + _dot(v_tile(hd, j), p.astype(bf16)))

        init = (jnp.full((1, tq), -jnp.inf, f32), jnp.zeros((1, tq), f32), jnp.zeros((V_HEAD, tq), f32))
        _, den, acc = lax.fori_loop(0, nk, body, init)
        write_output(hd, qi, acc, den)

    def write_output(hd, qi, acc, den):
        o_ref[head_rows(hd, V_HEAD), q_cols(qi)] = (acc / den).astype(o_ref.dtype)

    def query_tile(qi, qslot, next_qslot):
        if next_qslot is not None:
            shift_queries(qi + 1, next_qslot)
        acc_ref[...] = jnp.zeros(acc_ref.shape, f32)
        den_ref[...] = jnp.zeros(den_ref.shape, f32)
        if ng > 1:
            def body(g, carry):
                group(g, qslot, False)
                return carry
            lax.fori_loop(0, ng - 1, body, 0)
        group(ng - 1, qslot, next_qslot)
        for hd in heads:
            den = den_ref[hd]
            write_output(hd, qi, acc_ref[hd], den)
            dmin_ref[hd] = jnp.minimum(dmin_ref[hd], den)

    dmin_ref[...] = jnp.full(dmin_ref.shape, jnp.inf, f32)
    shift_queries(0, 0)
    for hd in heads:
        produce(hd, 0, 0, 0)
    if nq > 1:
        def q_body(qi, carry):
            query_tile(qi, qi % 2, (qi + 1) % 2)
            return carry
        lax.fori_loop(0, nq - 1, q_body, 0)
    query_tile(nq - 1, (nq - 1) % 2, None)

    for hd in heads:
        @pl.when(jnp.logical_not(jnp.min(dmin_ref[hd]) >= ATTN_DENOM_FLOOR))
        def _(hd=hd):
            def redo(qi, carry):
                recompute_exact(hd, qi)
                return carry
            lax.fori_loop(0, nq, redo, 0)


def _attn_tiles(L):
    nh = ATTN_HEADS_PER_STEP
    tk = min(ATTN_TK, L // 2)
    kv_buffers = 1 if L // tk > 2 else 2
    kv_bytes = kv_buffers * nh * L * (HEAD_PAD + V_HEAD) * 2
    tq = min(L, ATTN_TQ)
    while nh * 2 * tk * tq * 2 > ATTN_VMEM_BUDGET - kv_bytes and tq > LANES:
        tq //= 2
    return tq, tk, kv_buffers


def _attn_call(qT, k, vT, kmax):
    B, _, L = qT.shape
    tq, tk, kv_buffers = _attn_tiles(L)
    kv_mode = pl.Buffered(kv_buffers)
    nk = L // tk
    unroll = min(ATTN_UNROLL, nk)
    assert unroll % 2 == 0 and nk % unroll == 0, "key tiles alternate between two probability buffers"
    nq = L // tq if nk == unroll else 1
    tqb = nq * tq
    nh = ATTN_HEADS_PER_STEP
    return pl.pallas_call(
        functools.partial(_attn_kernel, tq=tq, tk=tk, nq=nq, nk=nk, unroll=unroll, nh=nh),
        grid=(B, N_HEADS // nh, L // tqb),
        scratch_shapes=[pltpu.VMEM((2, nh, HEAD_PAD, tq), bf16), pltpu.VMEM((nh, 2, tk, tq), bf16),
                        pltpu.VMEM((nh, 2, 1, tq), f32), pltpu.VMEM((nh, V_HEAD, tq), f32),
                        pltpu.VMEM((nh, 1, tq), f32), pltpu.VMEM((nh, 1, tq), f32)],
        in_specs=[
            pl.BlockSpec((None, nh, 1, tq), lambda b, h, i: (b, h, 0, 0)),
            pl.BlockSpec((None, nh * HEAD_PAD, tqb), lambda b, h, i: (b, h, i)),
            pl.BlockSpec((None, nh, L, HEAD_PAD), lambda b, h, i: (b, h, 0, 0), pipeline_mode=kv_mode),
            pl.BlockSpec((None, nh * V_HEAD, L), lambda b, h, i: (b, h, 0), pipeline_mode=kv_mode),
        ],
        out_specs=pl.BlockSpec((None, nh * V_HEAD, tqb), lambda b, h, i: (b, h, i)),
        out_shape=jax.ShapeDtypeStruct((B, N_HEADS * V_HEAD, L), bf16),
        name="attn",
        compiler_params=pltpu.CompilerParams(
            dimension_semantics=("parallel", "parallel", "arbitrary"), vmem_limit_bytes=VMEM_LIMIT),
    )(kmax, qT, k, vT)


def _split3(v):
    a1 = v.astype(bf16)
    r1 = v - a1.astype(f32)
    a2 = r1.astype(bf16)
    a3 = (r1 - a2.astype(f32)).astype(bf16)
    return a1, a2, a3


def _chunk_scalars(dtT_raw, biasT, aT, triu):
    nh2 = 2 * SSM_HEADS
    dtT = jax.nn.softplus(dtT_raw + biasT)
    stepT = dtT * aT
    cs3 = _dot(jnp.concatenate(_split3(stepT), axis=0), triu)
    csT = cs3[0:nh2] + cs3[nh2:2 * nh2] + cs3[2 * nh2:3 * nh2]
    return dtT, stepT, csT


def _expand_heads(partsT, e_ref):
    n = len(partsT)
    stk = jnp.concatenate(partsT, axis=0)
    hi = stk.astype(bf16).astype(f32)
    pieces = [hi, stk - hi]
    pad = LANES - 2 * stk.shape[0]
    if pad:
        pieces.append(jnp.zeros((pad, stk.shape[1]), f32))
    nat = jnp.transpose(jnp.concatenate(pieces, axis=0)).astype(bf16)
    full = _dot(nat, e_ref[...])
    return [full[:, j * D_SSM:(j + 1) * D_SSM] for j in range(n)]


def _ssd_kernel(act_ref, dtT_ref, z_ref, biasT_ref, aT_ref, dskipT_ref, norm_ref, triu_ref, e4_ref, e2_ref,
                gT_ref, o_ref, hf_ref, hb_ref, hbs_ref, *, nblk, nsub, Q):
    ph = pl.program_id(1)
    c = pl.program_id(2)
    H = SSM_HEADS

    def group_slices(g):
        return slice(g * GROUP_WIDTH, (g + 1) * GROUP_WIDTH), slice(g * D_STATE, (g + 1) * D_STATE)

    dtT, stepT, csT = _chunk_scalars(dtT_ref[...], biasT_ref[...], aT_ref[...], triu_ref[...])
    totT = jnp.concatenate(
        [jnp.broadcast_to(csT[:, (j + 1) * Q - 1:(j + 1) * Q], (2 * H, Q)) for j in range(nsub)], axis=1)
    cbT_b = csT[H:] - stepT[H:]
    Bm = act_ref[:, D_SSM:D_SSM + GN]

    @pl.when(ph == 0)
    def _backward_states():
        @pl.when(c == 0)
        def _():
            hb_ref[...] = jnp.zeros_like(hb_ref)

        blk = nblk - 1 - c
        wb, eb = _expand_heads([jnp.exp2(cbT_b) * dtT[H:], jnp.exp2(totT[H:] - cbT_b)], e2_ref)
        xw = (act_ref[:, :D_SSM].astype(f32) * wb).astype(bf16)
        for j in reversed(range(nsub)):
            rows = slice(j * Q, (j + 1) * Q)
            hbs_ref[blk * nsub + j] = hb_ref[...].astype(bf16)
            for g in range(SSM_GROUPS):
                sl, sn = group_slices(g)
                upd = lax.dot_general(Bm[rows, sn], xw[rows, sl], TN_DIMS, preferred_element_type=f32)
                hb_ref[g] = eb[j * Q:j * Q + 1, sl] * hb_ref[g] + upd

    @pl.when(ph == 1)
    def _outputs():
        @pl.when(c == 0)
        def _():
            hf_ref[...] = jnp.zeros_like(hf_ref)

        ti = lax.broadcasted_iota(jnp.int32, (Q, Q), 0)
        si = lax.broadcasted_iota(jnp.int32, (Q, Q), 1)
        lower = ti >= si
        lane = lax.broadcasted_iota(jnp.int32, (1, LANES), 1)
        first_half = lane < SSM_HEAD_DIM
        gT = gT_ref[...]
        nw = norm_ref[...]

        xs_b = act_ref[:, :D_SSM]
        Cm = act_ref[:, D_SSM + GN:]
        xs = xs_b.astype(f32)
        csT_f = csT[:H]
        l2dt = jnp.log2(dtT)
        rowf = csT_f - l2dt[:H]
        rowb = cbT_b + l2dt[H:]
        cols = jnp.transpose(jnp.concatenate([csT_f, cbT_b], axis=0))
        cb = Cm.astype(f32) * Bm.astype(f32)
        cb_h = cb.astype(bf16)
        cb_l = (cb - cb_h.astype(f32)).astype(bf16)
        diagT = (lax.dot_general(gT, cb_h, NT_DIMS, preferred_element_type=f32)
                 + lax.dot_general(gT, cb_l, NT_DIMS, preferred_element_type=f32))
        partsT = [jnp.exp2(csT_f), jnp.exp2(totT[H:] - cbT_b), jnp.exp2(totT[:H] - csT_f) * dtT[:H],
                  dskipT_ref[...] + dtT[H:] * diagT]

        def expand_chunk(j):
            return _expand_heads([p[:, j * Q:(j + 1) * Q] for p in partsT], e4_ref)

        expanded = expand_chunk(0)
        for j in range(nsub):
            rows = slice(j * Q, (j + 1) * Q)
            chunk_id = c * nsub + j
            ef, eb, wf, coef = expanded
            if j + 1 < nsub:
                expanded = expand_chunk(j + 1)
            y_groups = []
            for g in range(SSM_GROUPS):
                sl, sn = group_slices(g)
                Cg = Cm[rows, sn]
                Bg = Bm[rows, sn]
                cbm = lax.dot_general(Cg, Bg, NT_DIMS, preferred_element_type=f32)
                y_off = (ef[:, sl] * _dot(Cg, hf_ref[g].astype(bf16))
                         + eb[:, sl] * _dot(Cg, hbs_ref[chunk_id, g]))
                pairs = []
                for hp in range(HEADS_PER_GROUP // 2):
                    lo = g * GROUP_WIDTH + hp * LANES
                    x_pair = xs_b[rows, lo:lo + LANES]
                    y_pair = None
                    for k in range(2):
                        hh = g * HEADS_PER_GROUP + 2 * hp + k
                        arg = jnp.where(lower, cols[rows, hh:hh + 1] - rowf[hh:hh + 1, rows],
                                        rowb[hh:hh + 1, rows] - cols[rows, H + hh:H + hh + 1])
                        mat = (cbm * jnp.exp2(arg)).astype(bf16)
                        keep = first_half if k == 0 else jnp.logical_not(first_half)
                        contrib = _dot(mat, jnp.where(keep, x_pair, jnp.zeros_like(x_pair)))
                        y_pair = contrib if y_pair is None else y_pair + contrib
                    pairs.append(y_pair)
                y_groups.append(jnp.concatenate(pairs, axis=1) + y_off)
                xw = (xs[rows, sl] * wf[:, sl]).astype(bf16)
                upd = lax.dot_general(Bg, xw, TN_DIMS, preferred_element_type=f32)
                hf_ref[g] = ef[Q - 1:Q, sl] * hf_ref[g] + upd

            y = jnp.concatenate(y_groups, axis=1) + xs[rows] * coef
            y = y * _silu(z_ref[rows, :].astype(f32))
            for g in range(SSM_GROUPS):
                sl, _ = group_slices(g)
                o_ref[rows, sl] = _rms(y[:, sl], nw[:, sl]).astype(o_ref.dtype)


def _ssd_call(act, dtT, z, biasT, aT, dskipT, norm, triu, e4, e2, gT):
    B, L, _ = act.shape
    blk = min(SSD_BLOCK, L)
    Q = min(SSD_Q, blk)
    nsub = blk // Q
    nblk = L // blk

    def bidx(p, c):
        return jnp.where(p == 0, nblk - 1 - c, c)

    consts = (biasT, aT, dskipT, norm, triu, e4, e2, gT)
    in_specs = [
        pl.BlockSpec((None, blk, D_XBC), lambda b, p, c: (b, bidx(p, c), 0)),
        pl.BlockSpec((None, 2 * SSM_HEADS, blk), lambda b, p, c: (b, 0, bidx(p, c))),
        pl.BlockSpec((None, blk, D_SSM), lambda b, p, c: (b, p * c, 0)),
    ] + [_const_spec(a.shape) for a in consts]
    state = (SSM_GROUPS, D_STATE, GROUP_WIDTH)
    return pl.pallas_call(
        functools.partial(_ssd_kernel, nblk=nblk, nsub=nsub, Q=Q),
        grid=(B, 2, nblk),
        in_specs=in_specs,
        out_specs=pl.BlockSpec((None, blk, D_SSM), lambda b, p, c: (b, p * c, 0)),
        out_shape=jax.ShapeDtypeStruct((B, L, D_SSM), bf16),
        scratch_shapes=[pltpu.VMEM(state, f32), pltpu.VMEM(state, f32),
                        pltpu.VMEM((nblk * nsub,) + state, bf16)],
        name="ssd",
        compiler_params=pltpu.CompilerParams(
            dimension_semantics=("parallel", "arbitrary", "arbitrary"), vmem_limit_bytes=VMEM_LIMIT),
    )(act, dtT, z, *consts)


def _oproj_kernel(x_ref, aT_ref, s_ref, wa_ref, ws_ref, o_ref):
    o_ref[...] = (x_ref[...]
                  + lax.dot_general(aT_ref[...], wa_ref[...], TN_DIMS, preferred_element_type=f32)
                  + _dot(s_ref[...], ws_ref[...]))


def _oproj_call(x, attnT, ssm, wa, ws):
    B, L, _ = x.shape
    tm = min(OPROJ_TM, L)
    tok = lambda w: pl.BlockSpec((None, tm, w), lambda b, i: (b, i, 0))
    return pl.pallas_call(
        _oproj_kernel, grid=(B, L // tm),
        in_specs=[tok(D_MODEL), pl.BlockSpec((None, N_HEADS * V_HEAD, tm), lambda b, i: (b, 0, i)),
                  tok(D_SSM), _const_spec(wa.shape), _const_spec(ws.shape)],
        out_specs=tok(D_MODEL),
        out_shape=jax.ShapeDtypeStruct((B, L, D_MODEL), f32),
        name="oproj",
        compiler_params=pltpu.CompilerParams(
            dimension_semantics=("parallel", "parallel"), vmem_limit_bytes=VMEM_LIMIT),
    )(x, attnT, ssm, wa, ws)


def _ffn_kernel(x_ref, xp_ref, xn_ref, n2_ref, wg_ref, wu_ref, cw_ref, cb_ref, wd_ref, fn_ref,
                o_ref, h_scr, act_scr, *, nfc):
    i = pl.program_id(1)
    nt = pl.num_programs(1)
    tm = x_ref.shape[0]
    n2 = n2_ref[...]
    h_scr[0:tm, :] = _rms(x_ref[...], n2).astype(bf16)
    halo = jnp.concatenate([xp_ref[...], xn_ref[...]], axis=0)
    h_scr[tm:tm + 2 * F32_ROWS, :] = _rms(halo, n2).astype(bf16)
    has_prev = i > 0
    has_next = i < nt - 1

    def chunk(cf):
        col = cf * FFN_FC
        g_ext = _dot(h_scr[...], wg_ref[:, pl.ds(col, FFN_FC)])
        before = jnp.where(has_prev, g_ext[tm + F32_ROWS - 1:tm + F32_ROWS], 0.0)
        after = jnp.where(has_next, g_ext[tm + F32_ROWS:tm + F32_ROWS + 1], 0.0)
        gc = _conv3_rows(g_ext[:tm], before, after, cw_ref[:, pl.ds(col, FFN_FC)], cb_ref[:, pl.ds(col, FFN_FC)])
        u = _dot(h_scr[0:tm, :], wu_ref[:, pl.ds(col, FFN_FC)])
        act_scr[:, pl.ds(col, FFN_FC)] = (_silu(gc) * u).astype(bf16)

    for cf in range(nfc):
        chunk(cf)
    y = x_ref[...] + _dot(act_scr[...], wd_ref[...])
    o_ref[...] = _rms(y, fn_ref[...])


def _ffn_call(x, n2, wg, wu, cw, cb, wd, fn):
    B, L, _ = x.shape
    tm = min(FFN_TM, L)
    tok = pl.BlockSpec((None, tm, D_MODEL), lambda b, i: (b, i, 0))
    consts = (n2, wg, wu, cw, cb, wd, fn)
    return pl.pallas_call(
        functools.partial(_ffn_kernel, nfc=D_FF // FFN_FC),
        grid=(B, L // tm),
        in_specs=[tok] + _halo_specs(tm, L, D_MODEL) + [_const_spec(a.shape) for a in consts],
        out_specs=tok,
        out_shape=jax.ShapeDtypeStruct((B, L, D_MODEL), f32),
        scratch_shapes=[pltpu.VMEM((tm + 2 * F32_ROWS, D_MODEL), bf16), pltpu.VMEM((tm, D_FF), bf16)],
        name="ffn",
        compiler_params=pltpu.CompilerParams(
            dimension_semantics=("parallel", "parallel"), vmem_limit_bytes=VMEM_LIMIT),
    )(x, x, x, *consts)


def _head_lane_sources():
    src = np.full((HEAD_PAD,), QK_DIM, np.int32)
    half = HEAD_PAD // 2
    src[0:HALF_ROPE] = QK_NOPE + np.arange(HALF_ROPE)
    src[HALF_ROPE:half] = np.arange(half - HALF_ROPE)
    src[half:half + HALF_ROPE] = QK_NOPE + HALF_ROPE + np.arange(HALF_ROPE)
    n_rest = QK_NOPE - (half - HALF_ROPE)
    src[half + HALF_ROPE:half + HALF_ROPE + n_rest] = (half - HALF_ROPE) + np.arange(n_rest)
    return src


def _rope_tables(L):
    inv = ROPE_THETA ** (-jnp.arange(0, QK_ROPE, 2, dtype=f32) / QK_ROPE)
    ang = jnp.arange(L, dtype=f32)[:, None] * inv[None, :]
    cos, sin = jnp.cos(ang), jnp.sin(ang)
    half = HEAD_PAD // 2
    c_tab = jnp.ones((L, HEAD_PAD), f32)
    c_tab = c_tab.at[:, 0:HALF_ROPE].set(cos).at[:, half:half + HALF_ROPE].set(cos)
    s_tab = jnp.zeros((L, HEAD_PAD), f32)
    s_tab = s_tab.at[:, 0:HALF_ROPE].set(-sin).at[:, half:half + HALF_ROPE].set(sin)
    return c_tab, s_tab


def _expand_matrix(n):
    m = np.zeros((LANES, n * D_SSM), np.float32)
    for part in range(2):
        for j in range(n):
            for h in range(SSM_HEADS):
                r = part * n * SSM_HEADS + j * SSM_HEADS + h
                m[r, j * D_SSM + h * SSM_HEAD_DIM:j * D_SSM + (h + 1) * SSM_HEAD_DIM] = 1.0
    return jnp.asarray(m, bf16)


def _prepare_weights(norm1, w_in, q_a_norm, kv_a_norm, w_q_b, w_kv_b, conv_w, conv_b,
                     dt_bias_f, dt_bias_b, a_log_f, a_log_b, d_skip, ssm_norm, w_out,
                     norm2, w_gate, w_up, ffn_conv_w, ffn_conv_b, w_down, final_norm):
    half = HEAD_PAD // 2
    o_kr = Q_LORA + KV_LORA
    o_z = o_kr + QK_ROPE
    o_dt = o_z + D_SSM + D_XBC
    kr_blk = jnp.zeros((D_MODEL, HEAD_PAD), f32)
    kr_blk = kr_blk.at[:, 0:HALF_ROPE].set(w_in[:, o_kr:o_kr + HALF_ROPE])
    kr_blk = kr_blk.at[:, half:half + HALF_ROPE].set(w_in[:, o_kr + HALF_ROPE:o_kr + QK_ROPE])
    win = jnp.concatenate([w_in[:, :o_kr], kr_blk, w_in[:, o_z:o_dt]], axis=1).astype(bf16)
    w_dt = w_in[:, o_dt:o_dt + 2 * SSM_HEADS]

    src = _head_lane_sources()
    wq = w_q_b.reshape(Q_LORA, N_HEADS, QK_DIM)
    wq = jnp.concatenate([wq, jnp.zeros((Q_LORA, N_HEADS, 1), f32)], axis=-1)[:, :, src]
    wqT = wq.reshape(Q_LORA, N_HEADS * HEAD_PAD).T.astype(bf16)
    wkv = w_kv_b.reshape(KV_LORA, N_HEADS, QK_NOPE + V_HEAD)
    src_k = np.where(src < QK_NOPE, src, QK_NOPE)
    wk = jnp.concatenate([wkv[:, :, :QK_NOPE], jnp.zeros((KV_LORA, N_HEADS, 1), f32)], axis=-1)[:, :, src_k]
    wk = wk.reshape(KV_LORA, N_HEADS * HEAD_PAD).astype(bf16)
    wvT = wkv[:, :, QK_NOPE:].reshape(KV_LORA, N_HEADS * V_HEAD).T.astype(bf16)

    row = lambda v: v.reshape(1, -1).astype(f32)
    col = lambda v: v.reshape(-1, 1).astype(f32)
    a_neg = -jnp.exp(jnp.concatenate([a_log_f, a_log_b]).astype(f32)) * LOG2E
    group_of_lane = np.arange(GN) // D_STATE
    group_of_head = np.arange(SSM_HEADS) // HEADS_PER_GROUP
    gT = jnp.asarray(group_of_head[:, None] == group_of_lane[None, :], bf16)
    return dict(
        n1=row(norm1), win=win, qan=row(q_a_norm), kvan=row(kv_a_norm), wqT=wqT, wk=wk, wvT=wvT,
        wdtT=w_dt.T.astype(bf16), cw=conv_w.astype(f32), cb=row(conv_b),
        biasT=col(jnp.concatenate([dt_bias_f, dt_bias_b])), aT=col(a_neg), dskipT=col(d_skip),
        ssm_norm=row(ssm_norm), e4=_expand_matrix(4), e2=_expand_matrix(2), gT=gT,
        wa=w_out[:N_HEADS * V_HEAD].astype(bf16), ws=w_out[N_HEADS * V_HEAD:].astype(bf16),
        n2=row(norm2), wg=w_gate.astype(bf16), wu=w_up.astype(bf16), fcw=ffn_conv_w.astype(f32),
        fcb=row(ffn_conv_b), wd=w_down.astype(bf16), fn=row(final_norm),
    )


def _encoder(x, w):
    B, L, _ = x.shape
    c_tab, s_tab = _rope_tables(L)
    scale = QK_DIM ** -0.5 * LOG2E
    qT, k, vT, z, act, dtT, kn2 = _proj_call(
        x, w["n1"], w["win"], w["qan"], w["kvan"], w["wqT"], w["wk"], w["wvT"], w["wdtT"], w["cw"], w["cb"],
        c_tab, s_tab, (c_tab * scale).T, (s_tab * scale).T)
    kmax = jnp.sqrt(jnp.max(kn2, axis=1))[:, :, :1]
    kmax = jnp.broadcast_to(kmax[..., None], (B, N_HEADS, 1, _attn_tiles(L)[0]))
    attnT = _attn_call(qT, k, vT, kmax)
    blk = min(SSD_BLOCK, L)
    Q = min(SSD_Q, blk)
    triu = jnp.kron(jnp.eye(blk // Q, dtype=f32), jnp.triu(jnp.ones((Q, Q), f32))).astype(bf16)
    ssm = _ssd_call(act, dtT, z, w["biasT"], w["aT"], w["dskipT"], w["ssm_norm"], triu, w["e4"], w["e2"], w["gT"])
    x1 = _oproj_call(x, attnT, ssm, w["wa"], w["ws"])
    return _ffn_call(x1, w["n2"], w["wg"], w["wu"], w["fcw"], w["fcb"], w["wd"], w["fn"])


def kernel(x_prompt, x_sample, norm1, w_in, q_a_norm, kv_a_norm, w_q_b, w_kv_b, conv_w, conv_b,
           dt_bias_f, dt_bias_b, a_log_f, a_log_b, d_skip, ssm_norm, w_out, norm2, w_gate, w_up,
           ffn_conv_w, ffn_conv_b, w_down, final_norm):
    w = _prepare_weights(norm1[0], w_in[0], q_a_norm[0], kv_a_norm[0], w_q_b[0], w_kv_b[0], conv_w[0],
                         conv_b[0], dt_bias_f[0], dt_bias_b[0], a_log_f[0], a_log_b[0], d_skip[0],
                         ssm_norm[0], w_out[0], norm2[0], w_gate[0], w_up[0], ffn_conv_w[0],
                         ffn_conv_b[0], w_down[0], final_norm)
    return (_encoder(x_prompt, w), _encoder(x_sample, w))
```

```python
import functools

import numpy as np
import jax
import jax.numpy as jnp
from jax import lax
from jax.experimental import pallas as pl
from jax.experimental.pallas import tpu as pltpu

D_MODEL = 1024
N_HEADS = 16
QK_NOPE = 64
QK_ROPE = 32
HALF_ROPE = QK_ROPE // 2
QK_DIM = QK_NOPE + QK_ROPE
V_HEAD = 64
Q_LORA = 384
KV_LORA = 256
ROPE_THETA = 10000.0
SSM_HEADS = 16
SSM_HEAD_DIM = 64
D_SSM = SSM_HEADS * SSM_HEAD_DIM
SSM_GROUPS = 2
HEADS_PER_GROUP = SSM_HEADS // SSM_GROUPS
GROUP_WIDTH = D_SSM // SSM_GROUPS
D_STATE = 64
GN = SSM_GROUPS * D_STATE
D_XBC = D_SSM + 2 * GN
D_FF = 2816
EPS = 1e-6
LOG2E = float(np.log2(np.e))

LANES = 128
HEAD_PAD = LANES
F32_ROWS = 8
VMEM_LIMIT = 56 * 1024 * 1024

OFF_Q = 0
OFF_CKV = OFF_Q + Q_LORA
OFF_KR = OFF_CKV + KV_LORA
OFF_Z = OFF_KR + HEAD_PAD
OFF_XBC = OFF_Z + D_SSM

PROJ_TM = 1024
PROJ_CHUNK = 256
ATTN_VMEM_BUDGET = 42 * 1024 * 1024
ATTN_TK = 2048
ATTN_UNROLL = 4
ATTN_HEADS_PER_STEP = 2
SHIFT_ROW = QK_DIM
ATTN_BOUND_MARGIN = 1.0 + 2.0 ** -6
ATTN_DENOM_FLOOR = 2.0 ** -80
SSD_Q = 128
SSD_BLOCK = 512
OPROJ_TM = 1024
FFN_TM = 1024
FFN_FC = 256

NT_DIMS = (((1,), (1,)), ((), ()))
TN_DIMS = (((0,), (0,)), ((), ()))

f32 = jnp.float32
bf16 = jnp.bfloat16


def _rms(x, w):
    return x * lax.rsqrt(jnp.mean(x * x, axis=-1, keepdims=True) + EPS) * w


def _dot(a, b):
    return jnp.dot(a, b, preferred_element_type=f32)


def _silu(x):
    h = 0.5 * x
    return h * jnp.tanh(h) + h


def _conv3_rows(x, before, after, cw, cb):
    n = x.shape[0]
    sub = lax.broadcasted_iota(jnp.int32, (F32_ROWS, 1), 0)
    down = pltpu.roll(x, 1, axis=0)
    up = pltpu.roll(x, n - 1, axis=0)
    x_prev = jnp.concatenate([jnp.where(sub == 0, before, down[:F32_ROWS]), down[F32_ROWS:]], axis=0)
    x_next = jnp.concatenate([up[:n - F32_ROWS],
                              jnp.where(sub == F32_ROWS - 1, after, up[n - F32_ROWS:])], axis=0)
    return x_prev * cw[0:1] + x * cw[1:2] + x_next * cw[2:3] + cb


def _halo_specs(tm, L, width):
    hb = tm // F32_ROWS
    last = L // F32_ROWS - 1
    return [pl.BlockSpec((None, F32_ROWS, width), lambda b, i: (b, jnp.maximum(i * hb - 1, 0), 0)),
            pl.BlockSpec((None, F32_ROWS, width), lambda b, i: (b, jnp.minimum((i + 1) * hb, last), 0))]


def _const_spec(shape):
    nd = len(shape)
    return pl.BlockSpec(shape, lambda *_: (0,) * nd, pipeline_mode=pl.Buffered(1))


def _proj_kernel(x_ref, xp_ref, xn_ref, n1_ref, win_ref, qan_ref, kvan_ref, wqT_ref, wk_ref, wvT_ref, wdtT_ref,
                 cw_ref, cb_ref, c_ref, s_ref, cT_ref, sT_ref,
                 qT_out, k_out, vT_out, z_out, act_out, dtT_out, kn2_out, h_scr):
    i = pl.program_id(1)
    tm = x_ref.shape[0]
    n1 = n1_ref[...]
    h_scr[0:tm, :] = _rms(x_ref[...], n1).astype(bf16)
    halo = jnp.concatenate([xp_ref[...], xn_ref[...]], axis=0)
    h_scr[tm:tm + 2 * F32_ROWS, :] = _rms(halo, n1).astype(bf16)
    h = h_scr[0:tm, :]
    half = HEAD_PAD // 2
    lat = _dot(h, win_ref[:, OFF_Q:OFF_Z])
    hq = _rms(lat[:, OFF_Q:OFF_Q + Q_LORA], qan_ref[...]).astype(bf16)
    hc = _rms(lat[:, OFF_CKV:OFF_CKV + KV_LORA], kvan_ref[...]).astype(bf16)
    kr = lat[:, OFF_KR:OFF_KR + HEAD_PAD]
    krf = kr * c_ref[...] + pltpu.roll(kr, half, axis=1) * s_ref[...]

    has_prev = i > 0
    has_next = i < pl.num_programs(1) - 1
    cT = cT_ref[...]
    sT = sT_ref[...]
    lane = lax.broadcasted_iota(jnp.int32, (1, HEAD_PAD), 1)
    shift_lanes = (lane == SHIFT_ROW) | (lane == SHIFT_ROW + 1)
    heads_per_chunk = PROJ_CHUNK // HEAD_PAD
    norms = [None] * N_HEADS

    def ssd_chunk(col):
        xbc = _dot(h_scr[...], win_ref[:, OFF_XBC + col:OFF_XBC + col + PROJ_CHUNK])
        before = jnp.where(has_prev, xbc[tm + F32_ROWS - 1:tm + F32_ROWS], 0.0)
        after = jnp.where(has_next, xbc[tm + F32_ROWS:tm + F32_ROWS + 1], 0.0)
        conv = _conv3_rows(xbc[:tm], before, after, cw_ref[:, col:col + PROJ_CHUNK], cb_ref[:, col:col + PROJ_CHUNK])
        act_out[:, col:col + PROJ_CHUNK] = _silu(conv).astype(bf16)

    def qk_chunk(h0):
        rows = slice(h0 * HEAD_PAD, (h0 + heads_per_chunk) * HEAD_PAD)
        qT = lax.dot_general(wqT_ref[rows, :], hq, NT_DIMS, preferred_element_type=f32)
        kn = _dot(hc, wk_ref[:, rows])
        for hh in range(heads_per_chunk):
            blk = qT[hh * HEAD_PAD:(hh + 1) * HEAD_PAD, :]
            rot = jnp.concatenate([blk[half:], blk[:half]], axis=0)
            qT_out[(h0 + hh) * HEAD_PAD:(h0 + hh + 1) * HEAD_PAD, :] = (blk * cT + rot * sT).astype(bf16)
            kb = (kn[:, hh * HEAD_PAD:(hh + 1) * HEAD_PAD] + krf).astype(bf16)
            kf = kb.astype(f32)
            n2 = jnp.max(jnp.sum(kf * kf, axis=1, keepdims=True), axis=0, keepdims=True)
            norms[h0 + hh] = jnp.broadcast_to(n2, (1, HEAD_PAD))
            k_out[h0 + hh] = jnp.where(shift_lanes, jnp.ones_like(kb), kb)

    def zv_chunk(col):
        z_out[:, col:col + PROJ_CHUNK] = _dot(h, win_ref[:, OFF_Z + col:OFF_Z + col + PROJ_CHUNK]).astype(bf16)
        vT_out[col:col + PROJ_CHUNK, :] = lax.dot_general(
            wvT_ref[col:col + PROJ_CHUNK, :], hc, NT_DIMS, preferred_element_type=f32).astype(bf16)

    light = ([functools.partial(qk_chunk, h0) for h0 in range(0, N_HEADS, heads_per_chunk)]
             + [functools.partial(zv_chunk, col) for col in range(0, D_SSM, PROJ_CHUNK)])
    heavy = [functools.partial(ssd_chunk, col) for col in range(0, D_XBC, PROJ_CHUNK)]
    per_heavy = -(-len(light) // len(heavy))
    for n, chunk in enumerate(heavy):
        chunk()
        for other in light[n * per_heavy:(n + 1) * per_heavy]:
            other()
    kn2_out[...] = jnp.concatenate(norms, axis=0)
    dtT_out[...] = lax.dot_general(wdtT_ref[...], h, NT_DIMS, preferred_element_type=f32)


def _proj_call(x, n1, win, qan, kvan, wqT, wk, wvT, wdtT, cw, cb, c_tab, s_tab, cT_tab, sT_tab):
    B, L, _ = x.shape
    tm = min(PROJ_TM, L)
    grid = (B, L // tm)
    tok = lambda w: pl.BlockSpec((None, tm, w), lambda b, i: (b, i, 0))
    tokT = lambda w: pl.BlockSpec((None, w, tm), lambda b, i: (b, 0, i))
    consts = (n1, win, qan, kvan, wqT, wk, wvT, wdtT, cw, cb)
    in_specs = [tok(D_MODEL)] + _halo_specs(tm, L, D_MODEL) + [_const_spec(a.shape) for a in consts] + [
        pl.BlockSpec((tm, HEAD_PAD), lambda b, i: (i, 0)),
        pl.BlockSpec((tm, HEAD_PAD), lambda b, i: (i, 0)),
        pl.BlockSpec((HEAD_PAD, tm), lambda b, i: (0, i)),
        pl.BlockSpec((HEAD_PAD, tm), lambda b, i: (0, i)),
    ]
    out_shape = [
        jax.ShapeDtypeStruct((B, N_HEADS * HEAD_PAD, L), bf16),
        jax.ShapeDtypeStruct((B, N_HEADS, L, HEAD_PAD), bf16),
        jax.ShapeDtypeStruct((B, N_HEADS * V_HEAD, L), bf16),
        jax.ShapeDtypeStruct((B, L, D_SSM), bf16),
        jax.ShapeDtypeStruct((B, L, D_XBC), bf16),
        jax.ShapeDtypeStruct((B, 2 * SSM_HEADS, L), f32),
        jax.ShapeDtypeStruct((B, L // tm, N_HEADS, HEAD_PAD), f32),
    ]
    k_spec = pl.BlockSpec((None, N_HEADS, tm, HEAD_PAD), lambda b, i: (b, 0, i, 0))
    out_specs = [tokT(N_HEADS * HEAD_PAD), k_spec, tokT(N_HEADS * V_HEAD),
                 tok(D_SSM), tok(D_XBC), tokT(2 * SSM_HEADS),
                 pl.BlockSpec((None, None, N_HEADS, HEAD_PAD), lambda b, i: (b, i, 0, 0))]
    return pl.pallas_call(
        _proj_kernel, grid=grid, in_specs=in_specs, out_specs=out_specs, out_shape=out_shape,
        scratch_shapes=[pltpu.VMEM((tm + 2 * F32_ROWS, D_MODEL), bf16)],
        name="proj",
        compiler_params=pltpu.CompilerParams(
            dimension_semantics=("parallel", "parallel"), vmem_limit_bytes=VMEM_LIMIT),
    )(x, x, x, *consts, c_tab, s_tab, cT_tab, sT_tab)


def _attn_kernel(kmax_ref, qT_ref, k_ref, vT_ref, o_ref, q_scr, p_scr, psum_scr, acc_ref, den_ref, dmin_ref, *, tq, tk, nq, nk, unroll, nh):
    ng = nk // unroll
    heads = range(nh)

    def q_cols(qi):
        return pl.ds(pl.multiple_of(qi * tq, tq), tq)

    def head_rows(hd, width):
        return slice(hd * width, (hd + 1) * width)

    def v_tile(hd, j):
        return vT_ref[head_rows(hd, V_HEAD), pl.ds(pl.multiple_of(j * tk, tk), tk)]

    def k_tile(hd, j):
        return k_ref[hd, pl.ds(pl.multiple_of(j * tk, tk), tk), :]

    def shift_queries(qi, qslot):
        row = lax.broadcasted_iota(jnp.int32, (HEAD_PAD, 1), 0)
        for hd in heads:
            q = qT_ref[head_rows(hd, HEAD_PAD), q_cols(qi)]
            qf = q.astype(f32)
            bound = jnp.sqrt(jnp.sum(qf * qf, axis=0, keepdims=True)) * kmax_ref[hd] * ATTN_BOUND_MARGIN
            hi = bound.astype(bf16)
            lo = (bound - hi.astype(f32)).astype(bf16)
            q_scr[qslot, hd] = jnp.where(row == SHIFT_ROW, -hi, jnp.where(row == SHIFT_ROW + 1, -lo, q))

    def produce(hd, qslot, j, slot):
        p = jnp.exp2(_dot(k_tile(hd, j), q_scr[qslot, hd]))
        p_scr[hd, slot] = p.astype(bf16)
        psum_scr[hd, slot] = jnp.sum(p, axis=0, keepdims=True)

    def group(g, qslot, next_qslot):
        acc = [acc_ref[hd] for hd in heads]
        den = [den_ref[hd] for hd in heads]
        for u in range(unroll):
            j = g * unroll + u
            nxt = (u + 1) % 2
            for hd in heads:
                if next_qslot is False or u < unroll - 1:
                    produce(hd, qslot, j + 1, nxt)
                elif next_qslot is not None:
                    produce(hd, next_qslot, 0, nxt)
            for hd in heads:
                acc[hd] = acc[hd] + _dot(v_tile(hd, j), p_scr[hd, u % 2])
                den[hd] = den[hd] + psum_scr[hd, u % 2]
        for hd in heads:
            acc_ref[hd] = acc[hd]
            den_ref[hd] = den[hd]

    def recompute_exact(hd, qi):
        q = qT_ref[head_rows(hd, HEAD_PAD), q_cols(qi)]

        def body(j, carry):
            m, den, acc = carry
            s = _dot(k_tile(hd, j), q)
            m_new = jnp.maximum(m, jnp.max(s, axis=0, keepdims=True))
            alpha = jnp.exp2(m - m_new)
            p = jnp.exp2(s - m_new)
            return (m_new, alpha * den + jnp.sum(p, axis=0, keepdims=True),
                    alpha * acc + _dot(v_tile(hd, j), p.astype(bf16)))

        init = (jnp.full((1, tq), -jnp.inf, f32), jnp.zeros((1, tq), f32), jnp.zeros((V_HEAD, tq), f32))
        _, den, acc = lax.fori_loop(0, nk, body, init)
        write_output(hd, qi, acc, den)

    def write_output(hd, qi, acc, den):
        o_ref[head_rows(hd, V_HEAD), q_cols(qi)] = (acc / den).astype(o_ref.dtype)

    def query_tile(qi, qslot, next_qslot):
        if next_qslot is not None:
            shift_queries(qi + 1, next_qslot)
        acc_ref[...] = jnp.zeros(acc_ref.shape, f32)
        den_ref[...] = jnp.zeros(den_ref.shape, f32)
        if ng > 1:
            def body(g, carry):
                group(g, qslot, False)
                return carry
            lax.fori_loop(0, ng - 1, body, 0)
        group(ng - 1, qslot, next_qslot)
        for hd in heads:
            den = den_ref[hd]
            write_output(hd, qi, acc_ref[hd], den)
            dmin_ref[hd] = jnp.minimum(dmin_ref[hd], den)

    dmin_ref[...] = jnp.full(dmin_ref.shape, jnp.inf, f32)
    shift_queries(0, 0)
    for hd in heads:
        produce(hd, 0, 0, 0)
    if nq > 1:
        def q_body(qi, carry):
            query_tile(qi, qi % 2, (qi + 1) % 2)
            return carry
        lax.fori_loop(0, nq - 1, q_body, 0)
    query_tile(nq - 1, (nq - 1) % 2, None)

    for hd in heads:
        @pl.when(jnp.logical_not(jnp.min(dmin_ref[hd]) >= ATTN_DENOM_FLOOR))
        def _(hd=hd):
            def redo(qi, carry):
                recompute_exact(hd, qi)
                return carry
            lax.fori_loop(0, nq, redo, 0)


def _attn_tiles(L):
    nh = ATTN_HEADS_PER_STEP
    tk = min(ATTN_TK, L // 2)
    kv_buffers = 1 if L // tk > 2 else 2
    kv_bytes = kv_buffers * nh * L * (HEAD_PAD + V_HEAD) * 2
    tq = L
    while nh * 2 * tk * tq * 2 > ATTN_VMEM_BUDGET - kv_bytes and tq > LANES:
        tq //= 2
    return tq, tk, kv_buffers


def _attn_call(qT, k, vT, kmax):
    B, _, L = qT.shape
    tq, tk, kv_buffers = _attn_tiles(L)
    kv_mode = pl.Buffered(kv_buffers)
    nk = L // tk
    unroll = min(ATTN_UNROLL, nk)
    assert unroll % 2 == 0 and nk % unroll == 0, "key tiles alternate between two probability buffers"
    nq = L // tq if nk == unroll else 1
    tqb = nq * tq
    nh = ATTN_HEADS_PER_STEP
    return pl.pallas_call(
        functools.partial(_attn_kernel, tq=tq, tk=tk, nq=nq, nk=nk, unroll=unroll, nh=nh),
        grid=(B, N_HEADS // nh, L // tqb),
        scratch_shapes=[pltpu.VMEM((2, nh, HEAD_PAD, tq), bf16), pltpu.VMEM((nh, 2, tk, tq), bf16),
                        pltpu.VMEM((nh, 2, 1, tq), f32), pltpu.VMEM((nh, V_HEAD, tq), f32),
                        pltpu.VMEM((nh, 1, tq), f32), pltpu.VMEM((nh, 1, tq), f32)],
        in_specs=[
            pl.BlockSpec((None, nh, 1, tq), lambda b, h, i: (b, h, 0, 0)),
            pl.BlockSpec((None, nh * HEAD_PAD, tqb), lambda b, h, i: (b, h, i)),
            pl.BlockSpec((None, nh, L, HEAD_PAD), lambda b, h, i: (b, h, 0, 0), pipeline_mode=kv_mode),
            pl.BlockSpec((None, nh * V_HEAD, L), lambda b, h, i: (b, h, 0), pipeline_mode=kv_mode),
        ],
        out_specs=pl.BlockSpec((None, nh * V_HEAD, tqb), lambda b, h, i: (b, h, i)),
        out_shape=jax.ShapeDtypeStruct((B, N_HEADS * V_HEAD, L), bf16),
        name="attn",
        compiler_params=pltpu.CompilerParams(
            dimension_semantics=("parallel", "parallel", "arbitrary"), vmem_limit_bytes=VMEM_LIMIT),
    )(kmax, qT, k, vT)


def _split3(v):
    a1 = v.astype(bf16)
    r1 = v - a1.astype(f32)
    a2 = r1.astype(bf16)
    a3 = (r1 - a2.astype(f32)).astype(bf16)
    return a1, a2, a3


def _chunk_scalars(dtT_raw, biasT, aT, triu):
    nh2 = 2 * SSM_HEADS
    dtT = jax.nn.softplus(dtT_raw + biasT)
    stepT = dtT * aT
    cs3 = _dot(jnp.concatenate(_split3(stepT), axis=0), triu)
    csT = cs3[0:nh2] + cs3[nh2:2 * nh2] + cs3[2 * nh2:3 * nh2]
    return dtT, stepT, csT


def _expand_heads(partsT, e_ref):
    n = len(partsT)
    stk = jnp.concatenate(partsT, axis=0)
    hi = stk.astype(bf16).astype(f32)
    pieces = [hi, stk - hi]
    pad = LANES - 2 * stk.shape[0]
    if pad:
        pieces.append(jnp.zeros((pad, stk.shape[1]), f32))
    nat = jnp.transpose(jnp.concatenate(pieces, axis=0)).astype(bf16)
    full = _dot(nat, e_ref[...])
    return [full[:, j * D_SSM:(j + 1) * D_SSM] for j in range(n)]


def _ssd_kernel(act_ref, dtT_ref, z_ref, x_ref, attnT_ref, biasT_ref, aT_ref, dskipT_ref, norm_ref, triu_ref,
                e4_ref, e2_ref, gT_ref, wa_ref, ws_ref, o_ref, hf_ref, hb_ref, hbs_ref, ssm_scr, *, nblk, nsub, Q):
    ph = pl.program_id(1)
    c = pl.program_id(2)
    H = SSM_HEADS

    def group_slices(g):
        return slice(g * GROUP_WIDTH, (g + 1) * GROUP_WIDTH), slice(g * D_STATE, (g + 1) * D_STATE)

    dtT, stepT, csT = _chunk_scalars(dtT_ref[...], biasT_ref[...], aT_ref[...], triu_ref[...])
    totT = jnp.concatenate(
        [jnp.broadcast_to(csT[:, (j + 1) * Q - 1:(j + 1) * Q], (2 * H, Q)) for j in range(nsub)], axis=1)
    cbT_b = csT[H:] - stepT[H:]
    Bm = act_ref[:, D_SSM:D_SSM + GN]

    @pl.when(ph == 0)
    def _backward_states():
        @pl.when(c == 0)
        def _():
            hb_ref[...] = jnp.zeros_like(hb_ref)

        blk = nblk - 1 - c
        wb, eb = _expand_heads([jnp.exp2(cbT_b) * dtT[H:], jnp.exp2(totT[H:] - cbT_b)], e2_ref)
        xw = (act_ref[:, :D_SSM].astype(f32) * wb).astype(bf16)
        for j in reversed(range(nsub)):
            rows = slice(j * Q, (j + 1) * Q)
            hbs_ref[blk * nsub + j] = hb_ref[...].astype(bf16)
            for g in range(SSM_GROUPS):
                sl, sn = group_slices(g)
                upd = lax.dot_general(Bm[rows, sn], xw[rows, sl], TN_DIMS, preferred_element_type=f32)
                hb_ref[g] = eb[j * Q:j * Q + 1, sl] * hb_ref[g] + upd

    @pl.when(ph == 1)
    def _outputs():
        @pl.when(c == 0)
        def _():
            hf_ref[...] = jnp.zeros_like(hf_ref)

        ti = lax.broadcasted_iota(jnp.int32, (Q, Q), 0)
        si = lax.broadcasted_iota(jnp.int32, (Q, Q), 1)
        lower = ti >= si
        lane = lax.broadcasted_iota(jnp.int32, (1, LANES), 1)
        first_half = lane < SSM_HEAD_DIM
        gT = gT_ref[...]
        nw = norm_ref[...]

        xs_b = act_ref[:, :D_SSM]
        Cm = act_ref[:, D_SSM + GN:]
        xs = xs_b.astype(f32)
        csT_f = csT[:H]
        l2dt = jnp.log2(dtT)
        rowf = csT_f - l2dt[:H]
        rowb = cbT_b + l2dt[H:]
        cols = jnp.transpose(jnp.concatenate([csT_f, cbT_b], axis=0))
        cb = Cm.astype(f32) * Bm.astype(f32)
        cb_h = cb.astype(bf16)
        cb_l = (cb - cb_h.astype(f32)).astype(bf16)
        diagT = (lax.dot_general(gT, cb_h, NT_DIMS, preferred_element_type=f32)
                 + lax.dot_general(gT, cb_l, NT_DIMS, preferred_element_type=f32))
        partsT = [jnp.exp2(csT_f), jnp.exp2(totT[H:] - cbT_b), jnp.exp2(totT[:H] - csT_f) * dtT[:H],
                  dskipT_ref[...] + dtT[H:] * diagT]

        def expand_chunk(j):
            return _expand_heads([p[:, j * Q:(j + 1) * Q] for p in partsT], e4_ref)

        expanded = expand_chunk(0)
        for j in range(nsub):
            rows = slice(j * Q, (j + 1) * Q)
            chunk_id = c * nsub + j
            ef, eb, wf, coef = expanded
            if j + 1 < nsub:
                expanded = expand_chunk(j + 1)
            y_groups = []
            for g in range(SSM_GROUPS):
                sl, sn = group_slices(g)
                Cg = Cm[rows, sn]
                Bg = Bm[rows, sn]
                cbm = lax.dot_general(Cg, Bg, NT_DIMS, preferred_element_type=f32)
                y_off = (ef[:, sl] * _dot(Cg, hf_ref[g].astype(bf16))
                         + eb[:, sl] * _dot(Cg, hbs_ref[chunk_id, g]))
                pairs = []
                for hp in range(HEADS_PER_GROUP // 2):
                    lo = g * GROUP_WIDTH + hp * LANES
                    x_pair = xs_b[rows, lo:lo + LANES]
                    y_pair = None
                    for k in range(2):
                        hh = g * HEADS_PER_GROUP + 2 * hp + k
                        arg = jnp.where(lower, cols[rows, hh:hh + 1] - rowf[hh:hh + 1, rows],
                                        rowb[hh:hh + 1, rows] - cols[rows, H + hh:H + hh + 1])
                        mat = (cbm * jnp.exp2(arg)).astype(bf16)
                        keep = first_half if k == 0 else jnp.logical_not(first_half)
                        contrib = _dot(mat, jnp.where(keep, x_pair, jnp.zeros_like(x_pair)))
                        y_pair = contrib if y_pair is None else y_pair + contrib
                    pairs.append(y_pair)
                y_groups.append(jnp.concatenate(pairs, axis=1) + y_off)
                xw = (xs[rows, sl] * wf[:, sl]).astype(bf16)
                upd = lax.dot_general(Bg, xw, TN_DIMS, preferred_element_type=f32)
                hf_ref[g] = ef[Q - 1:Q, sl] * hf_ref[g] + upd

            y = jnp.concatenate(y_groups, axis=1) + xs[rows] * coef
            y = y * _silu(z_ref[rows, :].astype(f32))
            for g in range(SSM_GROUPS):
                sl, _ = group_slices(g)
                ssm_scr[rows, sl] = _rms(y[:, sl], nw[:, sl]).astype(bf16)

        o_ref[...] = (x_ref[...]
                      + lax.dot_general(attnT_ref[...], wa_ref[...], TN_DIMS, preferred_element_type=f32)
                      + _dot(ssm_scr[...], ws_ref[...]))


def _ssd_call(act, dtT, z, x, attnT, biasT, aT, dskipT, norm, triu, e4, e2, gT, wa, ws):
    B, L, _ = act.shape
    blk = min(SSD_BLOCK, L)
    Q = min(SSD_Q, blk)
    nsub = blk // Q
    nblk = L // blk

    def bidx(p, c):
        return jnp.where(p == 0, nblk - 1 - c, c)

    consts = (biasT, aT, dskipT, norm, triu, e4, e2, gT, wa, ws)
    in_specs = [
        pl.BlockSpec((None, blk, D_XBC), lambda b, p, c: (b, bidx(p, c), 0)),
        pl.BlockSpec((None, 2 * SSM_HEADS, blk), lambda b, p, c: (b, 0, bidx(p, c))),
        pl.BlockSpec((None, blk, D_SSM), lambda b, p, c: (b, p * c, 0)),
        pl.BlockSpec((None, blk, D_MODEL), lambda b, p, c: (b, p * c, 0)),
        pl.BlockSpec((None, N_HEADS * V_HEAD, blk), lambda b, p, c: (b, 0, p * c)),
    ] + [_const_spec(a.shape) for a in consts]
    state = (SSM_GROUPS, D_STATE, GROUP_WIDTH)
    return pl.pallas_call(
        functools.partial(_ssd_kernel, nblk=nblk, nsub=nsub, Q=Q),
        grid=(B, 2, nblk),
        in_specs=in_specs,
        out_specs=pl.BlockSpec((None, blk, D_MODEL), lambda b, p, c: (b, p * c, 0)),
        out_shape=jax.ShapeDtypeStruct((B, L, D_MODEL), f32),
        scratch_shapes=[pltpu.VMEM(state, f32), pltpu.VMEM(state, f32),
                        pltpu.VMEM((nblk * nsub,) + state, bf16), pltpu.VMEM((blk, D_SSM), bf16)],
        name="ssd",
        compiler_params=pltpu.CompilerParams(
            dimension_semantics=("parallel", "arbitrary", "arbitrary"), vmem_limit_bytes=VMEM_LIMIT),
    )(act, dtT, z, x, attnT, *consts)


def _oproj_kernel(x_ref, aT_ref, s_ref, wa_ref, ws_ref, o_ref):
    o_ref[...] = (x_ref[...]
                  + lax.dot_general(aT_ref[...], wa_ref[...], TN_DIMS, preferred_element_type=f32)
                  + _dot(s_ref[...], ws_ref[...]))


def _oproj_call(x, attnT, ssm, wa, ws):
    B, L, _ = x.shape
    tm = min(OPROJ_TM, L)
    tok = lambda w: pl.BlockSpec((None, tm, w), lambda b, i: (b, i, 0))
    return pl.pallas_call(
        _oproj_kernel, grid=(B, L // tm),
        in_specs=[tok(D_MODEL), pl.BlockSpec((None, N_HEADS * V_HEAD, tm), lambda b, i: (b, 0, i)),
                  tok(D_SSM), _const_spec(wa.shape), _const_spec(ws.shape)],
        out_specs=tok(D_MODEL),
        out_shape=jax.ShapeDtypeStruct((B, L, D_MODEL), f32),
        name="oproj",
        compiler_params=pltpu.CompilerParams(
            dimension_semantics=("parallel", "parallel"), vmem_limit_bytes=VMEM_LIMIT),
    )(x, attnT, ssm, wa, ws)


def _ffn_kernel(x_ref, xp_ref, xn_ref, n2_ref, wg_ref, wu_ref, cw_ref, cb_ref, wd_ref, fn_ref,
                o_ref, h_scr, act_scr, *, nfc):
    i = pl.program_id(1)
    nt = pl.num_programs(1)
    tm = x_ref.shape[0]
    n2 = n2_ref[...]
    h_scr[0:tm, :] = _rms(x_ref[...], n2).astype(bf16)
    halo = jnp.concatenate([xp_ref[...], xn_ref[...]], axis=0)
    h_scr[tm:tm + 2 * F32_ROWS, :] = _rms(halo, n2).astype(bf16)
    has_prev = i > 0
    has_next = i < nt - 1

    def chunk(cf):
        col = cf * FFN_FC
        g_ext = _dot(h_scr[...], wg_ref[:, pl.ds(col, FFN_FC)])
        before = jnp.where(has_prev, g_ext[tm + F32_ROWS - 1:tm + F32_ROWS], 0.0)
        after = jnp.where(has_next, g_ext[tm + F32_ROWS:tm + F32_ROWS + 1], 0.0)
        gc = _conv3_rows(g_ext[:tm], before, after, cw_ref[:, pl.ds(col, FFN_FC)], cb_ref[:, pl.ds(col, FFN_FC)])
        u = _dot(h_scr[0:tm, :], wu_ref[:, pl.ds(col, FFN_FC)])
        act_scr[:, pl.ds(col, FFN_FC)] = (_silu(gc) * u).astype(bf16)

    for cf in range(nfc):
        chunk(cf)
    y = x_ref[...] + _dot(act_scr[...], wd_ref[...])
    o_ref[...] = _rms(y, fn_ref[...])


def _ffn_call(x, n2, wg, wu, cw, cb, wd, fn):
    B, L, _ = x.shape
    tm = min(FFN_TM, L)
    tok = pl.BlockSpec((None, tm, D_MODEL), lambda b, i: (b, i, 0))
    consts = (n2, wg, wu, cw, cb, wd, fn)
    return pl.pallas_call(
        functools.partial(_ffn_kernel, nfc=D_FF // FFN_FC),
        grid=(B, L // tm),
        in_specs=[tok] + _halo_specs(tm, L, D_MODEL) + [_const_spec(a.shape) for a in consts],
        out_specs=tok,
        out_shape=jax.ShapeDtypeStruct((B, L, D_MODEL), f32),
        scratch_shapes=[pltpu.VMEM((tm + 2 * F32_ROWS, D_MODEL), bf16), pltpu.VMEM((tm, D_FF), bf16)],
        name="ffn",
        compiler_params=pltpu.CompilerParams(
            dimension_semantics=("parallel", "parallel"), vmem_limit_bytes=VMEM_LIMIT),
    )(x, x, x, *consts)


def _head_lane_sources():
    src = np.full((HEAD_PAD,), QK_DIM, np.int32)
    half = HEAD_PAD // 2
    src[0:HALF_ROPE] = QK_NOPE + np.arange(HALF_ROPE)
    src[HALF_ROPE:half] = np.arange(half - HALF_ROPE)
    src[half:half + HALF_ROPE] = QK_NOPE + HALF_ROPE + np.arange(HALF_ROPE)
    n_rest = QK_NOPE - (half - HALF_ROPE)
    src[half + HALF_ROPE:half + HALF_ROPE + n_rest] = (half - HALF_ROPE) + np.arange(n_rest)
    return src


def _rope_tables(L):
    inv = ROPE_THETA ** (-jnp.arange(0, QK_ROPE, 2, dtype=f32) / QK_ROPE)
    ang = jnp.arange(L, dtype=f32)[:, None] * inv[None, :]
    cos, sin = jnp.cos(ang), jnp.sin(ang)
    half = HEAD_PAD // 2
    c_tab = jnp.ones((L, HEAD_PAD), f32)
    c_tab = c_tab.at[:, 0:HALF_ROPE].set(cos).at[:, half:half + HALF_ROPE].set(cos)
    s_tab = jnp.zeros((L, HEAD_PAD), f32)
    s_tab = s_tab.at[:, 0:HALF_ROPE].set(-sin).at[:, half:half + HALF_ROPE].set(sin)
    return c_tab, s_tab


def _expand_matrix(n):
    m = np.zeros((LANES, n * D_SSM), np.float32)
    for part in range(2):
        for j in range(n):
            for h in range(SSM_HEADS):
                r = part * n * SSM_HEADS + j * SSM_HEADS + h
                m[r, j * D_SSM + h * SSM_HEAD_DIM:j * D_SSM + (h + 1) * SSM_HEAD_DIM] = 1.0
    return jnp.asarray(m, bf16)


def _prepare_weights(norm1, w_in, q_a_norm, kv_a_norm, w_q_b, w_kv_b, conv_w, conv_b,
                     dt_bias_f, dt_bias_b, a_log_f, a_log_b, d_skip, ssm_norm, w_out,
                     norm2, w_gate, w_up, ffn_conv_w, ffn_conv_b, w_down, final_norm):
    half = HEAD_PAD // 2
    o_kr = Q_LORA + KV_LORA
    o_z = o_kr + QK_ROPE
    o_dt = o_z + D_SSM + D_XBC
    kr_blk = jnp.zeros((D_MODEL, HEAD_PAD), f32)
    kr_blk = kr_blk.at[:, 0:HALF_ROPE].set(w_in[:, o_kr:o_kr + HALF_ROPE])
    kr_blk = kr_blk.at[:, half:half + HALF_ROPE].set(w_in[:, o_kr + HALF_ROPE:o_kr + QK_ROPE])
    win = jnp.concatenate([w_in[:, :o_kr], kr_blk, w_in[:, o_z:o_dt]], axis=1).astype(bf16)
    w_dt = w_in[:, o_dt:o_dt + 2 * SSM_HEADS]

    src = _head_lane_sources()
    wq = w_q_b.reshape(Q_LORA, N_HEADS, QK_DIM)
    wq = jnp.concatenate([wq, jnp.zeros((Q_LORA, N_HEADS, 1), f32)], axis=-1)[:, :, src]
    wqT = wq.reshape(Q_LORA, N_HEADS * HEAD_PAD).T.astype(bf16)
    wkv = w_kv_b.reshape(KV_LORA, N_HEADS, QK_NOPE + V_HEAD)
    src_k = np.where(src < QK_NOPE, src, QK_NOPE)
    wk = jnp.concatenate([wkv[:, :, :QK_NOPE], jnp.zeros((KV_LORA, N_HEADS, 1), f32)], axis=-1)[:, :, src_k]
    wk = wk.reshape(KV_LORA, N_HEADS * HEAD_PAD).astype(bf16)
    wvT = wkv[:, :, QK_NOPE:].reshape(KV_LORA, N_HEADS * V_HEAD).T.astype(bf16)

    row = lambda v: v.reshape(1, -1).astype(f32)
    col = lambda v: v.reshape(-1, 1).astype(f32)
    a_neg = -jnp.exp(jnp.concatenate([a_log_f, a_log_b]).astype(f32)) * LOG2E
    group_of_lane = np.arange(GN) // D_STATE
    group_of_head = np.arange(SSM_HEADS) // HEADS_PER_GROUP
    gT = jnp.asarray(group_of_head[:, None] == group_of_lane[None, :], bf16)
    return dict(
        n1=row(norm1), win=win, qan=row(q_a_norm), kvan=row(kv_a_norm), wqT=wqT, wk=wk, wvT=wvT,
        wdtT=w_dt.T.astype(bf16), cw=conv_w.astype(f32), cb=row(conv_b),
        biasT=col(jnp.concatenate([dt_bias_f, dt_bias_b])), aT=col(a_neg), dskipT=col(d_skip),
        ssm_norm=row(ssm_norm), e4=_expand_matrix(4), e2=_expand_matrix(2), gT=gT,
        wa=w_out[:N_HEADS * V_HEAD].astype(bf16), ws=w_out[N_HEADS * V_HEAD:].astype(bf16),
        n2=row(norm2), wg=w_gate.astype(bf16), wu=w_up.astype(bf16), fcw=ffn_conv_w.astype(f32),
        fcb=row(ffn_conv_b), wd=w_down.astype(bf16), fn=row(final_norm),
    )


def _encoder(x, w):
    B, L, _ = x.shape
    c_tab, s_tab = _rope_tables(L)
    scale = QK_DIM ** -0.5 * LOG2E
    qT, k, vT, z, act, dtT, kn2 = _proj_call(
        x, w["n1"], w["win"], w["qan"], w["kvan"], w["wqT"], w["wk"], w["wvT"], w["wdtT"], w["cw"], w["cb"],
        c_tab, s_tab, (c_tab * scale).T, (s_tab * scale).T)
    kmax = jnp.sqrt(jnp.max(kn2, axis=1))[:, :, :1]
    kmax = jnp.broadcast_to(kmax[..., None], (B, N_HEADS, 1, _attn_tiles(L)[0]))
    attnT = _attn_call(qT, k, vT, kmax)
    blk = min(SSD_BLOCK, L)
    Q = min(SSD_Q, blk)
    triu = jnp.kron(jnp.eye(blk // Q, dtype=f32), jnp.triu(jnp.ones((Q, Q), f32))).astype(bf16)
    x1 = _ssd_call(act, dtT, z, x, attnT, w["biasT"], w["aT"], w["dskipT"], w["ssm_norm"], triu, w["e4"], w["e2"],
                   w["gT"], w["wa"], w["ws"])
    return _ffn_call(x1, w["n2"], w["wg"], w["wu"], w["fcw"], w["fcb"], w["wd"], w["fn"])


def kernel(x_prompt, x_sample, norm1, w_in, q_a_norm, kv_a_norm, w_q_b, w_kv_b, conv_w, conv_b,
           dt_bias_f, dt_bias_b, a_log_f, a_log_b, d_skip, ssm_norm, w_out, norm2, w_gate, w_up,
           ffn_conv_w, ffn_conv_b, w_down, final_norm):
    w = _prepare_weights(norm1[0], w_in[0], q_a_norm[0], kv_a_norm[0], w_q_b[0], w_kv_b[0], conv_w[0],
                         conv_b[0], dt_bias_f[0], dt_bias_b[0], a_log_f[0], a_log_b[0], d_skip[0],
                         ssm_norm[0], w_out[0], norm2[0], w_gate[0], w_up[0], ffn_conv_w[0],
                         ffn_conv_b[0], w_down[0], final_norm)
    return (_encoder(x_prompt, w), _encoder(x_sample, w))
```
